```python
import math
import jax, jax.numpy as jnp
from jax import lax
import numpy as np

D_MODEL = 1024
BATCH = 2
SEQ = 8192
DEPTH = 2
DEC_BATCH = 32
DEC_SEQ = 1
PAST_LEN = 8192
PAGE_SIZE = 128

N_EVEN = (DEPTH + 1) // 2
N_ODD = DEPTH // 2
MOBA_HEADS = 8
MOBA_HEAD_DIM = 64
MOBA_WIDTH = MOBA_HEADS * MOBA_HEAD_DIM
MOBA_BLOCK = 256
MOBA_TOPK = 3
MOBA_QUERY_BLOCK = 64
MOBA_SCALE = MOBA_HEAD_DIM ** -0.5
CONV_CH = D_MODEL // 2
CONV_WIDTH = 31
EVEN_IN = 3 * MOBA_WIDTH + 2 * CONV_CH
EVEN_MIX = MOBA_WIDTH + CONV_CH
HGRN_HEADS = 8
HGRN_DK = D_MODEL // HGRN_HEADS
HGRN_DV = D_MODEL // HGRN_HEADS
HGRN_KW = HGRN_HEADS * HGRN_DK
HGRN_VW = HGRN_HEADS * HGRN_DV
ODD_IN = 2 * HGRN_KW + 2 * HGRN_VW
HGRN_CHUNK = 64
PEER_HEADS = 8
PEER_NKEYS = 128
PEER_EXPERTS = PEER_NKEYS * PEER_NKEYS
PEER_DKEY = 256
PEER_DHALF = PEER_DKEY // 2
PEER_TOPK = 16
PEER_TOKEN_BLOCK = 256
LN_EPS = 1e-5
RMS_EPS = 1e-6
DN_ALPHA = (2 * DEPTH) ** 0.25
DN_BETA = (8 * DEPTH) ** -0.25

kernel_name = 'hybrid_moba_conformer_hgrn2_peer_step'


def layer_norm(x, g, b):
    xf = x.astype(jnp.float32)
    mu = xf.mean(-1, keepdims=True)
    var = jnp.mean(jnp.square(xf - mu), -1, keepdims=True)
    y = (xf - mu) * lax.rsqrt(var + LN_EPS) * g.astype(jnp.float32) + b.astype(jnp.float32)
    return y.astype(x.dtype)


def conformer_conv(a, gate, buf, w, b, ln_g, ln_b):
    u = a * jax.nn.sigmoid(gate)
    ext = jnp.concatenate([buf.astype(u.dtype), u], axis=1)
    y = lax.conv_general_dilated(ext, w[:, None, :].astype(u.dtype), (1,), 'VALID',
                                 dimension_numbers=('NWC', 'WIO', 'NWC'),
                                 feature_group_count=CONV_CH) + b
    y = layer_norm(y, ln_g, ln_b)
    return y * jax.nn.sigmoid(y), ext[:, -(CONV_WIDTH - 1):]


def moba_core(q, qpos, kmean, gather):
    bq, nq, nh, _ = q.shape
    nb = kmean.shape[1]
    if nb < MOBA_TOPK:
        kmean = jnp.pad(kmean, ((0, 0), (0, MOBA_TOPK - nb), (0, 0), (0, 0)))
        nb = MOBA_TOPK
    qf = q.astype(jnp.float32)
    own = qpos // MOBA_BLOCK
    gate = jnp.einsum('bqhd,bnhd->bqhn', qf, kmean)
    cand = jnp.arange(nb)[None, :] < own[:, None]
    gate = jnp.where(cand[None, :, None, :], gate, -jnp.inf)
    _, top = lax.top_k(gate, MOBA_TOPK)
    valid = jnp.arange(MOBA_TOPK)[None, :] < own[:, None]
    own_b = jnp.broadcast_to(own[None, :, None, None], (bq, nq, nh, 1)).astype(top.dtype)
    blk = jnp.concatenate([jnp.where(valid[None, :, None, :], top, own_b), own_b], -1)
    k, v = gather(blk)
    logits = jnp.einsum('bqhd,bqhtkd->bqhtk', qf, k.astype(jnp.float32)) * MOBA_SCALE
    kpos = blk[..., None] * MOBA_BLOCK + jnp.arange(MOBA_BLOCK, dtype=blk.dtype)
    slot_ok = jnp.concatenate([valid, jnp.ones((nq, 1), bool)], -1)
    mask = slot_ok[None, :, None, :, None] & (kpos <= qpos[None, :, None, None, None])
    logits = jnp.where(mask, logits, -jnp.inf)
    p = jax.nn.softmax(logits.reshape(bq, nq, nh, -1), axis=-1).reshape(logits.shape)
    out = jnp.einsum('bqhtk,bqhtkd->bqhd', p, v.astype(jnp.float32))
    return out.astype(q.dtype)


def moba_prompt(q, k, v):
    bx, s, nh, hd = q.shape
    s_pad = -(-s // MOBA_BLOCK) * MOBA_BLOCK
    nb = s_pad // MOBA_BLOCK
    padw = ((0, 0), (0, s_pad - s), (0, 0), (0, 0))
    kb = jnp.pad(k, padw).reshape(bx, nb, MOBA_BLOCK, nh, hd)
    vb = jnp.pad(v, padw).reshape(bx, nb, MOBA_BLOCK, nh, hd)
    kmean = kb.astype(jnp.float32).mean(axis=2)
    kbt = kb.transpose(0, 3, 1, 2, 4)
    vbt = vb.transpose(0, 3, 1, 2, 4)
    bidx = jnp.arange(bx)[:, None, None, None]
    hidx = jnp.arange(nh)[None, None, :, None]

    def gather(blk):
        return kbt[bidx, hidx, blk], vbt[bidx, hidx, blk]

    nqb = s // MOBA_QUERY_BLOCK
    qb = q.reshape(bx, nqb, MOBA_QUERY_BLOCK, nh, hd).transpose(1, 0, 2, 3, 4)
    pos = jnp.arange(s, dtype=jnp.int32).reshape(nqb, MOBA_QUERY_BLOCK)
    out = lax.map(lambda qp: moba_core(qp[0], qp[1], kmean, gather), (qb, pos))
    return out.transpose(1, 0, 2, 3, 4).reshape(bx, s, nh, hd)


def moba_sample(q, k_new, v_new, k_pool, v_pool, page_table):
    db, L, nh, hd = q.shape
    n_pages = page_table.shape[1]
    past = n_pages * PAGE_SIZE
    total = past + L
    t_pad = -(-total // MOBA_BLOCK) * MOBA_BLOCK
    k_past = k_pool[page_table].reshape(db, past, nh, hd)
    k_all = jnp.concatenate([k_past, k_new.astype(k_past.dtype)], axis=1)
    k_all = jnp.pad(k_all, ((0, 0), (0, t_pad - total), (0, 0), (0, 0)))
    kmean = k_all.astype(jnp.float32).reshape(db, t_pad // MOBA_BLOCK, MOBA_BLOCK, nh, hd).mean(axis=2)
    bidx = jnp.arange(db)[:, None, None, None, None]
    hidx = jnp.arange(nh)[None, None, :, None, None]

    def rows(pool, new, pos):
        phys = page_table[bidx, jnp.clip(pos // PAGE_SIZE, 0, n_pages - 1)]
        from_past = pool[phys, pos % PAGE_SIZE, hidx]
        from_new = new.astype(pool.dtype)[bidx, jnp.clip(pos - past, 0, L - 1), hidx]
        return jnp.where((pos < past)[..., None], from_past, from_new)

    def gather(blk):
        pos = blk[..., None] * MOBA_BLOCK + jnp.arange(MOBA_BLOCK, dtype=blk.dtype)
        return rows(k_pool, k_new, pos), rows(v_pool, v_new, pos)

    qpos = past + jnp.arange(L, dtype=jnp.int32)
    return moba_core(q, qpos, kmean, gather)


def even_mixer(x, w_in, w_out, conv_w, conv_b, cln_g, cln_b, buf, attend):
    bx, L, _ = x.shape
    h = x @ w_in
    q, k, v, a, gt = jnp.split(h, [MOBA_WIDTH, 2 * MOBA_WIDTH, 3 * MOBA_WIDTH,
                                   3 * MOBA_WIDTH + CONV_CH], axis=-1)
    heads = lambda t: t.reshape(bx, L, MOBA_HEADS, MOBA_HEAD_DIM)
    q, k, v = heads(q), heads(k), heads(v)
    att = attend(q, k, v).reshape(bx, L, MOBA_WIDTH)
    cy, new_buf = conformer_conv(a, gt, buf, conv_w, conv_b, cln_g, cln_b)
    y = jnp.concatenate([att, cy.astype(att.dtype)], axis=-1) @ w_out
    return y, k, v, new_buf


def hgrn_chunked(q, k, logf, i):
    bx, s, nh, dk = q.shape
    dv = i.shape[-1]
    nc = s // HGRN_CHUNK
    chunks = lambda t: t.reshape(bx, nc, HGRN_CHUNK, nh, t.shape[-1]).transpose(1, 0, 3, 2, 4)
    causal = jnp.tril(jnp.ones((HGRN_CHUNK, HGRN_CHUNK), bool))

    def step(state, inp):
        qc, kc, lc, ic = inp
        b = jnp.cumsum(lc, axis=2)
        diff = b[:, :, :, None, :] - b[:, :, None, :, :]
        decay = jnp.exp(jnp.where(causal[:, :, None], diff, -jnp.inf))
        att = jnp.einsum('bhtk,bhsk,bhtsk->bhts', qc, kc, decay)
        o = att @ ic + jnp.einsum('bhtk,bhkv->bhtv', qc * jnp.exp(b), state)
        b_last = b[:, :, -1, :]
        state = jnp.exp(b_last)[..., None] * state + jnp.einsum(
            'bhsk,bhsv->bhkv', kc * jnp.exp(b_last[:, :, None, :] - b), ic)
        return state, o

    s0 = jnp.zeros((bx, nh, dk, dv), jnp.float32)
    sf, o = lax.scan(step, s0, (chunks(q), chunks(k), chunks(logf), chunks(i)))
    return o.transpose(1, 0, 3, 2, 4).reshape(bx, s, nh, dv), sf


def hgrn_stepwise(q, k, logf, i, state):
    tm = lambda t: t.transpose(1, 0, 2, 3)

    def step(st, inp):
        qt, kt, lt, it = inp
        st = jnp.exp(lt)[..., None] * st + kt[..., None] * it[..., None, :]
        return st, jnp.einsum('bhk,bhkv->bhv', qt, st)

    sf, o = lax.scan(step, state, (tm(q), tm(k), tm(logf), tm(i)))
    return o.transpose(1, 0, 2, 3), sf


def odd_mixer(x, w_in, w_out, norm_g, lb, recur):
    bx, L, _ = x.shape
    h = x @ w_in
    q, fz, i, g = jnp.split(h, [HGRN_KW, 2 * HGRN_KW, 2 * HGRN_KW + HGRN_VW], axis=-1)
    f = lb + (1.0 - lb) * jax.nn.sigmoid(fz.astype(jnp.float32))
    heads = lambda t, d: t.astype(jnp.float32).reshape(bx, L, HGRN_HEADS, d)
    o, st = recur(heads(q, HGRN_DK), heads(1.0 - f, HGRN_DK), heads(jnp.log(f), HGRN_DK),
                  heads(i, HGRN_DV))
    o = o * lax.rsqrt(jnp.mean(o * o, -1, keepdims=True) + RMS_EPS) * norm_g.astype(jnp.float32)
    o = o.reshape(bx, L, HGRN_VW) * jax.nn.silu(g.astype(jnp.float32))
    return o.astype(x.dtype) @ w_out, st.astype(x.dtype)


def peer(x, wq, keys, u, v):
    t = x.shape[0]
    blk = min(PEER_TOKEN_BLOCK, t)
    pad = (-t) % blk
    xb = jnp.pad(x, ((0, pad), (0, 0))).reshape(-1, blk, D_MODEL)
    k1 = keys[0].astype(jnp.float32)
    k2 = keys[1].astype(jnp.float32)

    def one(xt):
        qh = (xt @ wq).astype(jnp.float32).reshape(blk, PEER_HEADS, 2, PEER_DHALF)
        s1 = jnp.einsum('thd,hnd->thn', qh[:, :, 0], k1)
        s2 = jnp.einsum('thd,hnd->thn', qh[:, :, 1], k2)
        v1, i1 = lax.top_k(s1, PEER_TOPK)
        v2, i2 = lax.top_k(s2, PEER_TOPK)
        cand = (v1[..., :, None] + v2[..., None, :]).reshape(blk, PEER_HEADS, -1)
        cidx = (i1[..., :, None] * PEER_NKEYS + i2[..., None, :]).reshape(blk, PEER_HEADS, -1)
        sc, j = lax.top_k(cand, PEER_TOPK)
        e = jnp.take_along_axis(cidx, j, axis=-1)
        g = jax.nn.softmax(sc, axis=-1)
        act = jax.nn.gelu(jnp.einsum('td,thkd->thk', xt, u[e]).astype(jnp.float32), approximate=False)
        return jnp.einsum('thk,thkd->td', (g * act).astype(xt.dtype), v[e])

    return lax.map(one, xb).reshape(-1, D_MODEL)[:t]


def setup_inputs(seed: int = 0) -> dict:
    key = jax.random.key(seed)
    ks = jax.random.split(key, 24)
    n_pages = PAST_LEN // PAGE_SIZE
    used = DEC_BATCH * n_pages
    n_pool = used + max(1, used // 4)
    nrm = lambda k, shape, s: jax.random.normal(k, shape, jnp.float32) * s
    kv_shape = (N_EVEN, n_pool, PAGE_SIZE, MOBA_HEADS, MOBA_HEAD_DIM)
    page_table = jax.random.permutation(ks[6], n_pool)[:used].reshape(DEC_BATCH, n_pages).astype(jnp.int32)
    return {
        'x_prompt': nrm(ks[0], (BATCH, SEQ, D_MODEL), 1.0),
        'x_sample': nrm(ks[1], (DEC_BATCH, DEC_SEQ, D_MODEL), 1.0),
        'cache_k': nrm(ks[2], kv_shape, 1.0),
        'cache_v': nrm(ks[3], kv_shape, 1.0),
        'state_conv': nrm(ks[4], (N_EVEN, DEC_BATCH, CONV_WIDTH - 1, CONV_CH), 0.5),
        'state_hgrn': nrm(ks[5], (N_ODD, DEC_BATCH, HGRN_HEADS, HGRN_DK, HGRN_DV), 0.5),
        'page_table': page_table,
        'w_in_even': nrm(ks[7], (N_EVEN, D_MODEL, EVEN_IN), D_MODEL ** -0.5),
        'conv_w': nrm(ks[8], (N_EVEN, CONV_WIDTH, CONV_CH), CONV_WIDTH ** -0.5),
        'conv_b': nrm(ks[9], (N_EVEN, CONV_CH), 0.01),
        'conv_ln_g': 1.0 + nrm(ks[10], (N_EVEN, CONV_CH), 0.01),
        'conv_ln_b': nrm(ks[11], (N_EVEN, CONV_CH), 0.01),
        'w_out_even': nrm(ks[12], (N_EVEN, EVEN_MIX, D_MODEL), EVEN_MIX ** -0.5 * DN_BETA),
        'w_in_odd': nrm(ks[13], (N_ODD, D_MODEL, ODD_IN), D_MODEL ** -0.5),
        'hgrn_lb_logits': nrm(ks[14], (DEPTH, HGRN_KW), 0.5),
        'hgrn_norm_g': 1.0 + nrm(ks[15], (N_ODD, HGRN_HEADS, HGRN_DV), 0.01),
        'w_out_odd': nrm(ks[16], (N_ODD, HGRN_VW, D_MODEL), HGRN_VW ** -0.5 * DN_BETA),
        'ln_g': 1.0 + nrm(ks[17], (DEPTH, 2, D_MODEL), 0.01),
        'ln_b': nrm(ks[18], (DEPTH, 2, D_MODEL), 0.01),
        'peer_wq': nrm(ks[19], (DEPTH, D_MODEL, PEER_HEADS * PEER_DKEY), D_MODEL ** -0.5),
        'peer_keys': nrm(ks[20], (DEPTH, 2, PEER_HEADS, PEER_NKEYS, PEER_DHALF), PEER_DHALF ** -0.5),
        'peer_u': nrm(ks[21], (DEPTH, PEER_EXPERTS, D_MODEL), D_MODEL ** -0.5),
        'peer_v': nrm(ks[22], (DEPTH, PEER_EXPERTS, D_MODEL), DN_BETA * PEER_HEADS ** -0.5),
    }


def reference(x_prompt, x_sample, cache_k, cache_v, state_conv, state_hgrn, page_table,
              w_in_even, conv_w, conv_b, conv_ln_g, conv_ln_b, w_out_even,
              w_in_odd, hgrn_lb_logits, hgrn_norm_g, w_out_odd,
              ln_g, ln_b, peer_wq, peer_keys, peer_u, peer_v):
    lb_p = jax.nn.softmax(hgrn_lb_logits.astype(jnp.float32), axis=0)
    lb_all = jnp.cumsum(lb_p, axis=0) - lb_p[0]
    xp, xs = x_prompt, x_sample
    kp_l, vp_l, ks_l, vs_l, cp_l, cs_l, hp_l, hs_l = [], [], [], [], [], [], [], []
    for l in range(DEPTH):
        if l % 2 == 0:
            e = l // 2
            prm = (w_in_even[e], w_out_even[e], conv_w[e], conv_b[e], conv_ln_g[e], conv_ln_b[e])
            zero_buf = jnp.zeros((xp.shape[0], CONV_WIDTH - 1, CONV_CH), xp.dtype)
            mp, kp, vp, bp = even_mixer(xp, *prm, zero_buf, moba_prompt)
            ck, cv = cache_k[e], cache_v[e]
            ms, kn, vn, bs = even_mixer(
                xs, *prm, state_conv[e],
                lambda q, k, v: moba_sample(q, k, v, ck, cv, page_table))
            kp_l.append(kp.reshape(kp.shape[0], -1, PAGE_SIZE, MOBA_HEADS, MOBA_HEAD_DIM))
            vp_l.append(vp.reshape(vp.shape[0], -1, PAGE_SIZE, MOBA_HEADS, MOBA_HEAD_DIM))
            ks_l.append(kn)
            vs_l.append(vn)
            cp_l.append(bp)
            cs_l.append(bs)
        else:
            oi = l // 2
            prm = (w_in_odd[oi], w_out_odd[oi], hgrn_norm_g[oi], lb_all[l])
            mp, sp = odd_mixer(xp, *prm, hgrn_chunked)
            st0 = state_hgrn[oi].astype(jnp.float32)
            ms, ss = odd_mixer(xs, *prm, lambda q, k, lf, i: hgrn_stepwise(q, k, lf, i, st0))
            hp_l.append(sp)
            hs_l.append(ss)
        xp = layer_norm(DN_ALPHA * xp + mp, ln_g[l, 0], ln_b[l, 0])
        xs = layer_norm(DN_ALPHA * xs + ms, ln_g[l, 0], ln_b[l, 0])
        cprm = (peer_wq[l], peer_keys[l], peer_u[l], peer_v[l])
        fp = peer(xp.reshape(-1, D_MODEL), *cprm).reshape(xp.shape)
        fs = peer(xs.reshape(-1, D_MODEL), *cprm).reshape(xs.shape)
        xp = layer_norm(DN_ALPHA * xp + fp, ln_g[l, 1], ln_b[l, 1])
        xs = layer_norm(DN_ALPHA * xs + fs, ln_g[l, 1], ln_b[l, 1])
    return (xp, xs, jnp.stack(kp_l), jnp.stack(vp_l), jnp.stack(ks_l), jnp.stack(vs_l),
            jnp.stack(cp_l), jnp.stack(cs_l), jnp.stack(hp_l), jnp.stack(hs_l))
```

```python
import functools
import math

import jax
import jax.numpy as jnp
from jax import lax
from jax.experimental import pallas as pl
from jax.experimental.pallas import tpu as pltpu

f32 = jnp.float32
bf16 = jnp.bfloat16
i32 = jnp.int32
HI = lax.Precision.HIGHEST

D_MODEL = 1024
DEPTH = 2
PAGE_SIZE = 128
MOBA_HEADS = 8
MOBA_HEAD_DIM = 64
MOBA_WIDTH = MOBA_HEADS * MOBA_HEAD_DIM
MOBA_BLOCK = 256
MOBA_TOPK = 3
MOBA_SCALE = MOBA_HEAD_DIM ** -0.5
CONV_CH = D_MODEL // 2
CONV_WIDTH = 31
HGRN_HEADS = 8
HGRN_DK = D_MODEL // HGRN_HEADS
HGRN_CHUNK = 64
HGRN_SUB = 16
PEER_HEADS = 8
PEER_NKEYS = 128
PEER_DHALF = 128
PEER_TOPK = 16
LN_EPS = 1e-5
RMS_EPS = 1e-6
DN_ALPHA = (2 * DEPTH) ** 0.25

LANES = 128
TOKEN_TILE = 512
MASK_NEG = -1e30
FLOOR = -3e38
EXP_CLAMP = 60.0
VMEM_LIMIT = 56 << 20

NT_DIMS = (((1,), (1,)), ((), ()))
TN_DIMS = (((0,), (0,)), ((), ()))


def _params(semantics, vmem=VMEM_LIMIT):
    return pltpu.CompilerParams(dimension_semantics=semantics, vmem_limit_bytes=vmem)


def _layer_norm(y, g, b):
    mu = jnp.mean(y, axis=-1, keepdims=True)
    d = y - mu
    var = jnp.mean(d * d, axis=-1, keepdims=True)
    return d * lax.rsqrt(var + LN_EPS) * g + b


def _sigmoid(x):
    return 1.0 / (1.0 + jnp.exp(-x))


def _mm_kernel(x_ref, *refs, n_chunk, split):
    if split:
        wh_ref, wl_ref, o_ref = refs
    else:
        wh_ref, o_ref = refs
    x = x_ref[...]
    xh = x.astype(bf16)
    if split:
        xl = (x - xh.astype(f32)).astype(bf16)
    for j in range(0, o_ref.shape[1], n_chunk):
        wh = wh_ref[:, j:j + n_chunk]
        acc = jnp.dot(xh, wh, preferred_element_type=f32)
        if split:
            acc = acc + jnp.dot(xl, wh, preferred_element_type=f32)
            acc = acc + jnp.dot(xh, wl_ref[:, j:j + n_chunk], preferred_element_type=f32)
        o_ref[:, j:j + n_chunk] = acc


def _mm(x, w, *, split=False, tm=TOKEN_TILE):
    m, k = x.shape
    n = w.shape[1]
    wh = w.astype(bf16)
    ws = [wh] + ([(w - wh.astype(f32)).astype(bf16)] if split else [])
    return pl.pallas_call(
        functools.partial(_mm_kernel, n_chunk=512, split=split),
        grid=(m // tm,),
        in_specs=[pl.BlockSpec((tm, k), lambda i: (i, 0))] + [pl.BlockSpec((k, n), lambda i: (0, 0))] * len(ws),
        out_specs=pl.BlockSpec((tm, n), lambda i: (i, 0)),
        out_shape=jax.ShapeDtypeStruct((m, n), f32),
        compiler_params=_params(("parallel",)),
        name="proj_split" if split else "proj",
    )(x, *ws)


def _mm_res_ln_kernel(a_ref, w_ref, x_ref, g_ref, b_ref, o_ref):
    acc = jnp.dot(a_ref[...].astype(bf16), w_ref[...], preferred_element_type=f32)
    o_ref[...] = _layer_norm(DN_ALPHA * x_ref[...] + acc, g_ref[...], b_ref[...])


def _mm_res_ln(a, w, x, g, b, *, tm=TOKEN_TILE):
    m, k = a.shape
    n = w.shape[1]
    return pl.pallas_call(
        _mm_res_ln_kernel,
        grid=(m // tm,),
        in_specs=[pl.BlockSpec((tm, k), lambda i: (i, 0)), pl.BlockSpec((k, n), lambda i: (0, 0)),
                  pl.BlockSpec((tm, n), lambda i: (i, 0)), pl.BlockSpec((1, n), lambda i: (0, 0)),
                  pl.BlockSpec((1, n), lambda i: (0, 0))],
        out_specs=pl.BlockSpec((tm, n), lambda i: (i, 0)),
        out_shape=jax.ShapeDtypeStruct((m, n), f32),
        compiler_params=_params(("parallel",)),
        name="out_proj_ln",
    )(a, w.astype(bf16), x, g.reshape(1, n), b.reshape(1, n))


CONV_TILE = 256
CONV_ROWS = 64
CONV_HIST = 32


def _conv_prompt_kernel(a_ref, gt_ref, w_ref, cb_ref, g_ref, b_ref, y_ref, st_ref, buf_ref):
    t = pl.program_id(1)

    @pl.when(t == 0)
    def _():
        buf_ref[0:CONV_HIST, :] = jnp.zeros((CONV_HIST, CONV_CH), f32)

    buf_ref[CONV_HIST:CONV_HIST + CONV_TILE, :] = a_ref[...] * _sigmoid(gt_ref[...])
    first = CONV_HIST - (CONV_WIDTH - 1)
    for r in range(0, CONV_TILE, CONV_ROWS):
        acc = jnp.zeros((CONV_ROWS, CONV_CH), f32) + cb_ref[...]
        for j in range(CONV_WIDTH):
            acc = acc + w_ref[j:j + 1, :] * buf_ref[r + first + j:r + first + j + CONV_ROWS, :]
        y = _layer_norm(acc, g_ref[...], b_ref[...])
        y_ref[r:r + CONV_ROWS, :] = y * _sigmoid(y)
    tail = buf_ref[CONV_TILE:CONV_TILE + CONV_HIST, :]
    buf_ref[0:CONV_HIST, :] = tail

    @pl.when(t == pl.num_programs(1) - 1)
    def _():
        st_ref[0] = tail


def _conv_prompt(vag, batch, seq, w, cb, g, b):
    nt = seq // CONV_TILE
    wpad = jnp.zeros((CONV_HIST, CONV_CH), f32).at[:CONV_WIDTH].set(w)
    row = lambda v: v.reshape(1, CONV_CH)
    y, st = pl.pallas_call(
        _conv_prompt_kernel,
        grid=(batch, nt),
        in_specs=[pl.BlockSpec((CONV_TILE, CONV_CH), lambda bi, t: (bi * nt + t, 1)),
                  pl.BlockSpec((CONV_TILE, CONV_CH), lambda bi, t: (bi * nt + t, 2)),
                  pl.BlockSpec((CONV_HIST, CONV_CH), lambda bi, t: (0, 0))]
                 + [pl.BlockSpec((1, CONV_CH), lambda bi, t: (0, 0))] * 3,
        out_specs=[pl.BlockSpec((CONV_TILE, CONV_CH), lambda bi, t: (bi * nt + t, 0)),
                   pl.BlockSpec((1, CONV_HIST, CONV_CH), lambda bi, t: (bi, 0, 0))],
        out_shape=[jax.ShapeDtypeStruct((batch * seq, CONV_CH), f32),
                   jax.ShapeDtypeStruct((batch, CONV_HIST, CONV_CH), f32)],
        scratch_shapes=[pltpu.VMEM((CONV_TILE + CONV_HIST, CONV_CH), f32)],
        compiler_params=_params(("arbitrary", "arbitrary")),
        name="conv_prompt",
    )(vag, vag, wpad, row(cb), row(g), row(b))
    return y, st[:, CONV_HIST - (CONV_WIDTH - 1):]


def _conv_sample_kernel(a_ref, gt_ref, st_ref, w_ref, cb_ref, g_ref, b_ref, y_ref, nst_ref):
    u = a_ref[...] * _sigmoid(gt_ref[...])
    acc = cb_ref[...] + w_ref[CONV_WIDTH - 1:CONV_WIDTH, :] * u
    for j in range(CONV_WIDTH - 1):
        acc = acc + w_ref[j:j + 1, :] * st_ref[j]
    y = _layer_norm(acc, g_ref[...], b_ref[...])
    y_ref[...] = y * _sigmoid(y)
    for j in range(CONV_WIDTH - 2):
        nst_ref[j] = st_ref[j + 1]
    nst_ref[CONV_WIDTH - 2] = u


def _conv_sample(a, gt, state, w, cb, g, b):
    db = a.shape[0]
    wpad = jnp.zeros((CONV_HIST, CONV_CH), f32).at[:CONV_WIDTH].set(w)
    row = lambda v: v.reshape(1, CONV_CH)
    y, nst = pl.pallas_call(
        _conv_sample_kernel,
        out_shape=[jax.ShapeDtypeStruct((db, CONV_CH), f32),
                   jax.ShapeDtypeStruct((CONV_WIDTH - 1, db, CONV_CH), f32)],
        name="conv_sample",
    )(a, gt, state.transpose(1, 0, 2), wpad, row(cb), row(g), row(b))
    return y, nst.transpose(1, 0, 2)


GATE_ROWS = 1024


def _block_sum_kernel(k_ref, o_ref):
    o_ref[0] = jnp.sum(k_ref[...], axis=0, keepdims=True)


def _block_sums(qk, batch, seq):
    nb = seq // MOBA_BLOCK
    return pl.pallas_call(
        _block_sum_kernel,
        grid=(batch * nb,),
        in_specs=[pl.BlockSpec((MOBA_BLOCK, MOBA_WIDTH), lambda i: (i, 1))],
        out_specs=pl.BlockSpec((1, 1, MOBA_WIDTH), lambda i: (i, 0, 0)),
        out_shape=jax.ShapeDtypeStruct((batch * nb, 1, MOBA_WIDTH), f32),
        compiler_params=_params(("parallel",)),
        name="moba_block_sums",
    )(qk)


def _moba_gate_kernel(q_ref, k_ref, v_ref, ks_ref, qa_ref, ka_ref, vb_ref, *, nb):
    t = pl.program_id(2)
    lane = lax.broadcasted_iota(i32, (MOBA_BLOCK, LANES), 1)
    blk = lane - MOBA_HEAD_DIM
    km_lane = lax.broadcasted_iota(i32, (nb, LANES), 1)
    ksum = ks_ref[0] * (1.0 / MOBA_BLOCK)
    pad_top = jnp.zeros((MOBA_HEAD_DIM, LANES), f32)
    pad_bot = jnp.zeros((LANES - MOBA_HEAD_DIM - nb, LANES), f32)
    for c in range(GATE_ROWS // MOBA_BLOCK):
        own = t * (GATE_ROWS // MOBA_BLOCK) + c
        rows = slice(c * MOBA_BLOCK, (c + 1) * MOBA_BLOCK)
        q2 = q_ref[rows, :]
        k2 = k_ref[rows, :]
        vb_ref[rows, :] = v_ref[rows, :].astype(bf16)
        cand = (blk >= 0) & (blk < own)
        for j in range(2):
            head = (km_lane >= j * MOBA_HEAD_DIM) & (km_lane < (j + 1) * MOBA_HEAD_DIM)
            km = jnp.concatenate([pad_top, jnp.where(head, ksum, 0.0), pad_bot], axis=0)
            gate = lax.dot_general(q2, km, NT_DIMS, precision=HI, preferred_element_type=f32)
            masked = jnp.where(cand, gate, FLOOR)
            sel = jnp.zeros((MOBA_BLOCK, LANES), f32)
            for _ in range(MOBA_TOPK):
                mx = jnp.max(masked, axis=1, keepdims=True)
                hit = (masked == mx) & cand
                sel = jnp.where(hit, 1.0, sel)
                masked = jnp.where(hit, FLOOR, masked)
            bias = jnp.where((sel > 0.0) | (blk == own), 0.0, MASK_NEG)
            qj = q2 if j == 0 else pltpu.roll(q2, MOBA_HEAD_DIM, 1)
            kj = k2 if j == 0 else pltpu.roll(k2, MOBA_HEAD_DIM, 1)
            in_head = lane < MOBA_HEAD_DIM
            in_bias = lane < MOBA_HEAD_DIM + nb
            qa = jnp.where(in_head, qj * MOBA_SCALE, jnp.where(in_bias, bias, 0.0))
            ka = jnp.where(in_head, kj, jnp.where(blk == own, 1.0, 0.0))
            qa_ref[0, j, rows, :] = qa.astype(bf16)
            ka_ref[0, j, rows, :] = ka.astype(bf16)


def _moba_attn_kernel(qa_ref, ka_ref, vb_ref, o_ref):
    i = pl.program_id(2)
    row = lax.broadcasted_iota(i32, (MOBA_BLOCK, MOBA_BLOCK), 0)
    col = lax.broadcasted_iota(i32, (MOBA_BLOCK, MOBA_BLOCK), 1)
    lane = lax.broadcasted_iota(i32, (MOBA_BLOCK, LANES), 1)
    outs = []
    for j in range(2):
        qa = qa_ref[0, j]
        r0 = pl.multiple_of(i * MOBA_BLOCK, MOBA_BLOCK)
        s = lax.dot_general(qa, ka_ref[0, j, pl.ds(r0, MOBA_BLOCK), :], NT_DIMS, preferred_element_type=f32)
        s = jnp.where(col <= row, s, MASK_NEG)
        m = jnp.max(s, axis=1, keepdims=True)
        p = jnp.exp(s - m)
        l = jnp.sum(p, axis=1, keepdims=True)
        acc = jnp.dot(p.astype(bf16), vb_ref[pl.ds(r0, MOBA_BLOCK), :], preferred_element_type=f32)

        def body(n, carry, qa=qa, j=j):
            m, l, acc = carry
            c0 = pl.multiple_of(n * MOBA_BLOCK, MOBA_BLOCK)
            s = lax.dot_general(qa, ka_ref[0, j, pl.ds(c0, MOBA_BLOCK), :], NT_DIMS, preferred_element_type=f32)
            mn = jnp.maximum(m, jnp.max(s, axis=1, keepdims=True))
            alpha = jnp.exp(m - mn)
            p = jnp.exp(s - mn)
            l = alpha * l + jnp.sum(p, axis=1, keepdims=True)
            acc = alpha * acc + jnp.dot(p.astype(bf16), vb_ref[pl.ds(c0, MOBA_BLOCK), :], preferred_element_type=f32)
            return mn, l, acc

        m, l, acc = lax.fori_loop(0, i, body, (m, l, acc))
        outs.append(acc / l)
    o_ref[...] = jnp.where(lane < MOBA_HEAD_DIM, outs[0], outs[1])


def _moba_prompt(qk, vag, batch, seq):
    nb = seq // MOBA_BLOCK
    npair = MOBA_WIDTH // LANES
    nt = seq // GATE_ROWS
    ksums = _block_sums(qk, batch, seq).reshape(batch, nb, MOBA_WIDTH)
    qa, ka, vb = pl.pallas_call(
        functools.partial(_moba_gate_kernel, nb=nb),
        grid=(batch, npair, nt),
        in_specs=[pl.BlockSpec((GATE_ROWS, LANES), lambda b, p, t: (b * nt + t, p)),
                  pl.BlockSpec((GATE_ROWS, LANES), lambda b, p, t: (b * nt + t, npair + p)),
                  pl.BlockSpec((GATE_ROWS, LANES), lambda b, p, t: (b * nt + t, p)),
                  pl.BlockSpec((1, nb, LANES), lambda b, p, t: (b, 0, p))],
        out_specs=[pl.BlockSpec((1, 2, GATE_ROWS, LANES), lambda b, p, t: (b, p, t, 0)),
                   pl.BlockSpec((1, 2, GATE_ROWS, LANES), lambda b, p, t: (b, p, t, 0)),
                   pl.BlockSpec((GATE_ROWS, LANES), lambda b, p, t: (b * nt + t, p))],
        out_shape=[jax.ShapeDtypeStruct((batch, MOBA_HEADS, seq, LANES), bf16),
                   jax.ShapeDtypeStruct((batch, MOBA_HEADS, seq, LANES), bf16),
                   jax.ShapeDtypeStruct((batch * seq, MOBA_WIDTH), bf16)],
        compiler_params=_params(("parallel", "parallel", "parallel")),
        name="moba_gate",
    )(qk, qk, vag, ksums)
    return pl.pallas_call(
        _moba_attn_kernel,
        grid=(batch, npair, nb),
        in_specs=[pl.BlockSpec((1, 2, MOBA_BLOCK, LANES), lambda b, p, i: (b, p, i, 0)),
                  pl.BlockSpec((1, 2, seq, LANES), lambda b, p, i: (b, p, 0, 0)),
                  pl.BlockSpec((seq, LANES), lambda b, p, i: (b, p))],
        out_specs=pl.BlockSpec((MOBA_BLOCK, LANES), lambda b, p, i: (b * nb + i, p)),
        out_shape=jax.ShapeDtypeStruct((batch * seq, MOBA_WIDTH), f32),
        compiler_params=_params(("parallel", "parallel", "arbitrary")),
        name="moba_attn",
    )(qa, ka, vb)


PAGES_PER_STEP = 8
PAGES_PER_BLOCK = MOBA_BLOCK // PAGE_SIZE
BLOCKS_PER_STEP = PAGES_PER_STEP // PAGES_PER_BLOCK


def _head_expand(dtype):
    r = lax.broadcasted_iota(i32, (LANES, MOBA_WIDTH), 0)
    c = lax.broadcasted_iota(i32, (LANES, MOBA_WIDTH), 1)
    return jnp.where(c // MOBA_HEAD_DIM == r, 1.0, 0.0).astype(dtype)


def _head_reduce():
    r = lax.broadcasted_iota(i32, (MOBA_WIDTH, LANES), 0)
    c = lax.broadcasted_iota(i32, (MOBA_WIDTH, LANES), 1)
    return jnp.where(r // MOBA_HEAD_DIM == c, 1.0, 0.0).astype(f32)


def _moba_sample_stream_kernel(pt_ref, qm_ref, *refs):
    del pt_ref
    k_refs = refs[:PAGES_PER_STEP]
    v_refs = refs[PAGES_PER_STEP:2 * PAGES_PER_STEP]
    ks_ref, m_ref, l_ref, acc_ref = refs[2 * PAGES_PER_STEP:]
    qm = qm_ref[0]
    expand = _head_expand(bf16)
    for jj in range(BLOCKS_PER_STEP):
        kb = jnp.concatenate([k_refs[PAGES_PER_BLOCK * jj + x][0] for x in range(PAGES_PER_BLOCK)], axis=0)
        vb = jnp.concatenate([v_refs[PAGES_PER_BLOCK * jj + x][0] for x in range(PAGES_PER_BLOCK)], axis=0)
        s = jnp.dot(kb.astype(bf16), qm, preferred_element_type=f32)
        m = jnp.max(s, axis=0, keepdims=True)
        p = jnp.exp(s - m)
        pe = jnp.dot(p.astype(bf16), expand, preferred_element_type=f32)
        ks_ref[0, 0, jj:jj + 1, :] = jnp.sum(kb, axis=0, keepdims=True)
        m_ref[0, 0, jj:jj + 1, :] = m
        l_ref[0, 0, jj:jj + 1, :] = jnp.sum(p, axis=0, keepdims=True)
        acc_ref[0, 0, jj:jj + 1, :] = jnp.sum(pe * vb, axis=0, keepdims=True)


def _moba_sample_combine_kernel(ks_ref, m_ref, l_ref, acc_ref, q_ref, kn_ref, vn_ref, o_ref):
    db, nb, _ = ks_ref.shape
    reduce_h = _head_reduce()
    expand = _head_expand(f32)
    q = q_ref[...]
    prod = (ks_ref[...] * q[:, None, :]).reshape(db * nb, MOBA_WIDTH)
    gate = jnp.dot(prod, reduce_h, precision=HI, preferred_element_type=f32).reshape(db, nb, LANES)
    masked = gate
    sel = jnp.zeros((db, nb, LANES), f32)
    for _ in range(MOBA_TOPK):
        mx = jnp.max(masked, axis=1, keepdims=True)
        hit = masked == mx
        sel = jnp.where(hit, 1.0, sel)
        masked = jnp.where(hit, FLOOR, masked)
    chosen = sel > 0.0
    s_self = jnp.dot(q * kn_ref[...], reduce_h, precision=HI, preferred_element_type=f32) * MOBA_SCALE
    m = m_ref[...]
    top = jnp.maximum(jnp.max(jnp.where(chosen, m, FLOOR), axis=1), s_self)
    w = jnp.where(chosen, jnp.exp(m - top[:, None, :]), 0.0)
    w_self = jnp.exp(s_self - top)
    denom = jnp.sum(w * l_ref[...], axis=1) + w_self
    we = jnp.dot(w.reshape(db * nb, LANES), expand, precision=HI, preferred_element_type=f32)
    num = jnp.sum(we.reshape(db, nb, MOBA_WIDTH) * acc_ref[...], axis=1)
    num = num + jnp.dot(w_self, expand, precision=HI, preferred_element_type=f32) * vn_ref[...]
    o_ref[...] = num / jnp.dot(denom, expand, precision=HI, preferred_element_type=f32)


def _moba_sample(q, kn, vn, k_pool, v_pool, page_table):
    db, n_pages = page_table.shape
    nb = n_pages // PAGES_PER_BLOCK
    ng = n_pages // PAGES_PER_STEP
    head_of = jnp.arange(MOBA_WIDTH) // MOBA_HEAD_DIM
    qm = jnp.where(head_of[:, None] == jnp.arange(LANES)[None, :], (q * MOBA_SCALE)[:, :, None], 0.0).astype(bf16)
    page_spec = lambda x: pl.BlockSpec((1, PAGE_SIZE, MOBA_WIDTH),
                                       lambda b, g, pt, x=x: (pt[b, g * PAGES_PER_STEP + x], 0, 0))
    stat = lambda w: pl.BlockSpec((1, 1, BLOCKS_PER_STEP, w), lambda b, g, pt: (b, g, 0, 0))
    stat_shape = lambda w: jax.ShapeDtypeStruct((db, ng, BLOCKS_PER_STEP, w), f32)
    ks, m, l, acc = pl.pallas_call(
        _moba_sample_stream_kernel,
        grid_spec=pltpu.PrefetchScalarGridSpec(
            num_scalar_prefetch=1,
            grid=(db, ng),
            in_specs=[pl.BlockSpec((1, MOBA_WIDTH, LANES), lambda b, g, pt: (b, 0, 0))]
                     + [page_spec(x) for x in range(PAGES_PER_STEP)] * 2,
            out_specs=[stat(MOBA_WIDTH), stat(LANES), stat(LANES), stat(MOBA_WIDTH)]),
        out_shape=[stat_shape(MOBA_WIDTH), stat_shape(LANES), stat_shape(LANES), stat_shape(MOBA_WIDTH)],
        compiler_params=_params(("parallel", "parallel")),
        name="moba_sample_stream",
    )(page_table, qm, *([k_pool] * PAGES_PER_STEP), *([v_pool] * PAGES_PER_STEP))
    ks, m, l, acc = (a.reshape(db, nb, a.shape[-1]) for a in (ks, m, l, acc))
    return pl.pallas_call(
        _moba_sample_combine_kernel,
        out_shape=jax.ShapeDtypeStruct((db, MOBA_WIDTH), f32),
        compiler_params=pltpu.CompilerParams(vmem_limit_bytes=VMEM_LIMIT),
        name="moba_sample_combine",
    )(ks, m, l, acc, q, kn, vn)


HGRN_TILE = 512


def _hgrn_prompt_kernel(q_ref, fz_ref, i_ref, g_ref, lb_ref, ng_ref, o_ref, st_ref, state_ref):
    t = pl.program_id(1)
    c_rows = HGRN_CHUNK

    @pl.when(t == 0)
    def _():
        state_ref[...] = jnp.zeros(state_ref.shape, f32)

    row = lax.broadcasted_iota(i32, (c_rows, c_rows), 0)
    col = lax.broadcasted_iota(i32, (c_rows, c_rows), 1)
    causal = col <= row
    cum = jnp.where(causal, 1.0, 0.0)
    cum_sub = jnp.where(col < (row // HGRN_SUB) * HGRN_SUB, 1.0, 0.0)
    cum_both = jnp.concatenate([cum, cum_sub], axis=0)

    for h in range(HGRN_HEADS):
        lanes = slice(h * HGRN_DK, (h + 1) * HGRN_DK)
        lb = lb_ref[:, lanes]
        ng = ng_ref[:, lanes]

        def chunk(c, carry, lanes=lanes, lb=lb, ng=ng, h=h):
            rows = pl.ds(pl.multiple_of(c * c_rows, c_rows), c_rows)
            q = q_ref[rows, lanes]
            v = i_ref[rows, lanes]
            gg = g_ref[rows, lanes]
            f = lb + (1.0 - lb) * _sigmoid(fz_ref[rows, lanes])
            kk = 1.0 - f
            both = jnp.dot(cum_both, jnp.log(f), precision=HI, preferred_element_type=f32)
            b = both[:c_rows]
            ref_row = both[c_rows:]
            st = state_ref[h]
            o = lax.dot_general((q * jnp.exp(b)).astype(bf16), st.astype(bf16), NT_DIMS, preferred_element_type=f32)
            qh = (q * jnp.exp(b - ref_row)).astype(bf16)
            parts = []
            for s in range(c_rows // HGRN_SUB):
                ref_s = ref_row[s * HGRN_SUB:s * HGRN_SUB + 1, :]
                kh = (kk * jnp.exp(jnp.minimum(ref_s - b, EXP_CLAMP))).astype(bf16)
                parts.append(lax.dot_general(qh[s * HGRN_SUB:(s + 1) * HGRN_SUB], kh, NT_DIMS,
                                             preferred_element_type=f32))
            att = jnp.where(causal, jnp.concatenate(parts, axis=0), 0.0)
            o = o + jnp.dot(att.astype(bf16), v.astype(bf16), preferred_element_type=f32)
            b_last = b[c_rows - 1:c_rows, :]
            kd = (kk * jnp.exp(b_last - b)).astype(bf16)
            state_ref[h] = st * jnp.exp(b_last) + lax.dot_general(v.astype(bf16), kd, TN_DIMS,
                                                                   preferred_element_type=f32)
            ms = jnp.mean(o * o, axis=1, keepdims=True)
            o_ref[rows, lanes] = o * lax.rsqrt(ms + RMS_EPS) * ng * (gg * _sigmoid(gg))
            return carry

        lax.fori_loop(0, HGRN_TILE // c_rows, chunk, 0)

    @pl.when(t == pl.num_programs(1) - 1)
    def _():
        for h in range(HGRN_HEADS):
            st_ref[0, h] = state_ref[h].T


def _hgrn_prompt(hproj, batch, seq, lb, norm_g):
    nt = seq // HGRN_TILE
    width = HGRN_HEADS * HGRN_DK
    col = lambda j: pl.BlockSpec((HGRN_TILE, width), lambda b, t, j=j: (b * nt + t, j))
    vec = pl.BlockSpec((1, width), lambda b, t: (0, 0))
    return pl.pallas_call(
        _hgrn_prompt_kernel,
        grid=(batch, nt),
        in_specs=[col(0), col(1), col(2), col(3), vec, vec],
        out_specs=[pl.BlockSpec((HGRN_TILE, width), lambda b, t: (b * nt + t, 0)),
                   pl.BlockSpec((1, HGRN_HEADS, HGRN_DK, HGRN_DK), lambda b, t: (b, 0, 0, 0))],
        out_shape=[jax.ShapeDtypeStruct((batch * seq, width), f32),
                   jax.ShapeDtypeStruct((batch, HGRN_HEADS, HGRN_DK, HGRN_DK), f32)],
        scratch_shapes=[pltpu.VMEM((HGRN_HEADS, HGRN_DK, HGRN_DK), f32)],
        compiler_params=_params(("parallel", "arbitrary")),
        name="hgrn_prompt",
    )(hproj, hproj, hproj, hproj, lb.reshape(1, width), norm_g.reshape(1, width))


def _hgrn_sample_kernel(qc_ref, fzc_ref, lbc_ref, i_ref, g_ref, ng_ref, st_ref, o_ref, nst_ref):
    for h in range(HGRN_HEADS):
        lb = lbc_ref[h]
        f = lb + (1.0 - lb) * _sigmoid(fzc_ref[0, h])
        st = f * st_ref[0, h] + (1.0 - f) * i_ref[0, h]
        nst_ref[0, h] = st
        o = jnp.sum(qc_ref[0, h] * st, axis=0, keepdims=True)
        ms = jnp.mean(o * o, axis=1, keepdims=True)
        gg = g_ref[0, h]
        o_ref[0, h] = o * lax.rsqrt(ms + RMS_EPS) * ng_ref[h] * (gg * _sigmoid(gg))


def _hgrn_sample(hs, state, lb, norm_g):
    db = hs.shape[0]
    width = HGRN_HEADS * HGRN_DK
    colv = lambda x: x.reshape(db, HGRN_HEADS, HGRN_DK, 1)
    rowv = lambda x: x.reshape(db, HGRN_HEADS, 1, HGRN_DK)
    cspec = pl.BlockSpec((1, HGRN_HEADS, HGRN_DK, 1), lambda b: (b, 0, 0, 0))
    rspec = pl.BlockSpec((1, HGRN_HEADS, 1, HGRN_DK), lambda b: (b, 0, 0, 0))
    sspec = pl.BlockSpec((1, HGRN_HEADS, HGRN_DK, HGRN_DK), lambda b: (b, 0, 0, 0))
    o, nst = pl.pallas_call(
        _hgrn_sample_kernel,
        grid=(db,),
        in_specs=[cspec, cspec, pl.BlockSpec((HGRN_HEADS, HGRN_DK, 1), lambda b: (0, 0, 0)), rspec, rspec,
                  pl.BlockSpec((HGRN_HEADS, 1, HGRN_DK), lambda b: (0, 0, 0)), sspec],
        out_specs=[rspec, sspec],
        out_shape=[jax.ShapeDtypeStruct((db, HGRN_HEADS, 1, HGRN_DK), f32),
                   jax.ShapeDtypeStruct((db, HGRN_HEADS, HGRN_DK, HGRN_DK), f32)],
        compiler_params=_params(("parallel",)),
        name="hgrn_sample",
    )(colv(hs[:, :width]), colv(hs[:, width:2 * width]), lb.reshape(HGRN_HEADS, HGRN_DK, 1),
      rowv(hs[:, 2 * width:3 * width]), rowv(hs[:, 3 * width:]), norm_g.reshape(HGRN_HEADS, 1, HGRN_DK), state)
    return o.reshape(db, width), nst


PEER_EBLK = 1024
PEER_RANKS = PEER_TOPK + 1
PEER_VROWS = 24
PEER_SUB = 8


def _extract_max(tiles):
    mx = functools.reduce(jnp.maximum, tiles)
    mx = jnp.max(mx, axis=0, keepdims=True)
    return mx, [jnp.where(x == mx, FLOOR, x) for x in tiles]


def _peer_kernel(x_ref, q_ref, k1_ref, k2_ref, u_ref, vt_ref, g_ref, b_ref, o_ref,
                 xt_ref, acc_ref, act_ref, p_ref, a_ref, bw_ref, c_ref, s2_ref, v1_ref, v2_ref, ab_ref, cb_ref):
    e = pl.program_id(1)
    tt = x_ref.shape[0]
    n_lane_chunks = tt // LANES
    sub = PEER_SUB
    n_sub = PEER_NKEYS // sub
    keys_per_step = PEER_EBLK // PEER_NKEYS

    @pl.when(e == 0)
    def _():
        xt_ref[...] = x_ref[...].T.astype(bf16)
        acc_ref[...] = jnp.zeros(acc_ref.shape, f32)
        floor_rows = jnp.full((PEER_VROWS, LANES), FLOOR, f32)
        for h in range(PEER_HEADS):
            q1 = q_ref[:, (2 * h) * PEER_DHALF:(2 * h + 1) * PEER_DHALF]
            q2 = q_ref[:, (2 * h + 1) * PEER_DHALF:(2 * h + 2) * PEER_DHALF]
            s1 = lax.dot_general(k1_ref[h], q1, NT_DIMS, precision=HI, preferred_element_type=f32)
            c_ref[h] = s1.reshape(n_sub, sub, tt)
            s2_ref[h] = lax.dot_general(k2_ref[h], q2, NT_DIMS, precision=HI, preferred_element_type=f32)

        def per_head(h, carry):
            for lc in range(n_lane_chunks):
                lanes = slice(lc * LANES, (lc + 1) * LANES)
                for side, v_ref in enumerate((v1_ref, v2_ref)):
                    v_ref[...] = floor_rows
                    if side == 0:
                        tiles = [c_ref[h, k, :, lanes] for k in range(n_sub)]
                    else:
                        tiles = [s2_ref[h, k * sub:(k + 1) * sub, lanes] for k in range(n_sub)]
                    for r in range(PEER_RANKS):
                        mx, tiles = _extract_max(tiles)
                        v_ref[r:r + 1, :] = mx
                cands = [v1_ref[0:1, :] + v2_ref[r:r + sub, :] for r in range(0, PEER_VROWS, sub)]
                cands += [v1_ref[r:r + 1, :] + v2_ref[0:sub, :] for r in range(1, sub)]
                cands += [v1_ref[r:r + sub, :] + v2_ref[0:1, :] for r in range(sub, PEER_VROWS, sub)]
                best = v1_ref[0:1, :] + v2_ref[0:1, :]
                zsum = jnp.zeros((1, LANES), f32)
                kth = best
                for r in range(PEER_TOPK):
                    kth, cands = _extract_max(cands)
                    zsum = zsum + jnp.exp(kth - best)
                nxt, _ = _extract_max(cands)
                thresh = 0.5 * (kth + nxt)
                s1 = c_ref[h, :, :, lanes]
                a_ref[h, :, :, lanes] = jnp.exp(s1 - v1_ref[0:1, :]) / zsum
                bw_ref[h, :, lanes] = jnp.exp(s2_ref[h, :, lanes] - v2_ref[0:1, :])
                c_ref[h, :, :, lanes] = thresh - s1
            return carry

        lax.fori_loop(0, PEER_HEADS, per_head, 0)

    act_ref[...] = jnp.dot(u_ref[...], xt_ref[...], preferred_element_type=f32)
    for h in range(PEER_HEADS):
        for ii in range(keys_per_step):
            ab_ref[ii, h] = jnp.broadcast_to(a_ref[h, e, ii:ii + 1, :], (sub, tt))
            cb_ref[ii, h] = jnp.broadcast_to(c_ref[h, e, ii:ii + 1, :], (sub, tt))

    def per_key(ii, carry):
        rows = pl.ds(pl.multiple_of(ii * PEER_NKEYS, PEER_NKEYS), PEER_NKEYS)
        for lc in range(n_lane_chunks):
            lanes = slice(lc * LANES, (lc + 1) * LANES)
            w = jnp.zeros((n_sub, sub, LANES), f32)
            for h in range(PEER_HEADS):
                s2 = s2_ref[h, :, lanes].reshape(n_sub, sub, LANES)
                bw = bw_ref[h, :, lanes].reshape(n_sub, sub, LANES)
                hit = s2 >= cb_ref[ii, h, :, lanes][None]
                w = w + ab_ref[ii, h, :, lanes][None] * jnp.where(hit, bw, 0.0)
            act = act_ref[rows, lanes]
            gelu = 0.5 * act * (1.0 + lax.erf(act * (2.0 ** -0.5)))
            p_ref[rows, lanes] = (w.reshape(PEER_NKEYS, LANES) * gelu).astype(bf16)
        return carry

    lax.fori_loop(0, keys_per_step, per_key, 0)
    acc_ref[...] += jnp.dot(vt_ref[...], p_ref[...], preferred_element_type=f32)

    @pl.when(e == pl.num_programs(1) - 1)
    def _():
        o_ref[...] = _layer_norm(DN_ALPHA * x_ref[...] + acc_ref[...].T, g_ref[...], b_ref[...])


def _peer_ln(x, wq, keys, u, v, g, b, *, tt=TOKEN_TILE):
    m, d = x.shape
    n_exp = u.shape[0]
    q = _mm(x, wq, split=True)
    nq = q.shape[1]
    return pl.pallas_call(
        _peer_kernel,
        grid=(m // tt, n_exp // PEER_EBLK),
        in_specs=[pl.BlockSpec((tt, d), lambda t, e: (t, 0)),
                  pl.BlockSpec((tt, nq), lambda t, e: (t, 0)),
                  pl.BlockSpec((PEER_HEADS, PEER_NKEYS, PEER_DHALF), lambda t, e: (0, 0, 0)),
                  pl.BlockSpec((PEER_HEADS, PEER_NKEYS, PEER_DHALF), lambda t, e: (0, 0, 0)),
                  pl.BlockSpec((PEER_EBLK, d), lambda t, e: (e, 0)),
                  pl.BlockSpec((d, PEER_EBLK), lambda t, e: (0, e)),
                  pl.BlockSpec((1, d), lambda t, e: (0, 0)),
                  pl.BlockSpec((1, d), lambda t, e: (0, 0))],
        out_specs=pl.BlockSpec((tt, d), lambda t, e: (t, 0)),
        out_shape=jax.ShapeDtypeStruct((m, d), f32),
        scratch_shapes=[pltpu.VMEM((d, tt), bf16), pltpu.VMEM((d, tt), f32),
                        pltpu.VMEM((PEER_EBLK, tt), f32), pltpu.VMEM((PEER_EBLK, tt), bf16)]
                       + [pltpu.VMEM((PEER_HEADS, PEER_NKEYS // PEER_SUB, PEER_SUB, tt), f32),
                          pltpu.VMEM((PEER_HEADS, PEER_NKEYS, tt), f32),
                          pltpu.VMEM((PEER_HEADS, PEER_NKEYS // PEER_SUB, PEER_SUB, tt), f32),
                          pltpu.VMEM((PEER_HEADS, PEER_NKEYS, tt), f32)]
                       + [pltpu.VMEM((PEER_VROWS, LANES), f32)] * 2
                       + [pltpu.VMEM((PEER_EBLK // PEER_NKEYS, PEER_HEADS, PEER_SUB, tt), f32)] * 2,
        compiler_params=_params(("parallel", "arbitrary")),
        name="peer",
    )(x, q, keys[0], keys[1], u.astype(bf16), v.T.astype(bf16), g.reshape(1, d), b.reshape(1, d))


def kernel(x_prompt, x_sample, cache_k, cache_v, state_conv, state_hgrn, page_table, w_in_even, conv_w, conv_b, conv_ln_g, conv_ln_b, w_out_even, w_in_odd, hgrn_lb_logits, hgrn_norm_g, w_out_odd, ln_g, ln_b, peer_wq, peer_keys, peer_u, peer_v):
    batch, seq, d = x_prompt.shape
    db = x_sample.shape[0]
    n_prompt = batch * seq
    rows = n_prompt + TOKEN_TILE
    srows = slice(n_prompt, n_prompt + db)
    pad_rows = lambda a: jnp.concatenate([a, jnp.zeros((rows - a.shape[0], a.shape[1]), a.dtype)], axis=0)

    lb_p = jax.nn.softmax(hgrn_lb_logits.astype(f32), axis=0)
    lb_all = jnp.cumsum(lb_p, axis=0) - lb_p[0]

    x = pad_rows(jnp.concatenate([x_prompt.reshape(n_prompt, d), x_sample.reshape(db, d)], axis=0))
    kp_l, vp_l, ks_l, vs_l, cp_l, cs_l, hp_l, hs_l = [], [], [], [], [], [], [], []
    for l in range(DEPTH):
        if l % 2 == 0:
            e = l // 2
            w_in = w_in_even[e]
            qk = _mm(x, w_in[:, :2 * MOBA_WIDTH], split=True)
            vag = _mm(x, w_in[:, 2 * MOBA_WIDTH:])
            k_all, v_all = qk[:, MOBA_WIDTH:], vag[:, :MOBA_WIDTH]
            kp_l.append(k_all[:n_prompt].reshape(batch, seq // PAGE_SIZE, PAGE_SIZE, MOBA_HEADS, MOBA_HEAD_DIM))
            vp_l.append(v_all[:n_prompt].reshape(batch, seq // PAGE_SIZE, PAGE_SIZE, MOBA_HEADS, MOBA_HEAD_DIM))
            ks_l.append(k_all[srows].reshape(db, 1, MOBA_HEADS, MOBA_HEAD_DIM))
            vs_l.append(v_all[srows].reshape(db, 1, MOBA_HEADS, MOBA_HEAD_DIM))
            att_p = _moba_prompt(qk, vag, batch, seq)
            n_pool = cache_k.shape[1]
            att_s = _moba_sample(qk[srows, :MOBA_WIDTH], k_all[srows], v_all[srows],
                                 cache_k[e].reshape(n_pool, PAGE_SIZE, MOBA_WIDTH),
                                 cache_v[e].reshape(n_pool, PAGE_SIZE, MOBA_WIDTH), page_table)
            cprm = (conv_w[e], conv_b[e], conv_ln_g[e], conv_ln_b[e])
            cy_p, buf_p = _conv_prompt(vag, batch, seq, *cprm)
            cy_s, buf_s = _conv_sample(vag[srows, MOBA_WIDTH:MOBA_WIDTH + CONV_CH],
                                       vag[srows, MOBA_WIDTH + CONV_CH:], state_conv[e], *cprm)
            cp_l.append(buf_p)
            cs_l.append(buf_s)
            mix = pad_rows(jnp.concatenate([jnp.concatenate([att_p, cy_p], axis=1),
                                            jnp.concatenate([att_s, cy_s], axis=1)], axis=0))
            w_out = w_out_even[e]
        else:
            oi = l // 2
            hproj = _mm(x, w_in_odd[oi], tm=256)
            o_p, st_p = _hgrn_prompt(hproj, batch, seq, lb_all[l], hgrn_norm_g[oi])
            o_s, st_s = _hgrn_sample(hproj[srows], state_hgrn[oi].astype(f32), lb_all[l], hgrn_norm_g[oi])
            hp_l.append(st_p)
            hs_l.append(st_s)
            mix = pad_rows(jnp.concatenate([o_p, o_s], axis=0))
            w_out = w_out_odd[oi]
        x = _mm_res_ln(mix, w_out, x, ln_g[l, 0], ln_b[l, 0])
        x = _peer_ln(x, peer_wq[l], peer_keys[l], peer_u[l], peer_v[l], ln_g[l, 1], ln_b[l, 1])
    y_prompt = x[:n_prompt].reshape(batch, seq, d)
    y_sample = x[srows].reshape(db, 1, d)
    return (y_prompt, y_sample, jnp.stack(kp_l), jnp.stack(vp_l), jnp.stack(ks_l), jnp.stack(vs_l),
            jnp.stack(cp_l), jnp.stack(cs_l), jnp.stack(hp_l), jnp.stack(hs_l))
```

```python
import functools
import math

import jax
import jax.numpy as jnp
from jax import lax
from jax.experimental import pallas as pl
from jax.experimental.pallas import tpu as pltpu

f32 = jnp.float32
bf16 = jnp.bfloat16
i32 = jnp.int32
HI = lax.Precision.HIGHEST

D_MODEL = 1024
DEPTH = 2
PAGE_SIZE = 128
MOBA_HEADS = 8
MOBA_HEAD_DIM = 64
MOBA_WIDTH = MOBA_HEADS * MOBA_HEAD_DIM
MOBA_BLOCK = 256
MOBA_TOPK = 3
MOBA_SCALE = MOBA_HEAD_DIM ** -0.5
CONV_CH = D_MODEL // 2
CONV_WIDTH = 31
HGRN_HEADS = 8
HGRN_DK = D_MODEL // HGRN_HEADS
HGRN_CHUNK = 64
HGRN_SUB = 16
PEER_HEADS = 8
PEER_NKEYS = 128
PEER_DHALF = 128
PEER_TOPK = 16
LN_EPS = 1e-5
RMS_EPS = 1e-6
DN_ALPHA = (2 * DEPTH) ** 0.25

LANES = 128
TOKEN_TILE = 512
MASK_NEG = -1e30
FLOOR = -3e38
EXP_CLAMP = 60.0
VMEM_LIMIT = 56 << 20

NT_DIMS = (((1,), (1,)), ((), ()))
TN_DIMS = (((0,), (0,)), ((), ()))


def _params(semantics, vmem=VMEM_LIMIT):
    return pltpu.CompilerParams(dimension_semantics=semantics, vmem_limit_bytes=vmem)


def _layer_norm(y, g, b):
    mu = jnp.mean(y, axis=-1, keepdims=True)
    d = y - mu
    var = jnp.mean(d * d, axis=-1, keepdims=True)
    return d * lax.rsqrt(var + LN_EPS) * g + b


def _sigmoid(x):
    return 1.0 / (1.0 + jnp.exp(-x))


def _mm_kernel(x_ref, *refs, n_chunk, split):
    if split:
        wh_ref, wl_ref, o_ref = refs
    else:
        wh_ref, o_ref = refs
    x = x_ref[...]
    xh = x.astype(bf16)
    if split:
        xl = (x - xh.astype(f32)).astype(bf16)
    for j in range(0, o_ref.shape[1], n_chunk):
        wh = wh_ref[:, j:j + n_chunk]
        acc = jnp.dot(xh, wh, preferred_element_type=f32)
        if split:
            acc = acc + jnp.dot(xl, wh, preferred_element_type=f32)
            acc = acc + jnp.dot(xh, wl_ref[:, j:j + n_chunk], preferred_element_type=f32)
        o_ref[:, j:j + n_chunk] = acc


def _mm(x, w, *, split=False, tm=TOKEN_TILE):
    m, k = x.shape
    n = w.shape[1]
    wh = w.astype(bf16)
    ws = [wh] + ([(w - wh.astype(f32)).astype(bf16)] if split else [])
    return pl.pallas_call(
        functools.partial(_mm_kernel, n_chunk=512, split=split),
        grid=(m // tm,),
        in_specs=[pl.BlockSpec((tm, k), lambda i: (i, 0))] + [pl.BlockSpec((k, n), lambda i: (0, 0))] * len(ws),
        out_specs=pl.BlockSpec((tm, n), lambda i: (i, 0)),
        out_shape=jax.ShapeDtypeStruct((m, n), f32),
        compiler_params=_params(("parallel",)),
        name="proj_split" if split else "proj",
    )(x, *ws)


def _mm_res_ln_kernel(a_ref, w_ref, x_ref, g_ref, b_ref, o_ref):
    acc = jnp.dot(a_ref[...].astype(bf16), w_ref[...], preferred_element_type=f32)
    o_ref[...] = _layer_norm(DN_ALPHA * x_ref[...] + acc, g_ref[...], b_ref[...])


def _mm_res_ln(a, w, x, g, b, *, tm=TOKEN_TILE):
    m, k = a.shape
    n = w.shape[1]
    return pl.pallas_call(
        _mm_res_ln_kernel,
        grid=(m // tm,),
        in_specs=[pl.BlockSpec((tm, k), lambda i: (i, 0)), pl.BlockSpec((k, n), lambda i: (0, 0)),
                  pl.BlockSpec((tm, n), lambda i: (i, 0)), pl.BlockSpec((1, n), lambda i: (0, 0)),
                  pl.BlockSpec((1, n), lambda i: (0, 0))],
        out_specs=pl.BlockSpec((tm, n), lambda i: (i, 0)),
        out_shape=jax.ShapeDtypeStruct((m, n), f32),
        compiler_params=_params(("parallel",)),
        name="out_proj_ln",
    )(a, w.astype(bf16), x, g.reshape(1, n), b.reshape(1, n))


CONV_TILE = 256
CONV_ROWS = 64
CONV_HIST = 32


def _conv_prompt_kernel(a_ref, gt_ref, w_ref, cb_ref, g_ref, b_ref, y_ref, st_ref, buf_ref):
    t = pl.program_id(1)

    @pl.when(t == 0)
    def _():
        buf_ref[0:CONV_HIST, :] = jnp.zeros((CONV_HIST, CONV_CH), f32)

    buf_ref[CONV_HIST:CONV_HIST + CONV_TILE, :] = a_ref[...] * _sigmoid(gt_ref[...])
    first = CONV_HIST - (CONV_WIDTH - 1)
    for r in range(0, CONV_TILE, CONV_ROWS):
        acc = jnp.zeros((CONV_ROWS, CONV_CH), f32) + cb_ref[...]
        for j in range(CONV_WIDTH):
            acc = acc + w_ref[j:j + 1, :] * buf_ref[r + first + j:r + first + j + CONV_ROWS, :]
        y = _layer_norm(acc, g_ref[...], b_ref[...])
        y_ref[r:r + CONV_ROWS, :] = y * _sigmoid(y)
    tail = buf_ref[CONV_TILE:CONV_TILE + CONV_HIST, :]
    buf_ref[0:CONV_HIST, :] = tail

    @pl.when(t == pl.num_programs(1) - 1)
    def _():
        st_ref[0] = tail


def _conv_prompt(vag, batch, seq, w, cb, g, b):
    nt = seq // CONV_TILE
    wpad = jnp.zeros((CONV_HIST, CONV_CH), f32).at[:CONV_WIDTH].set(w)
    row = lambda v: v.reshape(1, CONV_CH)
    y, st = pl.pallas_call(
        _conv_prompt_kernel,
        grid=(batch, nt),
        in_specs=[pl.BlockSpec((CONV_TILE, CONV_CH), lambda bi, t: (bi * nt + t, 1)),
                  pl.BlockSpec((CONV_TILE, CONV_CH), lambda bi, t: (bi * nt + t, 2)),
                  pl.BlockSpec((CONV_HIST, CONV_CH), lambda bi, t: (0, 0))]
                 + [pl.BlockSpec((1, CONV_CH), lambda bi, t: (0, 0))] * 3,
        out_specs=[pl.BlockSpec((CONV_TILE, CONV_CH), lambda bi, t: (bi * nt + t, 0)),
                   pl.BlockSpec((1, CONV_HIST, CONV_CH), lambda bi, t: (bi, 0, 0))],
        out_shape=[jax.ShapeDtypeStruct((batch * seq, CONV_CH), f32),
                   jax.ShapeDtypeStruct((batch, CONV_HIST, CONV_CH), f32)],
        scratch_shapes=[pltpu.VMEM((CONV_TILE + CONV_HIST, CONV_CH), f32)],
        compiler_params=_params(("arbitrary", "arbitrary")),
        name="conv_prompt",
    )(vag, vag, wpad, row(cb), row(g), row(b))
    return y, st[:, CONV_HIST - (CONV_WIDTH - 1):]


def _conv_sample_kernel(a_ref, gt_ref, st_ref, w_ref, cb_ref, g_ref, b_ref, y_ref, nst_ref):
    u = a_ref[...] * _sigmoid(gt_ref[...])
    acc = cb_ref[...] + w_ref[CONV_WIDTH - 1:CONV_WIDTH, :] * u
    for j in range(CONV_WIDTH - 1):
        acc = acc + w_ref[j:j + 1, :] * st_ref[j]
    y = _layer_norm(acc, g_ref[...], b_ref[...])
    y_ref[...] = y * _sigmoid(y)
    for j in range(CONV_WIDTH - 2):
        nst_ref[j] = st_ref[j + 1]
    nst_ref[CONV_WIDTH - 2] = u


def _conv_sample(a, gt, state, w, cb, g, b):
    db = a.shape[0]
    wpad = jnp.zeros((CONV_HIST, CONV_CH), f32).at[:CONV_WIDTH].set(w)
    row = lambda v: v.reshape(1, CONV_CH)
    y, nst = pl.pallas_call(
        _conv_sample_kernel,
        out_shape=[jax.ShapeDtypeStruct((db, CONV_CH), f32),
                   jax.ShapeDtypeStruct((CONV_WIDTH - 1, db, CONV_CH), f32)],
        name="conv_sample",
    )(a, gt, state.transpose(1, 0, 2), wpad, row(cb), row(g), row(b))
    return y, nst.transpose(1, 0, 2)


GATE_ROWS = 1024
ATTN_KEYS = 2 * MOBA_BLOCK


def _block_sum_kernel(k_ref, o_ref):
    o_ref[0] = jnp.sum(k_ref[...], axis=0, keepdims=True)


def _block_sums(qk, batch, seq):
    nb = seq // MOBA_BLOCK
    return pl.pallas_call(
        _block_sum_kernel,
        grid=(batch * nb,),
        in_specs=[pl.BlockSpec((MOBA_BLOCK, MOBA_WIDTH), lambda i: (i, 1))],
        out_specs=pl.BlockSpec((1, 1, MOBA_WIDTH), lambda i: (i, 0, 0)),
        out_shape=jax.ShapeDtypeStruct((batch * nb, 1, MOBA_WIDTH), f32),
        compiler_params=_params(("parallel",)),
        name="moba_block_sums",
    )(qk)


def _moba_gate_kernel(q_ref, k_ref, v_ref, ks_ref, qa_ref, ka_ref, vb_ref, *, nb):
    t = pl.program_id(2)
    lane = lax.broadcasted_iota(i32, (MOBA_BLOCK, LANES), 1)
    blk = lane - MOBA_HEAD_DIM
    km_lane = lax.broadcasted_iota(i32, (nb, LANES), 1)
    ksum = ks_ref[0] * (1.0 / MOBA_BLOCK)
    pad_top = jnp.zeros((MOBA_HEAD_DIM, LANES), f32)
    pad_bot = jnp.zeros((LANES - MOBA_HEAD_DIM - nb, LANES), f32)
    for c in range(GATE_ROWS // MOBA_BLOCK):
        own = t * (GATE_ROWS // MOBA_BLOCK) + c
        rows = slice(c * MOBA_BLOCK, (c + 1) * MOBA_BLOCK)
        q2 = q_ref[rows, :]
        k2 = k_ref[rows, :]
        vb_ref[rows, :] = v_ref[rows, :].astype(bf16)
        cand = (blk >= 0) & (blk < own)
        for j in range(2):
            head = (km_lane >= j * MOBA_HEAD_DIM) & (km_lane < (j + 1) * MOBA_HEAD_DIM)
            km = jnp.concatenate([pad_top, jnp.where(head, ksum, 0.0), pad_bot], axis=0)
            gate = lax.dot_general(q2, km, NT_DIMS, precision=HI, preferred_element_type=f32)
            masked = jnp.where(cand, gate, FLOOR)
            sel = jnp.zeros((MOBA_BLOCK, LANES), f32)
            for _ in range(MOBA_TOPK):
                mx = jnp.max(masked, axis=1, keepdims=True)
                hit = (masked == mx) & cand
                sel = jnp.where(hit, 1.0, sel)
                masked = jnp.where(hit, FLOOR, masked)
            bias = jnp.where((sel > 0.0) | (blk == own), 0.0, MASK_NEG)
            qj = q2 if j == 0 else pltpu.roll(q2, MOBA_HEAD_DIM, 1)
            kj = k2 if j == 0 else pltpu.roll(k2, MOBA_HEAD_DIM, 1)
            in_head = lane < MOBA_HEAD_DIM
            in_bias = lane < MOBA_HEAD_DIM + nb
            qa = jnp.where(in_head, qj * MOBA_SCALE, jnp.where(in_bias, bias, 0.0))
            ka = jnp.where(in_head, kj, jnp.where(blk == own, 1.0, 0.0))
            qa_ref[0, j, rows, :] = qa.astype(bf16)
            ka_ref[0, j, rows, :] = ka.astype(bf16)


def _moba_attn_kernel(qa_ref, ka_ref, vb_ref, o_ref):
    i = pl.program_id(2)
    last = i // 2
    row = lax.broadcasted_iota(i32, (MOBA_BLOCK, ATTN_KEYS), 0)
    col = lax.broadcasted_iota(i32, (MOBA_BLOCK, ATTN_KEYS), 1)
    lane = lax.broadcasted_iota(i32, (MOBA_BLOCK, LANES), 1)

    def tile(j, k0, mask):
        s = lax.dot_general(qa_ref[0, j], ka_ref[0, j, pl.ds(k0, ATTN_KEYS), :], NT_DIMS, preferred_element_type=f32)
        return s if mask is None else jnp.where(mask, s, MASK_NEG)

    k_last = pl.multiple_of(last * ATTN_KEYS, ATTN_KEYS)
    visible = (k_last + col) <= (i * MOBA_BLOCK + row)
    v_last = vb_ref[pl.ds(k_last, ATTN_KEYS), :]
    state = []
    for j in range(2):
        s = tile(j, k_last, visible)
        m = jnp.max(s, axis=1, keepdims=True)
        p = jnp.exp(s - m)
        state += [m, jnp.sum(p, axis=1, keepdims=True), jnp.dot(p.astype(bf16), v_last, preferred_element_type=f32)]

    def body(n, carry):
        k0 = pl.multiple_of(n * ATTN_KEYS, ATTN_KEYS)
        vt = vb_ref[pl.ds(k0, ATTN_KEYS), :]
        out = []
        for j in range(2):
            m, l, acc = carry[3 * j:3 * j + 3]
            s = tile(j, k0, None)
            mn = jnp.maximum(m, jnp.max(s, axis=1, keepdims=True))
            alpha = jnp.exp(m - mn)
            p = jnp.exp(s - mn)
            out += [mn, alpha * l + jnp.sum(p, axis=1, keepdims=True),
                    alpha * acc + jnp.dot(p.astype(bf16), vt, preferred_element_type=f32)]
        return tuple(out)

    m0, l0, acc0, m1, l1, acc1 = lax.fori_loop(0, last, body, tuple(state))
    o_ref[...] = jnp.where(lane < MOBA_HEAD_DIM, acc0 / l0, acc1 / l1)


def _moba_prompt(qk, vag, batch, seq):
    nb = seq // MOBA_BLOCK
    npair = MOBA_WIDTH // LANES
    nt = seq // GATE_ROWS
    ksums = _block_sums(qk, batch, seq).reshape(batch, nb, MOBA_WIDTH)
    qa, ka, vb = pl.pallas_call(
        functools.partial(_moba_gate_kernel, nb=nb),
        grid=(batch, npair, nt),
        in_specs=[pl.BlockSpec((GATE_ROWS, LANES), lambda b, p, t: (b * nt + t, p)),
                  pl.BlockSpec((GATE_ROWS, LANES), lambda b, p, t: (b * nt + t, npair + p)),
                  pl.BlockSpec((GATE_ROWS, LANES), lambda b, p, t: (b * nt + t, p)),
                  pl.BlockSpec((1, nb, LANES), lambda b, p, t: (b, 0, p))],
        out_specs=[pl.BlockSpec((1, 2, GATE_ROWS, LANES), lambda b, p, t: (b, p, t, 0)),
                   pl.BlockSpec((1, 2, GATE_ROWS, LANES), lambda b, p, t: (b, p, t, 0)),
                   pl.BlockSpec((GATE_ROWS, LANES), lambda b, p, t: (b * nt + t, p))],
        out_shape=[jax.ShapeDtypeStruct((batch, MOBA_HEADS, seq, LANES), bf16),
                   jax.ShapeDtypeStruct((batch, MOBA_HEADS, seq, LANES), bf16),
                   jax.ShapeDtypeStruct((batch * seq, MOBA_WIDTH), bf16)],
        compiler_params=_params(("parallel", "parallel", "parallel")),
        name="moba_gate",
    )(qk, qk, vag, ksums)
    return pl.pallas_call(
        _moba_attn_kernel,
        grid=(batch, npair, nb),
        in_specs=[pl.BlockSpec((1, 2, MOBA_BLOCK, LANES), lambda b, p, i: (b, p, i, 0)),
                  pl.BlockSpec((1, 2, seq, LANES), lambda b, p, i: (b, p, 0, 0)),
                  pl.BlockSpec((seq, LANES), lambda b, p, i: (b, p))],
        out_specs=pl.BlockSpec((MOBA_BLOCK, LANES), lambda b, p, i: (b * nb + i, p)),
        out_shape=jax.ShapeDtypeStruct((batch * seq, MOBA_WIDTH), f32),
        compiler_params=_params(("parallel", "parallel", "arbitrary")),
        name="moba_attn",
    )(qa, ka, vb)


PAGES_PER_STEP = 8
PAGES_PER_BLOCK = MOBA_BLOCK // PAGE_SIZE
BLOCKS_PER_STEP = PAGES_PER_STEP // PAGES_PER_BLOCK


def _head_expand(dtype):
    r = lax.broadcasted_iota(i32, (LANES, MOBA_WIDTH), 0)
    c = lax.broadcasted_iota(i32, (LANES, MOBA_WIDTH), 1)
    return jnp.where(c // MOBA_HEAD_DIM == r, 1.0, 0.0).astype(dtype)


def _head_reduce():
    r = lax.broadcasted_iota(i32, (MOBA_WIDTH, LANES), 0)
    c = lax.broadcasted_iota(i32, (MOBA_WIDTH, LANES), 1)
    return jnp.where(r // MOBA_HEAD_DIM == c, 1.0, 0.0).astype(f32)


def _moba_sample_stream_kernel(pt_ref, qm_ref, *refs):
    del pt_ref
    k_refs = refs[:PAGES_PER_STEP]
    v_refs = refs[PAGES_PER_STEP:2 * PAGES_PER_STEP]
    ks_ref, m_ref, l_ref, acc_ref = refs[2 * PAGES_PER_STEP:]
    qm = qm_ref[0]
    expand = _head_expand(bf16)
    for jj in range(BLOCKS_PER_STEP):
        kb = jnp.concatenate([k_refs[PAGES_PER_BLOCK * jj + x][0] for x in range(PAGES_PER_BLOCK)], axis=0)
        vb = jnp.concatenate([v_refs[PAGES_PER_BLOCK * jj + x][0] for x in range(PAGES_PER_BLOCK)], axis=0)
        s = jnp.dot(kb.astype(bf16), qm, preferred_element_type=f32)
        m = jnp.max(s, axis=0, keepdims=True)
        p = jnp.exp(s - m)
        pe = jnp.dot(p.astype(bf16), expand, preferred_element_type=f32)
        ks_ref[0, 0, jj:jj + 1, :] = jnp.sum(kb, axis=0, keepdims=True)
        m_ref[0, 0, jj:jj + 1, :] = m
        l_ref[0, 0, jj:jj + 1, :] = jnp.sum(p, axis=0, keepdims=True)
        acc_ref[0, 0, jj:jj + 1, :] = jnp.sum(pe * vb, axis=0, keepdims=True)


def _moba_sample_combine_kernel(ks_ref, m_ref, l_ref, acc_ref, q_ref, kn_ref, vn_ref, o_ref):
    db, nb, _ = ks_ref.shape
    reduce_h = _head_reduce()
    expand = _head_expand(f32)
    q = q_ref[...]
    prod = (ks_ref[...] * q[:, None, :]).reshape(db * nb, MOBA_WIDTH)
    gate = jnp.dot(prod, reduce_h, precision=HI, preferred_element_type=f32).reshape(db, nb, LANES)
    masked = gate
    sel = jnp.zeros((db, nb, LANES), f32)
    for _ in range(MOBA_TOPK):
        mx = jnp.max(masked, axis=1, keepdims=True)
        hit = masked == mx
        sel = jnp.where(hit, 1.0, sel)
        masked = jnp.where(hit, FLOOR, masked)
    chosen = sel > 0.0
    s_self = jnp.dot(q * kn_ref[...], reduce_h, precision=HI, preferred_element_type=f32) * MOBA_SCALE
    m = m_ref[...]
    top = jnp.maximum(jnp.max(jnp.where(chosen, m, FLOOR), axis=1), s_self)
    w = jnp.where(chosen, jnp.exp(m - top[:, None, :]), 0.0)
    w_self = jnp.exp(s_self - top)
    denom = jnp.sum(w * l_ref[...], axis=1) + w_self
    we = jnp.dot(w.reshape(db * nb, LANES), expand, precision=HI, preferred_element_type=f32)
    num = jnp.sum(we.reshape(db, nb, MOBA_WIDTH) * acc_ref[...], axis=1)
    num = num + jnp.dot(w_self, expand, precision=HI, preferred_element_type=f32) * vn_ref[...]
    o_ref[...] = num / jnp.dot(denom, expand, precision=HI, preferred_element_type=f32)


def _moba_sample(q, kn, vn, k_pool, v_pool, page_table):
    db, n_pages = page_table.shape
    nb = n_pages // PAGES_PER_BLOCK
    ng = n_pages // PAGES_PER_STEP
    head_of = jnp.arange(MOBA_WIDTH) // MOBA_HEAD_DIM
    qm = jnp.where(head_of[:, None] == jnp.arange(LANES)[None, :], (q * MOBA_SCALE)[:, :, None], 0.0).astype(bf16)
    page_spec = lambda x: pl.BlockSpec((1, PAGE_SIZE, MOBA_WIDTH),
                                       lambda b, g, pt, x=x: (pt[b, g * PAGES_PER_STEP + x], 0, 0))
    stat = lambda w: pl.BlockSpec((1, 1, BLOCKS_PER_STEP, w), lambda b, g, pt: (b, g, 0, 0))
    stat_shape = lambda w: jax.ShapeDtypeStruct((db, ng, BLOCKS_PER_STEP, w), f32)
    ks, m, l, acc = pl.pallas_call(
        _moba_sample_stream_kernel,
        grid_spec=pltpu.PrefetchScalarGridSpec(
            num_scalar_prefetch=1,
            grid=(db, ng),
            in_specs=[pl.BlockSpec((1, MOBA_WIDTH, LANES), lambda b, g, pt: (b, 0, 0))]
                     + [page_spec(x) for x in range(PAGES_PER_STEP)] * 2,
            out_specs=[stat(MOBA_WIDTH), stat(LANES), stat(LANES), stat(MOBA_WIDTH)]),
        out_shape=[stat_shape(MOBA_WIDTH), stat_shape(LANES), stat_shape(LANES), stat_shape(MOBA_WIDTH)],
        compiler_params=_params(("parallel", "parallel")),
        name="moba_sample_stream",
    )(page_table, qm, *([k_pool] * PAGES_PER_STEP), *([v_pool] * PAGES_PER_STEP))
    ks, m, l, acc = (a.reshape(db, nb, a.shape[-1]) for a in (ks, m, l, acc))
    return pl.pallas_call(
        _moba_sample_combine_kernel,
        out_shape=jax.ShapeDtypeStruct((db, MOBA_WIDTH), f32),
        compiler_params=pltpu.CompilerParams(vmem_limit_bytes=VMEM_LIMIT),
        name="moba_sample_combine",
    )(ks, m, l, acc, q, kn, vn)


HGRN_TILE = 512


def _hgrn_prompt_kernel(q_ref, fz_ref, i_ref, g_ref, lb_ref, ng_ref, o_ref, st_ref, state_ref):
    t = pl.program_id(1)
    c_rows = HGRN_CHUNK

    @pl.when(t == 0)
    def _():
        state_ref[...] = jnp.zeros(state_ref.shape, f32)

    row = lax.broadcasted_iota(i32, (c_rows, c_rows), 0)
    col = lax.broadcasted_iota(i32, (c_rows, c_rows), 1)
    causal = col <= row
    cum = jnp.where(causal, 1.0, 0.0)
    cum_sub = jnp.where(col < (row // HGRN_SUB) * HGRN_SUB, 1.0, 0.0)
    cum_both = jnp.concatenate([cum, cum_sub], axis=0)

    for h in range(HGRN_HEADS):
        lanes = slice(h * HGRN_DK, (h + 1) * HGRN_DK)
        lb = lb_ref[:, lanes]
        ng = ng_ref[:, lanes]

        def chunk(c, carry, lanes=lanes, lb=lb, ng=ng, h=h):
            rows = pl.ds(pl.multiple_of(c * c_rows, c_rows), c_rows)
            q = q_ref[rows, lanes]
            v = i_ref[rows, lanes]
            gg = g_ref[rows, lanes]
            f = lb + (1.0 - lb) * _sigmoid(fz_ref[rows, lanes])
            kk = 1.0 - f
            both = jnp.dot(cum_both, jnp.log(f), precision=HI, preferred_element_type=f32)
            b = both[:c_rows]
            ref_row = both[c_rows:]
            st = state_ref[h]
            o = lax.dot_general((q * jnp.exp(b)).astype(bf16), st.astype(bf16), NT_DIMS, preferred_element_type=f32)
            qh = (q * jnp.exp(b - ref_row)).astype(bf16)
            parts = []
            for s in range(c_rows // HGRN_SUB):
                ref_s = ref_row[s * HGRN_SUB:s * HGRN_SUB + 1, :]
                kh = (kk * jnp.exp(jnp.minimum(ref_s - b, EXP_CLAMP))).astype(bf16)
                parts.append(lax.dot_general(qh[s * HGRN_SUB:(s + 1) * HGRN_SUB], kh, NT_DIMS,
                                             preferred_element_type=f32))
            att = jnp.where(causal, jnp.concatenate(parts, axis=0), 0.0)
            o = o + jnp.dot(att.astype(bf16), v.astype(bf16), preferred_element_type=f32)
            b_last = b[c_rows - 1:c_rows, :]
            kd = (kk * jnp.exp(b_last - b)).astype(bf16)
            state_ref[h] = st * jnp.exp(b_last) + lax.dot_general(v.astype(bf16), kd, TN_DIMS,
                                                                   preferred_element_type=f32)
            ms = jnp.mean(o * o, axis=1, keepdims=True)
            o_ref[rows, lanes] = o * lax.rsqrt(ms + RMS_EPS) * ng * (gg * _sigmoid(gg))
            return carry

        lax.fori_loop(0, HGRN_TILE // c_rows, chunk, 0)

    @pl.when(t == pl.num_programs(1) - 1)
    def _():
        for h in range(HGRN_HEADS):
            st_ref[0, h] = state_ref[h].T


def _hgrn_prompt(hproj, batch, seq, lb, norm_g):
    nt = seq // HGRN_TILE
    width = HGRN_HEADS * HGRN_DK
    col = lambda j: pl.BlockSpec((HGRN_TILE, width), lambda b, t, j=j: (b * nt + t, j))
    vec = pl.BlockSpec((1, width), lambda b, t: (0, 0))
    return pl.pallas_call(
        _hgrn_prompt_kernel,
        grid=(batch, nt),
        in_specs=[col(0), col(1), col(2), col(3), vec, vec],
        out_specs=[pl.BlockSpec((HGRN_TILE, width), lambda b, t: (b * nt + t, 0)),
                   pl.BlockSpec((1, HGRN_HEADS, HGRN_DK, HGRN_DK), lambda b, t: (b, 0, 0, 0))],
        out_shape=[jax.ShapeDtypeStruct((batch * seq, width), f32),
                   jax.ShapeDtypeStruct((batch, HGRN_HEADS, HGRN_DK, HGRN_DK), f32)],
        scratch_shapes=[pltpu.VMEM((HGRN_HEADS, HGRN_DK, HGRN_DK), f32)],
        compiler_params=_params(("parallel", "arbitrary")),
        name="hgrn_prompt",
    )(hproj, hproj, hproj, hproj, lb.reshape(1, width), norm_g.reshape(1, width))


def _hgrn_sample_kernel(qc_ref, fzc_ref, lbc_ref, i_ref, g_ref, ng_ref, st_ref, o_ref, nst_ref):
    for h in range(HGRN_HEADS):
        lb = lbc_ref[h]
        f = lb + (1.0 - lb) * _sigmoid(fzc_ref[0, h])
        st = f * st_ref[0, h] + (1.0 - f) * i_ref[0, h]
        nst_ref[0, h] = st
        o = jnp.sum(qc_ref[0, h] * st, axis=0, keepdims=True)
        ms = jnp.mean(o * o, axis=1, keepdims=True)
        gg = g_ref[0, h]
        o_ref[0, h] = o * lax.rsqrt(ms + RMS_EPS) * ng_ref[h] * (gg * _sigmoid(gg))


def _hgrn_sample(hs, state, lb, norm_g):
    db = hs.shape[0]
    width = HGRN_HEADS * HGRN_DK
    colv = lambda x: x.reshape(db, HGRN_HEADS, HGRN_DK, 1)
    rowv = lambda x: x.reshape(db, HGRN_HEADS, 1, HGRN_DK)
    cspec = pl.BlockSpec((1, HGRN_HEADS, HGRN_DK, 1), lambda b: (b, 0, 0, 0))
    rspec = pl.BlockSpec((1, HGRN_HEADS, 1, HGRN_DK), lambda b: (b, 0, 0, 0))
    sspec = pl.BlockSpec((1, HGRN_HEADS, HGRN_DK, HGRN_DK), lambda b: (b, 0, 0, 0))
    o, nst = pl.pallas_call(
        _hgrn_sample_kernel,
        grid=(db,),
        in_specs=[cspec, cspec, pl.BlockSpec((HGRN_HEADS, HGRN_DK, 1), lambda b: (0, 0, 0)), rspec, rspec,
                  pl.BlockSpec((HGRN_HEADS, 1, HGRN_DK), lambda b: (0, 0, 0)), sspec],
        out_specs=[rspec, sspec],
        out_shape=[jax.ShapeDtypeStruct((db, HGRN_HEADS, 1, HGRN_DK), f32),
                   jax.ShapeDtypeStruct((db, HGRN_HEADS, HGRN_DK, HGRN_DK), f32)],
        compiler_params=_params(("parallel",)),
        name="hgrn_sample",
    )(colv(hs[:, :width]), colv(hs[:, width:2 * width]), lb.reshape(HGRN_HEADS, HGRN_DK, 1),
      rowv(hs[:, 2 * width:3 * width]), rowv(hs[:, 3 * width:]), norm_g.reshape(HGRN_HEADS, 1, HGRN_DK), state)
    return o.reshape(db, width), nst


PEER_EBLK = 1024
PEER_RANKS = PEER_TOPK + 1
PEER_VROWS = 24
PEER_SUB = 8
PEER_ROWS = 16


def _extract_max(tiles):
    mx = functools.reduce(jnp.maximum, tiles)
    mx = jnp.max(mx, axis=0, keepdims=True)
    return mx, [jnp.where(x == mx, FLOOR, x) for x in tiles]


def _peer_kernel(x_ref, q_ref, k1_ref, k2_ref, u_ref, v_ref, g_ref, b_ref, o_ref,
                 xt_ref, acc_ref, act_ref, p_ref, a_ref, bw_ref, c_ref, s2_ref, v1_ref, v2_ref, ab_ref, cb_ref):
    e = pl.program_id(1)
    tt = x_ref.shape[0]
    n_lane_chunks = tt // LANES
    sub = PEER_SUB
    n_sub = PEER_NKEYS // sub
    keys_per_step = PEER_EBLK // PEER_NKEYS

    @pl.when(e == 0)
    def _():
        xt_ref[...] = x_ref[...].T.astype(bf16)
        acc_ref[...] = jnp.zeros(acc_ref.shape, f32)
        floor_rows = jnp.full((PEER_VROWS, LANES), FLOOR, f32)
        for h in range(PEER_HEADS):
            q1 = q_ref[:, (2 * h) * PEER_DHALF:(2 * h + 1) * PEER_DHALF]
            q2 = q_ref[:, (2 * h + 1) * PEER_DHALF:(2 * h + 2) * PEER_DHALF]
            s1 = lax.dot_general(k1_ref[h], q1, NT_DIMS, precision=HI, preferred_element_type=f32)
            c_ref[h] = s1.reshape(n_sub, sub, tt)
            s2_ref[h] = lax.dot_general(k2_ref[h], q2, NT_DIMS, precision=HI, preferred_element_type=f32)

        def per_head(h, carry):
            for lc in range(n_lane_chunks):
                lanes = slice(lc * LANES, (lc + 1) * LANES)
                for side, v_ref in enumerate((v1_ref, v2_ref)):
                    v_ref[...] = floor_rows
                    if side == 0:
                        tiles = [c_ref[h, k, :, lanes] for k in range(n_sub)]
                    else:
                        tiles = [s2_ref[h, k * sub:(k + 1) * sub, lanes] for k in range(n_sub)]
                    for r in range(PEER_RANKS):
                        mx, tiles = _extract_max(tiles)
                        v_ref[r:r + 1, :] = mx
                cands = [v1_ref[0:1, :] + v2_ref[r:r + sub, :] for r in range(0, PEER_VROWS, sub)]
                cands += [v1_ref[r:r + 1, :] + v2_ref[0:sub, :] for r in range(1, sub)]
                cands += [v1_ref[r:r + sub, :] + v2_ref[0:1, :] for r in range(sub, PEER_VROWS, sub)]
                best = v1_ref[0:1, :] + v2_ref[0:1, :]
                zsum = jnp.zeros((1, LANES), f32)
                kth = best
                for r in range(PEER_TOPK):
                    kth, cands = _extract_max(cands)
                    zsum = zsum + jnp.exp(kth - best)
                nxt, _ = _extract_max(cands)
                thresh = 0.5 * (kth + nxt)
                s1 = c_ref[h, :, :, lanes]
                a_ref[h, :, :, lanes] = jnp.exp(s1 - v1_ref[0:1, :]) / zsum
                bw_ref[h, :, lanes] = jnp.exp(s2_ref[h, :, lanes] - v2_ref[0:1, :])
                c_ref[h, :, :, lanes] = thresh - s1
            return carry

        lax.fori_loop(0, PEER_HEADS, per_head, 0)

    act_ref[...] = jnp.dot(u_ref[...], xt_ref[...], preferred_element_type=f32)
    for h in range(PEER_HEADS):
        for ii in range(keys_per_step):
            ab_ref[ii, h] = jnp.broadcast_to(a_ref[h, e, ii:ii + 1, :], (sub, tt))
            cb_ref[ii, h] = jnp.broadcast_to(c_ref[h, e, ii:ii + 1, :], (sub, tt))

    def per_second_keys(k, carry):
        r0 = pl.multiple_of(k * PEER_ROWS, PEER_ROWS)
        for lc in range(n_lane_chunks):
            lanes = slice(lc * LANES, (lc + 1) * LANES)
            w = [jnp.zeros((PEER_ROWS // sub, sub, LANES), f32) for _ in range(keys_per_step)]
            for h in range(PEER_HEADS):
                s2 = s2_ref[h, pl.ds(r0, PEER_ROWS), lanes].reshape(PEER_ROWS // sub, sub, LANES)
                bw = bw_ref[h, pl.ds(r0, PEER_ROWS), lanes].reshape(PEER_ROWS // sub, sub, LANES)
                for ii in range(keys_per_step):
                    hit = s2 >= cb_ref[ii, h, :, lanes][None]
                    w[ii] = w[ii] + ab_ref[ii, h, :, lanes][None] * jnp.where(hit, bw, 0.0)
            for ii in range(keys_per_step):
                rows = pl.ds(pl.multiple_of(ii * PEER_NKEYS + r0, PEER_ROWS), PEER_ROWS)
                act = act_ref[rows, lanes]
                gelu = 0.5 * act * (1.0 + lax.erf(act * (2.0 ** -0.5)))
                p_ref[rows, lanes] = (w[ii].reshape(PEER_ROWS, LANES) * gelu).astype(bf16)
        return carry

    lax.fori_loop(0, PEER_NKEYS // PEER_ROWS, per_second_keys, 0)
    acc_ref[...] += lax.dot_general(p_ref[...], v_ref[...], TN_DIMS, preferred_element_type=f32)

    @pl.when(e == pl.num_programs(1) - 1)
    def _():
        o_ref[...] = _layer_norm(DN_ALPHA * x_ref[...] + acc_ref[...], g_ref[...], b_ref[...])


def _peer_ln(x, wq, keys, u, v, g, b, *, tt=TOKEN_TILE):
    m, d = x.shape
    n_exp = u.shape[0]
    q = _mm(x, wq, split=True)
    nq = q.shape[1]
    return pl.pallas_call(
        _peer_kernel,
        grid=(m // tt, n_exp // PEER_EBLK),
        in_specs=[pl.BlockSpec((tt, d), lambda t, e: (t, 0)),
                  pl.BlockSpec((tt, nq), lambda t, e: (t, 0)),
                  pl.BlockSpec((PEER_HEADS, PEER_NKEYS, PEER_DHALF), lambda t, e: (0, 0, 0)),
                  pl.BlockSpec((PEER_HEADS, PEER_NKEYS, PEER_DHALF), lambda t, e: (0, 0, 0)),
                  pl.BlockSpec((PEER_EBLK, d), lambda t, e: (e, 0)),
                  pl.BlockSpec((PEER_EBLK, d), lambda t, e: (e, 0)),
                  pl.BlockSpec((1, d), lambda t, e: (0, 0)),
                  pl.BlockSpec((1, d), lambda t, e: (0, 0))],
        out_specs=pl.BlockSpec((tt, d), lambda t, e: (t, 0)),
        out_shape=jax.ShapeDtypeStruct((m, d), f32),
        scratch_shapes=[pltpu.VMEM((d, tt), bf16), pltpu.VMEM((tt, d), f32),
                        pltpu.VMEM((PEER_EBLK, tt), f32), pltpu.VMEM((PEER_EBLK, tt), bf16)]
                       + [pltpu.VMEM((PEER_HEADS, PEER_NKEYS // PEER_SUB, PEER_SUB, tt), f32),
                          pltpu.VMEM((PEER_HEADS, PEER_NKEYS, tt), f32),
                          pltpu.VMEM((PEER_HEADS, PEER_NKEYS // PEER_SUB, PEER_SUB, tt), f32),
                          pltpu.VMEM((PEER_HEADS, PEER_NKEYS, tt), f32)]
                       + [pltpu.VMEM((PEER_VROWS, LANES), f32)] * 2
                       + [pltpu.VMEM((PEER_EBLK // PEER_NKEYS, PEER_HEADS, PEER_SUB, tt), f32)] * 2,
        compiler_params=_params(("parallel", "arbitrary")),
        name="peer",
    )(x, q, keys[0], keys[1], u.astype(bf16), v.astype(bf16), g.reshape(1, d), b.reshape(1, d))


def kernel(x_prompt, x_sample, cache_k, cache_v, state_conv, state_hgrn, page_table, w_in_even, conv_w, conv_b, conv_ln_g, conv_ln_b, w_out_even, w_in_odd, hgrn_lb_logits, hgrn_norm_g, w_out_odd, ln_g, ln_b, peer_wq, peer_keys, peer_u, peer_v):
    batch, seq, d = x_prompt.shape
    db = x_sample.shape[0]
    n_prompt = batch * seq
    rows = n_prompt + TOKEN_TILE
    srows = slice(n_prompt, n_prompt + db)
    pad_rows = lambda a: jnp.concatenate([a, jnp.zeros((rows - a.shape[0], a.shape[1]), a.dtype)], axis=0)

    lb_p = jax.nn.softmax(hgrn_lb_logits.astype(f32), axis=0)
    lb_all = jnp.cumsum(lb_p, axis=0) - lb_p[0]

    x = pad_rows(jnp.concatenate([x_prompt.reshape(n_prompt, d), x_sample.reshape(db, d)], axis=0))
    kp_l, vp_l, ks_l, vs_l, cp_l, cs_l, hp_l, hs_l = [], [], [], [], [], [], [], []
    for l in range(DEPTH):
        if l % 2 == 0:
            e = l // 2
            w_in = w_in_even[e]
            qk = _mm(x, w_in[:, :2 * MOBA_WIDTH], split=True)
            vag = _mm(x, w_in[:, 2 * MOBA_WIDTH:])
            k_all, v_all = qk[:, MOBA_WIDTH:], vag[:, :MOBA_WIDTH]
            kp_l.append(k_all[:n_prompt].reshape(batch, seq // PAGE_SIZE, PAGE_SIZE, MOBA_HEADS, MOBA_HEAD_DIM))
            vp_l.append(v_all[:n_prompt].reshape(batch, seq // PAGE_SIZE, PAGE_SIZE, MOBA_HEADS, MOBA_HEAD_DIM))
            ks_l.append(k_all[srows].reshape(db, 1, MOBA_HEADS, MOBA_HEAD_DIM))
            vs_l.append(v_all[srows].reshape(db, 1, MOBA_HEADS, MOBA_HEAD_DIM))
            att_p = _moba_prompt(qk, vag, batch, seq)
            n_pool = cache_k.shape[1]
            att_s = _moba_sample(qk[srows, :MOBA_WIDTH], k_all[srows], v_all[srows],
                                 cache_k[e].reshape(n_pool, PAGE_SIZE, MOBA_WIDTH),
                                 cache_v[e].reshape(n_pool, PAGE_SIZE, MOBA_WIDTH), page_table)
            cprm = (conv_w[e], conv_b[e], conv_ln_g[e], conv_ln_b[e])
            cy_p, buf_p = _conv_prompt(vag, batch, seq, *cprm)
            cy_s, buf_s = _conv_sample(vag[srows, MOBA_WIDTH:MOBA_WIDTH + CONV_CH],
                                       vag[srows, MOBA_WIDTH + CONV_CH:], state_conv[e], *cprm)
            cp_l.append(buf_p)
            cs_l.append(buf_s)
            mix = pad_rows(jnp.concatenate([jnp.concatenate([att_p, cy_p], axis=1),
                                            jnp.concatenate([att_s, cy_s], axis=1)], axis=0))
            w_out = w_out_even[e]
        else:
            oi = l // 2
            hproj = _mm(x, w_in_odd[oi], tm=256)
            o_p, st_p = _hgrn_prompt(hproj, batch, seq, lb_all[l], hgrn_norm_g[oi])
            o_s, st_s = _hgrn_sample(hproj[srows], state_hgrn[oi].astype(f32), lb_all[l], hgrn_norm_g[oi])
            hp_l.append(st_p)
            hs_l.append(st_s)
            mix = pad_rows(jnp.concatenate([o_p, o_s], axis=0))
            w_out = w_out_odd[oi]
        x = _mm_res_ln(mix, w_out, x, ln_g[l, 0], ln_b[l, 0])
        x = _peer_ln(x, peer_wq[l], peer_keys[l], peer_u[l], peer_v[l], ln_g[l, 1], ln_b[l, 1])
    y_prompt = x[:n_prompt].reshape(batch, seq, d)
    y_sample = x[srows].reshape(db, 1, d)
    return (y_prompt, y_sample, jnp.stack(kp_l), jnp.stack(vp_l), jnp.stack(ks_l), jnp.stack(vs_l),
            jnp.stack(cp_l), jnp.stack(cs_l), jnp.stack(hp_l), jnp.stack(hs_l))
```

```python
import functools
import math

import jax
import jax.numpy as jnp
from jax import lax
from jax.experimental import pallas as pl
from jax.experimental.pallas import tpu as pltpu

f32 = jnp.float32
bf16 = jnp.bfloat16
i32 = jnp.int32
HI = lax.Precision.HIGHEST

D_MODEL = 1024
DEPTH = 2
PAGE_SIZE = 128
MOBA_HEADS = 8
MOBA_HEAD_DIM = 64
MOBA_WIDTH = MOBA_HEADS * MOBA_HEAD_DIM
MOBA_BLOCK = 256
MOBA_TOPK = 3
MOBA_SCALE = MOBA_HEAD_DIM ** -0.5
CONV_CH = D_MODEL // 2
CONV_WIDTH = 31
HGRN_HEADS = 8
HGRN_DK = D_MODEL // HGRN_HEADS
HGRN_CHUNK = 64
HGRN_SUB = 16
PEER_HEADS = 8
PEER_NKEYS = 128
PEER_DHALF = 128
PEER_TOPK = 16
LN_EPS = 1e-5
RMS_EPS = 1e-6
DN_ALPHA = (2 * DEPTH) ** 0.25

LANES = 128
TOKEN_TILE = 512
MASK_NEG = -1e30
FLOOR = -3e38
EXP_CLAMP = 60.0
VMEM_LIMIT = 56 << 20

NT_DIMS = (((1,), (1,)), ((), ()))
TN_DIMS = (((0,), (0,)), ((), ()))


def _params(semantics, vmem=VMEM_LIMIT):
    return pltpu.CompilerParams(dimension_semantics=semantics, vmem_limit_bytes=vmem)


def _layer_norm(y, g, b):
    mu = jnp.mean(y, axis=-1, keepdims=True)
    d = y - mu
    var = jnp.mean(d * d, axis=-1, keepdims=True)
    return d * lax.rsqrt(var + LN_EPS) * g + b


def _sigmoid(x):
    return 1.0 / (1.0 + jnp.exp(-x))


def _mm_kernel(x_ref, *refs, n_chunk, split):
    if split:
        wh_ref, wl_ref, o_ref = refs
    else:
        wh_ref, o_ref = refs
    x = x_ref[...]
    xh = x.astype(bf16)
    if split:
        xl = (x - xh.astype(f32)).astype(bf16)
    for j in range(0, o_ref.shape[1], n_chunk):
        wh = wh_ref[:, j:j + n_chunk]
        acc = jnp.dot(xh, wh, preferred_element_type=f32)
        if split:
            acc = acc + jnp.dot(xl, wh, preferred_element_type=f32)
            acc = acc + jnp.dot(xh, wl_ref[:, j:j + n_chunk], preferred_element_type=f32)
        o_ref[:, j:j + n_chunk] = acc


def _mm(x, w, *, split=False, tm=TOKEN_TILE):
    m, k = x.shape
    n = w.shape[1]
    wh = w.astype(bf16)
    ws = [wh] + ([(w - wh.astype(f32)).astype(bf16)] if split else [])
    return pl.pallas_call(
        functools.partial(_mm_kernel, n_chunk=512, split=split),
        grid=(m // tm,),
        in_specs=[pl.BlockSpec((tm, k), lambda i: (i, 0))] + [pl.BlockSpec((k, n), lambda i: (0, 0))] * len(ws),
        out_specs=pl.BlockSpec((tm, n), lambda i: (i, 0)),
        out_shape=jax.ShapeDtypeStruct((m, n), f32),
        compiler_params=_params(("parallel",)),
        name="proj_split" if split else "proj",
    )(x, *ws)


def _mm_res_ln_kernel(a_ref, w_ref, x_ref, g_ref, b_ref, o_ref):
    acc = jnp.dot(a_ref[...].astype(bf16), w_ref[...], preferred_element_type=f32)
    o_ref[...] = _layer_norm(DN_ALPHA * x_ref[...] + acc, g_ref[...], b_ref[...])


def _mm_res_ln(a, w, x, g, b, *, tm=TOKEN_TILE):
    m, k = a.shape
    n = w.shape[1]
    return pl.pallas_call(
        _mm_res_ln_kernel,
        grid=(m // tm,),
        in_specs=[pl.BlockSpec((tm, k), lambda i: (i, 0)), pl.BlockSpec((k, n), lambda i: (0, 0)),
                  pl.BlockSpec((tm, n), lambda i: (i, 0)), pl.BlockSpec((1, n), lambda i: (0, 0)),
                  pl.BlockSpec((1, n), lambda i: (0, 0))],
        out_specs=pl.BlockSpec((tm, n), lambda i: (i, 0)),
        out_shape=jax.ShapeDtypeStruct((m, n), f32),
        compiler_params=_params(("parallel",)),
        name="out_proj_ln",
    )(a, w.astype(bf16), x, g.reshape(1, n), b.reshape(1, n))


CONV_TILE = 256
CONV_ROWS = 64
CONV_HIST = 32


def _conv_prompt_kernel(a_ref, gt_ref, w_ref, cb_ref, g_ref, b_ref, y_ref, st_ref, buf_ref):
    t = pl.program_id(1)

    @pl.when(t == 0)
    def _():
        buf_ref[0:CONV_HIST, :] = jnp.zeros((CONV_HIST, CONV_CH), f32)

    buf_ref[CONV_HIST:CONV_HIST + CONV_TILE, :] = a_ref[...] * _sigmoid(gt_ref[...])
    first = CONV_HIST - (CONV_WIDTH - 1)
    for r in range(0, CONV_TILE, CONV_ROWS):
        acc = jnp.zeros((CONV_ROWS, CONV_CH), f32) + cb_ref[...]
        for j in range(CONV_WIDTH):
            acc = acc + w_ref[j:j + 1, :] * buf_ref[r + first + j:r + first + j + CONV_ROWS, :]
        y = _layer_norm(acc, g_ref[...], b_ref[...])
        y_ref[r:r + CONV_ROWS, :] = y * _sigmoid(y)
    tail = buf_ref[CONV_TILE:CONV_TILE + CONV_HIST, :]
    buf_ref[0:CONV_HIST, :] = tail

    @pl.when(t == pl.num_programs(1) - 1)
    def _():
        st_ref[0] = tail


def _conv_prompt(vag, batch, seq, w, cb, g, b):
    nt = seq // CONV_TILE
    wpad = jnp.zeros((CONV_HIST, CONV_CH), f32).at[:CONV_WIDTH].set(w)
    row = lambda v: v.reshape(1, CONV_CH)
    y, st = pl.pallas_call(
        _conv_prompt_kernel,
        grid=(batch, nt),
        in_specs=[pl.BlockSpec((CONV_TILE, CONV_CH), lambda bi, t: (bi * nt + t, 1)),
                  pl.BlockSpec((CONV_TILE, CONV_CH), lambda bi, t: (bi * nt + t, 2)),
                  pl.BlockSpec((CONV_HIST, CONV_CH), lambda bi, t: (0, 0))]
                 + [pl.BlockSpec((1, CONV_CH), lambda bi, t: (0, 0))] * 3,
        out_specs=[pl.BlockSpec((CONV_TILE, CONV_CH), lambda bi, t: (bi * nt + t, 0)),
                   pl.BlockSpec((1, CONV_HIST, CONV_CH), lambda bi, t: (bi, 0, 0))],
        out_shape=[jax.ShapeDtypeStruct((batch * seq, CONV_CH), f32),
                   jax.ShapeDtypeStruct((batch, CONV_HIST, CONV_CH), f32)],
        scratch_shapes=[pltpu.VMEM((CONV_TILE + CONV_HIST, CONV_CH), f32)],
        compiler_params=_params(("arbitrary", "arbitrary")),
        name="conv_prompt",
    )(vag, vag, wpad, row(cb), row(g), row(b))
    return y, st[:, CONV_HIST - (CONV_WIDTH - 1):]


def _conv_sample_kernel(a_ref, gt_ref, st_ref, w_ref, cb_ref, g_ref, b_ref, y_ref, nst_ref):
    u = a_ref[...] * _sigmoid(gt_ref[...])
    acc = cb_ref[...] + w_ref[CONV_WIDTH - 1:CONV_WIDTH, :] * u
    for j in range(CONV_WIDTH - 1):
        acc = acc + w_ref[j:j + 1, :] * st_ref[j]
    y = _layer_norm(acc, g_ref[...], b_ref[...])
    y_ref[...] = y * _sigmoid(y)
    for j in range(CONV_WIDTH - 2):
        nst_ref[j] = st_ref[j + 1]
    nst_ref[CONV_WIDTH - 2] = u


def _conv_sample(a, gt, state, w, cb, g, b):
    db = a.shape[0]
    wpad = jnp.zeros((CONV_HIST, CONV_CH), f32).at[:CONV_WIDTH].set(w)
    row = lambda v: v.reshape(1, CONV_CH)
    y, nst = pl.pallas_call(
        _conv_sample_kernel,
        out_shape=[jax.ShapeDtypeStruct((db, CONV_CH), f32),
                   jax.ShapeDtypeStruct((CONV_WIDTH - 1, db, CONV_CH), f32)],
        name="conv_sample",
    )(a, gt, state.transpose(1, 0, 2), wpad, row(cb), row(g), row(b))
    return y, nst.transpose(1, 0, 2)


GATE_ROWS = 1024
ATTN_KEYS = 2 * MOBA_BLOCK


def _block_sum_kernel(k_ref, o_ref):
    o_ref[0] = jnp.sum(k_ref[...], axis=0, keepdims=True)


def _block_sums(qk, batch, seq):
    nb = seq // MOBA_BLOCK
    return pl.pallas_call(
        _block_sum_kernel,
        grid=(batch * nb,),
        in_specs=[pl.BlockSpec((MOBA_BLOCK, MOBA_WIDTH), lambda i: (i, 1))],
        out_specs=pl.BlockSpec((1, 1, MOBA_WIDTH), lambda i: (i, 0, 0)),
        out_shape=jax.ShapeDtypeStruct((batch * nb, 1, MOBA_WIDTH), f32),
        compiler_params=_params(("parallel",)),
        name="moba_block_sums",
    )(qk)


def _moba_gate_kernel(q_ref, k_ref, v_ref, ks_ref, qa_ref, ka_ref, vb_ref, *, nb):
    t = pl.program_id(2)
    lane = lax.broadcasted_iota(i32, (MOBA_BLOCK, LANES), 1)
    blk = lane - MOBA_HEAD_DIM
    km_lane = lax.broadcasted_iota(i32, (nb, LANES), 1)
    ksum = ks_ref[0] * (1.0 / MOBA_BLOCK)
    pad_top = jnp.zeros((MOBA_HEAD_DIM, LANES), f32)
    pad_bot = jnp.zeros((LANES - MOBA_HEAD_DIM - nb, LANES), f32)
    for c in range(GATE_ROWS // MOBA_BLOCK):
        own = t * (GATE_ROWS // MOBA_BLOCK) + c
        rows = slice(c * MOBA_BLOCK, (c + 1) * MOBA_BLOCK)
        q2 = q_ref[rows, :]
        k2 = k_ref[rows, :]
        vb_ref[rows, :] = v_ref[rows, :].astype(bf16)
        cand = (blk >= 0) & (blk < own)
        for j in range(2):
            head = (km_lane >= j * MOBA_HEAD_DIM) & (km_lane < (j + 1) * MOBA_HEAD_DIM)
            km = jnp.concatenate([pad_top, jnp.where(head, ksum, 0.0), pad_bot], axis=0)
            gate = lax.dot_general(q2, km, NT_DIMS, precision=HI, preferred_element_type=f32)
            masked = jnp.where(cand, gate, FLOOR)
            sel = jnp.zeros((MOBA_BLOCK, LANES), f32)
            for _ in range(MOBA_TOPK):
                mx = jnp.max(masked, axis=1, keepdims=True)
                hit = (masked == mx) & cand
                sel = jnp.where(hit, 1.0, sel)
                masked = jnp.where(hit, FLOOR, masked)
            bias = jnp.where((sel > 0.0) | (blk == own), 0.0, MASK_NEG)
            qj = q2 if j == 0 else pltpu.roll(q2, MOBA_HEAD_DIM, 1)
            kj = k2 if j == 0 else pltpu.roll(k2, MOBA_HEAD_DIM, 1)
            in_head = lane < MOBA_HEAD_DIM
            in_bias = lane < MOBA_HEAD_DIM + nb
            qa = jnp.where(in_head, qj * MOBA_SCALE, jnp.where(in_bias, bias, 0.0))
            ka = jnp.where(in_head, kj, jnp.where(blk == own, 1.0, 0.0))
            qa_ref[0, j, rows, :] = qa.astype(bf16)
            ka_ref[0, j, rows, :] = ka.astype(bf16)


def _moba_attn_kernel(qa_ref, ka_ref, vb_ref, o_ref):
    i = pl.program_id(1)
    last = i // 2
    row = lax.broadcasted_iota(i32, (MOBA_BLOCK, ATTN_KEYS), 0)
    col = lax.broadcasted_iota(i32, (MOBA_BLOCK, ATTN_KEYS), 1)
    lane = lax.broadcasted_iota(i32, (MOBA_BLOCK, LANES), 1)

    def tile(h, k0, mask):
        s = lax.dot_general(qa_ref[0, h], ka_ref[0, h, pl.ds(k0, ATTN_KEYS), :], NT_DIMS, preferred_element_type=f32)
        return s if mask is None else jnp.where(mask, s, MASK_NEG)

    def pair_values(h, k0):
        return vb_ref[pl.ds(k0, ATTN_KEYS), (h // 2) * LANES:(h // 2 + 1) * LANES]

    k_last = pl.multiple_of(last * ATTN_KEYS, ATTN_KEYS)
    visible = (k_last + col) <= (i * MOBA_BLOCK + row)
    state = []
    for h in range(MOBA_HEADS):
        s = tile(h, k_last, visible)
        m = jnp.max(s, axis=1, keepdims=True)
        p = jnp.exp(s - m)
        state += [m, jnp.sum(p, axis=1, keepdims=True),
                  jnp.dot(p.astype(bf16), pair_values(h, k_last), preferred_element_type=f32)]

    def body(n, carry):
        k0 = pl.multiple_of(n * ATTN_KEYS, ATTN_KEYS)
        out = []
        for h in range(MOBA_HEADS):
            m, l, acc = carry[3 * h:3 * h + 3]
            s = tile(h, k0, None)
            mn = jnp.maximum(m, jnp.max(s, axis=1, keepdims=True))
            alpha = jnp.exp(m - mn)
            p = jnp.exp(s - mn)
            out += [mn, alpha * l + jnp.sum(p, axis=1, keepdims=True),
                    alpha * acc + jnp.dot(p.astype(bf16), pair_values(h, k0), preferred_element_type=f32)]
        return tuple(out)

    final = lax.fori_loop(0, last, body, tuple(state))
    for pair in range(MOBA_HEADS // 2):
        (_, l0, acc0), (_, l1, acc1) = final[6 * pair:6 * pair + 3], final[6 * pair + 3:6 * pair + 6]
        o_ref[:, pair * LANES:(pair + 1) * LANES] = jnp.where(lane < MOBA_HEAD_DIM, acc0 / l0, acc1 / l1)


def _moba_prompt(qk, vag, batch, seq):
    nb = seq // MOBA_BLOCK
    npair = MOBA_WIDTH // LANES
    nt = seq // GATE_ROWS
    ksums = _block_sums(qk, batch, seq).reshape(batch, nb, MOBA_WIDTH)
    qa, ka, vb = pl.pallas_call(
        functools.partial(_moba_gate_kernel, nb=nb),
        grid=(batch, npair, nt),
        in_specs=[pl.BlockSpec((GATE_ROWS, LANES), lambda b, p, t: (b * nt + t, p)),
                  pl.BlockSpec((GATE_ROWS, LANES), lambda b, p, t: (b * nt + t, npair + p)),
                  pl.BlockSpec((GATE_ROWS, LANES), lambda b, p, t: (b * nt + t, p)),
                  pl.BlockSpec((1, nb, LANES), lambda b, p, t: (b, 0, p))],
        out_specs=[pl.BlockSpec((1, 2, GATE_ROWS, LANES), lambda b, p, t: (b, p, t, 0)),
                   pl.BlockSpec((1, 2, GATE_ROWS, LANES), lambda b, p, t: (b, p, t, 0)),
                   pl.BlockSpec((GATE_ROWS, LANES), lambda b, p, t: (b * nt + t, p))],
        out_shape=[jax.ShapeDtypeStruct((batch, MOBA_HEADS, seq, LANES), bf16),
                   jax.ShapeDtypeStruct((batch, MOBA_HEADS, seq, LANES), bf16),
                   jax.ShapeDtypeStruct((batch * seq, MOBA_WIDTH), bf16)],
        compiler_params=_params(("parallel", "parallel", "parallel")),
        name="moba_gate",
    )(qk, qk, vag, ksums)
    return pl.pallas_call(
        _moba_attn_kernel,
        grid=(batch, nb),
        in_specs=[pl.BlockSpec((1, MOBA_HEADS, MOBA_BLOCK, LANES), lambda b, i: (b, 0, i, 0)),
                  pl.BlockSpec((1, MOBA_HEADS, seq, LANES), lambda b, i: (b, 0, 0, 0),
                               pipeline_mode=pl.Buffered(1)),
                  pl.BlockSpec((seq, MOBA_WIDTH), lambda b, i: (b, 0), pipeline_mode=pl.Buffered(1))],
        out_specs=pl.BlockSpec((MOBA_BLOCK, MOBA_WIDTH), lambda b, i: (b * nb + i, 0)),
        out_shape=jax.ShapeDtypeStruct((batch * seq, MOBA_WIDTH), f32),
        compiler_params=_params(("parallel", "arbitrary")),
        name="moba_attn",
    )(qa, ka, vb)


PAGES_PER_STEP = 8
PAGES_PER_BLOCK = MOBA_BLOCK // PAGE_SIZE
BLOCKS_PER_STEP = PAGES_PER_STEP // PAGES_PER_BLOCK


def _lane_sum_replicated(x2d, precision=None):
    ones = jnp.ones((MOBA_HEAD_DIM, MOBA_HEAD_DIM), x2d.dtype)
    return jnp.dot(x2d, ones, precision=precision, preferred_element_type=f32)


def _moba_sample_stream_kernel(pt_ref, q_ref, *refs):
    del pt_ref
    k_refs = refs[:PAGES_PER_STEP]
    v_refs = refs[PAGES_PER_STEP:2 * PAGES_PER_STEP]
    ks_ref, m_ref, l_ref, acc_ref = refs[2 * PAGES_PER_STEP:]
    qs = q_ref[0] * MOBA_SCALE
    rows = PAGE_SIZE * MOBA_HEADS
    for jj in range(BLOCKS_PER_STEP):
        pages = range(PAGES_PER_BLOCK * jj, PAGES_PER_BLOCK * (jj + 1))
        ksum = jnp.zeros((MOBA_HEADS, MOBA_HEAD_DIM), f32)
        logits = []
        for x in pages:
            k3 = k_refs[x][0]
            ksum = ksum + jnp.sum(k3, axis=0)
            prod = (k3 * qs[None]).reshape(rows, MOBA_HEAD_DIM).astype(bf16)
            logits.append(_lane_sum_replicated(prod).reshape(PAGE_SIZE, MOBA_HEADS, MOBA_HEAD_DIM))
        m = functools.reduce(jnp.maximum, [jnp.max(s, axis=0) for s in logits])
        l = jnp.zeros((MOBA_HEADS, MOBA_HEAD_DIM), f32)
        acc = jnp.zeros((MOBA_HEADS, MOBA_HEAD_DIM), f32)
        for s, x in zip(logits, pages):
            p = jnp.exp(s - m[None])
            l = l + jnp.sum(p, axis=0)
            acc = acc + jnp.sum(p * v_refs[x][0], axis=0)
        ks_ref[0, 0, jj] = ksum
        m_ref[0, 0, jj] = m
        l_ref[0, 0, jj] = l
        acc_ref[0, 0, jj] = acc


def _moba_sample_combine_kernel(ks_ref, m_ref, l_ref, acc_ref, q_ref, kn_ref, vn_ref, o_ref):
    db, nb, nh, hd = ks_ref.shape
    q = q_ref[...]
    gate = _lane_sum_replicated((ks_ref[...] * q[:, None]).reshape(db * nb * nh, hd), HI).reshape(db, nb, nh, hd)
    masked = gate
    sel = jnp.zeros((db, nb, nh, hd), f32)
    for _ in range(MOBA_TOPK):
        mx = jnp.max(masked, axis=1, keepdims=True)
        hit = masked == mx
        sel = jnp.where(hit, 1.0, sel)
        masked = jnp.where(hit, FLOOR, masked)
    chosen = sel > 0.0
    s_self = _lane_sum_replicated((q * kn_ref[...]).reshape(db * nh, hd), HI).reshape(db, nh, hd) * MOBA_SCALE
    m = m_ref[...]
    top = jnp.maximum(jnp.max(jnp.where(chosen, m, FLOOR), axis=1), s_self)
    w = jnp.where(chosen, jnp.exp(m - top[:, None]), 0.0)
    w_self = jnp.exp(s_self - top)
    denom = jnp.sum(w * l_ref[...], axis=1) + w_self
    num = jnp.sum(w * acc_ref[...], axis=1) + w_self * vn_ref[...]
    o_ref[...] = num / denom


def _moba_sample(q, kn, vn, k_pool, v_pool, page_table):
    db, n_pages = page_table.shape
    nb = n_pages // PAGES_PER_BLOCK
    ng = n_pages // PAGES_PER_STEP
    page_spec = lambda x: pl.BlockSpec((1, PAGE_SIZE, MOBA_HEADS, MOBA_HEAD_DIM),
                                       lambda b, g, pt, x=x: (pt[b, g * PAGES_PER_STEP + x], 0, 0, 0))
    stat = pl.BlockSpec((1, 1, BLOCKS_PER_STEP, MOBA_HEADS, MOBA_HEAD_DIM), lambda b, g, pt: (b, g, 0, 0, 0))
    stat_shape = jax.ShapeDtypeStruct((db, ng, BLOCKS_PER_STEP, MOBA_HEADS, MOBA_HEAD_DIM), f32)
    stats = pl.pallas_call(
        _moba_sample_stream_kernel,
        grid_spec=pltpu.PrefetchScalarGridSpec(
            num_scalar_prefetch=1,
            grid=(db, ng),
            in_specs=[pl.BlockSpec((1, MOBA_HEADS, MOBA_HEAD_DIM), lambda b, g, pt: (b, 0, 0))]
                     + [page_spec(x) for x in range(PAGES_PER_STEP)] * 2,
            out_specs=[stat] * 4),
        out_shape=[stat_shape] * 4,
        compiler_params=_params(("parallel", "parallel")),
        name="moba_sample_stream",
    )(page_table, q, *([k_pool] * PAGES_PER_STEP), *([v_pool] * PAGES_PER_STEP))
    ks, m, l, acc = (a.reshape(db, nb, MOBA_HEADS, MOBA_HEAD_DIM) for a in stats)
    return pl.pallas_call(
        _moba_sample_combine_kernel,
        out_shape=jax.ShapeDtypeStruct((db, MOBA_HEADS, MOBA_HEAD_DIM), f32),
        compiler_params=pltpu.CompilerParams(vmem_limit_bytes=VMEM_LIMIT),
        name="moba_sample_combine",
    )(ks, m, l, acc, q, kn, vn)


HGRN_TILE = 512


def _hgrn_prompt_kernel(q_ref, fz_ref, i_ref, g_ref, lb_ref, ng_ref, o_ref, st_ref, state_ref):
    t = pl.program_id(1)
    c_rows = HGRN_CHUNK

    @pl.when(t == 0)
    def _():
        state_ref[...] = jnp.zeros(state_ref.shape, f32)

    row = lax.broadcasted_iota(i32, (c_rows, c_rows), 0)
    col = lax.broadcasted_iota(i32, (c_rows, c_rows), 1)
    causal = col <= row
    cum = jnp.where(causal, 1.0, 0.0)
    cum_sub = jnp.where(col < (row // HGRN_SUB) * HGRN_SUB, 1.0, 0.0)
    cum_both = jnp.concatenate([cum, cum_sub], axis=0)

    def chunk(c, carry):
        rows = pl.ds(pl.multiple_of(c * c_rows, c_rows), c_rows)
        lb_all = lb_ref[...]
        f_all = lb_all + (1.0 - lb_all) * _sigmoid(fz_ref[rows, :])
        both_all = jnp.dot(cum_both, jnp.log(f_all), precision=HI, preferred_element_type=f32)
        for h in range(HGRN_HEADS):
            lanes = slice(h * HGRN_DK, (h + 1) * HGRN_DK)
            ng = ng_ref[:, lanes]
            q = q_ref[rows, lanes]
            v = i_ref[rows, lanes]
            gg = g_ref[rows, lanes]
            kk = 1.0 - f_all[:, lanes]
            b = both_all[:c_rows, lanes]
            ref_row = both_all[c_rows:, lanes]
            st = state_ref[h]
            o = lax.dot_general((q * jnp.exp(b)).astype(bf16), st.astype(bf16), NT_DIMS, preferred_element_type=f32)
            qh = (q * jnp.exp(b - ref_row)).astype(bf16)
            parts = []
            for s in range(c_rows // HGRN_SUB):
                ref_s = ref_row[s * HGRN_SUB:s * HGRN_SUB + 1, :]
                kh = (kk * jnp.exp(jnp.minimum(ref_s - b, EXP_CLAMP))).astype(bf16)
                parts.append(lax.dot_general(qh[s * HGRN_SUB:(s + 1) * HGRN_SUB], kh, NT_DIMS,
                                             preferred_element_type=f32))
            att = jnp.where(causal, jnp.concatenate(parts, axis=0), 0.0)
            o = o + jnp.dot(att.astype(bf16), v.astype(bf16), preferred_element_type=f32)
            b_last = b[c_rows - 1:c_rows, :]
            kd = (kk * jnp.exp(b_last - b)).astype(bf16)
            state_ref[h] = st * jnp.exp(b_last) + lax.dot_general(v.astype(bf16), kd, TN_DIMS,
                                                                   preferred_element_type=f32)
            ms = jnp.mean(o * o, axis=1, keepdims=True)
            o_ref[rows, lanes] = o * lax.rsqrt(ms + RMS_EPS) * ng * (gg * _sigmoid(gg))
        return carry

    lax.fori_loop(0, HGRN_TILE // c_rows, chunk, 0)

    @pl.when(t == pl.num_programs(1) - 1)
    def _():
        for h in range(HGRN_HEADS):
            st_ref[0, h] = state_ref[h].T


def _hgrn_prompt(hproj, batch, seq, lb, norm_g):
    nt = seq // HGRN_TILE
    width = HGRN_HEADS * HGRN_DK
    col = lambda j: pl.BlockSpec((HGRN_TILE, width), lambda b, t, j=j: (b * nt + t, j))
    vec = pl.BlockSpec((1, width), lambda b, t: (0, 0))
    return pl.pallas_call(
        _hgrn_prompt_kernel,
        grid=(batch, nt),
        in_specs=[col(0), col(1), col(2), col(3), vec, vec],
        out_specs=[pl.BlockSpec((HGRN_TILE, width), lambda b, t: (b * nt + t, 0)),
                   pl.BlockSpec((1, HGRN_HEADS, HGRN_DK, HGRN_DK), lambda b, t: (b, 0, 0, 0))],
        out_shape=[jax.ShapeDtypeStruct((batch * seq, width), f32),
                   jax.ShapeDtypeStruct((batch, HGRN_HEADS, HGRN_DK, HGRN_DK), f32)],
        scratch_shapes=[pltpu.VMEM((HGRN_HEADS, HGRN_DK, HGRN_DK), f32)],
        compiler_params=_params(("parallel", "arbitrary")),
        name="hgrn_prompt",
    )(hproj, hproj, hproj, hproj, lb.reshape(1, width), norm_g.reshape(1, width))


def _hgrn_sample_kernel(qc_ref, fzc_ref, lbc_ref, i_ref, g_ref, ng_ref, st_ref, o_ref, nst_ref):
    for h in range(HGRN_HEADS):
        lb = lbc_ref[h]
        f = lb + (1.0 - lb) * _sigmoid(fzc_ref[0, h])
        st = f * st_ref[0, h] + (1.0 - f) * i_ref[0, h]
        nst_ref[0, h] = st
        o = jnp.sum(qc_ref[0, h] * st, axis=0, keepdims=True)
        ms = jnp.mean(o * o, axis=1, keepdims=True)
        gg = g_ref[0, h]
        o_ref[0, h] = o * lax.rsqrt(ms + RMS_EPS) * ng_ref[h] * (gg * _sigmoid(gg))


def _hgrn_sample(hs, state, lb, norm_g):
    db = hs.shape[0]
    width = HGRN_HEADS * HGRN_DK
    colv = lambda x: x.reshape(db, HGRN_HEADS, HGRN_DK, 1)
    rowv = lambda x: x.reshape(db, HGRN_HEADS, 1, HGRN_DK)
    cspec = pl.BlockSpec((1, HGRN_HEADS, HGRN_DK, 1), lambda b: (b, 0, 0, 0))
    rspec = pl.BlockSpec((1, HGRN_HEADS, 1, HGRN_DK), lambda b: (b, 0, 0, 0))
    sspec = pl.BlockSpec((1, HGRN_HEADS, HGRN_DK, HGRN_DK), lambda b: (b, 0, 0, 0))
    o, nst = pl.pallas_call(
        _hgrn_sample_kernel,
        grid=(db,),
        in_specs=[cspec, cspec, pl.BlockSpec((HGRN_HEADS, HGRN_DK, 1), lambda b: (0, 0, 0)), rspec, rspec,
                  pl.BlockSpec((HGRN_HEADS, 1, HGRN_DK), lambda b: (0, 0, 0)), sspec],
        out_specs=[rspec, sspec],
        out_shape=[jax.ShapeDtypeStruct((db, HGRN_HEADS, 1, HGRN_DK), f32),
                   jax.ShapeDtypeStruct((db, HGRN_HEADS, HGRN_DK, HGRN_DK), f32)],
        compiler_params=_params(("parallel",)),
        name="hgrn_sample",
    )(colv(hs[:, :width]), colv(hs[:, width:2 * width]), lb.reshape(HGRN_HEADS, HGRN_DK, 1),
      rowv(hs[:, 2 * width:3 * width]), rowv(hs[:, 3 * width:]), norm_g.reshape(HGRN_HEADS, 1, HGRN_DK), state)
    return o.reshape(db, width), nst


PEER_EBLK = 1024
PEER_RANKS = PEER_TOPK + 1
PEER_VROWS = 24
PEER_SUB = 8
PEER_ROWS = 16


def _extract_max(tiles):
    mx = functools.reduce(jnp.maximum, tiles)
    mx = jnp.max(mx, axis=0, keepdims=True)
    return mx, [jnp.where(x == mx, FLOOR, x) for x in tiles]


def _peer_kernel(x_ref, q_ref, k1_ref, k2_ref, u_ref, v_ref, g_ref, b_ref, o_ref,
                 xt_ref, acc_ref, act_ref, p_ref, a_ref, bw_ref, c_ref, s2_ref, v1_ref, v2_ref, ab_ref, cb_ref):
    e = pl.program_id(1)
    tt = x_ref.shape[0]
    n_lane_chunks = tt // LANES
    sub = PEER_SUB
    n_sub = PEER_NKEYS // sub
    keys_per_step = PEER_EBLK // PEER_NKEYS

    @pl.when(e == 0)
    def _():
        xt_ref[...] = x_ref[...].T.astype(bf16)
        acc_ref[...] = jnp.zeros(acc_ref.shape, f32)
        floor_rows = jnp.full((PEER_VROWS, LANES), FLOOR, f32)
        for h in range(PEER_HEADS):
            q1 = q_ref[:, (2 * h) * PEER_DHALF:(2 * h + 1) * PEER_DHALF]
            q2 = q_ref[:, (2 * h + 1) * PEER_DHALF:(2 * h + 2) * PEER_DHALF]
            s1 = lax.dot_general(k1_ref[h], q1, NT_DIMS, precision=HI, preferred_element_type=f32)
            c_ref[h] = s1.reshape(n_sub, sub, tt)
            s2_ref[h] = lax.dot_general(k2_ref[h], q2, NT_DIMS, precision=HI, preferred_element_type=f32)

        def per_head(h, carry):
            for lc in range(n_lane_chunks):
                lanes = slice(lc * LANES, (lc + 1) * LANES)
                for side, vals_ref in enumerate((v1_ref, v2_ref)):
                    vals_ref[...] = floor_rows
                    if side == 0:
                        tiles = [c_ref[h, k, :, lanes] for k in range(n_sub)]
                    else:
                        tiles = [s2_ref[h, k * sub:(k + 1) * sub, lanes] for k in range(n_sub)]
                    for r in range(PEER_RANKS):
                        mx, tiles = _extract_max(tiles)
                        vals_ref[r:r + 1, :] = mx
                cands = [v1_ref[0:1, :] + v2_ref[r:r + sub, :] for r in range(0, PEER_VROWS, sub)]
                cands += [v1_ref[r:r + 1, :] + v2_ref[0:sub, :] for r in range(1, sub)]
                cands += [v1_ref[r:r + sub, :] + v2_ref[0:1, :] for r in range(sub, PEER_VROWS, sub)]
                best = v1_ref[0:1, :] + v2_ref[0:1, :]
                zsum = jnp.zeros((1, LANES), f32)
                kth = best
                for r in range(PEER_TOPK):
                    kth, cands = _extract_max(cands)
                    zsum = zsum + jnp.exp(kth - best)
                nxt, _ = _extract_max(cands)
                thresh = 0.5 * (kth + nxt)
                s1 = c_ref[h, :, :, lanes]
                a_ref[h, :, :, lanes] = jnp.exp(s1 - v1_ref[0:1, :]) / zsum
                bw_ref[h, :, lanes] = jnp.exp(s2_ref[h, :, lanes] - v2_ref[0:1, :])
                c_ref[h, :, :, lanes] = thresh - s1
            return carry

        lax.fori_loop(0, PEER_HEADS, per_head, 0)

    act_ref[...] = jnp.dot(u_ref[...], xt_ref[...], preferred_element_type=f32)
    for h in range(PEER_HEADS):
        for ii in range(keys_per_step):
            ab_ref[ii, h] = jnp.broadcast_to(a_ref[h, e, ii:ii + 1, :], (sub, tt))
            cb_ref[ii, h] = jnp.broadcast_to(c_ref[h, e, ii:ii + 1, :], (sub, tt))

    def per_second_keys(k, carry):
        r0 = pl.multiple_of(k * PEER_ROWS, PEER_ROWS)
        for lc in range(n_lane_chunks):
            lanes = slice(lc * LANES, (lc + 1) * LANES)
            w = [jnp.zeros((PEER_ROWS // sub, sub, LANES), f32) for _ in range(keys_per_step)]
            for h in range(PEER_HEADS):
                s2 = s2_ref[h, pl.ds(r0, PEER_ROWS), lanes].reshape(PEER_ROWS // sub, sub, LANES)
                bw = bw_ref[h, pl.ds(r0, PEER_ROWS), lanes].reshape(PEER_ROWS // sub, sub, LANES)
                for ii in range(keys_per_step):
                    hit = s2 >= cb_ref[ii, h, :, lanes][None]
                    w[ii] = w[ii] + ab_ref[ii, h, :, lanes][None] * jnp.where(hit, bw, 0.0)
            for ii in range(keys_per_step):
                rows = pl.ds(pl.multiple_of(ii * PEER_NKEYS + r0, PEER_ROWS), PEER_ROWS)
                act = act_ref[rows, lanes]
                gelu = 0.5 * act * (1.0 + lax.erf(act * (2.0 ** -0.5)))
                p_ref[rows, lanes] = (w[ii].reshape(PEER_ROWS, LANES) * gelu).astype(bf16)
        return carry

    lax.fori_loop(0, PEER_NKEYS // PEER_ROWS, per_second_keys, 0)
    acc_ref[...] += lax.dot_general(p_ref[...], v_ref[...], TN_DIMS, preferred_element_type=f32)

    @pl.when(e == pl.num_programs(1) - 1)
    def _():
        o_ref[...] = _layer_norm(DN_ALPHA * x_ref[...] + acc_ref[...], g_ref[...], b_ref[...])


def _peer_ln(x, wq, keys, u, v, g, b, *, tt=TOKEN_TILE):
    m, d = x.shape
    n_exp = u.shape[0]
    q = _mm(x, wq, split=True)
    nq = q.shape[1]
    return pl.pallas_call(
        _peer_kernel,
        grid=(m // tt, n_exp // PEER_EBLK),
        in_specs=[pl.BlockSpec((tt, d), lambda t, e: (t, 0)),
                  pl.BlockSpec((tt, nq), lambda t, e: (t, 0)),
                  pl.BlockSpec((PEER_HEADS, PEER_NKEYS, PEER_DHALF), lambda t, e: (0, 0, 0)),
                  pl.BlockSpec((PEER_HEADS, PEER_NKEYS, PEER_DHALF), lambda t, e: (0, 0, 0)),
                  pl.BlockSpec((PEER_EBLK, d), lambda t, e: (e, 0)),
                  pl.BlockSpec((PEER_EBLK, d), lambda t, e: (e, 0)),
                  pl.BlockSpec((1, d), lambda t, e: (0, 0)),
                  pl.BlockSpec((1, d), lambda t, e: (0, 0))],
        out_specs=pl.BlockSpec((tt, d), lambda t, e: (t, 0)),
        out_shape=jax.ShapeDtypeStruct((m, d), f32),
        scratch_shapes=[pltpu.VMEM((d, tt), bf16), pltpu.VMEM((tt, d), f32),
                        pltpu.VMEM((PEER_EBLK, tt), f32), pltpu.VMEM((PEER_EBLK, tt), bf16)]
                       + [pltpu.VMEM((PEER_HEADS, PEER_NKEYS // PEER_SUB, PEER_SUB, tt), f32),
                          pltpu.VMEM((PEER_HEADS, PEER_NKEYS, tt), f32),
                          pltpu.VMEM((PEER_HEADS, PEER_NKEYS // PEER_SUB, PEER_SUB, tt), f32),
                          pltpu.VMEM((PEER_HEADS, PEER_NKEYS, tt), f32)]
                       + [pltpu.VMEM((PEER_VROWS, LANES), f32)] * 2
                       + [pltpu.VMEM((PEER_EBLK // PEER_NKEYS, PEER_HEADS, PEER_SUB, tt), f32)] * 2,
        compiler_params=_params(("parallel", "arbitrary")),
        name="peer",
    )(x, q, keys[0], keys[1], u.astype(bf16), v.astype(bf16), g.reshape(1, d), b.reshape(1, d))


def kernel(x_prompt, x_sample, cache_k, cache_v, state_conv, state_hgrn, page_table, w_in_even, conv_w, conv_b, conv_ln_g, conv_ln_b, w_out_even, w_in_odd, hgrn_lb_logits, hgrn_norm_g, w_out_odd, ln_g, ln_b, peer_wq, peer_keys, peer_u, peer_v):
    batch, seq, d = x_prompt.shape
    db = x_sample.shape[0]
    n_prompt = batch * seq
    rows = n_prompt + TOKEN_TILE
    srows = slice(n_prompt, n_prompt + db)
    pad_rows = lambda a: jnp.concatenate([a, jnp.zeros((rows - a.shape[0], a.shape[1]), a.dtype)], axis=0)

    lb_p = jax.nn.softmax(hgrn_lb_logits.astype(f32), axis=0)
    lb_all = jnp.cumsum(lb_p, axis=0) - lb_p[0]

    x = pad_rows(jnp.concatenate([x_prompt.reshape(n_prompt, d), x_sample.reshape(db, d)], axis=0))
    kp_l, vp_l, ks_l, vs_l, cp_l, cs_l, hp_l, hs_l = [], [], [], [], [], [], [], []
    for l in range(DEPTH):
        if l % 2 == 0:
            e = l // 2
            w_in = w_in_even[e]
            qk = _mm(x, w_in[:, :2 * MOBA_WIDTH], split=True)
            vag = _mm(x, w_in[:, 2 * MOBA_WIDTH:])
            k_all, v_all = qk[:, MOBA_WIDTH:], vag[:, :MOBA_WIDTH]
            kp_l.append(k_all[:n_prompt].reshape(batch, seq // PAGE_SIZE, PAGE_SIZE, MOBA_HEADS, MOBA_HEAD_DIM))
            vp_l.append(v_all[:n_prompt].reshape(batch, seq // PAGE_SIZE, PAGE_SIZE, MOBA_HEADS, MOBA_HEAD_DIM))
            ks_l.append(k_all[srows].reshape(db, 1, MOBA_HEADS, MOBA_HEAD_DIM))
            vs_l.append(v_all[srows].reshape(db, 1, MOBA_HEADS, MOBA_HEAD_DIM))
            att_p = _moba_prompt(qk, vag, batch, seq)
            heads = lambda a: a.reshape(db, MOBA_HEADS, MOBA_HEAD_DIM)
            att_s = _moba_sample(heads(qk[srows, :MOBA_WIDTH]), heads(k_all[srows]), heads(v_all[srows]),
                                 cache_k[e], cache_v[e], page_table).reshape(db, MOBA_WIDTH)
            cprm = (conv_w[e], conv_b[e], conv_ln_g[e], conv_ln_b[e])
            cy_p, buf_p = _conv_prompt(vag, batch, seq, *cprm)
            cy_s, buf_s = _conv_sample(vag[srows, MOBA_WIDTH:MOBA_WIDTH + CONV_CH],
                                       vag[srows, MOBA_WIDTH + CONV_CH:], state_conv[e], *cprm)
            cp_l.append(buf_p)
            cs_l.append(buf_s)
            mix = pad_rows(jnp.concatenate([jnp.concatenate([att_p, cy_p], axis=1),
                                            jnp.concatenate([att_s, cy_s], axis=1)], axis=0))
            w_out = w_out_even[e]
        else:
            oi = l // 2
            hproj = _mm(x, w_in_odd[oi], tm=256)
            o_p, st_p = _hgrn_prompt(hproj, batch, seq, lb_all[l], hgrn_norm_g[oi])
            o_s, st_s = _hgrn_sample(hproj[srows], state_hgrn[oi].astype(f32), lb_all[l], hgrn_norm_g[oi])
            hp_l.append(st_p)
            hs_l.append(st_s)
            mix = pad_rows(jnp.concatenate([o_p, o_s], axis=0))
            w_out = w_out_odd[oi]
        x = _mm_res_ln(mix, w_out, x, ln_g[l, 0], ln_b[l, 0])
        x = _peer_ln(x, peer_wq[l], peer_keys[l], peer_u[l], peer_v[l], ln_g[l, 1], ln_b[l, 1])
    y_prompt = x[:n_prompt].reshape(batch, seq, d)
    y_sample = x[srows].reshape(db, 1, d)
    return (y_prompt, y_sample, jnp.stack(kp_l), jnp.stack(vp_l), jnp.stack(ks_l), jnp.stack(vs_l),
            jnp.stack(cp_l), jnp.stack(cs_l), jnp.stack(hp_l), jnp.stack(hs_l))
```

```python
import functools
import math

import jax
import jax.numpy as jnp
from jax import lax
from jax.experimental import pallas as pl
from jax.experimental.pallas import tpu as pltpu

f32 = jnp.float32
bf16 = jnp.bfloat16
i32 = jnp.int32
HI = lax.Precision.HIGHEST

D_MODEL = 1024
DEPTH = 2
PAGE_SIZE = 128
MOBA_HEADS = 8
MOBA_HEAD_DIM = 64
MOBA_WIDTH = MOBA_HEADS * MOBA_HEAD_DIM
MOBA_BLOCK = 256
MOBA_TOPK = 3
MOBA_SCALE = MOBA_HEAD_DIM ** -0.5
CONV_CH = D_MODEL // 2
CONV_WIDTH = 31
HGRN_HEADS = 8
HGRN_DK = D_MODEL // HGRN_HEADS
HGRN_CHUNK = 64
HGRN_SUB = 16
PEER_HEADS = 8
PEER_NKEYS = 128
PEER_DHALF = 128
PEER_TOPK = 16
LN_EPS = 1e-5
RMS_EPS = 1e-6
DN_ALPHA = (2 * DEPTH) ** 0.25

LANES = 128
TOKEN_TILE = 512
MASK_NEG = -1e30
FLOOR = -3e38
EXP_CLAMP = 60.0
VMEM_LIMIT = 56 << 20

NT_DIMS = (((1,), (1,)), ((), ()))
TN_DIMS = (((0,), (0,)), ((), ()))


def _params(semantics, vmem=VMEM_LIMIT):
    return pltpu.CompilerParams(dimension_semantics=semantics, vmem_limit_bytes=vmem)


def _layer_norm(y, g, b):
    mu = jnp.mean(y, axis=-1, keepdims=True)
    d = y - mu
    var = jnp.mean(d * d, axis=-1, keepdims=True)
    return d * lax.rsqrt(var + LN_EPS) * g + b


def _sigmoid(x):
    return 1.0 / (1.0 + jnp.exp(-x))


def _mm_kernel(x_ref, *refs, n_chunk, split):
    if split:
        wh_ref, wl_ref, o_ref = refs
    else:
        wh_ref, o_ref = refs
    x = x_ref[...]
    xh = x.astype(bf16)
    if split:
        xl = (x - xh.astype(f32)).astype(bf16)
    for j in range(0, o_ref.shape[1], n_chunk):
        wh = wh_ref[:, j:j + n_chunk]
        acc = jnp.dot(xh, wh, preferred_element_type=f32)
        if split:
            acc = acc + jnp.dot(xl, wh, preferred_element_type=f32)
            acc = acc + jnp.dot(xh, wl_ref[:, j:j + n_chunk], preferred_element_type=f32)
        o_ref[:, j:j + n_chunk] = acc


def _mm(x, w, *, split=False, tm=TOKEN_TILE):
    m, k = x.shape
    n = w.shape[1]
    wh = w.astype(bf16)
    ws = [wh] + ([(w - wh.astype(f32)).astype(bf16)] if split else [])
    return pl.pallas_call(
        functools.partial(_mm_kernel, n_chunk=512, split=split),
        grid=(m // tm,),
        in_specs=[pl.BlockSpec((tm, k), lambda i: (i, 0))] + [pl.BlockSpec((k, n), lambda i: (0, 0))] * len(ws),
        out_specs=pl.BlockSpec((tm, n), lambda i: (i, 0)),
        out_shape=jax.ShapeDtypeStruct((m, n), f32),
        compiler_params=_params(("parallel",)),
        name="proj_split" if split else "proj",
    )(x, *ws)


def _mm_res_ln_kernel(a_ref, w_ref, x_ref, g_ref, b_ref, o_ref):
    acc = jnp.dot(a_ref[...].astype(bf16), w_ref[...], preferred_element_type=f32)
    o_ref[...] = _layer_norm(DN_ALPHA * x_ref[...] + acc, g_ref[...], b_ref[...])


def _mm_res_ln(a, w, x, g, b, *, tm=TOKEN_TILE):
    m, k = a.shape
    n = w.shape[1]
    return pl.pallas_call(
        _mm_res_ln_kernel,
        grid=(m // tm,),
        in_specs=[pl.BlockSpec((tm, k), lambda i: (i, 0)), pl.BlockSpec((k, n), lambda i: (0, 0)),
                  pl.BlockSpec((tm, n), lambda i: (i, 0)), pl.BlockSpec((1, n), lambda i: (0, 0)),
                  pl.BlockSpec((1, n), lambda i: (0, 0))],
        out_specs=pl.BlockSpec((tm, n), lambda i: (i, 0)),
        out_shape=jax.ShapeDtypeStruct((m, n), f32),
        compiler_params=_params(("parallel",)),
        name="out_proj_ln",
    )(a, w.astype(bf16), x, g.reshape(1, n), b.reshape(1, n))


CONV_TILE = 256
CONV_ROWS = 64
CONV_HIST = 32


def _conv_prompt_kernel(a_ref, gt_ref, w_ref, cb_ref, g_ref, b_ref, y_ref, st_ref, buf_ref):
    t = pl.program_id(1)

    @pl.when(t == 0)
    def _():
        buf_ref[0:CONV_HIST, :] = jnp.zeros((CONV_HIST, CONV_CH), f32)

    buf_ref[CONV_HIST:CONV_HIST + CONV_TILE, :] = a_ref[...] * _sigmoid(gt_ref[...])
    first = CONV_HIST - (CONV_WIDTH - 1)
    for r in range(0, CONV_TILE, CONV_ROWS):
        acc = jnp.zeros((CONV_ROWS, CONV_CH), f32) + cb_ref[...]
        for j in range(CONV_WIDTH):
            acc = acc + w_ref[j:j + 1, :] * buf_ref[r + first + j:r + first + j + CONV_ROWS, :]
        y = _layer_norm(acc, g_ref[...], b_ref[...])
        y_ref[r:r + CONV_ROWS, :] = y * _sigmoid(y)
    tail = buf_ref[CONV_TILE:CONV_TILE + CONV_HIST, :]
    buf_ref[0:CONV_HIST, :] = tail

    @pl.when(t == pl.num_programs(1) - 1)
    def _():
        st_ref[0] = tail


def _conv_prompt(vag, batch, seq, w, cb, g, b):
    nt = seq // CONV_TILE
    wpad = jnp.zeros((CONV_HIST, CONV_CH), f32).at[:CONV_WIDTH].set(w)
    row = lambda v: v.reshape(1, CONV_CH)
    y, st = pl.pallas_call(
        _conv_prompt_kernel,
        grid=(batch, nt),
        in_specs=[pl.BlockSpec((CONV_TILE, CONV_CH), lambda bi, t: (bi * nt + t, 1)),
                  pl.BlockSpec((CONV_TILE, CONV_CH), lambda bi, t: (bi * nt + t, 2)),
                  pl.BlockSpec((CONV_HIST, CONV_CH), lambda bi, t: (0, 0))]
                 + [pl.BlockSpec((1, CONV_CH), lambda bi, t: (0, 0))] * 3,
        out_specs=[pl.BlockSpec((CONV_TILE, CONV_CH), lambda bi, t: (bi * nt + t, 0)),
                   pl.BlockSpec((1, CONV_HIST, CONV_CH), lambda bi, t: (bi, 0, 0))],
        out_shape=[jax.ShapeDtypeStruct((batch * seq, CONV_CH), f32),
                   jax.ShapeDtypeStruct((batch, CONV_HIST, CONV_CH), f32)],
        scratch_shapes=[pltpu.VMEM((CONV_TILE + CONV_HIST, CONV_CH), f32)],
        compiler_params=_params(("arbitrary", "arbitrary")),
        name="conv_prompt",
    )(vag, vag, wpad, row(cb), row(g), row(b))
    return y, st[:, CONV_HIST - (CONV_WIDTH - 1):]


def _conv_sample_kernel(a_ref, gt_ref, st_ref, w_ref, cb_ref, g_ref, b_ref, y_ref, nst_ref):
    u = a_ref[...] * _sigmoid(gt_ref[...])
    acc = cb_ref[...] + w_ref[CONV_WIDTH - 1:CONV_WIDTH, :] * u
    for j in range(CONV_WIDTH - 1):
        acc = acc + w_ref[j:j + 1, :] * st_ref[j]
    y = _layer_norm(acc, g_ref[...], b_ref[...])
    y_ref[...] = y * _sigmoid(y)
    for j in range(CONV_WIDTH - 2):
        nst_ref[j] = st_ref[j + 1]
    nst_ref[CONV_WIDTH - 2] = u


def _conv_sample(a, gt, state, w, cb, g, b):
    db = a.shape[0]
    wpad = jnp.zeros((CONV_HIST, CONV_CH), f32).at[:CONV_WIDTH].set(w)
    row = lambda v: v.reshape(1, CONV_CH)
    y, nst = pl.pallas_call(
        _conv_sample_kernel,
        out_shape=[jax.ShapeDtypeStruct((db, CONV_CH), f32),
                   jax.ShapeDtypeStruct((CONV_WIDTH - 1, db, CONV_CH), f32)],
        name="conv_sample",
    )(a, gt, state.transpose(1, 0, 2), wpad, row(cb), row(g), row(b))
    return y, nst.transpose(1, 0, 2)


GATE_ROWS = 1024
ATTN_KEYS = 2 * MOBA_BLOCK


def _block_sum_kernel(k_ref, o_ref):
    o_ref[0] = jnp.sum(k_ref[...], axis=0, keepdims=True)


def _block_sums(qk, batch, seq):
    nb = seq // MOBA_BLOCK
    return pl.pallas_call(
        _block_sum_kernel,
        grid=(batch * nb,),
        in_specs=[pl.BlockSpec((MOBA_BLOCK, MOBA_WIDTH), lambda i: (i, 1))],
        out_specs=pl.BlockSpec((1, 1, MOBA_WIDTH), lambda i: (i, 0, 0)),
        out_shape=jax.ShapeDtypeStruct((batch * nb, 1, MOBA_WIDTH), f32),
        compiler_params=_params(("parallel",)),
        name="moba_block_sums",
    )(qk)


def _moba_gate_kernel(q_ref, k_ref, v_ref, ks_ref, qa_ref, ka_ref, vb_ref, *, nb):
    t = pl.program_id(2)
    lane = lax.broadcasted_iota(i32, (MOBA_BLOCK, LANES), 1)
    blk = lane - MOBA_HEAD_DIM
    km_lane = lax.broadcasted_iota(i32, (nb, LANES), 1)
    ksum = ks_ref[0] * (1.0 / MOBA_BLOCK)
    pad_top = jnp.zeros((MOBA_HEAD_DIM, LANES), f32)
    pad_bot = jnp.zeros((LANES - MOBA_HEAD_DIM - nb, LANES), f32)
    for c in range(GATE_ROWS // MOBA_BLOCK):
        own = t * (GATE_ROWS // MOBA_BLOCK) + c
        rows = slice(c * MOBA_BLOCK, (c + 1) * MOBA_BLOCK)
        q2 = q_ref[rows, :]
        k2 = k_ref[rows, :]
        vb_ref[rows, :] = v_ref[rows, :].astype(bf16)
        cand = (blk >= 0) & (blk < own)
        for j in range(2):
            head = (km_lane >= j * MOBA_HEAD_DIM) & (km_lane < (j + 1) * MOBA_HEAD_DIM)
            km = jnp.concatenate([pad_top, jnp.where(head, ksum, 0.0), pad_bot], axis=0)
            gate = lax.dot_general(q2, km, NT_DIMS, precision=HI, preferred_element_type=f32)
            masked = jnp.where(cand, gate, FLOOR)
            sel = jnp.zeros((MOBA_BLOCK, LANES), f32)
            for _ in range(MOBA_TOPK):
                mx = jnp.max(masked, axis=1, keepdims=True)
                hit = (masked == mx) & cand
                sel = jnp.where(hit, 1.0, sel)
                masked = jnp.where(hit, FLOOR, masked)
            bias = jnp.where((sel > 0.0) | (blk == own), 0.0, MASK_NEG)
            qj = q2 if j == 0 else pltpu.roll(q2, MOBA_HEAD_DIM, 1)
            kj = k2 if j == 0 else pltpu.roll(k2, MOBA_HEAD_DIM, 1)
            in_head = lane < MOBA_HEAD_DIM
            in_bias = lane < MOBA_HEAD_DIM + nb
            qa = jnp.where(in_head, qj * MOBA_SCALE, jnp.where(in_bias, bias, 0.0))
            ka = jnp.where(in_head, kj, jnp.where(blk == own, 1.0, 0.0))
            qa_ref[0, j, rows, :] = qa.astype(bf16)
            ka_ref[0, j, rows, :] = ka.astype(bf16)


def _moba_attn_kernel(qa_ref, ka_ref, vb_ref, o_ref):
    i = pl.program_id(1)
    last = i // 2
    row = lax.broadcasted_iota(i32, (MOBA_BLOCK, ATTN_KEYS), 0)
    col = lax.broadcasted_iota(i32, (MOBA_BLOCK, ATTN_KEYS), 1)
    lane = lax.broadcasted_iota(i32, (MOBA_BLOCK, LANES), 1)

    def tile(h, k0, mask):
        s = lax.dot_general(qa_ref[0, h], ka_ref[0, h, pl.ds(k0, ATTN_KEYS), :], NT_DIMS, preferred_element_type=f32)
        return s if mask is None else jnp.where(mask, s, MASK_NEG)

    def pair_values(h, k0):
        return vb_ref[pl.ds(k0, ATTN_KEYS), (h // 2) * LANES:(h // 2 + 1) * LANES]

    k_last = pl.multiple_of(last * ATTN_KEYS, ATTN_KEYS)
    visible = (k_last + col) <= (i * MOBA_BLOCK + row)
    state = []
    for h in range(MOBA_HEADS):
        s = tile(h, k_last, visible)
        m = jnp.max(s, axis=1, keepdims=True)
        p = jnp.exp(s - m)
        state += [m, jnp.sum(p, axis=1, keepdims=True),
                  jnp.dot(p.astype(bf16), pair_values(h, k_last), preferred_element_type=f32)]

    def body(n, carry):
        k0 = pl.multiple_of(n * ATTN_KEYS, ATTN_KEYS)
        out = []
        for h in range(MOBA_HEADS):
            m, l, acc = carry[3 * h:3 * h + 3]
            s = tile(h, k0, None)
            mn = jnp.maximum(m, jnp.max(s, axis=1, keepdims=True))
            alpha = jnp.exp(m - mn)
            p = jnp.exp(s - mn)
            out += [mn, alpha * l + jnp.sum(p, axis=1, keepdims=True),
                    alpha * acc + jnp.dot(p.astype(bf16), pair_values(h, k0), preferred_element_type=f32)]
        return tuple(out)

    final = lax.fori_loop(0, last, body, tuple(state))
    for pair in range(MOBA_HEADS // 2):
        (_, l0, acc0), (_, l1, acc1) = final[6 * pair:6 * pair + 3], final[6 * pair + 3:6 * pair + 6]
        o_ref[:, pair * LANES:(pair + 1) * LANES] = jnp.where(lane < MOBA_HEAD_DIM, acc0 / l0, acc1 / l1)


def _moba_prompt(qk, vag, batch, seq):
    nb = seq // MOBA_BLOCK
    npair = MOBA_WIDTH // LANES
    nt = seq // GATE_ROWS
    ksums = _block_sums(qk, batch, seq).reshape(batch, nb, MOBA_WIDTH)
    qa, ka, vb = pl.pallas_call(
        functools.partial(_moba_gate_kernel, nb=nb),
        grid=(batch, npair, nt),
        in_specs=[pl.BlockSpec((GATE_ROWS, LANES), lambda b, p, t: (b * nt + t, p)),
                  pl.BlockSpec((GATE_ROWS, LANES), lambda b, p, t: (b * nt + t, npair + p)),
                  pl.BlockSpec((GATE_ROWS, LANES), lambda b, p, t: (b * nt + t, p)),
                  pl.BlockSpec((1, nb, LANES), lambda b, p, t: (b, 0, p))],
        out_specs=[pl.BlockSpec((1, 2, GATE_ROWS, LANES), lambda b, p, t: (b, p, t, 0)),
                   pl.BlockSpec((1, 2, GATE_ROWS, LANES), lambda b, p, t: (b, p, t, 0)),
                   pl.BlockSpec((GATE_ROWS, LANES), lambda b, p, t: (b * nt + t, p))],
        out_shape=[jax.ShapeDtypeStruct((batch, MOBA_HEADS, seq, LANES), bf16),
                   jax.ShapeDtypeStruct((batch, MOBA_HEADS, seq, LANES), bf16),
                   jax.ShapeDtypeStruct((batch * seq, MOBA_WIDTH), bf16)],
        compiler_params=_params(("parallel", "parallel", "parallel")),
        name="moba_gate",
    )(qk, qk, vag, ksums)
    return pl.pallas_call(
        _moba_attn_kernel,
        grid=(batch, nb),
        in_specs=[pl.BlockSpec((1, MOBA_HEADS, MOBA_BLOCK, LANES), lambda b, i: (b, 0, i, 0)),
                  pl.BlockSpec((1, MOBA_HEADS, seq, LANES), lambda b, i: (b, 0, 0, 0),
                               pipeline_mode=pl.Buffered(1)),
                  pl.BlockSpec((seq, MOBA_WIDTH), lambda b, i: (b, 0), pipeline_mode=pl.Buffered(1))],
        out_specs=pl.BlockSpec((MOBA_BLOCK, MOBA_WIDTH), lambda b, i: (b * nb + i, 0)),
        out_shape=jax.ShapeDtypeStruct((batch * seq, MOBA_WIDTH), f32),
        compiler_params=_params(("parallel", "arbitrary")),
        name="moba_attn",
    )(qa, ka, vb)


PAGES_PER_STEP = 8
PAGES_PER_BLOCK = MOBA_BLOCK // PAGE_SIZE
BLOCKS_PER_STEP = PAGES_PER_STEP // PAGES_PER_BLOCK


def _moba_sample_stream_kernel(pt_ref, qb_ref, *refs):
    del pt_ref
    k_refs = refs[:PAGES_PER_STEP]
    v_refs = refs[PAGES_PER_STEP:2 * PAGES_PER_STEP]
    gate_ref, m_ref, l_ref, acc_ref = refs[2 * PAGES_PER_STEP:]
    qb = qb_ref[0]
    head = lax.broadcasted_iota(i32, (MOBA_HEADS, LANES), 0)
    tile = (MOBA_HEADS, LANES)
    for jj in range(BLOCKS_PER_STEP):
        pages = range(PAGES_PER_BLOCK * jj, PAGES_PER_BLOCK * (jj + 1))
        raw = [jnp.sum(k_refs[x][0] * qb, axis=1) for x in pages]
        gate = functools.reduce(jnp.add, [jnp.sum(r, axis=1, keepdims=True) for r in raw]) * (1.0 / MOBA_BLOCK)
        m = functools.reduce(jnp.maximum, [jnp.max(r, axis=1, keepdims=True) for r in raw]) * MOBA_SCALE
        l = jnp.zeros((MOBA_HEADS, 1), f32)
        pv = jnp.zeros((MOBA_HEADS, MOBA_WIDTH), f32)
        for r, x in zip(raw, pages):
            p = jnp.exp(r * MOBA_SCALE - m)
            l = l + jnp.sum(p, axis=1, keepdims=True)
            vt = v_refs[x][0].reshape(MOBA_WIDTH, PAGE_SIZE).astype(bf16)
            pv = pv + lax.dot_general(p.astype(bf16), vt, NT_DIMS, preferred_element_type=f32)
        acc = jnp.zeros(tile, f32)
        for pair in range(MOBA_WIDTH // LANES):
            both = pv[:, pair * LANES:(pair + 1) * LANES]
            acc = acc + jnp.where(head == 2 * pair, both, 0.0)
            acc = acc + jnp.where(head == 2 * pair + 1, pltpu.roll(both, MOBA_HEAD_DIM, 1), 0.0)
        gate_ref[0, 0, jj] = jnp.broadcast_to(gate, tile)
        m_ref[0, 0, jj] = jnp.broadcast_to(m, tile)
        l_ref[0, 0, jj] = jnp.broadcast_to(l, tile)
        acc_ref[0, 0, jj] = acc


def _moba_sample_combine_kernel(gate_ref, m_ref, l_ref, acc_ref, q_ref, kn_ref, vn_ref, o_ref):
    db, nb, nh, width = gate_ref.shape
    masked = gate_ref[...]
    sel = jnp.zeros((db, nb, nh, width), f32)
    for _ in range(MOBA_TOPK):
        mx = jnp.max(masked, axis=1, keepdims=True)
        hit = masked == mx
        sel = jnp.where(hit, 1.0, sel)
        masked = jnp.where(hit, FLOOR, masked)
    chosen = sel > 0.0
    s_self = jnp.sum(q_ref[...] * kn_ref[...], axis=-1, keepdims=True) * MOBA_SCALE
    m = m_ref[...]
    top = jnp.maximum(jnp.max(jnp.where(chosen, m, FLOOR), axis=1), s_self)
    w = jnp.where(chosen, jnp.exp(m - top[:, None]), 0.0)
    w_self = jnp.exp(s_self - top)
    denom = jnp.sum(w * l_ref[...], axis=1) + w_self
    num = jnp.sum(w * acc_ref[...], axis=1) + w_self * vn_ref[...]
    o_ref[...] = num / denom


def _moba_sample(q, kn, vn, k_pool, v_pool, page_table):
    db, n_pages = page_table.shape
    nb = n_pages // PAGES_PER_BLOCK
    ng = n_pages // PAGES_PER_STEP
    page_block = (1, MOBA_HEADS, MOBA_HEAD_DIM, PAGE_SIZE)
    page_spec = lambda x: pl.BlockSpec(page_block, lambda b, g, pt, x=x: (pt[b, g * PAGES_PER_STEP + x], 0, 0, 0))
    stat = pl.BlockSpec((1, 1, BLOCKS_PER_STEP, MOBA_HEADS, LANES), lambda b, g, pt: (b, g, 0, 0, 0))
    stat_shape = jax.ShapeDtypeStruct((db, ng, BLOCKS_PER_STEP, MOBA_HEADS, LANES), f32)
    qb = jnp.broadcast_to(q[..., None], (db,) + page_block[1:])
    stats = pl.pallas_call(
        _moba_sample_stream_kernel,
        grid_spec=pltpu.PrefetchScalarGridSpec(
            num_scalar_prefetch=1,
            grid=(db, ng),
            in_specs=[pl.BlockSpec(page_block, lambda b, g, pt: (b, 0, 0, 0))]
                     + [page_spec(x) for x in range(PAGES_PER_STEP)] * 2,
            out_specs=[stat] * 4),
        out_shape=[stat_shape] * 4,
        compiler_params=_params(("parallel", "parallel")),
        name="moba_sample_stream",
    )(page_table, qb, *([k_pool] * PAGES_PER_STEP), *([v_pool] * PAGES_PER_STEP))
    gate, m, l, acc = (a.reshape(db, nb, MOBA_HEADS, LANES) for a in stats)
    vn_wide = jnp.pad(vn, ((0, 0), (0, 0), (0, LANES - MOBA_HEAD_DIM)))
    return pl.pallas_call(
        _moba_sample_combine_kernel,
        out_shape=jax.ShapeDtypeStruct((db, MOBA_HEADS, LANES), f32),
        compiler_params=pltpu.CompilerParams(vmem_limit_bytes=VMEM_LIMIT),
        name="moba_sample_combine",
    )(gate, m, l, acc, q, kn, vn_wide)


HGRN_TILE = 512


def _hgrn_prompt_kernel(q_ref, fz_ref, i_ref, g_ref, lb_ref, ng_ref, o_ref, st_ref, state_ref):
    t = pl.program_id(1)
    c_rows = HGRN_CHUNK

    @pl.when(t == 0)
    def _():
        state_ref[...] = jnp.zeros(state_ref.shape, f32)

    row = lax.broadcasted_iota(i32, (c_rows, c_rows), 0)
    col = lax.broadcasted_iota(i32, (c_rows, c_rows), 1)
    causal = col <= row
    cum = jnp.where(causal, 1.0, 0.0)
    cum_sub = jnp.where(col < (row // HGRN_SUB) * HGRN_SUB, 1.0, 0.0)
    cum_both = jnp.concatenate([cum, cum_sub], axis=0)

    def chunk(c, carry):
        rows = pl.ds(pl.multiple_of(c * c_rows, c_rows), c_rows)
        lb_all = lb_ref[...]
        f_all = lb_all + (1.0 - lb_all) * _sigmoid(fz_ref[rows, :])
        both_all = jnp.dot(cum_both, jnp.log(f_all), precision=HI, preferred_element_type=f32)
        for h in range(HGRN_HEADS):
            lanes = slice(h * HGRN_DK, (h + 1) * HGRN_DK)
            ng = ng_ref[:, lanes]
            q = q_ref[rows, lanes]
            v = i_ref[rows, lanes]
            gg = g_ref[rows, lanes]
            kk = 1.0 - f_all[:, lanes]
            b = both_all[:c_rows, lanes]
            ref_row = both_all[c_rows:, lanes]
            st = state_ref[h]
            o = lax.dot_general((q * jnp.exp(b)).astype(bf16), st.astype(bf16), NT_DIMS, preferred_element_type=f32)
            qh = (q * jnp.exp(b - ref_row)).astype(bf16)
            parts = []
            for s in range(c_rows // HGRN_SUB):
                ref_s = ref_row[s * HGRN_SUB:s * HGRN_SUB + 1, :]
                kh = (kk * jnp.exp(jnp.minimum(ref_s - b, EXP_CLAMP))).astype(bf16)
                parts.append(lax.dot_general(qh[s * HGRN_SUB:(s + 1) * HGRN_SUB], kh, NT_DIMS,
                                             preferred_element_type=f32))
            att = jnp.where(causal, jnp.concatenate(parts, axis=0), 0.0)
            o = o + jnp.dot(att.astype(bf16), v.astype(bf16), preferred_element_type=f32)
            b_last = b[c_rows - 1:c_rows, :]
            kd = (kk * jnp.exp(b_last - b)).astype(bf16)
            state_ref[h] = st * jnp.exp(b_last) + lax.dot_general(v.astype(bf16), kd, TN_DIMS,
                                                                   preferred_element_type=f32)
            ms = jnp.mean(o * o, axis=1, keepdims=True)
            o_ref[rows, lanes] = o * lax.rsqrt(ms + RMS_EPS) * ng * (gg * _sigmoid(gg))
        return carry

    lax.fori_loop(0, HGRN_TILE // c_rows, chunk, 0)

    @pl.when(t == pl.num_programs(1) - 1)
    def _():
        for h in range(HGRN_HEADS):
            st_ref[0, h] = state_ref[h].T


def _hgrn_prompt(hproj, batch, seq, lb, norm_g):
    nt = seq // HGRN_TILE
    width = HGRN_HEADS * HGRN_DK
    col = lambda j: pl.BlockSpec((HGRN_TILE, width), lambda b, t, j=j: (b * nt + t, j))
    vec = pl.BlockSpec((1, width), lambda b, t: (0, 0))
    return pl.pallas_call(
        _hgrn_prompt_kernel,
        grid=(batch, nt),
        in_specs=[col(0), col(1), col(2), col(3), vec, vec],
        out_specs=[pl.BlockSpec((HGRN_TILE, width), lambda b, t: (b * nt + t, 0)),
                   pl.BlockSpec((1, HGRN_HEADS, HGRN_DK, HGRN_DK), lambda b, t: (b, 0, 0, 0))],
        out_shape=[jax.ShapeDtypeStruct((batch * seq, width), f32),
                   jax.ShapeDtypeStruct((batch, HGRN_HEADS, HGRN_DK, HGRN_DK), f32)],
        scratch_shapes=[pltpu.VMEM((HGRN_HEADS, HGRN_DK, HGRN_DK), f32)],
        compiler_params=_params(("parallel", "arbitrary")),
        name="hgrn_prompt",
    )(hproj, hproj, hproj, hproj, lb.reshape(1, width), norm_g.reshape(1, width))


def _hgrn_sample_kernel(qc_ref, fzc_ref, lbc_ref, i_ref, g_ref, ng_ref, st_ref, o_ref, nst_ref):
    for h in range(HGRN_HEADS):
        lb = lbc_ref[h]
        f = lb + (1.0 - lb) * _sigmoid(fzc_ref[0, h])
        st = f * st_ref[0, h] + (1.0 - f) * i_ref[0, h]
        nst_ref[0, h] = st
        o = jnp.sum(qc_ref[0, h] * st, axis=0, keepdims=True)
        ms = jnp.mean(o * o, axis=1, keepdims=True)
        gg = g_ref[0, h]
        o_ref[0, h] = o * lax.rsqrt(ms + RMS_EPS) * ng_ref[h] * (gg * _sigmoid(gg))


def _hgrn_sample(hs, state, lb, norm_g):
    db = hs.shape[0]
    width = HGRN_HEADS * HGRN_DK
    colv = lambda x: x.reshape(db, HGRN_HEADS, HGRN_DK, 1)
    rowv = lambda x: x.reshape(db, HGRN_HEADS, 1, HGRN_DK)
    cspec = pl.BlockSpec((1, HGRN_HEADS, HGRN_DK, 1), lambda b: (b, 0, 0, 0))
    rspec = pl.BlockSpec((1, HGRN_HEADS, 1, HGRN_DK), lambda b: (b, 0, 0, 0))
    sspec = pl.BlockSpec((1, HGRN_HEADS, HGRN_DK, HGRN_DK), lambda b: (b, 0, 0, 0))
    o, nst = pl.pallas_call(
        _hgrn_sample_kernel,
        grid=(db,),
        in_specs=[cspec, cspec, pl.BlockSpec((HGRN_HEADS, HGRN_DK, 1), lambda b: (0, 0, 0)), rspec, rspec,
                  pl.BlockSpec((HGRN_HEADS, 1, HGRN_DK), lambda b: (0, 0, 0)), sspec],
        out_specs=[rspec, sspec],
        out_shape=[jax.ShapeDtypeStruct((db, HGRN_HEADS, 1, HGRN_DK), f32),
                   jax.ShapeDtypeStruct((db, HGRN_HEADS, HGRN_DK, HGRN_DK), f32)],
        compiler_params=_params(("parallel",)),
        name="hgrn_sample",
    )(colv(hs[:, :width]), colv(hs[:, width:2 * width]), lb.reshape(HGRN_HEADS, HGRN_DK, 1),
      rowv(hs[:, 2 * width:3 * width]), rowv(hs[:, 3 * width:]), norm_g.reshape(HGRN_HEADS, 1, HGRN_DK), state)
    return o.reshape(db, width), nst


PEER_EBLK = 1024
PEER_RANKS = PEER_TOPK + 1
PEER_VROWS = 24
PEER_SUB = 8
PEER_ROWS = 16


def _extract_max(tiles):
    mx = functools.reduce(jnp.maximum, tiles)
    mx = jnp.max(mx, axis=0, keepdims=True)
    return mx, [jnp.where(x == mx, FLOOR, x) for x in tiles]


def _peer_kernel(x_ref, q_ref, k1_ref, k2_ref, u_ref, v_ref, g_ref, b_ref, o_ref,
                 xt_ref, acc_ref, act_ref, p_ref, a_ref, bw_ref, c_ref, s2_ref, v1_ref, v2_ref, ab_ref, cb_ref):
    e = pl.program_id(1)
    tt = x_ref.shape[0]
    n_lane_chunks = tt // LANES
    sub = PEER_SUB
    n_sub = PEER_NKEYS // sub
    keys_per_step = PEER_EBLK // PEER_NKEYS

    @pl.when(e == 0)
    def _():
        xt_ref[...] = x_ref[...].T.astype(bf16)
        acc_ref[...] = jnp.zeros(acc_ref.shape, f32)
        floor_rows = jnp.full((PEER_VROWS, LANES), FLOOR, f32)
        for h in range(PEER_HEADS):
            q1 = q_ref[:, (2 * h) * PEER_DHALF:(2 * h + 1) * PEER_DHALF]
            q2 = q_ref[:, (2 * h + 1) * PEER_DHALF:(2 * h + 2) * PEER_DHALF]
            s1 = lax.dot_general(k1_ref[h], q1, NT_DIMS, precision=HI, preferred_element_type=f32)
            c_ref[h] = s1.reshape(n_sub, sub, tt)
            s2_ref[h] = lax.dot_general(k2_ref[h], q2, NT_DIMS, precision=HI, preferred_element_type=f32)

        def per_head(h, carry):
            for lc in range(n_lane_chunks):
                lanes = slice(lc * LANES, (lc + 1) * LANES)
                for side, vals_ref in enumerate((v1_ref, v2_ref)):
                    vals_ref[...] = floor_rows
                    if side == 0:
                        tiles = [c_ref[h, k, :, lanes] for k in range(n_sub)]
                    else:
                        tiles = [s2_ref[h, k * sub:(k + 1) * sub, lanes] for k in range(n_sub)]
                    for r in range(PEER_RANKS):
                        mx, tiles = _extract_max(tiles)
                        vals_ref[r:r + 1, :] = mx
                cands = [v1_ref[0:1, :] + v2_ref[r:r + sub, :] for r in range(0, PEER_VROWS, sub)]
                cands += [v1_ref[r:r + 1, :] + v2_ref[0:sub, :] for r in range(1, sub)]
                cands += [v1_ref[r:r + sub, :] + v2_ref[0:1, :] for r in range(sub, PEER_VROWS, sub)]
                best = v1_ref[0:1, :] + v2_ref[0:1, :]
                zsum = jnp.zeros((1, LANES), f32)
                kth = best
                for r in range(PEER_TOPK):
                    kth, cands = _extract_max(cands)
                    zsum = zsum + jnp.exp(kth - best)
                nxt, _ = _extract_max(cands)
                thresh = 0.5 * (kth + nxt)
                s1 = c_ref[h, :, :, lanes]
                a_ref[h, :, :, lanes] = jnp.exp(s1 - v1_ref[0:1, :]) / zsum
                bw_ref[h, :, lanes] = jnp.exp(s2_ref[h, :, lanes] - v2_ref[0:1, :])
                c_ref[h, :, :, lanes] = thresh - s1
            return carry

        lax.fori_loop(0, PEER_HEADS, per_head, 0)

    act_ref[...] = jnp.dot(u_ref[...], xt_ref[...], preferred_element_type=f32)
    for h in range(PEER_HEADS):
        for ii in range(keys_per_step):
            ab_ref[ii, h] = jnp.broadcast_to(a_ref[h, e, ii:ii + 1, :], (sub, tt))
            cb_ref[ii, h] = jnp.broadcast_to(c_ref[h, e, ii:ii + 1, :], (sub, tt))

    def per_second_keys(k, carry):
        r0 = pl.multiple_of(k * PEER_ROWS, PEER_ROWS)
        for lc in range(n_lane_chunks):
            lanes = slice(lc * LANES, (lc + 1) * LANES)
            w = [jnp.zeros((PEER_ROWS // sub, sub, LANES), f32) for _ in range(keys_per_step)]
            for h in range(PEER_HEADS):
                s2 = s2_ref[h, pl.ds(r0, PEER_ROWS), lanes].reshape(PEER_ROWS // sub, sub, LANES)
                bw = bw_ref[h, pl.ds(r0, PEER_ROWS), lanes].reshape(PEER_ROWS // sub, sub, LANES)
                for ii in range(keys_per_step):
                    hit = s2 >= cb_ref[ii, h, :, lanes][None]
                    w[ii] = w[ii] + ab_ref[ii, h, :, lanes][None] * jnp.where(hit, bw, 0.0)
            for ii in range(keys_per_step):
                rows = pl.ds(pl.multiple_of(ii * PEER_NKEYS + r0, PEER_ROWS), PEER_ROWS)
                act = act_ref[rows, lanes]
                gelu = 0.5 * act * (1.0 + lax.erf(act * (2.0 ** -0.5)))
                p_ref[rows, lanes] = (w[ii].reshape(PEER_ROWS, LANES) * gelu).astype(bf16)
        return carry

    lax.fori_loop(0, PEER_NKEYS // PEER_ROWS, per_second_keys, 0)
    acc_ref[...] += lax.dot_general(p_ref[...], v_ref[...], TN_DIMS, preferred_element_type=f32)

    @pl.when(e == pl.num_programs(1) - 1)
    def _():
        o_ref[...] = _layer_norm(DN_ALPHA * x_ref[...] + acc_ref[...], g_ref[...], b_ref[...])


def _peer_ln(x, wq, keys, u, v, g, b, *, tt=TOKEN_TILE):
    m, d = x.shape
    n_exp = u.shape[0]
    q = _mm(x, wq, split=True)
    nq = q.shape[1]
    return pl.pallas_call(
        _peer_kernel,
        grid=(m // tt, n_exp // PEER_EBLK),
        in_specs=[pl.BlockSpec((tt, d), lambda t, e: (t, 0)),
                  pl.BlockSpec((tt, nq), lambda t, e: (t, 0)),
                  pl.BlockSpec((PEER_HEADS, PEER_NKEYS, PEER_DHALF), lambda t, e: (0, 0, 0)),
                  pl.BlockSpec((PEER_HEADS, PEER_NKEYS, PEER_DHALF), lambda t, e: (0, 0, 0)),
                  pl.BlockSpec((PEER_EBLK, d), lambda t, e: (e, 0)),
                  pl.BlockSpec((PEER_EBLK, d), lambda t, e: (e, 0)),
                  pl.BlockSpec((1, d), lambda t, e: (0, 0)),
                  pl.BlockSpec((1, d), lambda t, e: (0, 0))],
        out_specs=pl.BlockSpec((tt, d), lambda t, e: (t, 0)),
        out_shape=jax.ShapeDtypeStruct((m, d), f32),
        scratch_shapes=[pltpu.VMEM((d, tt), bf16), pltpu.VMEM((tt, d), f32),
                        pltpu.VMEM((PEER_EBLK, tt), f32), pltpu.VMEM((PEER_EBLK, tt), bf16)]
                       + [pltpu.VMEM((PEER_HEADS, PEER_NKEYS // PEER_SUB, PEER_SUB, tt), f32),
                          pltpu.VMEM((PEER_HEADS, PEER_NKEYS, tt), f32),
                          pltpu.VMEM((PEER_HEADS, PEER_NKEYS // PEER_SUB, PEER_SUB, tt), f32),
                          pltpu.VMEM((PEER_HEADS, PEER_NKEYS, tt), f32)]
                       + [pltpu.VMEM((PEER_VROWS, LANES), f32)] * 2
                       + [pltpu.VMEM((PEER_EBLK // PEER_NKEYS, PEER_HEADS, PEER_SUB, tt), f32)] * 2,
        compiler_params=_params(("parallel", "arbitrary")),
        name="peer",
    )(x, q, keys[0], keys[1], u.astype(bf16), v.astype(bf16), g.reshape(1, d), b.reshape(1, d))


def kernel(x_prompt, x_sample, cache_k, cache_v, state_conv, state_hgrn, page_table, w_in_even, conv_w, conv_b, conv_ln_g, conv_ln_b, w_out_even, w_in_odd, hgrn_lb_logits, hgrn_norm_g, w_out_odd, ln_g, ln_b, peer_wq, peer_keys, peer_u, peer_v):
    batch, seq, d = x_prompt.shape
    db = x_sample.shape[0]
    n_prompt = batch * seq
    rows = n_prompt + TOKEN_TILE
    srows = slice(n_prompt, n_prompt + db)
    pad_rows = lambda a: jnp.concatenate([a, jnp.zeros((rows - a.shape[0], a.shape[1]), a.dtype)], axis=0)

    lb_p = jax.nn.softmax(hgrn_lb_logits.astype(f32), axis=0)
    lb_all = jnp.cumsum(lb_p, axis=0) - lb_p[0]

    x = pad_rows(jnp.concatenate([x_prompt.reshape(n_prompt, d), x_sample.reshape(db, d)], axis=0))
    kp_l, vp_l, ks_l, vs_l, cp_l, cs_l, hp_l, hs_l = [], [], [], [], [], [], [], []
    for l in range(DEPTH):
        if l % 2 == 0:
            e = l // 2
            w_in = w_in_even[e]
            qk = _mm(x, w_in[:, :2 * MOBA_WIDTH], split=True)
            vag = _mm(x, w_in[:, 2 * MOBA_WIDTH:])
            k_all, v_all = qk[:, MOBA_WIDTH:], vag[:, :MOBA_WIDTH]
            kp_l.append(k_all[:n_prompt].reshape(batch, seq // PAGE_SIZE, PAGE_SIZE, MOBA_HEADS, MOBA_HEAD_DIM))
            vp_l.append(v_all[:n_prompt].reshape(batch, seq // PAGE_SIZE, PAGE_SIZE, MOBA_HEADS, MOBA_HEAD_DIM))
            ks_l.append(k_all[srows].reshape(db, 1, MOBA_HEADS, MOBA_HEAD_DIM))
            vs_l.append(v_all[srows].reshape(db, 1, MOBA_HEADS, MOBA_HEAD_DIM))
            att_p = _moba_prompt(qk, vag, batch, seq)
            heads = lambda a: a.reshape(db, MOBA_HEADS, MOBA_HEAD_DIM)
            att_s = _moba_sample(heads(qk[srows, :MOBA_WIDTH]), heads(k_all[srows]), heads(v_all[srows]),
                                 cache_k[e].transpose(0, 2, 3, 1), cache_v[e].transpose(0, 2, 3, 1), page_table)
            att_s = att_s[:, :, :MOBA_HEAD_DIM].reshape(db, MOBA_WIDTH)
            cprm = (conv_w[e], conv_b[e], conv_ln_g[e], conv_ln_b[e])
            cy_p, buf_p = _conv_prompt(vag, batch, seq, *cprm)
            cy_s, buf_s = _conv_sample(vag[srows, MOBA_WIDTH:MOBA_WIDTH + CONV_CH],
                                       vag[srows, MOBA_WIDTH + CONV_CH:], state_conv[e], *cprm)
            cp_l.append(buf_p)
            cs_l.append(buf_s)
            mix = pad_rows(jnp.concatenate([jnp.concatenate([att_p, cy_p], axis=1),
                                            jnp.concatenate([att_s, cy_s], axis=1)], axis=0))
            w_out = w_out_even[e]
        else:
            oi = l // 2
            hproj = _mm(x, w_in_odd[oi], tm=256)
            o_p, st_p = _hgrn_prompt(hproj, batch, seq, lb_all[l], hgrn_norm_g[oi])
            o_s, st_s = _hgrn_sample(hproj[srows], state_hgrn[oi].astype(f32), lb_all[l], hgrn_norm_g[oi])
            hp_l.append(st_p)
            hs_l.append(st_s)
            mix = pad_rows(jnp.concatenate([o_p, o_s], axis=0))
            w_out = w_out_odd[oi]
        x = _mm_res_ln(mix, w_out, x, ln_g[l, 0], ln_b[l, 0])
        x = _peer_ln(x, peer_wq[l], peer_keys[l], peer_u[l], peer_v[l], ln_g[l, 1], ln_b[l, 1])
    y_prompt = x[:n_prompt].reshape(batch, seq, d)
    y_sample = x[srows].reshape(db, 1, d)
    return (y_prompt, y_sample, jnp.stack(kp_l), jnp.stack(vp_l), jnp.stack(ks_l), jnp.stack(vs_l),
            jnp.stack(cp_l), jnp.stack(cs_l), jnp.stack(hp_l), jnp.stack(hs_l))
```

```python
import functools
import math

import jax
import jax.numpy as jnp
from jax import lax
from jax.experimental import pallas as pl
from jax.experimental.pallas import tpu as pltpu

f32 = jnp.float32
bf16 = jnp.bfloat16
i32 = jnp.int32
HI = lax.Precision.HIGHEST

D_MODEL = 1024
DEPTH = 2
PAGE_SIZE = 128
MOBA_HEADS = 8
MOBA_HEAD_DIM = 64
MOBA_WIDTH = MOBA_HEADS * MOBA_HEAD_DIM
MOBA_BLOCK = 256
MOBA_TOPK = 3
MOBA_SCALE = MOBA_HEAD_DIM ** -0.5
CONV_CH = D_MODEL // 2
CONV_WIDTH = 31
HGRN_HEADS = 8
HGRN_DK = D_MODEL // HGRN_HEADS
HGRN_CHUNK = 64
HGRN_SUB = 16
PEER_HEADS = 8
PEER_NKEYS = 128
PEER_DHALF = 128
PEER_TOPK = 16
LN_EPS = 1e-5
RMS_EPS = 1e-6
DN_ALPHA = (2 * DEPTH) ** 0.25

LANES = 128
TOKEN_TILE = 512
MASK_NEG = -1e30
FLOOR = -3e38
EXP_CLAMP = 60.0
VMEM_LIMIT = 56 << 20

NT_DIMS = (((1,), (1,)), ((), ()))
TN_DIMS = (((0,), (0,)), ((), ()))


def _params(semantics, vmem=VMEM_LIMIT):
    return pltpu.CompilerParams(dimension_semantics=semantics, vmem_limit_bytes=vmem)


def _layer_norm(y, g, b):
    mu = jnp.mean(y, axis=-1, keepdims=True)
    d = y - mu
    var = jnp.mean(d * d, axis=-1, keepdims=True)
    return d * lax.rsqrt(var + LN_EPS) * g + b


def _sigmoid(x):
    return 1.0 / (1.0 + jnp.exp(-x))


def _mm_kernel(x_ref, *refs, n_chunk, split):
    if split:
        wh_ref, wl_ref, o_ref = refs
    else:
        wh_ref, o_ref = refs
    x = x_ref[...]
    xh = x.astype(bf16)
    if split:
        xl = (x - xh.astype(f32)).astype(bf16)
    for j in range(0, o_ref.shape[1], n_chunk):
        wh = wh_ref[:, j:j + n_chunk]
        acc = jnp.dot(xh, wh, preferred_element_type=f32)
        if split:
            acc = acc + jnp.dot(xl, wh, preferred_element_type=f32)
            acc = acc + jnp.dot(xh, wl_ref[:, j:j + n_chunk], preferred_element_type=f32)
        o_ref[:, j:j + n_chunk] = acc


def _mm(x, w, *, split=False, tm=TOKEN_TILE):
    m, k = x.shape
    n = w.shape[1]
    wh = w.astype(bf16)
    ws = [wh] + ([(w - wh.astype(f32)).astype(bf16)] if split else [])
    return pl.pallas_call(
        functools.partial(_mm_kernel, n_chunk=512, split=split),
        grid=(m // tm,),
        in_specs=[pl.BlockSpec((tm, k), lambda i: (i, 0))] + [pl.BlockSpec((k, n), lambda i: (0, 0))] * len(ws),
        out_specs=pl.BlockSpec((tm, n), lambda i: (i, 0)),
        out_shape=jax.ShapeDtypeStruct((m, n), f32),
        compiler_params=_params(("parallel",)),
        name="proj_split" if split else "proj",
    )(x, *ws)


def _mm_res_ln_kernel(a_ref, w_ref, x_ref, g_ref, b_ref, o_ref):
    acc = jnp.dot(a_ref[...].astype(bf16), w_ref[...], preferred_element_type=f32)
    o_ref[...] = _layer_norm(DN_ALPHA * x_ref[...] + acc, g_ref[...], b_ref[...])


def _mm_res_ln(a, w, x, g, b, *, tm=TOKEN_TILE):
    m, k = a.shape
    n = w.shape[1]
    return pl.pallas_call(
        _mm_res_ln_kernel,
        grid=(m // tm,),
        in_specs=[pl.BlockSpec((tm, k), lambda i: (i, 0)), pl.BlockSpec((k, n), lambda i: (0, 0)),
                  pl.BlockSpec((tm, n), lambda i: (i, 0)), pl.BlockSpec((1, n), lambda i: (0, 0)),
                  pl.BlockSpec((1, n), lambda i: (0, 0))],
        out_specs=pl.BlockSpec((tm, n), lambda i: (i, 0)),
        out_shape=jax.ShapeDtypeStruct((m, n), f32),
        compiler_params=_params(("parallel",)),
        name="out_proj_ln",
    )(a, w.astype(bf16), x, g.reshape(1, n), b.reshape(1, n))


CONV_TILE = 256
CONV_ROWS = 64
CONV_HIST = 32


def _conv_prompt_kernel(a_ref, gt_ref, w_ref, cb_ref, g_ref, b_ref, y_ref, st_ref, buf_ref):
    t = pl.program_id(1)

    @pl.when(t == 0)
    def _():
        buf_ref[0:CONV_HIST, :] = jnp.zeros((CONV_HIST, CONV_CH), f32)

    buf_ref[CONV_HIST:CONV_HIST + CONV_TILE, :] = a_ref[...] * _sigmoid(gt_ref[...])
    first = CONV_HIST - (CONV_WIDTH - 1)
    for r in range(0, CONV_TILE, CONV_ROWS):
        acc = jnp.zeros((CONV_ROWS, CONV_CH), f32) + cb_ref[...]
        for j in range(CONV_WIDTH):
            acc = acc + w_ref[j:j + 1, :] * buf_ref[r + first + j:r + first + j + CONV_ROWS, :]
        y = _layer_norm(acc, g_ref[...], b_ref[...])
        y_ref[r:r + CONV_ROWS, :] = y * _sigmoid(y)
    tail = buf_ref[CONV_TILE:CONV_TILE + CONV_HIST, :]
    buf_ref[0:CONV_HIST, :] = tail

    @pl.when(t == pl.num_programs(1) - 1)
    def _():
        st_ref[0] = tail


def _conv_prompt(vag, batch, seq, w, cb, g, b):
    nt = seq // CONV_TILE
    wpad = jnp.zeros((CONV_HIST, CONV_CH), f32).at[:CONV_WIDTH].set(w)
    row = lambda v: v.reshape(1, CONV_CH)
    y, st = pl.pallas_call(
        _conv_prompt_kernel,
        grid=(batch, nt),
        in_specs=[pl.BlockSpec((CONV_TILE, CONV_CH), lambda bi, t: (bi * nt + t, 1)),
                  pl.BlockSpec((CONV_TILE, CONV_CH), lambda bi, t: (bi * nt + t, 2)),
                  pl.BlockSpec((CONV_HIST, CONV_CH), lambda bi, t: (0, 0))]
                 + [pl.BlockSpec((1, CONV_CH), lambda bi, t: (0, 0))] * 3,
        out_specs=[pl.BlockSpec((CONV_TILE, CONV_CH), lambda bi, t: (bi * nt + t, 0)),
                   pl.BlockSpec((1, CONV_HIST, CONV_CH), lambda bi, t: (bi, 0, 0))],
        out_shape=[jax.ShapeDtypeStruct((batch * seq, CONV_CH), f32),
                   jax.ShapeDtypeStruct((batch, CONV_HIST, CONV_CH), f32)],
        scratch_shapes=[pltpu.VMEM((CONV_TILE + CONV_HIST, CONV_CH), f32)],
        compiler_params=_params(("arbitrary", "arbitrary")),
        name="conv_prompt",
    )(vag, vag, wpad, row(cb), row(g), row(b))
    return y, st[:, CONV_HIST - (CONV_WIDTH - 1):]


def _conv_sample_kernel(a_ref, gt_ref, st_ref, w_ref, cb_ref, g_ref, b_ref, y_ref, nst_ref):
    u = a_ref[...] * _sigmoid(gt_ref[...])
    acc = cb_ref[...] + w_ref[CONV_WIDTH - 1:CONV_WIDTH, :] * u
    for j in range(CONV_WIDTH - 1):
        acc = acc + w_ref[j:j + 1, :] * st_ref[j]
    y = _layer_norm(acc, g_ref[...], b_ref[...])
    y_ref[...] = y * _sigmoid(y)
    for j in range(CONV_WIDTH - 2):
        nst_ref[j] = st_ref[j + 1]
    nst_ref[CONV_WIDTH - 2] = u


def _conv_sample(a, gt, state, w, cb, g, b):
    db = a.shape[0]
    wpad = jnp.zeros((CONV_HIST, CONV_CH), f32).at[:CONV_WIDTH].set(w)
    row = lambda v: v.reshape(1, CONV_CH)
    y, nst = pl.pallas_call(
        _conv_sample_kernel,
        out_shape=[jax.ShapeDtypeStruct((db, CONV_CH), f32),
                   jax.ShapeDtypeStruct((CONV_WIDTH - 1, db, CONV_CH), f32)],
        name="conv_sample",
    )(a, gt, state.transpose(1, 0, 2), wpad, row(cb), row(g), row(b))
    return y, nst.transpose(1, 0, 2)


GATE_ROWS = 1024
ATTN_KEYS = 2 * MOBA_BLOCK


def _block_sum_kernel(k_ref, o_ref):
    o_ref[0] = jnp.sum(k_ref[...], axis=0, keepdims=True)


def _block_sums(qk, batch, seq):
    nb = seq // MOBA_BLOCK
    return pl.pallas_call(
        _block_sum_kernel,
        grid=(batch * nb,),
        in_specs=[pl.BlockSpec((MOBA_BLOCK, MOBA_WIDTH), lambda i: (i, 1))],
        out_specs=pl.BlockSpec((1, 1, MOBA_WIDTH), lambda i: (i, 0, 0)),
        out_shape=jax.ShapeDtypeStruct((batch * nb, 1, MOBA_WIDTH), f32),
        compiler_params=_params(("parallel",)),
        name="moba_block_sums",
    )(qk)


def _moba_gate_kernel(q_ref, k_ref, v_ref, ks_ref, qa_ref, ka_ref, vb_ref, *, nb):
    t = pl.program_id(2)
    lane = lax.broadcasted_iota(i32, (MOBA_BLOCK, LANES), 1)
    blk = lane - MOBA_HEAD_DIM
    km_lane = lax.broadcasted_iota(i32, (nb, LANES), 1)
    ksum = ks_ref[0] * (1.0 / MOBA_BLOCK)
    pad_top = jnp.zeros((MOBA_HEAD_DIM, LANES), f32)
    pad_bot = jnp.zeros((LANES - MOBA_HEAD_DIM - nb, LANES), f32)
    for c in range(GATE_ROWS // MOBA_BLOCK):
        own = t * (GATE_ROWS // MOBA_BLOCK) + c
        rows = slice(c * MOBA_BLOCK, (c + 1) * MOBA_BLOCK)
        q2 = q_ref[rows, :]
        k2 = k_ref[rows, :]
        vb_ref[rows, :] = v_ref[rows, :].astype(bf16)
        cand = (blk >= 0) & (blk < own)
        for j in range(2):
            head = (km_lane >= j * MOBA_HEAD_DIM) & (km_lane < (j + 1) * MOBA_HEAD_DIM)
            km = jnp.concatenate([pad_top, jnp.where(head, ksum, 0.0), pad_bot], axis=0)
            gate = lax.dot_general(q2, km, NT_DIMS, precision=HI, preferred_element_type=f32)
            masked = jnp.where(cand, gate, FLOOR)
            sel = jnp.zeros((MOBA_BLOCK, LANES), f32)
            for _ in range(MOBA_TOPK):
                mx = jnp.max(masked, axis=1, keepdims=True)
                hit = (masked == mx) & cand
                sel = jnp.where(hit, 1.0, sel)
                masked = jnp.where(hit, FLOOR, masked)
            bias = jnp.where((sel > 0.0) | (blk == own), 0.0, MASK_NEG)
            qj = q2 if j == 0 else pltpu.roll(q2, MOBA_HEAD_DIM, 1)
            kj = k2 if j == 0 else pltpu.roll(k2, MOBA_HEAD_DIM, 1)
            in_head = lane < MOBA_HEAD_DIM
            in_bias = lane < MOBA_HEAD_DIM + nb
            qa = jnp.where(in_head, qj * MOBA_SCALE, jnp.where(in_bias, bias, 0.0))
            ka = jnp.where(in_head, kj, jnp.where(blk == own, 1.0, 0.0))
            qa_ref[0, j, rows, :] = qa.astype(bf16)
            ka_ref[0, j, rows, :] = ka.astype(bf16)


def _moba_attn_kernel(qa_ref, ka_ref, vb_ref, o_ref):
    i = pl.program_id(1)
    last = i // 2
    row = lax.broadcasted_iota(i32, (MOBA_BLOCK, ATTN_KEYS), 0)
    col = lax.broadcasted_iota(i32, (MOBA_BLOCK, ATTN_KEYS), 1)
    lane = lax.broadcasted_iota(i32, (MOBA_BLOCK, LANES), 1)

    def tile(h, k0, mask):
        s = lax.dot_general(qa_ref[0, h], ka_ref[0, h, pl.ds(k0, ATTN_KEYS), :], NT_DIMS, preferred_element_type=f32)
        return s if mask is None else jnp.where(mask, s, MASK_NEG)

    def pair_values(h, k0):
        return vb_ref[pl.ds(k0, ATTN_KEYS), (h // 2) * LANES:(h // 2 + 1) * LANES]

    k_last = pl.multiple_of(last * ATTN_KEYS, ATTN_KEYS)
    visible = (k_last + col) <= (i * MOBA_BLOCK + row)
    state = []
    for h in range(MOBA_HEADS):
        s = tile(h, k_last, visible)
        m = jnp.max(s, axis=1, keepdims=True)
        p = jnp.exp(s - m)
        state += [m, jnp.sum(p, axis=1, keepdims=True),
                  jnp.dot(p.astype(bf16), pair_values(h, k_last), preferred_element_type=f32)]

    def body(n, carry):
        k0 = pl.multiple_of(n * ATTN_KEYS, ATTN_KEYS)
        out = []
        for h in range(MOBA_HEADS):
            m, l, acc = carry[3 * h:3 * h + 3]
            s = tile(h, k0, None)
            mn = jnp.maximum(m, jnp.max(s, axis=1, keepdims=True))
            alpha = jnp.exp(m - mn)
            p = jnp.exp(s - mn)
            out += [mn, alpha * l + jnp.sum(p, axis=1, keepdims=True),
                    alpha * acc + jnp.dot(p.astype(bf16), pair_values(h, k0), preferred_element_type=f32)]
        return tuple(out)

    final = lax.fori_loop(0, last, body, tuple(state))
    for pair in range(MOBA_HEADS // 2):
        (_, l0, acc0), (_, l1, acc1) = final[6 * pair:6 * pair + 3], final[6 * pair + 3:6 * pair + 6]
        o_ref[:, pair * LANES:(pair + 1) * LANES] = jnp.where(lane < MOBA_HEAD_DIM, acc0 / l0, acc1 / l1)


def _moba_prompt(qk, vag, batch, seq):
    nb = seq // MOBA_BLOCK
    npair = MOBA_WIDTH // LANES
    nt = seq // GATE_ROWS
    ksums = _block_sums(qk, batch, seq).reshape(batch, nb, MOBA_WIDTH)
    qa, ka, vb = pl.pallas_call(
        functools.partial(_moba_gate_kernel, nb=nb),
        grid=(batch, npair, nt),
        in_specs=[pl.BlockSpec((GATE_ROWS, LANES), lambda b, p, t: (b * nt + t, p)),
                  pl.BlockSpec((GATE_ROWS, LANES), lambda b, p, t: (b * nt + t, npair + p)),
                  pl.BlockSpec((GATE_ROWS, LANES), lambda b, p, t: (b * nt + t, p)),
                  pl.BlockSpec((1, nb, LANES), lambda b, p, t: (b, 0, p))],
        out_specs=[pl.BlockSpec((1, 2, GATE_ROWS, LANES), lambda b, p, t: (b, p, t, 0)),
                   pl.BlockSpec((1, 2, GATE_ROWS, LANES), lambda b, p, t: (b, p, t, 0)),
                   pl.BlockSpec((GATE_ROWS, LANES), lambda b, p, t: (b * nt + t, p))],
        out_shape=[jax.ShapeDtypeStruct((batch, MOBA_HEADS, seq, LANES), bf16),
                   jax.ShapeDtypeStruct((batch, MOBA_HEADS, seq, LANES), bf16),
                   jax.ShapeDtypeStruct((batch * seq, MOBA_WIDTH), bf16)],
        compiler_params=_params(("parallel", "parallel", "parallel")),
        name="moba_gate",
    )(qk, qk, vag, ksums)
    return pl.pallas_call(
        _moba_attn_kernel,
        grid=(batch, nb),
        in_specs=[pl.BlockSpec((1, MOBA_HEADS, MOBA_BLOCK, LANES), lambda b, i: (b, 0, i, 0)),
                  pl.BlockSpec((1, MOBA_HEADS, seq, LANES), lambda b, i: (b, 0, 0, 0),
                               pipeline_mode=pl.Buffered(1)),
                  pl.BlockSpec((seq, MOBA_WIDTH), lambda b, i: (b, 0), pipeline_mode=pl.Buffered(1))],
        out_specs=pl.BlockSpec((MOBA_BLOCK, MOBA_WIDTH), lambda b, i: (b * nb + i, 0)),
        out_shape=jax.ShapeDtypeStruct((batch * seq, MOBA_WIDTH), f32),
        compiler_params=_params(("parallel", "arbitrary")),
        name="moba_attn",
    )(qa, ka, vb)


PAGES_PER_STEP = 8
PAGES_PER_BLOCK = MOBA_BLOCK // PAGE_SIZE
BLOCKS_PER_STEP = PAGES_PER_STEP // PAGES_PER_BLOCK


def _moba_sample_stream_kernel(pt_ref, qb_ref, *refs):
    del pt_ref
    k_refs = refs[:PAGES_PER_STEP]
    v_refs = refs[PAGES_PER_STEP:2 * PAGES_PER_STEP]
    gate_ref, m_ref, l_ref, acc_ref = refs[2 * PAGES_PER_STEP:]
    qb = qb_ref[0]
    head = lax.broadcasted_iota(i32, (MOBA_HEADS, LANES), 0)
    tile = (MOBA_HEADS, LANES)
    for jj in range(BLOCKS_PER_STEP):
        pages = range(PAGES_PER_BLOCK * jj, PAGES_PER_BLOCK * (jj + 1))
        raw = [jnp.sum(k_refs[x][0] * qb, axis=1) for x in pages]
        gate = functools.reduce(jnp.add, [jnp.sum(r, axis=1, keepdims=True) for r in raw]) * (1.0 / MOBA_BLOCK)
        m = functools.reduce(jnp.maximum, [jnp.max(r, axis=1, keepdims=True) for r in raw]) * MOBA_SCALE
        l = jnp.zeros((MOBA_HEADS, 1), f32)
        pv = jnp.zeros((MOBA_HEADS, MOBA_WIDTH), f32)
        for r, x in zip(raw, pages):
            p = jnp.exp(r * MOBA_SCALE - m)
            l = l + jnp.sum(p, axis=1, keepdims=True)
            vt = v_refs[x][0].reshape(MOBA_WIDTH, PAGE_SIZE).astype(bf16)
            pv = pv + lax.dot_general(p.astype(bf16), vt, NT_DIMS, preferred_element_type=f32)
        acc = jnp.zeros(tile, f32)
        for pair in range(MOBA_WIDTH // LANES):
            both = pv[:, pair * LANES:(pair + 1) * LANES]
            acc = acc + jnp.where(head == 2 * pair, both, 0.0)
            acc = acc + jnp.where(head == 2 * pair + 1, pltpu.roll(both, MOBA_HEAD_DIM, 1), 0.0)
        gate_ref[0, 0, jj] = jnp.broadcast_to(gate, tile)
        m_ref[0, 0, jj] = jnp.broadcast_to(m, tile)
        l_ref[0, 0, jj] = jnp.broadcast_to(l, tile)
        acc_ref[0, 0, jj] = acc


def _moba_sample_combine_kernel(gate_ref, m_ref, l_ref, acc_ref, q_ref, kn_ref, vn_ref, o_ref):
    db, nb, nh, width = gate_ref.shape
    masked = gate_ref[...]
    sel = jnp.zeros((db, nb, nh, width), f32)
    for _ in range(MOBA_TOPK):
        mx = jnp.max(masked, axis=1, keepdims=True)
        hit = masked == mx
        sel = jnp.where(hit, 1.0, sel)
        masked = jnp.where(hit, FLOOR, masked)
    chosen = sel > 0.0
    s_self = jnp.sum(q_ref[...] * kn_ref[...], axis=-1, keepdims=True) * MOBA_SCALE
    m = m_ref[...]
    top = jnp.maximum(jnp.max(jnp.where(chosen, m, FLOOR), axis=1), s_self)
    w = jnp.where(chosen, jnp.exp(m - top[:, None]), 0.0)
    w_self = jnp.exp(s_self - top)
    denom = jnp.sum(w * l_ref[...], axis=1) + w_self
    num = jnp.sum(w * acc_ref[...], axis=1) + w_self * vn_ref[...]
    o_ref[...] = num / denom


def _moba_sample(q, kn, vn, k_pool, v_pool, page_table):
    db, n_pages = page_table.shape
    nb = n_pages // PAGES_PER_BLOCK
    ng = n_pages // PAGES_PER_STEP
    page_block = (1, MOBA_HEADS, MOBA_HEAD_DIM, PAGE_SIZE)
    page_spec = lambda x: pl.BlockSpec(page_block, lambda b, g, pt, x=x: (pt[b, g * PAGES_PER_STEP + x], 0, 0, 0))
    stat = pl.BlockSpec((1, 1, BLOCKS_PER_STEP, MOBA_HEADS, LANES), lambda b, g, pt: (b, g, 0, 0, 0))
    stat_shape = jax.ShapeDtypeStruct((db, ng, BLOCKS_PER_STEP, MOBA_HEADS, LANES), f32)
    qb = jnp.broadcast_to(q[..., None], (db,) + page_block[1:])
    stats = pl.pallas_call(
        _moba_sample_stream_kernel,
        grid_spec=pltpu.PrefetchScalarGridSpec(
            num_scalar_prefetch=1,
            grid=(db, ng),
            in_specs=[pl.BlockSpec(page_block, lambda b, g, pt: (b, 0, 0, 0))]
                     + [page_spec(x) for x in range(PAGES_PER_STEP)] * 2,
            out_specs=[stat] * 4),
        out_shape=[stat_shape] * 4,
        compiler_params=_params(("parallel", "parallel")),
        name="moba_sample_stream",
    )(page_table, qb, *([k_pool] * PAGES_PER_STEP), *([v_pool] * PAGES_PER_STEP))
    gate, m, l, acc = (a.reshape(db, nb, MOBA_HEADS, LANES) for a in stats)
    vn_wide = jnp.pad(vn, ((0, 0), (0, 0), (0, LANES - MOBA_HEAD_DIM)))
    return pl.pallas_call(
        _moba_sample_combine_kernel,
        out_shape=jax.ShapeDtypeStruct((db, MOBA_HEADS, LANES), f32),
        compiler_params=pltpu.CompilerParams(vmem_limit_bytes=VMEM_LIMIT),
        name="moba_sample_combine",
    )(gate, m, l, acc, q, kn, vn_wide)


HGRN_TILE = 512


def _hgrn_prompt_kernel(q_ref, fz_ref, i_ref, g_ref, lb_ref, ng_ref, o_ref, st_ref, state_ref):
    t = pl.program_id(1)
    c_rows = HGRN_CHUNK

    @pl.when(t == 0)
    def _():
        state_ref[...] = jnp.zeros(state_ref.shape, f32)

    row = lax.broadcasted_iota(i32, (c_rows, c_rows), 0)
    col = lax.broadcasted_iota(i32, (c_rows, c_rows), 1)
    causal = col <= row
    cum = jnp.where(causal, 1.0, 0.0)
    cum_sub = jnp.where(col < (row // HGRN_SUB) * HGRN_SUB, 1.0, 0.0)
    cum_both = jnp.concatenate([cum, cum_sub], axis=0)

    def chunk(c, carry):
        rows = pl.ds(pl.multiple_of(c * c_rows, c_rows), c_rows)
        lb_all = lb_ref[...]
        f_all = lb_all + (1.0 - lb_all) * _sigmoid(fz_ref[rows, :])
        both_all = jnp.dot(cum_both, jnp.log(f_all), precision=HI, preferred_element_type=f32)
        for h in range(HGRN_HEADS):
            lanes = slice(h * HGRN_DK, (h + 1) * HGRN_DK)
            ng = ng_ref[:, lanes]
            q = q_ref[rows, lanes]
            v = i_ref[rows, lanes]
            gg = g_ref[rows, lanes]
            kk = 1.0 - f_all[:, lanes]
            b = both_all[:c_rows, lanes]
            ref_row = both_all[c_rows:, lanes]
            st = state_ref[h]
            o = lax.dot_general((q * jnp.exp(b)).astype(bf16), st.astype(bf16), NT_DIMS, preferred_element_type=f32)
            qh = (q * jnp.exp(b - ref_row)).astype(bf16)
            parts = []
            for s in range(c_rows // HGRN_SUB):
                ref_s = ref_row[s * HGRN_SUB:s * HGRN_SUB + 1, :]
                kh = (kk * jnp.exp(jnp.minimum(ref_s - b, EXP_CLAMP))).astype(bf16)
                parts.append(lax.dot_general(qh[s * HGRN_SUB:(s + 1) * HGRN_SUB], kh, NT_DIMS,
                                             preferred_element_type=f32))
            att = jnp.where(causal, jnp.concatenate(parts, axis=0), 0.0)
            o = o + jnp.dot(att.astype(bf16), v.astype(bf16), preferred_element_type=f32)
            b_last = b[c_rows - 1:c_rows, :]
            kd = (kk * jnp.exp(b_last - b)).astype(bf16)
            state_ref[h] = st * jnp.exp(b_last) + lax.dot_general(v.astype(bf16), kd, TN_DIMS,
                                                                   preferred_element_type=f32)
            ms = jnp.mean(o * o, axis=1, keepdims=True)
            o_ref[rows, lanes] = o * lax.rsqrt(ms + RMS_EPS) * ng * (gg * _sigmoid(gg))
        return carry

    lax.fori_loop(0, HGRN_TILE // c_rows, chunk, 0)

    @pl.when(t == pl.num_programs(1) - 1)
    def _():
        for h in range(HGRN_HEADS):
            st_ref[0, h] = state_ref[h].T


def _hgrn_prompt(hproj, batch, seq, lb, norm_g):
    nt = seq // HGRN_TILE
    width = HGRN_HEADS * HGRN_DK
    col = lambda j: pl.BlockSpec((HGRN_TILE, width), lambda b, t, j=j: (b * nt + t, j))
    vec = pl.BlockSpec((1, width), lambda b, t: (0, 0))
    return pl.pallas_call(
        _hgrn_prompt_kernel,
        grid=(batch, nt),
        in_specs=[col(0), col(1), col(2), col(3), vec, vec],
        out_specs=[pl.BlockSpec((HGRN_TILE, width), lambda b, t: (b * nt + t, 0)),
                   pl.BlockSpec((1, HGRN_HEADS, HGRN_DK, HGRN_DK), lambda b, t: (b, 0, 0, 0))],
        out_shape=[jax.ShapeDtypeStruct((batch * seq, width), f32),
                   jax.ShapeDtypeStruct((batch, HGRN_HEADS, HGRN_DK, HGRN_DK), f32)],
        scratch_shapes=[pltpu.VMEM((HGRN_HEADS, HGRN_DK, HGRN_DK), f32)],
        compiler_params=_params(("parallel", "arbitrary")),
        name="hgrn_prompt",
    )(hproj, hproj, hproj, hproj, lb.reshape(1, width), norm_g.reshape(1, width))


def _hgrn_sample_kernel(qc_ref, fzc_ref, lbc_ref, i_ref, g_ref, ng_ref, st_ref, o_ref, nst_ref):
    for h in range(HGRN_HEADS):
        lb = lbc_ref[h]
        f = lb + (1.0 - lb) * _sigmoid(fzc_ref[0, h])
        st = f * st_ref[0, h] + (1.0 - f) * i_ref[0, h]
        nst_ref[0, h] = st
        o = jnp.sum(qc_ref[0, h] * st, axis=0, keepdims=True)
        ms = jnp.mean(o * o, axis=1, keepdims=True)
        gg = g_ref[0, h]
        o_ref[0, h] = o * lax.rsqrt(ms + RMS_EPS) * ng_ref[h] * (gg * _sigmoid(gg))


def _hgrn_sample(hs, state, lb, norm_g):
    db = hs.shape[0]
    width = HGRN_HEADS * HGRN_DK
    colv = lambda x: x.reshape(db, HGRN_HEADS, HGRN_DK, 1)
    rowv = lambda x: x.reshape(db, HGRN_HEADS, 1, HGRN_DK)
    cspec = pl.BlockSpec((1, HGRN_HEADS, HGRN_DK, 1), lambda b: (b, 0, 0, 0))
    rspec = pl.BlockSpec((1, HGRN_HEADS, 1, HGRN_DK), lambda b: (b, 0, 0, 0))
    sspec = pl.BlockSpec((1, HGRN_HEADS, HGRN_DK, HGRN_DK), lambda b: (b, 0, 0, 0))
    o, nst = pl.pallas_call(
        _hgrn_sample_kernel,
        grid=(db,),
        in_specs=[cspec, cspec, pl.BlockSpec((HGRN_HEADS, HGRN_DK, 1), lambda b: (0, 0, 0)), rspec, rspec,
                  pl.BlockSpec((HGRN_HEADS, 1, HGRN_DK), lambda b: (0, 0, 0)), sspec],
        out_specs=[rspec, sspec],
        out_shape=[jax.ShapeDtypeStruct((db, HGRN_HEADS, 1, HGRN_DK), f32),
                   jax.ShapeDtypeStruct((db, HGRN_HEADS, HGRN_DK, HGRN_DK), f32)],
        compiler_params=_params(("parallel",)),
        name="hgrn_sample",
    )(colv(hs[:, :width]), colv(hs[:, width:2 * width]), lb.reshape(HGRN_HEADS, HGRN_DK, 1),
      rowv(hs[:, 2 * width:3 * width]), rowv(hs[:, 3 * width:]), norm_g.reshape(HGRN_HEADS, 1, HGRN_DK), state)
    return o.reshape(db, width), nst


PEER_EBLK = 1024
PEER_RANKS = PEER_TOPK + 1
PEER_VROWS = 24
PEER_SUB = 8
PEER_ROWS = 16


def _extract_max(tiles):
    mx = functools.reduce(jnp.maximum, tiles)
    mx = jnp.max(mx, axis=0, keepdims=True)
    return mx, [jnp.where(x == mx, FLOOR, x) for x in tiles]


def _peer_kernel(x_ref, q_ref, k1_ref, k2_ref, u_ref, v_ref, g_ref, b_ref, o_ref,
                 xt_ref, acc_ref, act_ref, p_ref, a_ref, bw_ref, c_ref, s2_ref, r2_ref, v1_ref, v2_ref, ab_ref, cb_ref):
    e = pl.program_id(1)
    tt = x_ref.shape[0]
    n_lane_chunks = tt // LANES
    sub = PEER_SUB
    n_sub = PEER_NKEYS // sub
    keys_per_step = PEER_EBLK // PEER_NKEYS

    @pl.when(e == 0)
    def _():
        xt_ref[...] = x_ref[...].T.astype(bf16)
        acc_ref[...] = jnp.zeros(acc_ref.shape, f32)
        floor_rows = jnp.full((PEER_VROWS, LANES), FLOOR, f32)
        for h in range(PEER_HEADS):
            q1 = q_ref[:, (2 * h) * PEER_DHALF:(2 * h + 1) * PEER_DHALF]
            q2 = q_ref[:, (2 * h + 1) * PEER_DHALF:(2 * h + 2) * PEER_DHALF]
            s1 = lax.dot_general(k1_ref[h], q1, NT_DIMS, precision=HI, preferred_element_type=f32)
            c_ref[h] = s1.reshape(n_sub, sub, tt)
            s2_ref[h] = lax.dot_general(k2_ref[h], q2, NT_DIMS, precision=HI, preferred_element_type=f32)

        def per_head(h, carry):
            for lc in range(n_lane_chunks):
                lanes = slice(lc * LANES, (lc + 1) * LANES)
                ranks = [jnp.full((sub, LANES), float(PEER_RANKS), f32) for _ in range(n_sub)]
                for side, vals_ref in enumerate((v1_ref, v2_ref)):
                    vals_ref[...] = floor_rows
                    if side == 0:
                        tiles = [c_ref[h, k, :, lanes] for k in range(n_sub)]
                    else:
                        tiles = [s2_ref[h, k * sub:(k + 1) * sub, lanes] for k in range(n_sub)]
                    for r in range(PEER_RANKS):
                        mx, floored = _extract_max(tiles)
                        if side == 1:
                            ranks = [jnp.where(x == mx, float(r), rk) for x, rk in zip(tiles, ranks)]
                        tiles = floored
                        vals_ref[r:r + 1, :] = mx
                cands = [v1_ref[0:1, :] + v2_ref[r:r + sub, :] for r in range(0, PEER_VROWS, sub)]
                cands += [v1_ref[r:r + 1, :] + v2_ref[0:sub, :] for r in range(1, sub)]
                cands += [v1_ref[r:r + sub, :] + v2_ref[0:1, :] for r in range(sub, PEER_VROWS, sub)]
                best = v1_ref[0:1, :] + v2_ref[0:1, :]
                zsum = jnp.zeros((1, LANES), f32)
                kth = best
                for r in range(PEER_TOPK):
                    kth, cands = _extract_max(cands)
                    zsum = zsum + jnp.exp(kth - best)
                nxt, _ = _extract_max(cands)
                thresh = 0.5 * (kth + nxt)
                s1 = c_ref[h, :, :, lanes]
                a_ref[h, :, :, lanes] = jnp.exp(s1 - v1_ref[0:1, :]) / zsum
                bw_ref[h, :, lanes] = jnp.exp(s2_ref[h, :, lanes] - v2_ref[0:1, :]).astype(bf16)
                r2_ref[h, :, lanes] = jnp.concatenate(ranks, axis=0).astype(bf16)
                need = thresh - s1
                count = jnp.zeros(need.shape, f32)
                for r in range(PEER_RANKS):
                    count = jnp.where(v2_ref[r:r + 1, :] >= need, float(r + 1), count)
                c_ref[h, :, :, lanes] = count
            return carry

        lax.fori_loop(0, PEER_HEADS, per_head, 0)

    act_ref[...] = jnp.dot(u_ref[...], xt_ref[...], preferred_element_type=f32)
    for h in range(PEER_HEADS):
        for ii in range(keys_per_step):
            ab_ref[ii, h] = jnp.broadcast_to(a_ref[h, e, ii:ii + 1, :], (PEER_ROWS, tt)).astype(bf16)
            cb_ref[ii, h] = jnp.broadcast_to(c_ref[h, e, ii:ii + 1, :], (PEER_ROWS, tt)).astype(bf16)

    def per_second_keys(k, carry):
        r0 = pl.multiple_of(k * PEER_ROWS, PEER_ROWS)
        for lc in range(n_lane_chunks):
            lanes = slice(lc * LANES, (lc + 1) * LANES)
            w = [jnp.zeros((PEER_ROWS, LANES), bf16) for _ in range(keys_per_step)]
            for h in range(PEER_HEADS):
                rank = r2_ref[h, pl.ds(r0, PEER_ROWS), lanes]
                bw = bw_ref[h, pl.ds(r0, PEER_ROWS), lanes]
                for ii in range(keys_per_step):
                    hit = rank < cb_ref[ii, h, :, lanes]
                    w[ii] = w[ii] + ab_ref[ii, h, :, lanes] * jnp.where(hit, bw, jnp.zeros_like(bw))
            for ii in range(keys_per_step):
                rows = pl.ds(pl.multiple_of(ii * PEER_NKEYS + r0, PEER_ROWS), PEER_ROWS)
                act = act_ref[rows, lanes]
                gelu = 0.5 * act * (1.0 + lax.erf(act * (2.0 ** -0.5)))
                p_ref[rows, lanes] = w[ii] * gelu.astype(bf16)
        return carry

    lax.fori_loop(0, PEER_NKEYS // PEER_ROWS, per_second_keys, 0)
    acc_ref[...] += lax.dot_general(p_ref[...], v_ref[...], TN_DIMS, preferred_element_type=f32)

    @pl.when(e == pl.num_programs(1) - 1)
    def _():
        o_ref[...] = _layer_norm(DN_ALPHA * x_ref[...] + acc_ref[...], g_ref[...], b_ref[...])


def _peer_ln(x, wq, keys, u, v, g, b, *, tt=TOKEN_TILE):
    m, d = x.shape
    n_exp = u.shape[0]
    q = _mm(x, wq, split=True)
    nq = q.shape[1]
    return pl.pallas_call(
        _peer_kernel,
        grid=(m // tt, n_exp // PEER_EBLK),
        in_specs=[pl.BlockSpec((tt, d), lambda t, e: (t, 0)),
                  pl.BlockSpec((tt, nq), lambda t, e: (t, 0)),
                  pl.BlockSpec((PEER_HEADS, PEER_NKEYS, PEER_DHALF), lambda t, e: (0, 0, 0)),
                  pl.BlockSpec((PEER_HEADS, PEER_NKEYS, PEER_DHALF), lambda t, e: (0, 0, 0)),
                  pl.BlockSpec((PEER_EBLK, d), lambda t, e: (e, 0)),
                  pl.BlockSpec((PEER_EBLK, d), lambda t, e: (e, 0)),
                  pl.BlockSpec((1, d), lambda t, e: (0, 0)),
                  pl.BlockSpec((1, d), lambda t, e: (0, 0))],
        out_specs=pl.BlockSpec((tt, d), lambda t, e: (t, 0)),
        out_shape=jax.ShapeDtypeStruct((m, d), f32),
        scratch_shapes=[pltpu.VMEM((d, tt), bf16), pltpu.VMEM((tt, d), f32),
                        pltpu.VMEM((PEER_EBLK, tt), f32), pltpu.VMEM((PEER_EBLK, tt), bf16)]
                       + [pltpu.VMEM((PEER_HEADS, PEER_NKEYS // PEER_SUB, PEER_SUB, tt), f32),
                          pltpu.VMEM((PEER_HEADS, PEER_NKEYS, tt), bf16),
                          pltpu.VMEM((PEER_HEADS, PEER_NKEYS // PEER_SUB, PEER_SUB, tt), f32),
                          pltpu.VMEM((PEER_HEADS, PEER_NKEYS, tt), f32),
                          pltpu.VMEM((PEER_HEADS, PEER_NKEYS, tt), bf16)]
                       + [pltpu.VMEM((PEER_VROWS, LANES), f32)] * 2
                       + [pltpu.VMEM((PEER_EBLK // PEER_NKEYS, PEER_HEADS, PEER_ROWS, tt), bf16)] * 2,
        compiler_params=_params(("parallel", "arbitrary")),
        name="peer",
    )(x, q, keys[0], keys[1], u.astype(bf16), v.astype(bf16), g.reshape(1, d), b.reshape(1, d))


def kernel(x_prompt, x_sample, cache_k, cache_v, state_conv, state_hgrn, page_table, w_in_even, conv_w, conv_b, conv_ln_g, conv_ln_b, w_out_even, w_in_odd, hgrn_lb_logits, hgrn_norm_g, w_out_odd, ln_g, ln_b, peer_wq, peer_keys, peer_u, peer_v):
    batch, seq, d = x_prompt.shape
    db = x_sample.shape[0]
    n_prompt = batch * seq
    rows = n_prompt + TOKEN_TILE
    srows = slice(n_prompt, n_prompt + db)
    pad_rows = lambda a: jnp.concatenate([a, jnp.zeros((rows - a.shape[0], a.shape[1]), a.dtype)], axis=0)

    lb_p = jax.nn.softmax(hgrn_lb_logits.astype(f32), axis=0)
    lb_all = jnp.cumsum(lb_p, axis=0) - lb_p[0]

    x = pad_rows(jnp.concatenate([x_prompt.reshape(n_prompt, d), x_sample.reshape(db, d)], axis=0))
    kp_l, vp_l, ks_l, vs_l, cp_l, cs_l, hp_l, hs_l = [], [], [], [], [], [], [], []
    for l in range(DEPTH):
        if l % 2 == 0:
            e = l // 2
            w_in = w_in_even[e]
            qk = _mm(x, w_in[:, :2 * MOBA_WIDTH], split=True)
            vag = _mm(x, w_in[:, 2 * MOBA_WIDTH:])
            k_all, v_all = qk[:, MOBA_WIDTH:], vag[:, :MOBA_WIDTH]
            kp_l.append(k_all[:n_prompt].reshape(batch, seq // PAGE_SIZE, PAGE_SIZE, MOBA_HEADS, MOBA_HEAD_DIM))
            vp_l.append(v_all[:n_prompt].reshape(batch, seq // PAGE_SIZE, PAGE_SIZE, MOBA_HEADS, MOBA_HEAD_DIM))
            ks_l.append(k_all[srows].reshape(db, 1, MOBA_HEADS, MOBA_HEAD_DIM))
            vs_l.append(v_all[srows].reshape(db, 1, MOBA_HEADS, MOBA_HEAD_DIM))
            att_p = _moba_prompt(qk, vag, batch, seq)
            heads = lambda a: a.reshape(db, MOBA_HEADS, MOBA_HEAD_DIM)
            att_s = _moba_sample(heads(qk[srows, :MOBA_WIDTH]), heads(k_all[srows]), heads(v_all[srows]),
                                 cache_k[e].transpose(0, 2, 3, 1), cache_v[e].transpose(0, 2, 3, 1), page_table)
            att_s = att_s[:, :, :MOBA_HEAD_DIM].reshape(db, MOBA_WIDTH)
            cprm = (conv_w[e], conv_b[e], conv_ln_g[e], conv_ln_b[e])
            cy_p, buf_p = _conv_prompt(vag, batch, seq, *cprm)
            cy_s, buf_s = _conv_sample(vag[srows, MOBA_WIDTH:MOBA_WIDTH + CONV_CH],
                                       vag[srows, MOBA_WIDTH + CONV_CH:], state_conv[e], *cprm)
            cp_l.append(buf_p)
            cs_l.append(buf_s)
            mix = pad_rows(jnp.concatenate([jnp.concatenate([att_p, cy_p], axis=1),
                                            jnp.concatenate([att_s, cy_s], axis=1)], axis=0))
            w_out = w_out_even[e]
        else:
            oi = l // 2
            hproj = _mm(x, w_in_odd[oi], tm=256)
            o_p, st_p = _hgrn_prompt(hproj, batch, seq, lb_all[l], hgrn_norm_g[oi])
            o_s, st_s = _hgrn_sample(hproj[srows], state_hgrn[oi].astype(f32), lb_all[l], hgrn_norm_g[oi])
            hp_l.append(st_p)
            hs_l.append(st_s)
            mix = pad_rows(jnp.concatenate([o_p, o_s], axis=0))
            w_out = w_out_odd[oi]
        x = _mm_res_ln(mix, w_out, x, ln_g[l, 0], ln_b[l, 0])
        x = _peer_ln(x, peer_wq[l], peer_keys[l], peer_u[l], peer_v[l], ln_g[l, 1], ln_b[l, 1])
    y_prompt = x[:n_prompt].reshape(batch, seq, d)
    y_sample = x[srows].reshape(db, 1, d)
    return (y_prompt, y_sample, jnp.stack(kp_l), jnp.stack(vp_l), jnp.stack(ks_l), jnp.stack(vs_l),
            jnp.stack(cp_l), jnp.stack(cs_l), jnp.stack(hp_l), jnp.stack(hs_l))
```

```python
import functools
import math

import jax
import jax.numpy as jnp
from jax import lax
from jax.experimental import pallas as pl
from jax.experimental.pallas import tpu as pltpu

f32 = jnp.float32
bf16 = jnp.bfloat16
i32 = jnp.int32
HI = lax.Precision.HIGHEST

D_MODEL = 1024
DEPTH = 2
PAGE_SIZE = 128
MOBA_HEADS = 8
MOBA_HEAD_DIM = 64
MOBA_WIDTH = MOBA_HEADS * MOBA_HEAD_DIM
MOBA_BLOCK = 256
MOBA_TOPK = 3
MOBA_SCALE = MOBA_HEAD_DIM ** -0.5
CONV_CH = D_MODEL // 2
CONV_WIDTH = 31
HGRN_HEADS = 8
HGRN_DK = D_MODEL // HGRN_HEADS
HGRN_CHUNK = 64
HGRN_SUB = 16
PEER_HEADS = 8
PEER_NKEYS = 128
PEER_DHALF = 128
PEER_TOPK = 16
LN_EPS = 1e-5
RMS_EPS = 1e-6
DN_ALPHA = (2 * DEPTH) ** 0.25

LANES = 128
TOKEN_TILE = 512
SAMPLE_TILE = 128
MASK_NEG = -1e30
FLOOR = -3e38
EXP_CLAMP = 60.0
VMEM_LIMIT = 56 << 20

NT_DIMS = (((1,), (1,)), ((), ()))
TN_DIMS = (((0,), (0,)), ((), ()))


def _params(semantics, vmem=VMEM_LIMIT):
    return pltpu.CompilerParams(dimension_semantics=semantics, vmem_limit_bytes=vmem)


def _layer_norm(y, g, b):
    mu = jnp.mean(y, axis=-1, keepdims=True)
    d = y - mu
    var = jnp.mean(d * d, axis=-1, keepdims=True)
    return d * lax.rsqrt(var + LN_EPS) * g + b


def _sigmoid(x):
    return 1.0 / (1.0 + jnp.exp(-x))


def _mm_kernel(x_ref, *refs, n_chunk, split):
    if split:
        wh_ref, wl_ref, o_ref = refs
    else:
        wh_ref, o_ref = refs
    x = x_ref[...]
    xh = x.astype(bf16)
    if split:
        xl = (x - xh.astype(f32)).astype(bf16)
    for j in range(0, o_ref.shape[1], n_chunk):
        wh = wh_ref[:, j:j + n_chunk]
        acc = jnp.dot(xh, wh, preferred_element_type=f32)
        if split:
            acc = acc + jnp.dot(xl, wh, preferred_element_type=f32)
            acc = acc + jnp.dot(xh, wl_ref[:, j:j + n_chunk], preferred_element_type=f32)
        o_ref[:, j:j + n_chunk] = acc


def _mm(x, w, *, split=False, tm=TOKEN_TILE):
    m, k = x.shape
    n = w.shape[1]
    wh = w.astype(bf16)
    ws = [wh] + ([(w - wh.astype(f32)).astype(bf16)] if split else [])
    return pl.pallas_call(
        functools.partial(_mm_kernel, n_chunk=512, split=split),
        grid=(m // tm,),
        in_specs=[pl.BlockSpec((tm, k), lambda i: (i, 0))] + [pl.BlockSpec((k, n), lambda i: (0, 0))] * len(ws),
        out_specs=pl.BlockSpec((tm, n), lambda i: (i, 0)),
        out_shape=jax.ShapeDtypeStruct((m, n), f32),
        compiler_params=_params(("parallel",)),
        name="proj_split" if split else "proj",
    )(x, *ws)


def _mm_res_ln_kernel(*refs):
    *a_refs, w_ref, x_ref, g_ref, b_ref, o_ref = refs
    acc = DN_ALPHA * x_ref[...]
    row = 0
    for a_ref in a_refs:
        k = a_ref.shape[1]
        acc = acc + jnp.dot(a_ref[...].astype(bf16), w_ref[row:row + k, :], preferred_element_type=f32)
        row += k
    o_ref[...] = _layer_norm(acc, g_ref[...], b_ref[...])


def _mm_res_ln(parts, w, x, g, b, *, tm=TOKEN_TILE):
    m, n = x.shape
    k = w.shape[0]
    return pl.pallas_call(
        _mm_res_ln_kernel,
        grid=(m // tm,),
        in_specs=[pl.BlockSpec((tm, a.shape[1]), lambda i: (i, 0)) for a in parts]
                 + [pl.BlockSpec((k, n), lambda i: (0, 0)),
                    pl.BlockSpec((tm, n), lambda i: (i, 0)), pl.BlockSpec((1, n), lambda i: (0, 0)),
                    pl.BlockSpec((1, n), lambda i: (0, 0))],
        out_specs=pl.BlockSpec((tm, n), lambda i: (i, 0)),
        out_shape=jax.ShapeDtypeStruct((m, n), f32),
        compiler_params=_params(("parallel",)),
        name="out_proj_ln",
    )(*parts, w.astype(bf16), x, g.reshape(1, n), b.reshape(1, n))


CONV_TILE = 256
CONV_ROWS = 64
CONV_HIST = 32


def _conv_prompt_kernel(a_ref, gt_ref, w_ref, cb_ref, g_ref, b_ref, y_ref, st_ref, buf_ref):
    t = pl.program_id(1)

    @pl.when(t == 0)
    def _():
        buf_ref[0:CONV_HIST, :] = jnp.zeros((CONV_HIST, CONV_CH), f32)

    buf_ref[CONV_HIST:CONV_HIST + CONV_TILE, :] = a_ref[...] * _sigmoid(gt_ref[...])
    first = CONV_HIST - (CONV_WIDTH - 1)
    for r in range(0, CONV_TILE, CONV_ROWS):
        acc = jnp.zeros((CONV_ROWS, CONV_CH), f32) + cb_ref[...]
        for j in range(CONV_WIDTH):
            acc = acc + w_ref[j:j + 1, :] * buf_ref[r + first + j:r + first + j + CONV_ROWS, :]
        y = _layer_norm(acc, g_ref[...], b_ref[...])
        y_ref[r:r + CONV_ROWS, :] = y * _sigmoid(y)
    tail = buf_ref[CONV_TILE:CONV_TILE + CONV_HIST, :]
    buf_ref[0:CONV_HIST, :] = tail

    @pl.when(t == pl.num_programs(1) - 1)
    def _():
        st_ref[0] = tail


def _conv_prompt(vag, batch, seq, w, cb, g, b):
    nt = seq // CONV_TILE
    wpad = jnp.zeros((CONV_HIST, CONV_CH), f32).at[:CONV_WIDTH].set(w)
    row = lambda v: v.reshape(1, CONV_CH)
    y, st = pl.pallas_call(
        _conv_prompt_kernel,
        grid=(batch, nt),
        in_specs=[pl.BlockSpec((CONV_TILE, CONV_CH), lambda bi, t: (bi * nt + t, 1)),
                  pl.BlockSpec((CONV_TILE, CONV_CH), lambda bi, t: (bi * nt + t, 2)),
                  pl.BlockSpec((CONV_HIST, CONV_CH), lambda bi, t: (0, 0))]
                 + [pl.BlockSpec((1, CONV_CH), lambda bi, t: (0, 0))] * 3,
        out_specs=[pl.BlockSpec((CONV_TILE, CONV_CH), lambda bi, t: (bi * nt + t, 0)),
                   pl.BlockSpec((1, CONV_HIST, CONV_CH), lambda bi, t: (bi, 0, 0))],
        out_shape=[jax.ShapeDtypeStruct((batch * seq, CONV_CH), f32),
                   jax.ShapeDtypeStruct((batch, CONV_HIST, CONV_CH), f32)],
        scratch_shapes=[pltpu.VMEM((CONV_TILE + CONV_HIST, CONV_CH), f32)],
        compiler_params=_params(("arbitrary", "arbitrary")),
        name="conv_prompt",
    )(vag, vag, wpad, row(cb), row(g), row(b))
    return y, st[:, CONV_HIST - (CONV_WIDTH - 1):]


def _conv_sample_kernel(a_ref, gt_ref, st_ref, w_ref, cb_ref, g_ref, b_ref, y_ref, nst_ref):
    u = a_ref[...] * _sigmoid(gt_ref[...])
    acc = cb_ref[...] + w_ref[CONV_WIDTH - 1:CONV_WIDTH, :] * u
    for j in range(CONV_WIDTH - 1):
        acc = acc + w_ref[j:j + 1, :] * st_ref[j]
    y = _layer_norm(acc, g_ref[...], b_ref[...])
    y_ref[...] = y * _sigmoid(y)
    for j in range(CONV_WIDTH - 2):
        nst_ref[j] = st_ref[j + 1]
    nst_ref[CONV_WIDTH - 2] = u


def _conv_sample(a, gt, state, w, cb, g, b):
    db = a.shape[0]
    wpad = jnp.zeros((CONV_HIST, CONV_CH), f32).at[:CONV_WIDTH].set(w)
    row = lambda v: v.reshape(1, CONV_CH)
    y, nst = pl.pallas_call(
        _conv_sample_kernel,
        out_shape=[jax.ShapeDtypeStruct((db, CONV_CH), f32),
                   jax.ShapeDtypeStruct((CONV_WIDTH - 1, db, CONV_CH), f32)],
        name="conv_sample",
    )(a, gt, state.transpose(1, 0, 2), wpad, row(cb), row(g), row(b))
    return y, nst.transpose(1, 0, 2)


GATE_ROWS = 1024
ATTN_KEYS = 2 * MOBA_BLOCK


def _block_sum_kernel(k_ref, o_ref):
    o_ref[0] = jnp.sum(k_ref[...], axis=0, keepdims=True)


def _block_sums(qk, batch, seq):
    nb = seq // MOBA_BLOCK
    return pl.pallas_call(
        _block_sum_kernel,
        grid=(batch * nb,),
        in_specs=[pl.BlockSpec((MOBA_BLOCK, MOBA_WIDTH), lambda i: (i, 1))],
        out_specs=pl.BlockSpec((1, 1, MOBA_WIDTH), lambda i: (i, 0, 0)),
        out_shape=jax.ShapeDtypeStruct((batch * nb, 1, MOBA_WIDTH), f32),
        compiler_params=_params(("parallel",)),
        name="moba_block_sums",
    )(qk)


def _moba_gate_kernel(q_ref, k_ref, v_ref, ks_ref, qa_ref, ka_ref, vb_ref, *, nb):
    t = pl.program_id(2)
    lane = lax.broadcasted_iota(i32, (MOBA_BLOCK, LANES), 1)
    blk = lane - MOBA_HEAD_DIM
    km_lane = lax.broadcasted_iota(i32, (nb, LANES), 1)
    ksum = ks_ref[0] * (1.0 / MOBA_BLOCK)
    pad_top = jnp.zeros((MOBA_HEAD_DIM, LANES), f32)
    pad_bot = jnp.zeros((LANES - MOBA_HEAD_DIM - nb, LANES), f32)
    for c in range(GATE_ROWS // MOBA_BLOCK):
        own = t * (GATE_ROWS // MOBA_BLOCK) + c
        rows = slice(c * MOBA_BLOCK, (c + 1) * MOBA_BLOCK)
        q2 = q_ref[rows, :]
        k2 = k_ref[rows, :]
        vb_ref[rows, :] = v_ref[rows, :].astype(bf16)
        cand = (blk >= 0) & (blk < own)
        for j in range(2):
            head = (km_lane >= j * MOBA_HEAD_DIM) & (km_lane < (j + 1) * MOBA_HEAD_DIM)
            km = jnp.concatenate([pad_top, jnp.where(head, ksum, 0.0), pad_bot], axis=0)
            gate = lax.dot_general(q2, km, NT_DIMS, precision=HI, preferred_element_type=f32)
            masked = jnp.where(cand, gate, FLOOR)
            sel = jnp.zeros((MOBA_BLOCK, LANES), f32)
            for _ in range(MOBA_TOPK):
                mx = jnp.max(masked, axis=1, keepdims=True)
                hit = (masked == mx) & cand
                sel = jnp.where(hit, 1.0, sel)
                masked = jnp.where(hit, FLOOR, masked)
            bias = jnp.where((sel > 0.0) | (blk == own), 0.0, MASK_NEG)
            qj = q2 if j == 0 else pltpu.roll(q2, MOBA_HEAD_DIM, 1)
            kj = k2 if j == 0 else pltpu.roll(k2, MOBA_HEAD_DIM, 1)
            in_head = lane < MOBA_HEAD_DIM
            in_bias = lane < MOBA_HEAD_DIM + nb
            qa = jnp.where(in_head, qj * MOBA_SCALE, jnp.where(in_bias, bias, 0.0))
            ka = jnp.where(in_head, kj, jnp.where(blk == own, 1.0, 0.0))
            qa_ref[0, j, rows, :] = qa.astype(bf16)
            ka_ref[0, j, rows, :] = ka.astype(bf16)


def _moba_attn_kernel(qa_ref, ka_ref, vb_ref, o_ref):
    i = pl.program_id(1)
    last = i // 2
    row = lax.broadcasted_iota(i32, (MOBA_BLOCK, ATTN_KEYS), 0)
    col = lax.broadcasted_iota(i32, (MOBA_BLOCK, ATTN_KEYS), 1)
    lane = lax.broadcasted_iota(i32, (MOBA_BLOCK, LANES), 1)

    def tile(h, k0, mask):
        s = lax.dot_general(qa_ref[0, h], ka_ref[0, h, pl.ds(k0, ATTN_KEYS), :], NT_DIMS, preferred_element_type=f32)
        return s if mask is None else jnp.where(mask, s, MASK_NEG)

    def pair_values(h, k0):
        return vb_ref[pl.ds(k0, ATTN_KEYS), (h // 2) * LANES:(h // 2 + 1) * LANES]

    k_last = pl.multiple_of(last * ATTN_KEYS, ATTN_KEYS)
    visible = (k_last + col) <= (i * MOBA_BLOCK + row)
    state = []
    for h in range(MOBA_HEADS):
        s = tile(h, k_last, visible)
        m = jnp.max(s, axis=1, keepdims=True)
        p = jnp.exp(s - m)
        state += [m, jnp.sum(p, axis=1, keepdims=True),
                  jnp.dot(p.astype(bf16), pair_values(h, k_last), preferred_element_type=f32)]

    def body(n, carry):
        k0 = pl.multiple_of(n * ATTN_KEYS, ATTN_KEYS)
        out = []
        for h in range(MOBA_HEADS):
            m, l, acc = carry[3 * h:3 * h + 3]
            s = tile(h, k0, None)
            mn = jnp.maximum(m, jnp.max(s, axis=1, keepdims=True))
            alpha = jnp.exp(m - mn)
            p = jnp.exp(s - mn)
            out += [mn, alpha * l + jnp.sum(p, axis=1, keepdims=True),
                    alpha * acc + jnp.dot(p.astype(bf16), pair_values(h, k0), preferred_element_type=f32)]
        return tuple(out)

    final = lax.fori_loop(0, last, body, tuple(state))
    for pair in range(MOBA_HEADS // 2):
        (_, l0, acc0), (_, l1, acc1) = final[6 * pair:6 * pair + 3], final[6 * pair + 3:6 * pair + 6]
        o_ref[:, pair * LANES:(pair + 1) * LANES] = jnp.where(lane < MOBA_HEAD_DIM, acc0 / l0, acc1 / l1)


def _moba_prompt(qk, vag, batch, seq):
    nb = seq // MOBA_BLOCK
    npair = MOBA_WIDTH // LANES
    nt = seq // GATE_ROWS
    ksums = _block_sums(qk, batch, seq).reshape(batch, nb, MOBA_WIDTH)
    qa, ka, vb = pl.pallas_call(
        functools.partial(_moba_gate_kernel, nb=nb),
        grid=(batch, npair, nt),
        in_specs=[pl.BlockSpec((GATE_ROWS, LANES), lambda b, p, t: (b * nt + t, p)),
                  pl.BlockSpec((GATE_ROWS, LANES), lambda b, p, t: (b * nt + t, npair + p)),
                  pl.BlockSpec((GATE_ROWS, LANES), lambda b, p, t: (b * nt + t, p)),
                  pl.BlockSpec((1, nb, LANES), lambda b, p, t: (b, 0, p))],
        out_specs=[pl.BlockSpec((1, 2, GATE_ROWS, LANES), lambda b, p, t: (b, p, t, 0)),
                   pl.BlockSpec((1, 2, GATE_ROWS, LANES), lambda b, p, t: (b, p, t, 0)),
                   pl.BlockSpec((GATE_ROWS, LANES), lambda b, p, t: (b * nt + t, p))],
        out_shape=[jax.ShapeDtypeStruct((batch, MOBA_HEADS, seq, LANES), bf16),
                   jax.ShapeDtypeStruct((batch, MOBA_HEADS, seq, LANES), bf16),
                   jax.ShapeDtypeStruct((batch * seq, MOBA_WIDTH), bf16)],
        compiler_params=_params(("parallel", "parallel", "parallel")),
        name="moba_gate",
    )(qk, qk, vag, ksums)
    return pl.pallas_call(
        _moba_attn_kernel,
        grid=(batch, nb),
        in_specs=[pl.BlockSpec((1, MOBA_HEADS, MOBA_BLOCK, LANES), lambda b, i: (b, 0, i, 0)),
                  pl.BlockSpec((1, MOBA_HEADS, seq, LANES), lambda b, i: (b, 0, 0, 0),
                               pipeline_mode=pl.Buffered(1)),
                  pl.BlockSpec((seq, MOBA_WIDTH), lambda b, i: (b, 0), pipeline_mode=pl.Buffered(1))],
        out_specs=pl.BlockSpec((MOBA_BLOCK, MOBA_WIDTH), lambda b, i: (b * nb + i, 0)),
        out_shape=jax.ShapeDtypeStruct((batch * seq, MOBA_WIDTH), f32),
        compiler_params=_params(("parallel", "arbitrary")),
        name="moba_attn",
    )(qa, ka, vb)


PAGES_PER_STEP = 8
PAGES_PER_BLOCK = MOBA_BLOCK // PAGE_SIZE
BLOCKS_PER_STEP = PAGES_PER_STEP // PAGES_PER_BLOCK


def _moba_sample_stream_kernel(pt_ref, qb_ref, *refs):
    del pt_ref
    k_refs = refs[:PAGES_PER_STEP]
    v_refs = refs[PAGES_PER_STEP:2 * PAGES_PER_STEP]
    gate_ref, m_ref, l_ref, acc_ref = refs[2 * PAGES_PER_STEP:]
    qb = qb_ref[0]
    head = lax.broadcasted_iota(i32, (MOBA_HEADS, LANES), 0)
    tile = (MOBA_HEADS, LANES)
    for jj in range(BLOCKS_PER_STEP):
        pages = range(PAGES_PER_BLOCK * jj, PAGES_PER_BLOCK * (jj + 1))
        raw = [jnp.sum(k_refs[x][0] * qb, axis=1) for x in pages]
        gate = functools.reduce(jnp.add, [jnp.sum(r, axis=1, keepdims=True) for r in raw]) * (1.0 / MOBA_BLOCK)
        m = functools.reduce(jnp.maximum, [jnp.max(r, axis=1, keepdims=True) for r in raw]) * MOBA_SCALE
        l = jnp.zeros((MOBA_HEADS, 1), f32)
        pv = jnp.zeros((MOBA_HEADS, MOBA_WIDTH), f32)
        for r, x in zip(raw, pages):
            p = jnp.exp(r * MOBA_SCALE - m)
            l = l + jnp.sum(p, axis=1, keepdims=True)
            vt = v_refs[x][0].reshape(MOBA_WIDTH, PAGE_SIZE).astype(bf16)
            pv = pv + lax.dot_general(p.astype(bf16), vt, NT_DIMS, preferred_element_type=f32)
        acc = jnp.zeros(tile, f32)
        for pair in range(MOBA_WIDTH // LANES):
            both = pv[:, pair * LANES:(pair + 1) * LANES]
            acc = acc + jnp.where(head == 2 * pair, both, 0.0)
            acc = acc + jnp.where(head == 2 * pair + 1, pltpu.roll(both, MOBA_HEAD_DIM, 1), 0.0)
        gate_ref[0, 0, jj] = jnp.broadcast_to(gate, tile)
        m_ref[0, 0, jj] = jnp.broadcast_to(m, tile)
        l_ref[0, 0, jj] = jnp.broadcast_to(l, tile)
        acc_ref[0, 0, jj] = acc


def _moba_sample_combine_kernel(gate_ref, m_ref, l_ref, acc_ref, q_ref, kn_ref, vn_ref, o_ref):
    db, nb, nh, width = gate_ref.shape
    masked = gate_ref[...]
    sel = jnp.zeros((db, nb, nh, width), f32)
    for _ in range(MOBA_TOPK):
        mx = jnp.max(masked, axis=1, keepdims=True)
        hit = masked == mx
        sel = jnp.where(hit, 1.0, sel)
        masked = jnp.where(hit, FLOOR, masked)
    chosen = sel > 0.0
    s_self = jnp.sum(q_ref[...] * kn_ref[...], axis=-1, keepdims=True) * MOBA_SCALE
    m = m_ref[...]
    top = jnp.maximum(jnp.max(jnp.where(chosen, m, FLOOR), axis=1), s_self)
    w = jnp.where(chosen, jnp.exp(m - top[:, None]), 0.0)
    w_self = jnp.exp(s_self - top)
    denom = jnp.sum(w * l_ref[...], axis=1) + w_self
    num = jnp.sum(w * acc_ref[...], axis=1) + w_self * vn_ref[...]
    o_ref[...] = num / denom


def _moba_sample(q, kn, vn, k_pool, v_pool, page_table):
    db, n_pages = page_table.shape
    nb = n_pages // PAGES_PER_BLOCK
    ng = n_pages // PAGES_PER_STEP
    page_block = (1, MOBA_HEADS, MOBA_HEAD_DIM, PAGE_SIZE)
    page_spec = lambda x: pl.BlockSpec(page_block, lambda b, g, pt, x=x: (pt[b, g * PAGES_PER_STEP + x], 0, 0, 0))
    stat = pl.BlockSpec((1, 1, BLOCKS_PER_STEP, MOBA_HEADS, LANES), lambda b, g, pt: (b, g, 0, 0, 0))
    stat_shape = jax.ShapeDtypeStruct((db, ng, BLOCKS_PER_STEP, MOBA_HEADS, LANES), f32)
    qb = jnp.broadcast_to(q[..., None], (db,) + page_block[1:])
    stats = pl.pallas_call(
        _moba_sample_stream_kernel,
        grid_spec=pltpu.PrefetchScalarGridSpec(
            num_scalar_prefetch=1,
            grid=(db, ng),
            in_specs=[pl.BlockSpec(page_block, lambda b, g, pt: (b, 0, 0, 0))]
                     + [page_spec(x) for x in range(PAGES_PER_STEP)] * 2,
            out_specs=[stat] * 4),
        out_shape=[stat_shape] * 4,
        compiler_params=_params(("parallel", "parallel")),
        name="moba_sample_stream",
    )(page_table, qb, *([k_pool] * PAGES_PER_STEP), *([v_pool] * PAGES_PER_STEP))
    gate, m, l, acc = (a.reshape(db, nb, MOBA_HEADS, LANES) for a in stats)
    vn_wide = jnp.pad(vn, ((0, 0), (0, 0), (0, LANES - MOBA_HEAD_DIM)))
    return pl.pallas_call(
        _moba_sample_combine_kernel,
        out_shape=jax.ShapeDtypeStruct((db, MOBA_HEADS, LANES), f32),
        compiler_params=pltpu.CompilerParams(vmem_limit_bytes=VMEM_LIMIT),
        name="moba_sample_combine",
    )(gate, m, l, acc, q, kn, vn_wide)


HGRN_TILE = 512


def _hgrn_prompt_kernel(q_ref, fz_ref, i_ref, g_ref, lb_ref, ng_ref, o_ref, st_ref, state_ref):
    t = pl.program_id(1)
    c_rows = HGRN_CHUNK

    @pl.when(t == 0)
    def _():
        state_ref[...] = jnp.zeros(state_ref.shape, f32)

    row = lax.broadcasted_iota(i32, (c_rows, c_rows), 0)
    col = lax.broadcasted_iota(i32, (c_rows, c_rows), 1)
    causal = col <= row
    cum = jnp.where(causal, 1.0, 0.0)
    cum_sub = jnp.where(col < (row // HGRN_SUB) * HGRN_SUB, 1.0, 0.0)
    cum_both = jnp.concatenate([cum, cum_sub], axis=0)

    def chunk(c, carry):
        rows = pl.ds(pl.multiple_of(c * c_rows, c_rows), c_rows)
        lb_all = lb_ref[...]
        f_all = lb_all + (1.0 - lb_all) * _sigmoid(fz_ref[rows, :])
        both_all = jnp.dot(cum_both, jnp.log(f_all), precision=HI, preferred_element_type=f32)
        for h in range(HGRN_HEADS):
            lanes = slice(h * HGRN_DK, (h + 1) * HGRN_DK)
            ng = ng_ref[:, lanes]
            q = q_ref[rows, lanes]
            v = i_ref[rows, lanes]
            gg = g_ref[rows, lanes]
            kk = 1.0 - f_all[:, lanes]
            b = both_all[:c_rows, lanes]
            ref_row = both_all[c_rows:, lanes]
            st = state_ref[h]
            o = lax.dot_general((q * jnp.exp(b)).astype(bf16), st.astype(bf16), NT_DIMS, preferred_element_type=f32)
            qh = (q * jnp.exp(b - ref_row)).astype(bf16)
            parts = []
            for s in range(c_rows // HGRN_SUB):
                ref_s = ref_row[s * HGRN_SUB:s * HGRN_SUB + 1, :]
                kh = (kk * jnp.exp(jnp.minimum(ref_s - b, EXP_CLAMP))).astype(bf16)
                parts.append(lax.dot_general(qh[s * HGRN_SUB:(s + 1) * HGRN_SUB], kh, NT_DIMS,
                                             preferred_element_type=f32))
            att = jnp.where(causal, jnp.concatenate(parts, axis=0), 0.0)
            o = o + jnp.dot(att.astype(bf16), v.astype(bf16), preferred_element_type=f32)
            b_last = b[c_rows - 1:c_rows, :]
            kd = (kk * jnp.exp(b_last - b)).astype(bf16)
            state_ref[h] = st * jnp.exp(b_last) + lax.dot_general(v.astype(bf16), kd, TN_DIMS,
                                                                   preferred_element_type=f32)
            ms = jnp.mean(o * o, axis=1, keepdims=True)
            o_ref[rows, lanes] = o * lax.rsqrt(ms + RMS_EPS) * ng * (gg * _sigmoid(gg))
        return carry

    lax.fori_loop(0, HGRN_TILE // c_rows, chunk, 0)

    @pl.when(t == pl.num_programs(1) - 1)
    def _():
        for h in range(HGRN_HEADS):
            st_ref[0, h] = state_ref[h].T


def _hgrn_prompt(hproj, batch, seq, lb, norm_g):
    nt = seq // HGRN_TILE
    width = HGRN_HEADS * HGRN_DK
    col = lambda j: pl.BlockSpec((HGRN_TILE, width), lambda b, t, j=j: (b * nt + t, j))
    vec = pl.BlockSpec((1, width), lambda b, t: (0, 0))
    return pl.pallas_call(
        _hgrn_prompt_kernel,
        grid=(batch, nt),
        in_specs=[col(0), col(1), col(2), col(3), vec, vec],
        out_specs=[pl.BlockSpec((HGRN_TILE, width), lambda b, t: (b * nt + t, 0)),
                   pl.BlockSpec((1, HGRN_HEADS, HGRN_DK, HGRN_DK), lambda b, t: (b, 0, 0, 0))],
        out_shape=[jax.ShapeDtypeStruct((batch * seq, width), f32),
                   jax.ShapeDtypeStruct((batch, HGRN_HEADS, HGRN_DK, HGRN_DK), f32)],
        scratch_shapes=[pltpu.VMEM((HGRN_HEADS, HGRN_DK, HGRN_DK), f32)],
        compiler_params=_params(("parallel", "arbitrary")),
        name="hgrn_prompt",
    )(hproj, hproj, hproj, hproj, lb.reshape(1, width), norm_g.reshape(1, width))


def _hgrn_sample_kernel(qc_ref, fzc_ref, lbc_ref, i_ref, g_ref, ng_ref, st_ref, o_ref, nst_ref):
    for h in range(HGRN_HEADS):
        lb = lbc_ref[h]
        f = lb + (1.0 - lb) * _sigmoid(fzc_ref[0, h])
        st = f * st_ref[0, h] + (1.0 - f) * i_ref[0, h]
        nst_ref[0, h] = st
        o = jnp.sum(qc_ref[0, h] * st, axis=0, keepdims=True)
        ms = jnp.mean(o * o, axis=1, keepdims=True)
        gg = g_ref[0, h]
        o_ref[0, h] = o * lax.rsqrt(ms + RMS_EPS) * ng_ref[h] * (gg * _sigmoid(gg))


def _hgrn_sample(hs, state, lb, norm_g):
    db = hs.shape[0]
    width = HGRN_HEADS * HGRN_DK
    colv = lambda x: x.reshape(db, HGRN_HEADS, HGRN_DK, 1)
    rowv = lambda x: x.reshape(db, HGRN_HEADS, 1, HGRN_DK)
    cspec = pl.BlockSpec((1, HGRN_HEADS, HGRN_DK, 1), lambda b: (b, 0, 0, 0))
    rspec = pl.BlockSpec((1, HGRN_HEADS, 1, HGRN_DK), lambda b: (b, 0, 0, 0))
    sspec = pl.BlockSpec((1, HGRN_HEADS, HGRN_DK, HGRN_DK), lambda b: (b, 0, 0, 0))
    o, nst = pl.pallas_call(
        _hgrn_sample_kernel,
        grid=(db,),
        in_specs=[cspec, cspec, pl.BlockSpec((HGRN_HEADS, HGRN_DK, 1), lambda b: (0, 0, 0)), rspec, rspec,
                  pl.BlockSpec((HGRN_HEADS, 1, HGRN_DK), lambda b: (0, 0, 0)), sspec],
        out_specs=[rspec, sspec],
        out_shape=[jax.ShapeDtypeStruct((db, HGRN_HEADS, 1, HGRN_DK), f32),
                   jax.ShapeDtypeStruct((db, HGRN_HEADS, HGRN_DK, HGRN_DK), f32)],
        compiler_params=_params(("parallel",)),
        name="hgrn_sample",
    )(colv(hs[:, :width]), colv(hs[:, width:2 * width]), lb.reshape(HGRN_HEADS, HGRN_DK, 1),
      rowv(hs[:, 2 * width:3 * width]), rowv(hs[:, 3 * width:]), norm_g.reshape(HGRN_HEADS, 1, HGRN_DK), state)
    return o.reshape(db, width), nst


PEER_EBLK = 1024
PEER_RANKS = PEER_TOPK + 1
PEER_VROWS = 24
PEER_SUB = 8
PEER_ROWS = 16


def _extract_max(tiles):
    mx = functools.reduce(jnp.maximum, tiles)
    mx = jnp.max(mx, axis=0, keepdims=True)
    return mx, [jnp.where(x == mx, FLOOR, x) for x in tiles]


def _peer_kernel(x_ref, q_ref, k1_ref, k2_ref, u_ref, v_ref, g_ref, b_ref, o_ref,
                 xt_ref, acc_ref, act_ref, p_ref, a_ref, bw_ref, c_ref, s2_ref, v1_ref, v2_ref, ab_ref, cb_ref):
    e = pl.program_id(1)
    tt = x_ref.shape[0]
    n_lane_chunks = tt // LANES
    sub = PEER_SUB
    n_sub = PEER_NKEYS // sub
    keys_per_step = PEER_EBLK // PEER_NKEYS

    @pl.when(e == 0)
    def _():
        xt_ref[...] = x_ref[...].T.astype(bf16)
        acc_ref[...] = jnp.zeros(acc_ref.shape, f32)
        floor_rows = jnp.full((PEER_VROWS, LANES), FLOOR, f32)
        for h in range(PEER_HEADS):
            q1 = q_ref[:, (2 * h) * PEER_DHALF:(2 * h + 1) * PEER_DHALF]
            q2 = q_ref[:, (2 * h + 1) * PEER_DHALF:(2 * h + 2) * PEER_DHALF]
            s1 = lax.dot_general(k1_ref[h], q1, NT_DIMS, precision=HI, preferred_element_type=f32)
            c_ref[h] = s1.reshape(n_sub, sub, tt)
            s2_ref[h] = lax.dot_general(k2_ref[h], q2, NT_DIMS, precision=HI, preferred_element_type=f32)

        def per_head(h, carry):
            for lc in range(n_lane_chunks):
                lanes = slice(lc * LANES, (lc + 1) * LANES)
                for side, vals_ref in enumerate((v1_ref, v2_ref)):
                    vals_ref[...] = floor_rows
                    if side == 0:
                        tiles = [c_ref[h, k, :, lanes] for k in range(n_sub)]
                    else:
                        tiles = [s2_ref[h, k * sub:(k + 1) * sub, lanes] for k in range(n_sub)]
                    for r in range(PEER_RANKS):
                        mx, tiles = _extract_max(tiles)
                        vals_ref[r:r + 1, :] = mx
                cands = [v1_ref[0:1, :] + v2_ref[r:r + sub, :] for r in range(0, PEER_VROWS, sub)]
                cands += [v1_ref[r:r + 1, :] + v2_ref[0:sub, :] for r in range(1, sub)]
                cands += [v1_ref[r:r + sub, :] + v2_ref[0:1, :] for r in range(sub, PEER_VROWS, sub)]
                best = v1_ref[0:1, :] + v2_ref[0:1, :]
                zsum = jnp.zeros((1, LANES), f32)
                kth = best
                for r in range(PEER_TOPK):
                    kth, cands = _extract_max(cands)
                    zsum = zsum + jnp.exp(kth - best)
                nxt, _ = _extract_max(cands)
                thresh = 0.5 * (kth + nxt)
                s1 = c_ref[h, :, :, lanes]
                a_ref[h, :, :, lanes] = jnp.exp(s1 - v1_ref[0:1, :]) / zsum
                bw_ref[h, :, lanes] = jnp.exp(s2_ref[h, :, lanes] - v2_ref[0:1, :])
                c_ref[h, :, :, lanes] = thresh - s1
            return carry

        lax.fori_loop(0, PEER_HEADS, per_head, 0)

    act_ref[...] = jnp.dot(u_ref[...], xt_ref[...], preferred_element_type=f32)
    for h in range(PEER_HEADS):
        for ii in range(keys_per_step):
            ab_ref[ii, h] = jnp.broadcast_to(a_ref[h, e, ii:ii + 1, :], (sub, tt))
            cb_ref[ii, h] = jnp.broadcast_to(c_ref[h, e, ii:ii + 1, :], (sub, tt))

    def per_second_keys(k, carry):
        r0 = pl.multiple_of(k * PEER_ROWS, PEER_ROWS)
        for lc in range(n_lane_chunks):
            lanes = slice(lc * LANES, (lc + 1) * LANES)
            w = [jnp.zeros((PEER_ROWS // sub, sub, LANES), f32) for _ in range(keys_per_step)]
            for h in range(PEER_HEADS):
                s2 = s2_ref[h, pl.ds(r0, PEER_ROWS), lanes].reshape(PEER_ROWS // sub, sub, LANES)
                bw = bw_ref[h, pl.ds(r0, PEER_ROWS), lanes].reshape(PEER_ROWS // sub, sub, LANES)
                for ii in range(keys_per_step):
                    hit = s2 >= cb_ref[ii, h, :, lanes][None]
                    w[ii] = w[ii] + ab_ref[ii, h, :, lanes][None] * jnp.where(hit, bw, 0.0)
            for ii in range(keys_per_step):
                rows = pl.ds(pl.multiple_of(ii * PEER_NKEYS + r0, PEER_ROWS), PEER_ROWS)
                act = act_ref[rows, lanes]
                gelu = 0.5 * act * (1.0 + lax.erf(act * (2.0 ** -0.5)))
                p_ref[rows, lanes] = (w[ii].reshape(PEER_ROWS, LANES) * gelu).astype(bf16)
        return carry

    lax.fori_loop(0, PEER_NKEYS // PEER_ROWS, per_second_keys, 0)
    acc_ref[...] += lax.dot_general(p_ref[...], v_ref[...], TN_DIMS, preferred_element_type=f32)

    @pl.when(e == pl.num_programs(1) - 1)
    def _():
        o_ref[...] = _layer_norm(DN_ALPHA * x_ref[...] + acc_ref[...], g_ref[...], b_ref[...])


def _peer_ln(x, wq, keys, u, v, g, b, *, tt=TOKEN_TILE):
    m, d = x.shape
    n_exp = u.shape[0]
    q = _mm(x, wq, split=True, tm=tt)
    nq = q.shape[1]
    return pl.pallas_call(
        _peer_kernel,
        grid=(m // tt, n_exp // PEER_EBLK),
        in_specs=[pl.BlockSpec((tt, d), lambda t, e: (t, 0)),
                  pl.BlockSpec((tt, nq), lambda t, e: (t, 0)),
                  pl.BlockSpec((PEER_HEADS, PEER_NKEYS, PEER_DHALF), lambda t, e: (0, 0, 0)),
                  pl.BlockSpec((PEER_HEADS, PEER_NKEYS, PEER_DHALF), lambda t, e: (0, 0, 0)),
                  pl.BlockSpec((PEER_EBLK, d), lambda t, e: (e, 0)),
                  pl.BlockSpec((PEER_EBLK, d), lambda t, e: (e, 0)),
                  pl.BlockSpec((1, d), lambda t, e: (0, 0)),
                  pl.BlockSpec((1, d), lambda t, e: (0, 0))],
        out_specs=pl.BlockSpec((tt, d), lambda t, e: (t, 0)),
        out_shape=jax.ShapeDtypeStruct((m, d), f32),
        scratch_shapes=[pltpu.VMEM((d, tt), bf16), pltpu.VMEM((tt, d), f32),
                        pltpu.VMEM((PEER_EBLK, tt), f32), pltpu.VMEM((PEER_EBLK, tt), bf16)]
                       + [pltpu.VMEM((PEER_HEADS, PEER_NKEYS // PEER_SUB, PEER_SUB, tt), f32),
                          pltpu.VMEM((PEER_HEADS, PEER_NKEYS, tt), f32),
                          pltpu.VMEM((PEER_HEADS, PEER_NKEYS // PEER_SUB, PEER_SUB, tt), f32),
                          pltpu.VMEM((PEER_HEADS, PEER_NKEYS, tt), f32)]
                       + [pltpu.VMEM((PEER_VROWS, LANES), f32)] * 2
                       + [pltpu.VMEM((PEER_EBLK // PEER_NKEYS, PEER_HEADS, PEER_SUB, tt), f32)] * 2,
        compiler_params=_params(("parallel", "arbitrary")),
        name="peer",
    )(x, q, keys[0], keys[1], u.astype(bf16), v.astype(bf16), g.reshape(1, d), b.reshape(1, d))


def kernel(x_prompt, x_sample, cache_k, cache_v, state_conv, state_hgrn, page_table, w_in_even, conv_w, conv_b, conv_ln_g, conv_ln_b, w_out_even, w_in_odd, hgrn_lb_logits, hgrn_norm_g, w_out_odd, ln_g, ln_b, peer_wq, peer_keys, peer_u, peer_v):
    batch, seq, d = x_prompt.shape
    db = x_sample.shape[0]
    n_prompt = batch * seq
    pad_sample = lambda a: jnp.pad(a, ((0, SAMPLE_TILE - a.shape[0]), (0, 0)))
    pages = lambda a: a.reshape(batch, seq // PAGE_SIZE, PAGE_SIZE, MOBA_HEADS, MOBA_HEAD_DIM)
    heads = lambda a: a.reshape(db, MOBA_HEADS, MOBA_HEAD_DIM)

    lb_p = jax.nn.softmax(hgrn_lb_logits.astype(f32), axis=0)
    lb_all = jnp.cumsum(lb_p, axis=0) - lb_p[0]

    xp = x_prompt.reshape(n_prompt, d)
    xs = pad_sample(x_sample.reshape(db, d))
    kp_l, vp_l, ks_l, vs_l, cp_l, cs_l, hp_l, hs_l = [], [], [], [], [], [], [], []
    for l in range(DEPTH):
        if l % 2 == 0:
            e = l // 2
            w_qk, w_vag = w_in_even[e][:, :2 * MOBA_WIDTH], w_in_even[e][:, 2 * MOBA_WIDTH:]
            qk_p, qk_s = _mm(xp, w_qk, split=True), _mm(xs, w_qk, split=True, tm=SAMPLE_TILE)[:db]
            vag_p, vag_s = _mm(xp, w_vag), _mm(xs, w_vag, tm=SAMPLE_TILE)[:db]
            kp_l.append(pages(qk_p[:, MOBA_WIDTH:]))
            vp_l.append(pages(vag_p[:, :MOBA_WIDTH]))
            q_s, k_s, v_s = qk_s[:, :MOBA_WIDTH], qk_s[:, MOBA_WIDTH:], vag_s[:, :MOBA_WIDTH]
            ks_l.append(k_s.reshape(db, 1, MOBA_HEADS, MOBA_HEAD_DIM))
            vs_l.append(v_s.reshape(db, 1, MOBA_HEADS, MOBA_HEAD_DIM))
            att_p = _moba_prompt(qk_p, vag_p, batch, seq)
            att_s = _moba_sample(heads(q_s), heads(k_s), heads(v_s),
                                 cache_k[e].transpose(0, 2, 3, 1), cache_v[e].transpose(0, 2, 3, 1), page_table)
            att_s = att_s[:, :, :MOBA_HEAD_DIM].reshape(db, MOBA_WIDTH)
            cprm = (conv_w[e], conv_b[e], conv_ln_g[e], conv_ln_b[e])
            cy_p, buf_p = _conv_prompt(vag_p, batch, seq, *cprm)
            cy_s, buf_s = _conv_sample(vag_s[:, MOBA_WIDTH:MOBA_WIDTH + CONV_CH], vag_s[:, MOBA_WIDTH + CONV_CH:],
                                       state_conv[e], *cprm)
            cp_l.append(buf_p)
            cs_l.append(buf_s)
            mix_p, mix_s = [att_p, cy_p], [pad_sample(att_s), pad_sample(cy_s)]
            w_out = w_out_even[e]
        else:
            oi = l // 2
            hproj_p, hproj_s = _mm(xp, w_in_odd[oi], tm=256), _mm(xs, w_in_odd[oi], tm=SAMPLE_TILE)[:db]
            o_p, st_p = _hgrn_prompt(hproj_p, batch, seq, lb_all[l], hgrn_norm_g[oi])
            o_s, st_s = _hgrn_sample(hproj_s, state_hgrn[oi].astype(f32), lb_all[l], hgrn_norm_g[oi])
            hp_l.append(st_p)
            hs_l.append(st_s)
            mix_p, mix_s = [o_p], [pad_sample(o_s)]
            w_out = w_out_odd[oi]
        xp = _mm_res_ln(mix_p, w_out, xp, ln_g[l, 0], ln_b[l, 0])
        xs = _mm_res_ln(mix_s, w_out, xs, ln_g[l, 0], ln_b[l, 0], tm=SAMPLE_TILE)
        prm = (peer_wq[l], peer_keys[l], peer_u[l], peer_v[l], ln_g[l, 1], ln_b[l, 1])
        xp = _peer_ln(xp, *prm)
        xs = _peer_ln(xs, *prm, tt=SAMPLE_TILE)
    y_prompt = xp.reshape(batch, seq, d)
    y_sample = xs[:db].reshape(db, 1, d)
    return (y_prompt, y_sample, jnp.stack(kp_l), jnp.stack(vp_l), jnp.stack(ks_l), jnp.stack(vs_l),
            jnp.stack(cp_l), jnp.stack(cs_l), jnp.stack(hp_l), jnp.stack(hs_l))
```

```python
import functools
import math

import jax
import jax.numpy as jnp
from jax import lax
from jax.experimental import pallas as pl
from jax.experimental.pallas import tpu as pltpu

f32 = jnp.float32
bf16 = jnp.bfloat16
i32 = jnp.int32
HI = lax.Precision.HIGHEST

D_MODEL = 1024
DEPTH = 2
PAGE_SIZE = 128
MOBA_HEADS = 8
MOBA_HEAD_DIM = 64
MOBA_WIDTH = MOBA_HEADS * MOBA_HEAD_DIM
MOBA_BLOCK = 256
MOBA_TOPK = 3
MOBA_SCALE = MOBA_HEAD_DIM ** -0.5
CONV_CH = D_MODEL // 2
CONV_WIDTH = 31
HGRN_HEADS = 8
HGRN_DK = D_MODEL // HGRN_HEADS
HGRN_CHUNK = 64
HGRN_SUB = 16
PEER_HEADS = 8
PEER_NKEYS = 128
PEER_DHALF = 128
PEER_TOPK = 16
LN_EPS = 1e-5
RMS_EPS = 1e-6
DN_ALPHA = (2 * DEPTH) ** 0.25

LANES = 128
TOKEN_TILE = 512
SAMPLE_TILE = 128
MASK_NEG = -1e30
FLOOR = -3e38
EXP_CLAMP = 60.0
VMEM_LIMIT = 56 << 20

NT_DIMS = (((1,), (1,)), ((), ()))
TN_DIMS = (((0,), (0,)), ((), ()))


def _params(semantics, vmem=VMEM_LIMIT):
    return pltpu.CompilerParams(dimension_semantics=semantics, vmem_limit_bytes=vmem)


def _layer_norm(y, g, b):
    mu = jnp.mean(y, axis=-1, keepdims=True)
    d = y - mu
    var = jnp.mean(d * d, axis=-1, keepdims=True)
    return d * lax.rsqrt(var + LN_EPS) * g + b


def _sigmoid(x):
    return 1.0 / (1.0 + jnp.exp(-x))


MM_CHUNK = 512


def _mm_kernel(x_ref, *refs, split, paged_chunk):
    refs = list(refs)
    t_ref = refs.pop() if paged_chunk is not None else None
    o_ref = refs.pop()
    wh_ref = refs[0]
    x = x_ref[...]
    xh = x.astype(bf16)
    if split:
        xl = (x - xh.astype(f32)).astype(bf16)
    for c, j in enumerate(range(0, o_ref.shape[1], MM_CHUNK)):
        wh = wh_ref[:, j:j + MM_CHUNK]
        acc = jnp.dot(xh, wh, preferred_element_type=f32)
        if split:
            acc = acc + jnp.dot(xl, wh, preferred_element_type=f32)
            acc = acc + jnp.dot(xh, refs[1][:, j:j + MM_CHUNK], preferred_element_type=f32)
        o_ref[:, j:j + MM_CHUNK] = acc
        if c == paged_chunk:
            for pg in range(o_ref.shape[0] // PAGE_SIZE):
                t_ref[pg] = acc[pg * PAGE_SIZE:(pg + 1) * PAGE_SIZE, :].T


def _mm(x, w, *, split=False, tm=TOKEN_TILE, paged_chunk=None):
    m, k = x.shape
    n = w.shape[1]
    wh = w.astype(bf16)
    ws = [wh] + ([(w - wh.astype(f32)).astype(bf16)] if split else [])
    out_specs = [pl.BlockSpec((tm, n), lambda i: (i, 0))]
    out_shape = [jax.ShapeDtypeStruct((m, n), f32)]
    if paged_chunk is not None:
        out_specs.append(pl.BlockSpec((tm // PAGE_SIZE, MM_CHUNK, PAGE_SIZE), lambda i: (i, 0, 0)))
        out_shape.append(jax.ShapeDtypeStruct((m // PAGE_SIZE, MM_CHUNK, PAGE_SIZE), f32))
    outs = pl.pallas_call(
        functools.partial(_mm_kernel, split=split, paged_chunk=paged_chunk),
        grid=(m // tm,),
        in_specs=[pl.BlockSpec((tm, k), lambda i: (i, 0))] + [pl.BlockSpec((k, n), lambda i: (0, 0))] * len(ws),
        out_specs=out_specs,
        out_shape=out_shape,
        compiler_params=_params(("parallel",)),
        name="proj_split" if split else "proj",
    )(x, *ws)
    return outs[0] if paged_chunk is None else outs


def _mm_res_ln_kernel(*refs):
    *a_refs, w_ref, x_ref, g_ref, b_ref, o_ref = refs
    acc = DN_ALPHA * x_ref[...]
    row = 0
    for a_ref in a_refs:
        k = a_ref.shape[1]
        acc = acc + jnp.dot(a_ref[...].astype(bf16), w_ref[row:row + k, :], preferred_element_type=f32)
        row += k
    o_ref[...] = _layer_norm(acc, g_ref[...], b_ref[...])


def _mm_res_ln(parts, w, x, g, b, *, tm=TOKEN_TILE):
    m, n = x.shape
    k = w.shape[0]
    return pl.pallas_call(
        _mm_res_ln_kernel,
        grid=(m // tm,),
        in_specs=[pl.BlockSpec((tm, a.shape[1]), lambda i: (i, 0)) for a in parts]
                 + [pl.BlockSpec((k, n), lambda i: (0, 0)),
                    pl.BlockSpec((tm, n), lambda i: (i, 0)), pl.BlockSpec((1, n), lambda i: (0, 0)),
                    pl.BlockSpec((1, n), lambda i: (0, 0))],
        out_specs=pl.BlockSpec((tm, n), lambda i: (i, 0)),
        out_shape=jax.ShapeDtypeStruct((m, n), f32),
        compiler_params=_params(("parallel",)),
        name="out_proj_ln",
    )(*parts, w.astype(bf16), x, g.reshape(1, n), b.reshape(1, n))


CONV_TILE = 256
CONV_ROWS = 64
CONV_HIST = 32
SUBLANES = 8
CONV_SHIFT_ROWS = CONV_TILE + CONV_HIST - SUBLANES


def _conv_prompt_kernel(a_ref, gt_ref, w_ref, cb_ref, g_ref, b_ref, y_ref, st_ref, buf_ref, shift_ref):
    t = pl.program_id(1)

    @pl.when(t == 0)
    def _():
        buf_ref[0:CONV_HIST, :] = jnp.zeros((CONV_HIST, CONV_CH), f32)

    buf_ref[CONV_HIST:CONV_HIST + CONV_TILE, :] = a_ref[...] * _sigmoid(gt_ref[...])
    for q in range(1, SUBLANES):
        shift_ref[q - 1] = buf_ref[q:q + CONV_SHIFT_ROWS, :]
    first = CONV_HIST - (CONV_WIDTH - 1)
    for r in range(0, CONV_TILE, CONV_ROWS):
        acc = jnp.zeros((CONV_ROWS, CONV_CH), f32) + cb_ref[...]
        for j in range(CONV_WIDTH):
            q = (first + j) % SUBLANES
            base = r + first + j - q
            rows = buf_ref[base:base + CONV_ROWS, :] if q == 0 else shift_ref[q - 1, base:base + CONV_ROWS, :]
            acc = acc + w_ref[j:j + 1, :] * rows
        y = _layer_norm(acc, g_ref[...], b_ref[...])
        y_ref[r:r + CONV_ROWS, :] = y * _sigmoid(y)
    tail = buf_ref[CONV_TILE:CONV_TILE + CONV_HIST, :]
    buf_ref[0:CONV_HIST, :] = tail

    @pl.when(t == pl.num_programs(1) - 1)
    def _():
        st_ref[0] = tail


def _conv_prompt(vag, batch, seq, w, cb, g, b):
    nt = seq // CONV_TILE
    wpad = jnp.zeros((CONV_HIST, CONV_CH), f32).at[:CONV_WIDTH].set(w)
    row = lambda v: v.reshape(1, CONV_CH)
    y, st = pl.pallas_call(
        _conv_prompt_kernel,
        grid=(batch, nt),
        in_specs=[pl.BlockSpec((CONV_TILE, CONV_CH), lambda bi, t: (bi * nt + t, 1)),
                  pl.BlockSpec((CONV_TILE, CONV_CH), lambda bi, t: (bi * nt + t, 2)),
                  pl.BlockSpec((CONV_HIST, CONV_CH), lambda bi, t: (0, 0))]
                 + [pl.BlockSpec((1, CONV_CH), lambda bi, t: (0, 0))] * 3,
        out_specs=[pl.BlockSpec((CONV_TILE, CONV_CH), lambda bi, t: (bi * nt + t, 0)),
                   pl.BlockSpec((1, CONV_HIST, CONV_CH), lambda bi, t: (bi, 0, 0))],
        out_shape=[jax.ShapeDtypeStruct((batch * seq, CONV_CH), f32),
                   jax.ShapeDtypeStruct((batch, CONV_HIST, CONV_CH), f32)],
        scratch_shapes=[pltpu.VMEM((CONV_TILE + CONV_HIST, CONV_CH), f32),
                        pltpu.VMEM((SUBLANES - 1, CONV_SHIFT_ROWS, CONV_CH), f32)],
        compiler_params=_params(("arbitrary", "arbitrary")),
        name="conv_prompt",
    )(vag, vag, wpad, row(cb), row(g), row(b))
    return y, st[:, CONV_HIST - (CONV_WIDTH - 1):]


def _conv_sample_kernel(a_ref, gt_ref, st_ref, w_ref, cb_ref, g_ref, b_ref, y_ref, nst_ref):
    u = a_ref[...] * _sigmoid(gt_ref[...])
    acc = cb_ref[...] + w_ref[CONV_WIDTH - 1:CONV_WIDTH, :] * u
    for j in range(CONV_WIDTH - 1):
        acc = acc + w_ref[j:j + 1, :] * st_ref[j]
    y = _layer_norm(acc, g_ref[...], b_ref[...])
    y_ref[...] = y * _sigmoid(y)
    for j in range(CONV_WIDTH - 2):
        nst_ref[j] = st_ref[j + 1]
    nst_ref[CONV_WIDTH - 2] = u


def _conv_sample(a, gt, state, w, cb, g, b):
    db = a.shape[0]
    wpad = jnp.zeros((CONV_HIST, CONV_CH), f32).at[:CONV_WIDTH].set(w)
    row = lambda v: v.reshape(1, CONV_CH)
    y, nst = pl.pallas_call(
        _conv_sample_kernel,
        out_shape=[jax.ShapeDtypeStruct((db, CONV_CH), f32),
                   jax.ShapeDtypeStruct((CONV_WIDTH - 1, db, CONV_CH), f32)],
        name="conv_sample",
    )(a, gt, state.transpose(1, 0, 2), wpad, row(cb), row(g), row(b))
    return y, nst.transpose(1, 0, 2)


GATE_ROWS = 1024
ATTN_KEYS = 2 * MOBA_BLOCK


def _block_sum_kernel(k_ref, o_ref):
    o_ref[0] = jnp.sum(k_ref[...], axis=0, keepdims=True)


def _block_sums(qk, batch, seq):
    nb = seq // MOBA_BLOCK
    return pl.pallas_call(
        _block_sum_kernel,
        grid=(batch * nb,),
        in_specs=[pl.BlockSpec((MOBA_BLOCK, MOBA_WIDTH), lambda i: (i, 1))],
        out_specs=pl.BlockSpec((1, 1, MOBA_WIDTH), lambda i: (i, 0, 0)),
        out_shape=jax.ShapeDtypeStruct((batch * nb, 1, MOBA_WIDTH), f32),
        compiler_params=_params(("parallel",)),
        name="moba_block_sums",
    )(qk)


def _moba_gate_kernel(q_ref, k_ref, v_ref, ks_ref, qa_ref, ka_ref, vb_ref, *, nb):
    t = pl.program_id(2)
    lane = lax.broadcasted_iota(i32, (MOBA_BLOCK, LANES), 1)
    blk = lane - MOBA_HEAD_DIM
    km_lane = lax.broadcasted_iota(i32, (nb, LANES), 1)
    ksum = ks_ref[0] * (1.0 / MOBA_BLOCK)
    pad_top = jnp.zeros((MOBA_HEAD_DIM, LANES), f32)
    pad_bot = jnp.zeros((LANES - MOBA_HEAD_DIM - nb, LANES), f32)
    for c in range(GATE_ROWS // MOBA_BLOCK):
        own = t * (GATE_ROWS // MOBA_BLOCK) + c
        rows = slice(c * MOBA_BLOCK, (c + 1) * MOBA_BLOCK)
        q2 = q_ref[rows, :]
        k2 = k_ref[rows, :]
        vb_ref[rows, :] = v_ref[rows, :].astype(bf16)
        cand = (blk >= 0) & (blk < own)
        for j in range(2):
            head = (km_lane >= j * MOBA_HEAD_DIM) & (km_lane < (j + 1) * MOBA_HEAD_DIM)
            km = jnp.concatenate([pad_top, jnp.where(head, ksum, 0.0), pad_bot], axis=0)
            gate = lax.dot_general(q2, km, NT_DIMS, precision=HI, preferred_element_type=f32)
            masked = jnp.where(cand, gate, FLOOR)
            sel = jnp.zeros((MOBA_BLOCK, LANES), f32)
            for _ in range(MOBA_TOPK):
                mx = jnp.max(masked, axis=1, keepdims=True)
                hit = (masked == mx) & cand
                sel = jnp.where(hit, 1.0, sel)
                masked = jnp.where(hit, FLOOR, masked)
            bias = jnp.where((sel > 0.0) | (blk == own), 0.0, MASK_NEG)
            qj = q2 if j == 0 else pltpu.roll(q2, MOBA_HEAD_DIM, 1)
            kj = k2 if j == 0 else pltpu.roll(k2, MOBA_HEAD_DIM, 1)
            in_head = lane < MOBA_HEAD_DIM
            in_bias = lane < MOBA_HEAD_DIM + nb
            qa = jnp.where(in_head, qj * MOBA_SCALE, jnp.where(in_bias, bias, 0.0))
            ka = jnp.where(in_head, kj, jnp.where(blk == own, 1.0, 0.0))
            qa_ref[0, j, rows, :] = qa.astype(bf16)
            ka_ref[0, j, rows, :] = ka.astype(bf16)


def _moba_attn_kernel(qa_ref, ka_ref, vb_ref, o_ref):
    i = pl.program_id(1)
    last = i // 2
    row = lax.broadcasted_iota(i32, (MOBA_BLOCK, ATTN_KEYS), 0)
    col = lax.broadcasted_iota(i32, (MOBA_BLOCK, ATTN_KEYS), 1)
    lane = lax.broadcasted_iota(i32, (MOBA_BLOCK, LANES), 1)

    def tile(h, k0, mask):
        s = lax.dot_general(qa_ref[0, h], ka_ref[0, h, pl.ds(k0, ATTN_KEYS), :], NT_DIMS, preferred_element_type=f32)
        return s if mask is None else jnp.where(mask, s, MASK_NEG)

    def pair_values(h, k0):
        return vb_ref[pl.ds(k0, ATTN_KEYS), (h // 2) * LANES:(h // 2 + 1) * LANES]

    k_last = pl.multiple_of(last * ATTN_KEYS, ATTN_KEYS)
    visible = (k_last + col) <= (i * MOBA_BLOCK + row)
    state = []
    for h in range(MOBA_HEADS):
        s = tile(h, k_last, visible)
        m = jnp.max(s, axis=1, keepdims=True)
        p = jnp.exp(s - m)
        state += [m, jnp.sum(p, axis=1, keepdims=True),
                  jnp.dot(p.astype(bf16), pair_values(h, k_last), preferred_element_type=f32)]

    def body(n, carry):
        k0 = pl.multiple_of(n * ATTN_KEYS, ATTN_KEYS)
        out = []
        for h in range(MOBA_HEADS):
            m, l, acc = carry[3 * h:3 * h + 3]
            s = tile(h, k0, None)
            mn = jnp.maximum(m, jnp.max(s, axis=1, keepdims=True))
            alpha = jnp.exp(m - mn)
            p = jnp.exp(s - mn)
            out += [mn, alpha * l + jnp.sum(p, axis=1, keepdims=True),
                    alpha * acc + jnp.dot(p.astype(bf16), pair_values(h, k0), preferred_element_type=f32)]
        return tuple(out)

    final = lax.fori_loop(0, last, body, tuple(state))
    for pair in range(MOBA_HEADS // 2):
        (_, l0, acc0), (_, l1, acc1) = final[6 * pair:6 * pair + 3], final[6 * pair + 3:6 * pair + 6]
        o_ref[:, pair * LANES:(pair + 1) * LANES] = jnp.where(lane < MOBA_HEAD_DIM, acc0 / l0, acc1 / l1)


def _moba_prompt(qk, vag, batch, seq):
    nb = seq // MOBA_BLOCK
    npair = MOBA_WIDTH // LANES
    nt = seq // GATE_ROWS
    ksums = _block_sums(qk, batch, seq).reshape(batch, nb, MOBA_WIDTH)
    qa, ka, vb = pl.pallas_call(
        functools.partial(_moba_gate_kernel, nb=nb),
        grid=(batch, npair, nt),
        in_specs=[pl.BlockSpec((GATE_ROWS, LANES), lambda b, p, t: (b * nt + t, p)),
                  pl.BlockSpec((GATE_ROWS, LANES), lambda b, p, t: (b * nt + t, npair + p)),
                  pl.BlockSpec((GATE_ROWS, LANES), lambda b, p, t: (b * nt + t, p)),
                  pl.BlockSpec((1, nb, LANES), lambda b, p, t: (b, 0, p))],
        out_specs=[pl.BlockSpec((1, 2, GATE_ROWS, LANES), lambda b, p, t: (b, p, t, 0)),
                   pl.BlockSpec((1, 2, GATE_ROWS, LANES), lambda b, p, t: (b, p, t, 0)),
                   pl.BlockSpec((GATE_ROWS, LANES), lambda b, p, t: (b * nt + t, p))],
        out_shape=[jax.ShapeDtypeStruct((batch, MOBA_HEADS, seq, LANES), bf16),
                   jax.ShapeDtypeStruct((batch, MOBA_HEADS, seq, LANES), bf16),
                   jax.ShapeDtypeStruct((batch * seq, MOBA_WIDTH), bf16)],
        compiler_params=_params(("parallel", "parallel", "parallel")),
        name="moba_gate",
    )(qk, qk, vag, ksums)
    return pl.pallas_call(
        _moba_attn_kernel,
        grid=(batch, nb),
        in_specs=[pl.BlockSpec((1, MOBA_HEADS, MOBA_BLOCK, LANES), lambda b, i: (b, 0, i, 0)),
                  pl.BlockSpec((1, MOBA_HEADS, seq, LANES), lambda b, i: (b, 0, 0, 0),
                               pipeline_mode=pl.Buffered(1)),
                  pl.BlockSpec((seq, MOBA_WIDTH), lambda b, i: (b, 0), pipeline_mode=pl.Buffered(1))],
        out_specs=pl.BlockSpec((MOBA_BLOCK, MOBA_WIDTH), lambda b, i: (b * nb + i, 0)),
        out_shape=jax.ShapeDtypeStruct((batch * seq, MOBA_WIDTH), f32),
        compiler_params=_params(("parallel", "arbitrary")),
        name="moba_attn",
    )(qa, ka, vb)


PAGES_PER_STEP = 8
PAGES_PER_BLOCK = MOBA_BLOCK // PAGE_SIZE
BLOCKS_PER_STEP = PAGES_PER_STEP // PAGES_PER_BLOCK


def _moba_sample_stream_kernel(pt_ref, qb_ref, *refs):
    del pt_ref
    k_refs = refs[:PAGES_PER_STEP]
    v_refs = refs[PAGES_PER_STEP:2 * PAGES_PER_STEP]
    gate_ref, m_ref, l_ref, acc_ref = refs[2 * PAGES_PER_STEP:]
    qb = qb_ref[0]
    head = lax.broadcasted_iota(i32, (MOBA_HEADS, LANES), 0)
    tile = (MOBA_HEADS, LANES)
    for jj in range(BLOCKS_PER_STEP):
        pages = range(PAGES_PER_BLOCK * jj, PAGES_PER_BLOCK * (jj + 1))
        raw = [jnp.sum(k_refs[x][0] * qb, axis=1) for x in pages]
        gate = functools.reduce(jnp.add, [jnp.sum(r, axis=1, keepdims=True) for r in raw]) * (1.0 / MOBA_BLOCK)
        m = functools.reduce(jnp.maximum, [jnp.max(r, axis=1, keepdims=True) for r in raw]) * MOBA_SCALE
        l = jnp.zeros((MOBA_HEADS, 1), f32)
        pv = jnp.zeros((MOBA_HEADS, MOBA_WIDTH), f32)
        for r, x in zip(raw, pages):
            p = jnp.exp(r * MOBA_SCALE - m)
            l = l + jnp.sum(p, axis=1, keepdims=True)
            vt = v_refs[x][0].reshape(MOBA_WIDTH, PAGE_SIZE).astype(bf16)
            pv = pv + lax.dot_general(p.astype(bf16), vt, NT_DIMS, preferred_element_type=f32)
        acc = jnp.zeros(tile, f32)
        for pair in range(MOBA_WIDTH // LANES):
            both = pv[:, pair * LANES:(pair + 1) * LANES]
            acc = acc + jnp.where(head == 2 * pair, both, 0.0)
            acc = acc + jnp.where(head == 2 * pair + 1, pltpu.roll(both, MOBA_HEAD_DIM, 1), 0.0)
        gate_ref[0, 0, jj] = jnp.broadcast_to(gate, tile)
        m_ref[0, 0, jj] = jnp.broadcast_to(m, tile)
        l_ref[0, 0, jj] = jnp.broadcast_to(l, tile)
        acc_ref[0, 0, jj] = acc


def _moba_sample_combine_kernel(gate_ref, m_ref, l_ref, acc_ref, q_ref, kn_ref, vn_ref, o_ref):
    db, nb, nh, width = gate_ref.shape
    masked = gate_ref[...]
    sel = jnp.zeros((db, nb, nh, width), f32)
    for _ in range(MOBA_TOPK):
        mx = jnp.max(masked, axis=1, keepdims=True)
        hit = masked == mx
        sel = jnp.where(hit, 1.0, sel)
        masked = jnp.where(hit, FLOOR, masked)
    chosen = sel > 0.0
    s_self = jnp.sum(q_ref[...] * kn_ref[...], axis=-1, keepdims=True) * MOBA_SCALE
    m = m_ref[...]
    top = jnp.maximum(jnp.max(jnp.where(chosen, m, FLOOR), axis=1), s_self)
    w = jnp.where(chosen, jnp.exp(m - top[:, None]), 0.0)
    w_self = jnp.exp(s_self - top)
    denom = jnp.sum(w * l_ref[...], axis=1) + w_self
    num = jnp.sum(w * acc_ref[...], axis=1) + w_self * vn_ref[...]
    o_ref[...] = num / denom


def _moba_sample(q, kn, vn, k_pool, v_pool, page_table):
    db, n_pages = page_table.shape
    nb = n_pages // PAGES_PER_BLOCK
    ng = n_pages // PAGES_PER_STEP
    page_block = (1, MOBA_HEADS, MOBA_HEAD_DIM, PAGE_SIZE)
    page_spec = lambda x: pl.BlockSpec(page_block, lambda b, g, pt, x=x: (pt[b, g * PAGES_PER_STEP + x], 0, 0, 0))
    stat = pl.BlockSpec((1, 1, BLOCKS_PER_STEP, MOBA_HEADS, LANES), lambda b, g, pt: (b, g, 0, 0, 0))
    stat_shape = jax.ShapeDtypeStruct((db, ng, BLOCKS_PER_STEP, MOBA_HEADS, LANES), f32)
    qb = jnp.broadcast_to(q[..., None], (db,) + page_block[1:])
    stats = pl.pallas_call(
        _moba_sample_stream_kernel,
        grid_spec=pltpu.PrefetchScalarGridSpec(
            num_scalar_prefetch=1,
            grid=(db, ng),
            in_specs=[pl.BlockSpec(page_block, lambda b, g, pt: (b, 0, 0, 0))]
                     + [page_spec(x) for x in range(PAGES_PER_STEP)] * 2,
            out_specs=[stat] * 4),
        out_shape=[stat_shape] * 4,
        compiler_params=_params(("parallel", "parallel")),
        name="moba_sample_stream",
    )(page_table, qb, *([k_pool] * PAGES_PER_STEP), *([v_pool] * PAGES_PER_STEP))
    gate, m, l, acc = (a.reshape(db, nb, MOBA_HEADS, LANES) for a in stats)
    vn_wide = jnp.pad(vn, ((0, 0), (0, 0), (0, LANES - MOBA_HEAD_DIM)))
    return pl.pallas_call(
        _moba_sample_combine_kernel,
        out_shape=jax.ShapeDtypeStruct((db, MOBA_HEADS, LANES), f32),
        compiler_params=pltpu.CompilerParams(vmem_limit_bytes=VMEM_LIMIT),
        name="moba_sample_combine",
    )(gate, m, l, acc, q, kn, vn_wide)


HGRN_TILE = 512


def _hgrn_prompt_kernel(q_ref, fz_ref, i_ref, g_ref, lb_ref, ng_ref, o_ref, st_ref, state_ref):
    t = pl.program_id(1)
    c_rows = HGRN_CHUNK

    @pl.when(t == 0)
    def _():
        state_ref[...] = jnp.zeros(state_ref.shape, f32)

    row = lax.broadcasted_iota(i32, (c_rows, c_rows), 0)
    col = lax.broadcasted_iota(i32, (c_rows, c_rows), 1)
    causal = col <= row
    cum = jnp.where(causal, 1.0, 0.0)
    cum_sub = jnp.where(col < (row // HGRN_SUB) * HGRN_SUB, 1.0, 0.0)
    cum_both = jnp.concatenate([cum, cum_sub], axis=0)

    def chunk(c, carry):
        rows = pl.ds(pl.multiple_of(c * c_rows, c_rows), c_rows)
        lb_all = lb_ref[...]
        f_all = lb_all + (1.0 - lb_all) * _sigmoid(fz_ref[rows, :])
        both_all = jnp.dot(cum_both, jnp.log(f_all), precision=HI, preferred_element_type=f32)
        for h in range(HGRN_HEADS):
            lanes = slice(h * HGRN_DK, (h + 1) * HGRN_DK)
            ng = ng_ref[:, lanes]
            q = q_ref[rows, lanes]
            v = i_ref[rows, lanes]
            gg = g_ref[rows, lanes]
            kk = 1.0 - f_all[:, lanes]
            b = both_all[:c_rows, lanes]
            ref_row = both_all[c_rows:, lanes]
            st = state_ref[h]
            o = lax.dot_general((q * jnp.exp(b)).astype(bf16), st.astype(bf16), NT_DIMS, preferred_element_type=f32)
            qh = (q * jnp.exp(b - ref_row)).astype(bf16)
            parts = []
            for s in range(c_rows // HGRN_SUB):
                ref_s = ref_row[s * HGRN_SUB:s * HGRN_SUB + 1, :]
                kh = (kk * jnp.exp(jnp.minimum(ref_s - b, EXP_CLAMP))).astype(bf16)
                parts.append(lax.dot_general(qh[s * HGRN_SUB:(s + 1) * HGRN_SUB], kh, NT_DIMS,
                                             preferred_element_type=f32))
            att = jnp.where(causal, jnp.concatenate(parts, axis=0), 0.0)
            o = o + jnp.dot(att.astype(bf16), v.astype(bf16), preferred_element_type=f32)
            b_last = b[c_rows - 1:c_rows, :]
            kd = (kk * jnp.exp(b_last - b)).astype(bf16)
            state_ref[h] = st * jnp.exp(b_last) + lax.dot_general(v.astype(bf16), kd, TN_DIMS,
                                                                   preferred_element_type=f32)
            ms = jnp.mean(o * o, axis=1, keepdims=True)
            o_ref[rows, lanes] = o * lax.rsqrt(ms + RMS_EPS) * ng * (gg * _sigmoid(gg))
        return carry

    lax.fori_loop(0, HGRN_TILE // c_rows, chunk, 0)

    @pl.when(t == pl.num_programs(1) - 1)
    def _():
        for h in range(HGRN_HEADS):
            st_ref[0, h] = state_ref[h].T


def _hgrn_prompt(hproj, batch, seq, lb, norm_g):
    nt = seq // HGRN_TILE
    width = HGRN_HEADS * HGRN_DK
    col = lambda j: pl.BlockSpec((HGRN_TILE, width), lambda b, t, j=j: (b * nt + t, j))
    vec = pl.BlockSpec((1, width), lambda b, t: (0, 0))
    return pl.pallas_call(
        _hgrn_prompt_kernel,
        grid=(batch, nt),
        in_specs=[col(0), col(1), col(2), col(3), vec, vec],
        out_specs=[pl.BlockSpec((HGRN_TILE, width), lambda b, t: (b * nt + t, 0)),
                   pl.BlockSpec((1, HGRN_HEADS, HGRN_DK, HGRN_DK), lambda b, t: (b, 0, 0, 0))],
        out_shape=[jax.ShapeDtypeStruct((batch * seq, width), f32),
                   jax.ShapeDtypeStruct((batch, HGRN_HEADS, HGRN_DK, HGRN_DK), f32)],
        scratch_shapes=[pltpu.VMEM((HGRN_HEADS, HGRN_DK, HGRN_DK), f32)],
        compiler_params=_params(("parallel", "arbitrary")),
        name="hgrn_prompt",
    )(hproj, hproj, hproj, hproj, lb.reshape(1, width), norm_g.reshape(1, width))


def _hgrn_sample_kernel(qc_ref, fzc_ref, lbc_ref, i_ref, g_ref, ng_ref, st_ref, o_ref, nst_ref):
    for h in range(HGRN_HEADS):
        lb = lbc_ref[h]
        f = lb + (1.0 - lb) * _sigmoid(fzc_ref[0, h])
        st = f * st_ref[0, h] + (1.0 - f) * i_ref[0, h]
        nst_ref[0, h] = st
        o = jnp.sum(qc_ref[0, h] * st, axis=0, keepdims=True)
        ms = jnp.mean(o * o, axis=1, keepdims=True)
        gg = g_ref[0, h]
        o_ref[0, h] = o * lax.rsqrt(ms + RMS_EPS) * ng_ref[h] * (gg * _sigmoid(gg))


def _hgrn_sample(hs, state, lb, norm_g):
    db = hs.shape[0]
    width = HGRN_HEADS * HGRN_DK
    colv = lambda x: x.reshape(db, HGRN_HEADS, HGRN_DK, 1)
    rowv = lambda x: x.reshape(db, HGRN_HEADS, 1, HGRN_DK)
    cspec = pl.BlockSpec((1, HGRN_HEADS, HGRN_DK, 1), lambda b: (b, 0, 0, 0))
    rspec = pl.BlockSpec((1, HGRN_HEADS, 1, HGRN_DK), lambda b: (b, 0, 0, 0))
    sspec = pl.BlockSpec((1, HGRN_HEADS, HGRN_DK, HGRN_DK), lambda b: (b, 0, 0, 0))
    o, nst = pl.pallas_call(
        _hgrn_sample_kernel,
        grid=(db,),
        in_specs=[cspec, cspec, pl.BlockSpec((HGRN_HEADS, HGRN_DK, 1), lambda b: (0, 0, 0)), rspec, rspec,
                  pl.BlockSpec((HGRN_HEADS, 1, HGRN_DK), lambda b: (0, 0, 0)), sspec],
        out_specs=[rspec, sspec],
        out_shape=[jax.ShapeDtypeStruct((db, HGRN_HEADS, 1, HGRN_DK), f32),
                   jax.ShapeDtypeStruct((db, HGRN_HEADS, HGRN_DK, HGRN_DK), f32)],
        compiler_params=_params(("parallel",)),
        name="hgrn_sample",
    )(colv(hs[:, :width]), colv(hs[:, width:2 * width]), lb.reshape(HGRN_HEADS, HGRN_DK, 1),
      rowv(hs[:, 2 * width:3 * width]), rowv(hs[:, 3 * width:]), norm_g.reshape(HGRN_HEADS, 1, HGRN_DK), state)
    return o.reshape(db, width), nst


PEER_EBLK = 1024
PEER_RANKS = PEER_TOPK + 1
PEER_VROWS = 24
PEER_SUB = 8
PEER_ROWS = 16


def _extract_max(tiles):
    mx = functools.reduce(jnp.maximum, tiles)
    mx = jnp.max(mx, axis=0, keepdims=True)
    return mx, [jnp.where(x == mx, FLOOR, x) for x in tiles]


def _peer_kernel(x_ref, q_ref, k1_ref, k2_ref, u_ref, v_ref, g_ref, b_ref, o_ref,
                 xt_ref, acc_ref, act_ref, p_ref, a_ref, bw_ref, c_ref, s2_ref, v1_ref, v2_ref, ab_ref, cb_ref):
    e = pl.program_id(1)
    tt = x_ref.shape[0]
    n_lane_chunks = tt // LANES
    sub = PEER_SUB
    n_sub = PEER_NKEYS // sub
    keys_per_step = PEER_EBLK // PEER_NKEYS

    @pl.when(e == 0)
    def _():
        xt_ref[...] = x_ref[...].T.astype(bf16)
        acc_ref[...] = jnp.zeros(acc_ref.shape, f32)
        floor_rows = jnp.full((PEER_VROWS, LANES), FLOOR, f32)
        for h in range(PEER_HEADS):
            q1 = q_ref[:, (2 * h) * PEER_DHALF:(2 * h + 1) * PEER_DHALF]
            q2 = q_ref[:, (2 * h + 1) * PEER_DHALF:(2 * h + 2) * PEER_DHALF]
            s1 = lax.dot_general(k1_ref[h], q1, NT_DIMS, precision=HI, preferred_element_type=f32)
            c_ref[h] = s1.reshape(n_sub, sub, tt)
            s2_ref[h] = lax.dot_general(k2_ref[h], q2, NT_DIMS, precision=HI, preferred_element_type=f32)

        def per_head(h, carry):
            for lc in range(n_lane_chunks):
                lanes = slice(lc * LANES, (lc + 1) * LANES)
                for side, vals_ref in enumerate((v1_ref, v2_ref)):
                    vals_ref[...] = floor_rows
                    if side == 0:
                        tiles = [c_ref[h, k, :, lanes] for k in range(n_sub)]
                    else:
                        tiles = [s2_ref[h, k * sub:(k + 1) * sub, lanes] for k in range(n_sub)]
                    for r in range(PEER_RANKS):
                        mx, tiles = _extract_max(tiles)
                        vals_ref[r:r + 1, :] = mx
                cands = [v1_ref[0:1, :] + v2_ref[r:r + sub, :] for r in range(0, PEER_VROWS, sub)]
                cands += [v1_ref[r:r + 1, :] + v2_ref[0:sub, :] for r in range(1, sub)]
                cands += [v1_ref[r:r + sub, :] + v2_ref[0:1, :] for r in range(sub, PEER_VROWS, sub)]
                best = v1_ref[0:1, :] + v2_ref[0:1, :]
                zsum = jnp.zeros((1, LANES), f32)
                kth = best
                for r in range(PEER_TOPK):
                    kth, cands = _extract_max(cands)
                    zsum = zsum + jnp.exp(kth - best)
                nxt, _ = _extract_max(cands)
                thresh = 0.5 * (kth + nxt)
                s1 = c_ref[h, :, :, lanes]
                a_ref[h, :, :, lanes] = jnp.exp(s1 - v1_ref[0:1, :]) / zsum
                bw_ref[h, :, lanes] = jnp.exp(s2_ref[h, :, lanes] - v2_ref[0:1, :])
                c_ref[h, :, :, lanes] = thresh - s1
            return carry

        lax.fori_loop(0, PEER_HEADS, per_head, 0)

    act_ref[...] = jnp.dot(u_ref[...], xt_ref[...], preferred_element_type=f32)
    for h in range(PEER_HEADS):
        for ii in range(keys_per_step):
            ab_ref[ii, h] = jnp.broadcast_to(a_ref[h, e, ii:ii + 1, :], (sub, tt))
            cb_ref[ii, h] = jnp.broadcast_to(c_ref[h, e, ii:ii + 1, :], (sub, tt))

    def per_second_keys(k, carry):
        r0 = pl.multiple_of(k * PEER_ROWS, PEER_ROWS)
        for lc in range(n_lane_chunks):
            lanes = slice(lc * LANES, (lc + 1) * LANES)
            w = [jnp.zeros((PEER_ROWS // sub, sub, LANES), f32) for _ in range(keys_per_step)]
            for h in range(PEER_HEADS):
                s2 = s2_ref[h, pl.ds(r0, PEER_ROWS), lanes].reshape(PEER_ROWS // sub, sub, LANES)
                bw = bw_ref[h, pl.ds(r0, PEER_ROWS), lanes].reshape(PEER_ROWS // sub, sub, LANES)
                for ii in range(keys_per_step):
                    hit = s2 >= cb_ref[ii, h, :, lanes][None]
                    w[ii] = w[ii] + ab_ref[ii, h, :, lanes][None] * jnp.where(hit, bw, 0.0)
            for ii in range(keys_per_step):
                rows = pl.ds(pl.multiple_of(ii * PEER_NKEYS + r0, PEER_ROWS), PEER_ROWS)
                act = act_ref[rows, lanes]
                gelu = 0.5 * act * (1.0 + lax.erf(act * (2.0 ** -0.5)))
                p_ref[rows, lanes] = (w[ii].reshape(PEER_ROWS, LANES) * gelu).astype(bf16)
        return carry

    lax.fori_loop(0, PEER_NKEYS // PEER_ROWS, per_second_keys, 0)
    acc_ref[...] += lax.dot_general(p_ref[...], v_ref[...], TN_DIMS, preferred_element_type=f32)

    @pl.when(e == pl.num_programs(1) - 1)
    def _():
        o_ref[...] = _layer_norm(DN_ALPHA * x_ref[...] + acc_ref[...], g_ref[...], b_ref[...])


def _peer_ln(x, wq, keys, u, v, g, b, *, tt=TOKEN_TILE):
    m, d = x.shape
    n_exp = u.shape[0]
    q = _mm(x, wq, split=True, tm=tt)
    nq = q.shape[1]
    return pl.pallas_call(
        _peer_kernel,
        grid=(m // tt, n_exp // PEER_EBLK),
        in_specs=[pl.BlockSpec((tt, d), lambda t, e: (t, 0)),
                  pl.BlockSpec((tt, nq), lambda t, e: (t, 0)),
                  pl.BlockSpec((PEER_HEADS, PEER_NKEYS, PEER_DHALF), lambda t, e: (0, 0, 0)),
                  pl.BlockSpec((PEER_HEADS, PEER_NKEYS, PEER_DHALF), lambda t, e: (0, 0, 0)),
                  pl.BlockSpec((PEER_EBLK, d), lambda t, e: (e, 0)),
                  pl.BlockSpec((PEER_EBLK, d), lambda t, e: (e, 0)),
                  pl.BlockSpec((1, d), lambda t, e: (0, 0)),
                  pl.BlockSpec((1, d), lambda t, e: (0, 0))],
        out_specs=pl.BlockSpec((tt, d), lambda t, e: (t, 0)),
        out_shape=jax.ShapeDtypeStruct((m, d), f32),
        scratch_shapes=[pltpu.VMEM((d, tt), bf16), pltpu.VMEM((tt, d), f32),
                        pltpu.VMEM((PEER_EBLK, tt), f32), pltpu.VMEM((PEER_EBLK, tt), bf16)]
                       + [pltpu.VMEM((PEER_HEADS, PEER_NKEYS // PEER_SUB, PEER_SUB, tt), f32),
                          pltpu.VMEM((PEER_HEADS, PEER_NKEYS, tt), f32),
                          pltpu.VMEM((PEER_HEADS, PEER_NKEYS // PEER_SUB, PEER_SUB, tt), f32),
                          pltpu.VMEM((PEER_HEADS, PEER_NKEYS, tt), f32)]
                       + [pltpu.VMEM((PEER_VROWS, LANES), f32)] * 2
                       + [pltpu.VMEM((PEER_EBLK // PEER_NKEYS, PEER_HEADS, PEER_SUB, tt), f32)] * 2,
        compiler_params=_params(("parallel", "arbitrary")),
        name="peer",
    )(x, q, keys[0], keys[1], u.astype(bf16), v.astype(bf16), g.reshape(1, d), b.reshape(1, d))


def kernel(x_prompt, x_sample, cache_k, cache_v, state_conv, state_hgrn, page_table, w_in_even, conv_w, conv_b, conv_ln_g, conv_ln_b, w_out_even, w_in_odd, hgrn_lb_logits, hgrn_norm_g, w_out_odd, ln_g, ln_b, peer_wq, peer_keys, peer_u, peer_v):
    batch, seq, d = x_prompt.shape
    db = x_sample.shape[0]
    n_prompt = batch * seq
    pad_sample = lambda a: jnp.pad(a, ((0, SAMPLE_TILE - a.shape[0]), (0, 0)))
    pages = lambda a: a.reshape(batch, seq // PAGE_SIZE, MOBA_HEADS, MOBA_HEAD_DIM, PAGE_SIZE).transpose(0, 1, 4, 2, 3)
    heads = lambda a: a.reshape(db, MOBA_HEADS, MOBA_HEAD_DIM)

    lb_p = jax.nn.softmax(hgrn_lb_logits.astype(f32), axis=0)
    lb_all = jnp.cumsum(lb_p, axis=0) - lb_p[0]

    xp = x_prompt.reshape(n_prompt, d)
    xs = pad_sample(x_sample.reshape(db, d))
    kp_l, vp_l, ks_l, vs_l, cp_l, cs_l, hp_l, hs_l = [], [], [], [], [], [], [], []
    for l in range(DEPTH):
        if l % 2 == 0:
            e = l // 2
            w_qk, w_vag = w_in_even[e][:, :2 * MOBA_WIDTH], w_in_even[e][:, 2 * MOBA_WIDTH:]
            qk_p, k_pages = _mm(xp, w_qk, split=True, paged_chunk=1)
            vag_p, v_pages = _mm(xp, w_vag, paged_chunk=0)
            qk_s, vag_s = _mm(xs, w_qk, split=True, tm=SAMPLE_TILE)[:db], _mm(xs, w_vag, tm=SAMPLE_TILE)[:db]
            kp_l.append(pages(k_pages))
            vp_l.append(pages(v_pages))
            q_s, k_s, v_s = qk_s[:, :MOBA_WIDTH], qk_s[:, MOBA_WIDTH:], vag_s[:, :MOBA_WIDTH]
            ks_l.append(k_s.reshape(db, 1, MOBA_HEADS, MOBA_HEAD_DIM))
            vs_l.append(v_s.reshape(db, 1, MOBA_HEADS, MOBA_HEAD_DIM))
            att_p = _moba_prompt(qk_p, vag_p, batch, seq)
            att_s = _moba_sample(heads(q_s), heads(k_s), heads(v_s),
                                 cache_k[e].transpose(0, 2, 3, 1), cache_v[e].transpose(0, 2, 3, 1), page_table)
            att_s = att_s[:, :, :MOBA_HEAD_DIM].reshape(db, MOBA_WIDTH)
            cprm = (conv_w[e], conv_b[e], conv_ln_g[e], conv_ln_b[e])
            cy_p, buf_p = _conv_prompt(vag_p, batch, seq, *cprm)
            cy_s, buf_s = _conv_sample(vag_s[:, MOBA_WIDTH:MOBA_WIDTH + CONV_CH], vag_s[:, MOBA_WIDTH + CONV_CH:],
                                       state_conv[e], *cprm)
            cp_l.append(buf_p)
            cs_l.append(buf_s)
            mix_p, mix_s = [att_p, cy_p], [pad_sample(att_s), pad_sample(cy_s)]
            w_out = w_out_even[e]
        else:
            oi = l // 2
            hproj_p, hproj_s = _mm(xp, w_in_odd[oi], tm=256), _mm(xs, w_in_odd[oi], tm=SAMPLE_TILE)[:db]
            o_p, st_p = _hgrn_prompt(hproj_p, batch, seq, lb_all[l], hgrn_norm_g[oi])
            o_s, st_s = _hgrn_sample(hproj_s, state_hgrn[oi].astype(f32), lb_all[l], hgrn_norm_g[oi])
            hp_l.append(st_p)
            hs_l.append(st_s)
            mix_p, mix_s = [o_p], [pad_sample(o_s)]
            w_out = w_out_odd[oi]
        xp = _mm_res_ln(mix_p, w_out, xp, ln_g[l, 0], ln_b[l, 0])
        xs = _mm_res_ln(mix_s, w_out, xs, ln_g[l, 0], ln_b[l, 0], tm=SAMPLE_TILE)
        prm = (peer_wq[l], peer_keys[l], peer_u[l], peer_v[l], ln_g[l, 1], ln_b[l, 1])
        xp = _peer_ln(xp, *prm)
        xs = _peer_ln(xs, *prm, tt=SAMPLE_TILE)
    y_prompt = xp.reshape(batch, seq, d)
    y_sample = xs[:db].reshape(db, 1, d)
    return (y_prompt, y_sample, jnp.stack(kp_l), jnp.stack(vp_l), jnp.stack(ks_l), jnp.stack(vs_l),
            jnp.stack(cp_l), jnp.stack(cs_l), jnp.stack(hp_l), jnp.stack(hs_l))
```

```python
import functools
import math

import jax
import jax.numpy as jnp
from jax import lax
from jax.experimental import pallas as pl
from jax.experimental.pallas import tpu as pltpu

f32 = jnp.float32
bf16 = jnp.bfloat16
i32 = jnp.int32
HI = lax.Precision.HIGHEST

D_MODEL = 1024
DEPTH = 2
PAGE_SIZE = 128
MOBA_HEADS = 8
MOBA_HEAD_DIM = 64
MOBA_WIDTH = MOBA_HEADS * MOBA_HEAD_DIM
MOBA_BLOCK = 256
MOBA_TOPK = 3
MOBA_SCALE = MOBA_HEAD_DIM ** -0.5
CONV_CH = D_MODEL // 2
CONV_WIDTH = 31
HGRN_HEADS = 8
HGRN_DK = D_MODEL // HGRN_HEADS
HGRN_CHUNK = 64
HGRN_SUB = 16
PEER_HEADS = 8
PEER_NKEYS = 128
PEER_DHALF = 128
PEER_TOPK = 16
LN_EPS = 1e-5
RMS_EPS = 1e-6
DN_ALPHA = (2 * DEPTH) ** 0.25

LANES = 128
TOKEN_TILE = 512
SAMPLE_TILE = 128
MASK_NEG = -1e30
FLOOR = -3e38
EXP_CLAMP = 60.0
VMEM_LIMIT = 56 << 20

NT_DIMS = (((1,), (1,)), ((), ()))
TN_DIMS = (((0,), (0,)), ((), ()))


def _params(semantics, vmem=VMEM_LIMIT):
    return pltpu.CompilerParams(dimension_semantics=semantics, vmem_limit_bytes=vmem)


def _layer_norm(y, g, b):
    mu = jnp.mean(y, axis=-1, keepdims=True)
    d = y - mu
    var = jnp.mean(d * d, axis=-1, keepdims=True)
    return d * lax.rsqrt(var + LN_EPS) * g + b


def _sigmoid(x):
    return 1.0 / (1.0 + jnp.exp(-x))


MM_CHUNK = 512


def _mm_kernel(x_ref, *refs, split, paged_chunk):
    refs = list(refs)
    t_ref = refs.pop() if paged_chunk is not None else None
    o_ref = refs.pop()
    wh_ref = refs[0]
    x = x_ref[...]
    xh = x.astype(bf16)
    if split:
        xl = (x - xh.astype(f32)).astype(bf16)
    for c, j in enumerate(range(0, o_ref.shape[1], MM_CHUNK)):
        wh = wh_ref[:, j:j + MM_CHUNK]
        acc = jnp.dot(xh, wh, preferred_element_type=f32)
        if split:
            acc = acc + jnp.dot(xl, wh, preferred_element_type=f32)
            acc = acc + jnp.dot(xh, refs[1][:, j:j + MM_CHUNK], preferred_element_type=f32)
        o_ref[:, j:j + MM_CHUNK] = acc
        if c == paged_chunk:
            for pg in range(o_ref.shape[0] // PAGE_SIZE):
                t_ref[pg] = acc[pg * PAGE_SIZE:(pg + 1) * PAGE_SIZE, :].T


def _mm(x, w, *, split=False, tm=TOKEN_TILE, paged_chunk=None):
    m, k = x.shape
    n = w.shape[1]
    wh = w.astype(bf16)
    ws = [wh] + ([(w - wh.astype(f32)).astype(bf16)] if split else [])
    out_specs = [pl.BlockSpec((tm, n), lambda i: (i, 0))]
    out_shape = [jax.ShapeDtypeStruct((m, n), f32)]
    if paged_chunk is not None:
        out_specs.append(pl.BlockSpec((tm // PAGE_SIZE, MM_CHUNK, PAGE_SIZE), lambda i: (i, 0, 0)))
        out_shape.append(jax.ShapeDtypeStruct((m // PAGE_SIZE, MM_CHUNK, PAGE_SIZE), f32))
    outs = pl.pallas_call(
        functools.partial(_mm_kernel, split=split, paged_chunk=paged_chunk),
        grid=(m // tm,),
        in_specs=[pl.BlockSpec((tm, k), lambda i: (i, 0))] + [pl.BlockSpec((k, n), lambda i: (0, 0))] * len(ws),
        out_specs=out_specs,
        out_shape=out_shape,
        compiler_params=_params(("parallel",)),
        name="proj_split" if split else "proj",
    )(x, *ws)
    return outs[0] if paged_chunk is None else outs


def _mm_res_ln_kernel(*refs):
    *a_refs, w_ref, x_ref, g_ref, b_ref, o_ref = refs
    acc = DN_ALPHA * x_ref[...]
    row = 0
    for a_ref in a_refs:
        k = a_ref.shape[1]
        acc = acc + jnp.dot(a_ref[...].astype(bf16), w_ref[row:row + k, :], preferred_element_type=f32)
        row += k
    o_ref[...] = _layer_norm(acc, g_ref[...], b_ref[...])


def _mm_res_ln(parts, w, x, g, b, *, tm=TOKEN_TILE):
    m, n = x.shape
    k = w.shape[0]
    return pl.pallas_call(
        _mm_res_ln_kernel,
        grid=(m // tm,),
        in_specs=[pl.BlockSpec((tm, a.shape[1]), lambda i: (i, 0)) for a in parts]
                 + [pl.BlockSpec((k, n), lambda i: (0, 0)),
                    pl.BlockSpec((tm, n), lambda i: (i, 0)), pl.BlockSpec((1, n), lambda i: (0, 0)),
                    pl.BlockSpec((1, n), lambda i: (0, 0))],
        out_specs=pl.BlockSpec((tm, n), lambda i: (i, 0)),
        out_shape=jax.ShapeDtypeStruct((m, n), f32),
        compiler_params=_params(("parallel",)),
        name="out_proj_ln",
    )(*parts, w.astype(bf16), x, g.reshape(1, n), b.reshape(1, n))


CONV_TILE = 256
CONV_ROWS = 64
CONV_HIST = 32
SUBLANES = 8
CONV_SHIFT_ROWS = CONV_TILE + CONV_HIST - SUBLANES


def _conv_prompt_kernel(a_ref, gt_ref, w_ref, cb_ref, g_ref, b_ref, y_ref, st_ref, buf_ref, shift_ref):
    t = pl.program_id(1)

    @pl.when(t == 0)
    def _():
        buf_ref[0:CONV_HIST, :] = jnp.zeros((CONV_HIST, CONV_CH), f32)

    buf_ref[CONV_HIST:CONV_HIST + CONV_TILE, :] = a_ref[...] * _sigmoid(gt_ref[...])
    for q in range(1, SUBLANES):
        shift_ref[q - 1] = buf_ref[q:q + CONV_SHIFT_ROWS, :]
    first = CONV_HIST - (CONV_WIDTH - 1)
    for r in range(0, CONV_TILE, CONV_ROWS):
        acc = jnp.zeros((CONV_ROWS, CONV_CH), f32) + cb_ref[...]
        for j in range(CONV_WIDTH):
            q = (first + j) % SUBLANES
            base = r + first + j - q
            rows = buf_ref[base:base + CONV_ROWS, :] if q == 0 else shift_ref[q - 1, base:base + CONV_ROWS, :]
            acc = acc + w_ref[j:j + 1, :] * rows
        y = _layer_norm(acc, g_ref[...], b_ref[...])
        y_ref[r:r + CONV_ROWS, :] = y * _sigmoid(y)
    tail = buf_ref[CONV_TILE:CONV_TILE + CONV_HIST, :]
    buf_ref[0:CONV_HIST, :] = tail

    @pl.when(t == pl.num_programs(1) - 1)
    def _():
        st_ref[0] = tail


def _conv_prompt(vag, batch, seq, w, cb, g, b):
    nt = seq // CONV_TILE
    wpad = jnp.zeros((CONV_HIST, CONV_CH), f32).at[:CONV_WIDTH].set(w)
    row = lambda v: v.reshape(1, CONV_CH)
    y, st = pl.pallas_call(
        _conv_prompt_kernel,
        grid=(batch, nt),
        in_specs=[pl.BlockSpec((CONV_TILE, CONV_CH), lambda bi, t: (bi * nt + t, 1)),
                  pl.BlockSpec((CONV_TILE, CONV_CH), lambda bi, t: (bi * nt + t, 2)),
                  pl.BlockSpec((CONV_HIST, CONV_CH), lambda bi, t: (0, 0))]
                 + [pl.BlockSpec((1, CONV_CH), lambda bi, t: (0, 0))] * 3,
        out_specs=[pl.BlockSpec((CONV_TILE, CONV_CH), lambda bi, t: (bi * nt + t, 0)),
                   pl.BlockSpec((1, CONV_HIST, CONV_CH), lambda bi, t: (bi, 0, 0))],
        out_shape=[jax.ShapeDtypeStruct((batch * seq, CONV_CH), f32),
                   jax.ShapeDtypeStruct((batch, CONV_HIST, CONV_CH), f32)],
        scratch_shapes=[pltpu.VMEM((CONV_TILE + CONV_HIST, CONV_CH), f32),
                        pltpu.VMEM((SUBLANES - 1, CONV_SHIFT_ROWS, CONV_CH), f32)],
        compiler_params=_params(("arbitrary", "arbitrary")),
        name="conv_prompt",
    )(vag, vag, wpad, row(cb), row(g), row(b))
    return y, st[:, CONV_HIST - (CONV_WIDTH - 1):]


def _conv_sample_kernel(a_ref, gt_ref, st_ref, w_ref, cb_ref, g_ref, b_ref, y_ref, nst_ref):
    u = a_ref[...] * _sigmoid(gt_ref[...])
    acc = cb_ref[...] + w_ref[CONV_WIDTH - 1:CONV_WIDTH, :] * u
    for j in range(CONV_WIDTH - 1):
        acc = acc + w_ref[j:j + 1, :] * st_ref[j]
    y = _layer_norm(acc, g_ref[...], b_ref[...])
    y_ref[...] = y * _sigmoid(y)
    for j in range(CONV_WIDTH - 2):
        nst_ref[j] = st_ref[j + 1]
    nst_ref[CONV_WIDTH - 2] = u


def _conv_sample(a, gt, state, w, cb, g, b):
    db = a.shape[0]
    wpad = jnp.zeros((CONV_HIST, CONV_CH), f32).at[:CONV_WIDTH].set(w)
    row = lambda v: v.reshape(1, CONV_CH)
    y, nst = pl.pallas_call(
        _conv_sample_kernel,
        out_shape=[jax.ShapeDtypeStruct((db, CONV_CH), f32),
                   jax.ShapeDtypeStruct((CONV_WIDTH - 1, db, CONV_CH), f32)],
        name="conv_sample",
    )(a, gt, state.transpose(1, 0, 2), wpad, row(cb), row(g), row(b))
    return y, nst.transpose(1, 0, 2)


GATE_ROWS = 1024
ATTN_KEYS = 2 * MOBA_BLOCK


def _block_sum_kernel(k_ref, o_ref):
    o_ref[0] = jnp.sum(k_ref[...], axis=0, keepdims=True)


def _block_sums(qk, batch, seq):
    nb = seq // MOBA_BLOCK
    return pl.pallas_call(
        _block_sum_kernel,
        grid=(batch * nb,),
        in_specs=[pl.BlockSpec((MOBA_BLOCK, MOBA_WIDTH), lambda i: (i, 1))],
        out_specs=pl.BlockSpec((1, 1, MOBA_WIDTH), lambda i: (i, 0, 0)),
        out_shape=jax.ShapeDtypeStruct((batch * nb, 1, MOBA_WIDTH), f32),
        compiler_params=_params(("parallel",)),
        name="moba_block_sums",
    )(qk)


def _moba_gate_kernel(q_ref, k_ref, v_ref, ks_ref, qa_ref, ka_ref, va_ref, *, nb):
    t = pl.program_id(2)
    lane = lax.broadcasted_iota(i32, (MOBA_BLOCK, LANES), 1)
    blk = lane - MOBA_HEAD_DIM
    km_lane = lax.broadcasted_iota(i32, (nb, LANES), 1)
    ksum = ks_ref[0] * (1.0 / MOBA_BLOCK)
    pad_top = jnp.zeros((MOBA_HEAD_DIM, LANES), f32)
    pad_bot = jnp.zeros((LANES - MOBA_HEAD_DIM - nb, LANES), f32)
    for c in range(GATE_ROWS // MOBA_BLOCK):
        own = t * (GATE_ROWS // MOBA_BLOCK) + c
        rows = slice(c * MOBA_BLOCK, (c + 1) * MOBA_BLOCK)
        q2 = q_ref[rows, :]
        k2 = k_ref[rows, :]
        v2 = v_ref[rows, :]
        cand = (blk >= 0) & (blk < own)
        for j in range(2):
            head = (km_lane >= j * MOBA_HEAD_DIM) & (km_lane < (j + 1) * MOBA_HEAD_DIM)
            km = jnp.concatenate([pad_top, jnp.where(head, ksum, 0.0), pad_bot], axis=0)
            gate = lax.dot_general(q2, km, NT_DIMS, precision=HI, preferred_element_type=f32)
            masked = jnp.where(cand, gate, FLOOR)
            sel = jnp.zeros((MOBA_BLOCK, LANES), f32)
            for _ in range(MOBA_TOPK):
                mx = jnp.max(masked, axis=1, keepdims=True)
                hit = (masked == mx) & cand
                sel = jnp.where(hit, 1.0, sel)
                masked = jnp.where(hit, FLOOR, masked)
            bias = jnp.where((sel > 0.0) | (blk == own), 0.0, MASK_NEG)
            qj = q2 if j == 0 else pltpu.roll(q2, MOBA_HEAD_DIM, 1)
            kj = k2 if j == 0 else pltpu.roll(k2, MOBA_HEAD_DIM, 1)
            vj = v2 if j == 0 else pltpu.roll(v2, MOBA_HEAD_DIM, 1)
            in_head = lane < MOBA_HEAD_DIM
            in_bias = lane < MOBA_HEAD_DIM + nb
            qa = jnp.where(in_head, qj * MOBA_SCALE, jnp.where(in_bias, bias, 0.0))
            ka = jnp.where(in_head, kj, jnp.where(blk == own, 1.0, 0.0))
            va = jnp.where(in_head, vj, jnp.where(blk == 0, 1.0, 0.0))
            qa_ref[0, j, rows, :] = qa.astype(bf16)
            ka_ref[0, j, rows, :] = ka.astype(bf16)
            va_ref[0, j, rows, :] = va.astype(bf16)


def _moba_attn_kernel(qa_ref, ka_ref, va_ref, o_ref):
    i = pl.program_id(1)
    last = i // 2
    row = lax.broadcasted_iota(i32, (MOBA_BLOCK, ATTN_KEYS), 0)
    col = lax.broadcasted_iota(i32, (MOBA_BLOCK, ATTN_KEYS), 1)
    lane = lax.broadcasted_iota(i32, (MOBA_BLOCK, LANES), 1)

    def tile(h, k0, mask):
        s = lax.dot_general(qa_ref[0, h], ka_ref[0, h, pl.ds(k0, ATTN_KEYS), :], NT_DIMS, preferred_element_type=f32)
        return s if mask is None else jnp.where(mask, s, MASK_NEG)

    def weighted(p, h, k0):
        return jnp.dot(p.astype(bf16), va_ref[0, h, pl.ds(k0, ATTN_KEYS), :], preferred_element_type=f32)

    k_last = pl.multiple_of(last * ATTN_KEYS, ATTN_KEYS)
    visible = (k_last + col) <= (i * MOBA_BLOCK + row)
    state = []
    for h in range(MOBA_HEADS):
        s = tile(h, k_last, visible)
        m = jnp.max(s, axis=1, keepdims=True)
        state += [m, weighted(jnp.exp(s - m), h, k_last)]

    def body(n, carry):
        k0 = pl.multiple_of(n * ATTN_KEYS, ATTN_KEYS)
        out = []
        for h in range(MOBA_HEADS):
            m, acc = carry[2 * h:2 * h + 2]
            s = tile(h, k0, None)
            mn = jnp.maximum(m, jnp.max(s, axis=1, keepdims=True))
            out += [mn, jnp.exp(m - mn) * acc + weighted(jnp.exp(s - mn), h, k0)]
        return tuple(out)

    final = lax.fori_loop(0, last, body, tuple(state))
    for pair in range(MOBA_HEADS // 2):
        acc0, acc1 = final[4 * pair + 1], final[4 * pair + 3]
        out0 = acc0 / acc0[:, MOBA_HEAD_DIM:MOBA_HEAD_DIM + 1]
        out1 = acc1 / acc1[:, MOBA_HEAD_DIM:MOBA_HEAD_DIM + 1]
        o_ref[:, pair * LANES:(pair + 1) * LANES] = jnp.where(lane < MOBA_HEAD_DIM, out0,
                                                              pltpu.roll(out1, MOBA_HEAD_DIM, 1))


def _moba_prompt(qk, vag, batch, seq):
    nb = seq // MOBA_BLOCK
    npair = MOBA_WIDTH // LANES
    nt = seq // GATE_ROWS
    ksums = _block_sums(qk, batch, seq).reshape(batch, nb, MOBA_WIDTH)
    per_head = pl.BlockSpec((1, 2, GATE_ROWS, LANES), lambda b, p, t: (b, p, t, 0))
    qa, ka, va = pl.pallas_call(
        functools.partial(_moba_gate_kernel, nb=nb),
        grid=(batch, npair, nt),
        in_specs=[pl.BlockSpec((GATE_ROWS, LANES), lambda b, p, t: (b * nt + t, p)),
                  pl.BlockSpec((GATE_ROWS, LANES), lambda b, p, t: (b * nt + t, npair + p)),
                  pl.BlockSpec((GATE_ROWS, LANES), lambda b, p, t: (b * nt + t, p)),
                  pl.BlockSpec((1, nb, LANES), lambda b, p, t: (b, 0, p))],
        out_specs=[per_head] * 3,
        out_shape=[jax.ShapeDtypeStruct((batch, MOBA_HEADS, seq, LANES), bf16)] * 3,
        compiler_params=_params(("parallel", "parallel", "parallel")),
        name="moba_gate",
    )(qk, qk, vag, ksums)
    return pl.pallas_call(
        _moba_attn_kernel,
        grid=(batch, nb),
        in_specs=[pl.BlockSpec((1, MOBA_HEADS, MOBA_BLOCK, LANES), lambda b, i: (b, 0, i, 0)),
                  pl.BlockSpec((1, MOBA_HEADS, seq, LANES), lambda b, i: (b, 0, 0, 0),
                               pipeline_mode=pl.Buffered(1)),
                  pl.BlockSpec((1, MOBA_HEADS, seq, LANES), lambda b, i: (b, 0, 0, 0),
                               pipeline_mode=pl.Buffered(1))],
        out_specs=pl.BlockSpec((MOBA_BLOCK, MOBA_WIDTH), lambda b, i: (b * nb + i, 0)),
        out_shape=jax.ShapeDtypeStruct((batch * seq, MOBA_WIDTH), f32),
        compiler_params=_params(("parallel", "arbitrary")),
        name="moba_attn",
    )(qa, ka, va)


PAGES_PER_STEP = 8
PAGES_PER_BLOCK = MOBA_BLOCK // PAGE_SIZE
BLOCKS_PER_STEP = PAGES_PER_STEP // PAGES_PER_BLOCK


def _moba_sample_stream_kernel(pt_ref, qb_ref, *refs):
    del pt_ref
    k_refs = refs[:PAGES_PER_STEP]
    v_refs = refs[PAGES_PER_STEP:2 * PAGES_PER_STEP]
    gate_ref, m_ref, l_ref, acc_ref = refs[2 * PAGES_PER_STEP:]
    qb = qb_ref[0]
    head = lax.broadcasted_iota(i32, (MOBA_HEADS, LANES), 0)
    tile = (MOBA_HEADS, LANES)
    for jj in range(BLOCKS_PER_STEP):
        pages = range(PAGES_PER_BLOCK * jj, PAGES_PER_BLOCK * (jj + 1))
        raw = [jnp.sum(k_refs[x][0] * qb, axis=1) for x in pages]
        gate = functools.reduce(jnp.add, [jnp.sum(r, axis=1, keepdims=True) for r in raw]) * (1.0 / MOBA_BLOCK)
        m = functools.reduce(jnp.maximum, [jnp.max(r, axis=1, keepdims=True) for r in raw]) * MOBA_SCALE
        l = jnp.zeros((MOBA_HEADS, 1), f32)
        pv = jnp.zeros((MOBA_HEADS, MOBA_WIDTH), f32)
        for r, x in zip(raw, pages):
            p = jnp.exp(r * MOBA_SCALE - m)
            l = l + jnp.sum(p, axis=1, keepdims=True)
            vt = v_refs[x][0].reshape(MOBA_WIDTH, PAGE_SIZE).astype(bf16)
            pv = pv + lax.dot_general(p.astype(bf16), vt, NT_DIMS, preferred_element_type=f32)
        acc = jnp.zeros(tile, f32)
        for pair in range(MOBA_WIDTH // LANES):
            both = pv[:, pair * LANES:(pair + 1) * LANES]
            acc = acc + jnp.where(head == 2 * pair, both, 0.0)
            acc = acc + jnp.where(head == 2 * pair + 1, pltpu.roll(both, MOBA_HEAD_DIM, 1), 0.0)
        gate_ref[0, 0, jj] = jnp.broadcast_to(gate, tile)
        m_ref[0, 0, jj] = jnp.broadcast_to(m, tile)
        l_ref[0, 0, jj] = jnp.broadcast_to(l, tile)
        acc_ref[0, 0, jj] = acc


def _moba_sample_combine_kernel(gate_ref, m_ref, l_ref, acc_ref, q_ref, kn_ref, vn_ref, o_ref):
    db, nb, nh, width = gate_ref.shape
    masked = gate_ref[...]
    sel = jnp.zeros((db, nb, nh, width), f32)
    for _ in range(MOBA_TOPK):
        mx = jnp.max(masked, axis=1, keepdims=True)
        hit = masked == mx
        sel = jnp.where(hit, 1.0, sel)
        masked = jnp.where(hit, FLOOR, masked)
    chosen = sel > 0.0
    s_self = jnp.sum(q_ref[...] * kn_ref[...], axis=-1, keepdims=True) * MOBA_SCALE
    m = m_ref[...]
    top = jnp.maximum(jnp.max(jnp.where(chosen, m, FLOOR), axis=1), s_self)
    w = jnp.where(chosen, jnp.exp(m - top[:, None]), 0.0)
    w_self = jnp.exp(s_self - top)
    denom = jnp.sum(w * l_ref[...], axis=1) + w_self
    num = jnp.sum(w * acc_ref[...], axis=1) + w_self * vn_ref[...]
    o_ref[...] = num / denom


def _moba_sample(q, kn, vn, k_pool, v_pool, page_table):
    db, n_pages = page_table.shape
    nb = n_pages // PAGES_PER_BLOCK
    ng = n_pages // PAGES_PER_STEP
    page_block = (1, MOBA_HEADS, MOBA_HEAD_DIM, PAGE_SIZE)
    page_spec = lambda x: pl.BlockSpec(page_block, lambda b, g, pt, x=x: (pt[b, g * PAGES_PER_STEP + x], 0, 0, 0))
    stat = pl.BlockSpec((1, 1, BLOCKS_PER_STEP, MOBA_HEADS, LANES), lambda b, g, pt: (b, g, 0, 0, 0))
    stat_shape = jax.ShapeDtypeStruct((db, ng, BLOCKS_PER_STEP, MOBA_HEADS, LANES), f32)
    qb = jnp.broadcast_to(q[..., None], (db,) + page_block[1:])
    stats = pl.pallas_call(
        _moba_sample_stream_kernel,
        grid_spec=pltpu.PrefetchScalarGridSpec(
            num_scalar_prefetch=1,
            grid=(db, ng),
            in_specs=[pl.BlockSpec(page_block, lambda b, g, pt: (b, 0, 0, 0))]
                     + [page_spec(x) for x in range(PAGES_PER_STEP)] * 2,
            out_specs=[stat] * 4),
        out_shape=[stat_shape] * 4,
        compiler_params=_params(("parallel", "parallel")),
        name="moba_sample_stream",
    )(page_table, qb, *([k_pool] * PAGES_PER_STEP), *([v_pool] * PAGES_PER_STEP))
    gate, m, l, acc = (a.reshape(db, nb, MOBA_HEADS, LANES) for a in stats)
    vn_wide = jnp.pad(vn, ((0, 0), (0, 0), (0, LANES - MOBA_HEAD_DIM)))
    return pl.pallas_call(
        _moba_sample_combine_kernel,
        out_shape=jax.ShapeDtypeStruct((db, MOBA_HEADS, LANES), f32),
        compiler_params=pltpu.CompilerParams(vmem_limit_bytes=VMEM_LIMIT),
        name="moba_sample_combine",
    )(gate, m, l, acc, q, kn, vn_wide)


HGRN_TILE = 512


def _hgrn_prompt_kernel(q_ref, fz_ref, i_ref, g_ref, lb_ref, ng_ref, o_ref, st_ref, state_ref):
    t = pl.program_id(1)
    c_rows = HGRN_CHUNK

    @pl.when(t == 0)
    def _():
        state_ref[...] = jnp.zeros(state_ref.shape, f32)

    row = lax.broadcasted_iota(i32, (c_rows, c_rows), 0)
    col = lax.broadcasted_iota(i32, (c_rows, c_rows), 1)
    causal = col <= row
    cum = jnp.where(causal, 1.0, 0.0)
    cum_sub = jnp.where(col < (row // HGRN_SUB) * HGRN_SUB, 1.0, 0.0)
    cum_both = jnp.concatenate([cum, cum_sub], axis=0)

    def chunk(c, carry):
        rows = pl.ds(pl.multiple_of(c * c_rows, c_rows), c_rows)
        lb_all = lb_ref[...]
        f_all = lb_all + (1.0 - lb_all) * _sigmoid(fz_ref[rows, :])
        both_all = jnp.dot(cum_both, jnp.log(f_all), precision=HI, preferred_element_type=f32)
        for h in range(HGRN_HEADS):
            lanes = slice(h * HGRN_DK, (h + 1) * HGRN_DK)
            ng = ng_ref[:, lanes]
            q = q_ref[rows, lanes]
            v = i_ref[rows, lanes]
            gg = g_ref[rows, lanes]
            kk = 1.0 - f_all[:, lanes]
            b = both_all[:c_rows, lanes]
            ref_row = both_all[c_rows:, lanes]
            st = state_ref[h]
            o = lax.dot_general((q * jnp.exp(b)).astype(bf16), st.astype(bf16), NT_DIMS, preferred_element_type=f32)
            qh = (q * jnp.exp(b - ref_row)).astype(bf16)
            parts = []
            for s in range(c_rows // HGRN_SUB):
                ref_s = ref_row[s * HGRN_SUB:s * HGRN_SUB + 1, :]
                kh = (kk * jnp.exp(jnp.minimum(ref_s - b, EXP_CLAMP))).astype(bf16)
                parts.append(lax.dot_general(qh[s * HGRN_SUB:(s + 1) * HGRN_SUB], kh, NT_DIMS,
                                             preferred_element_type=f32))
            att = jnp.where(causal, jnp.concatenate(parts, axis=0), 0.0)
            o = o + jnp.dot(att.astype(bf16), v.astype(bf16), preferred_element_type=f32)
            b_last = b[c_rows - 1:c_rows, :]
            kd = (kk * jnp.exp(b_last - b)).astype(bf16)
            state_ref[h] = st * jnp.exp(b_last) + lax.dot_general(v.astype(bf16), kd, TN_DIMS,
                                                                   preferred_element_type=f32)
            ms = jnp.mean(o * o, axis=1, keepdims=True)
            o_ref[rows, lanes] = o * lax.rsqrt(ms + RMS_EPS) * ng * (gg * _sigmoid(gg))
        return carry

    lax.fori_loop(0, HGRN_TILE // c_rows, chunk, 0)

    @pl.when(t == pl.num_programs(1) - 1)
    def _():
        for h in range(HGRN_HEADS):
            st_ref[0, h] = state_ref[h].T


def _hgrn_prompt(hproj, batch, seq, lb, norm_g):
    nt = seq // HGRN_TILE
    width = HGRN_HEADS * HGRN_DK
    col = lambda j: pl.BlockSpec((HGRN_TILE, width), lambda b, t, j=j: (b * nt + t, j))
    vec = pl.BlockSpec((1, width), lambda b, t: (0, 0))
    return pl.pallas_call(
        _hgrn_prompt_kernel,
        grid=(batch, nt),
        in_specs=[col(0), col(1), col(2), col(3), vec, vec],
        out_specs=[pl.BlockSpec((HGRN_TILE, width), lambda b, t: (b * nt + t, 0)),
                   pl.BlockSpec((1, HGRN_HEADS, HGRN_DK, HGRN_DK), lambda b, t: (b, 0, 0, 0))],
        out_shape=[jax.ShapeDtypeStruct((batch * seq, width), f32),
                   jax.ShapeDtypeStruct((batch, HGRN_HEADS, HGRN_DK, HGRN_DK), f32)],
        scratch_shapes=[pltpu.VMEM((HGRN_HEADS, HGRN_DK, HGRN_DK), f32)],
        compiler_params=_params(("parallel", "arbitrary")),
        name="hgrn_prompt",
    )(hproj, hproj, hproj, hproj, lb.reshape(1, width), norm_g.reshape(1, width))


def _hgrn_sample_kernel(qc_ref, fzc_ref, lbc_ref, i_ref, g_ref, ng_ref, st_ref, o_ref, nst_ref):
    for h in range(HGRN_HEADS):
        lb = lbc_ref[h]
        f = lb + (1.0 - lb) * _sigmoid(fzc_ref[0, h])
        st = f * st_ref[0, h] + (1.0 - f) * i_ref[0, h]
        nst_ref[0, h] = st
        o = jnp.sum(qc_ref[0, h] * st, axis=0, keepdims=True)
        ms = jnp.mean(o * o, axis=1, keepdims=True)
        gg = g_ref[0, h]
        o_ref[0, h] = o * lax.rsqrt(ms + RMS_EPS) * ng_ref[h] * (gg * _sigmoid(gg))


def _hgrn_sample(hs, state, lb, norm_g):
    db = hs.shape[0]
    width = HGRN_HEADS * HGRN_DK
    colv = lambda x: x.reshape(db, HGRN_HEADS, HGRN_DK, 1)
    rowv = lambda x: x.reshape(db, HGRN_HEADS, 1, HGRN_DK)
    cspec = pl.BlockSpec((1, HGRN_HEADS, HGRN_DK, 1), lambda b: (b, 0, 0, 0))
    rspec = pl.BlockSpec((1, HGRN_HEADS, 1, HGRN_DK), lambda b: (b, 0, 0, 0))
    sspec = pl.BlockSpec((1, HGRN_HEADS, HGRN_DK, HGRN_DK), lambda b: (b, 0, 0, 0))
    o, nst = pl.pallas_call(
        _hgrn_sample_kernel,
        grid=(db,),
        in_specs=[cspec, cspec, pl.BlockSpec((HGRN_HEADS, HGRN_DK, 1), lambda b: (0, 0, 0)), rspec, rspec,
                  pl.BlockSpec((HGRN_HEADS, 1, HGRN_DK), lambda b: (0, 0, 0)), sspec],
        out_specs=[rspec, sspec],
        out_shape=[jax.ShapeDtypeStruct((db, HGRN_HEADS, 1, HGRN_DK), f32),
                   jax.ShapeDtypeStruct((db, HGRN_HEADS, HGRN_DK, HGRN_DK), f32)],
        compiler_params=_params(("parallel",)),
        name="hgrn_sample",
    )(colv(hs[:, :width]), colv(hs[:, width:2 * width]), lb.reshape(HGRN_HEADS, HGRN_DK, 1),
      rowv(hs[:, 2 * width:3 * width]), rowv(hs[:, 3 * width:]), norm_g.reshape(HGRN_HEADS, 1, HGRN_DK), state)
    return o.reshape(db, width), nst


PEER_EBLK = 1024
PEER_RANKS = PEER_TOPK + 1
PEER_VROWS = 24
PEER_SUB = 8
PEER_ROWS = 16


def _extract_max(tiles):
    mx = functools.reduce(jnp.maximum, tiles)
    mx = jnp.max(mx, axis=0, keepdims=True)
    return mx, [jnp.where(x == mx, FLOOR, x) for x in tiles]


def _peer_kernel(x_ref, q_ref, k1_ref, k2_ref, u_ref, v_ref, g_ref, b_ref, o_ref,
                 xt_ref, acc_ref, act_ref, p_ref, a_ref, bw_ref, c_ref, s2_ref, v1_ref, v2_ref, ab_ref, cb_ref):
    e = pl.program_id(1)
    tt = x_ref.shape[0]
    n_lane_chunks = tt // LANES
    sub = PEER_SUB
    n_sub = PEER_NKEYS // sub
    keys_per_step = PEER_EBLK // PEER_NKEYS

    @pl.when(e == 0)
    def _():
        xt_ref[...] = x_ref[...].T.astype(bf16)
        acc_ref[...] = jnp.zeros(acc_ref.shape, f32)
        floor_rows = jnp.full((PEER_VROWS, LANES), FLOOR, f32)
        for h in range(PEER_HEADS):
            q1 = q_ref[:, (2 * h) * PEER_DHALF:(2 * h + 1) * PEER_DHALF]
            q2 = q_ref[:, (2 * h + 1) * PEER_DHALF:(2 * h + 2) * PEER_DHALF]
            s1 = lax.dot_general(k1_ref[h], q1, NT_DIMS, precision=HI, preferred_element_type=f32)
            c_ref[h] = s1.reshape(n_sub, sub, tt)
            s2_ref[h] = lax.dot_general(k2_ref[h], q2, NT_DIMS, precision=HI, preferred_element_type=f32)

        def per_head(h, carry):
            for lc in range(n_lane_chunks):
                lanes = slice(lc * LANES, (lc + 1) * LANES)
                for side, vals_ref in enumerate((v1_ref, v2_ref)):
                    vals_ref[...] = floor_rows
                    if side == 0:
                        tiles = [c_ref[h, k, :, lanes] for k in range(n_sub)]
                    else:
                        tiles = [s2_ref[h, k * sub:(k + 1) * sub, lanes] for k in range(n_sub)]
                    for r in range(PEER_RANKS):
                        mx, tiles = _extract_max(tiles)
                        vals_ref[r:r + 1, :] = mx
                cands = [v1_ref[0:1, :] + v2_ref[r:r + sub, :] for r in range(0, PEER_VROWS, sub)]
                cands += [v1_ref[r:r + 1, :] + v2_ref[0:sub, :] for r in range(1, sub)]
                cands += [v1_ref[r:r + sub, :] + v2_ref[0:1, :] for r in range(sub, PEER_VROWS, sub)]
                best = v1_ref[0:1, :] + v2_ref[0:1, :]
                zsum = jnp.zeros((1, LANES), f32)
                kth = best
                for r in range(PEER_TOPK):
                    kth, cands = _extract_max(cands)
                    zsum = zsum + jnp.exp(kth - best)
                nxt, _ = _extract_max(cands)
                thresh = 0.5 * (kth + nxt)
                s1 = c_ref[h, :, :, lanes]
                a_ref[h, :, :, lanes] = jnp.exp(s1 - v1_ref[0:1, :]) / zsum
                bw_ref[h, :, lanes] = jnp.exp(s2_ref[h, :, lanes] - v2_ref[0:1, :])
                c_ref[h, :, :, lanes] = thresh - s1
            return carry

        lax.fori_loop(0, PEER_HEADS, per_head, 0)

    act_ref[...] = jnp.dot(u_ref[0], xt_ref[...], preferred_element_type=f32)
    for h in range(PEER_HEADS):
        for ii in range(keys_per_step):
            ab_ref[ii, h] = jnp.broadcast_to(a_ref[h, e, ii:ii + 1, :], (sub, tt))
            cb_ref[ii, h] = jnp.broadcast_to(c_ref[h, e, ii:ii + 1, :], (sub, tt))

    def per_second_keys(k, carry):
        r0 = pl.multiple_of(k * PEER_ROWS, PEER_ROWS)
        for lc in range(n_lane_chunks):
            lanes = slice(lc * LANES, (lc + 1) * LANES)
            w = [None] * keys_per_step
            for h in range(PEER_HEADS):
                s2 = s2_ref[h, pl.ds(r0, PEER_ROWS), lanes].reshape(PEER_ROWS // sub, sub, LANES)
                bw = bw_ref[h, pl.ds(r0, PEER_ROWS), lanes].reshape(PEER_ROWS // sub, sub, LANES)
                for ii in range(keys_per_step):
                    hit = s2 >= cb_ref[ii, h, :, lanes][None]
                    term = ab_ref[ii, h, :, lanes][None] * jnp.where(hit, bw, 0.0)
                    w[ii] = term if h == 0 else w[ii] + term
            for ii in range(keys_per_step):
                rows = pl.ds(pl.multiple_of(ii * PEER_NKEYS + r0, PEER_ROWS), PEER_ROWS)
                act = act_ref[rows, lanes]
                gelu = 0.5 * act * (1.0 + lax.erf(act * (2.0 ** -0.5)))
                p_ref[rows, lanes] = (w[ii].reshape(PEER_ROWS, LANES) * gelu).astype(bf16)
        return carry

    lax.fori_loop(0, PEER_NKEYS // PEER_ROWS, per_second_keys, 0)
    acc_ref[...] += lax.dot_general(p_ref[...], v_ref[0], TN_DIMS, preferred_element_type=f32)

    @pl.when(e == pl.num_programs(1) - 1)
    def _():
        o_ref[...] = _layer_norm(DN_ALPHA * x_ref[...] + acc_ref[...], g_ref[...], b_ref[...])


def _peer_ln(x, wq, keys, u, v, layer, g, b, *, tt=TOKEN_TILE):
    m, d = x.shape
    n_exp = u.shape[1]
    q = _mm(x, wq, split=True, tm=tt)
    nq = q.shape[1]
    return pl.pallas_call(
        _peer_kernel,
        grid=(m // tt, n_exp // PEER_EBLK),
        in_specs=[pl.BlockSpec((tt, d), lambda t, e: (t, 0)),
                  pl.BlockSpec((tt, nq), lambda t, e: (t, 0)),
                  pl.BlockSpec((PEER_HEADS, PEER_NKEYS, PEER_DHALF), lambda t, e: (0, 0, 0)),
                  pl.BlockSpec((PEER_HEADS, PEER_NKEYS, PEER_DHALF), lambda t, e: (0, 0, 0)),
                  pl.BlockSpec((1, PEER_EBLK, d), lambda t, e: (layer, e, 0)),
                  pl.BlockSpec((1, PEER_EBLK, d), lambda t, e: (layer, e, 0)),
                  pl.BlockSpec((1, d), lambda t, e: (0, 0)),
                  pl.BlockSpec((1, d), lambda t, e: (0, 0))],
        out_specs=pl.BlockSpec((tt, d), lambda t, e: (t, 0)),
        out_shape=jax.ShapeDtypeStruct((m, d), f32),
        scratch_shapes=[pltpu.VMEM((d, tt), bf16), pltpu.VMEM((tt, d), f32),
                        pltpu.VMEM((PEER_EBLK, tt), f32), pltpu.VMEM((PEER_EBLK, tt), bf16)]
                       + [pltpu.VMEM((PEER_HEADS, PEER_NKEYS // PEER_SUB, PEER_SUB, tt), f32),
                          pltpu.VMEM((PEER_HEADS, PEER_NKEYS, tt), f32),
                          pltpu.VMEM((PEER_HEADS, PEER_NKEYS // PEER_SUB, PEER_SUB, tt), f32),
                          pltpu.VMEM((PEER_HEADS, PEER_NKEYS, tt), f32)]
                       + [pltpu.VMEM((PEER_VROWS, LANES), f32)] * 2
                       + [pltpu.VMEM((PEER_EBLK // PEER_NKEYS, PEER_HEADS, PEER_SUB, tt), f32)] * 2,
        compiler_params=_params(("parallel", "arbitrary")),
        name="peer",
    )(x, q, keys[0], keys[1], u, v, g.reshape(1, d), b.reshape(1, d))


def kernel(x_prompt, x_sample, cache_k, cache_v, state_conv, state_hgrn, page_table, w_in_even, conv_w, conv_b, conv_ln_g, conv_ln_b, w_out_even, w_in_odd, hgrn_lb_logits, hgrn_norm_g, w_out_odd, ln_g, ln_b, peer_wq, peer_keys, peer_u, peer_v):
    batch, seq, d = x_prompt.shape
    db = x_sample.shape[0]
    n_prompt = batch * seq
    pad_sample = lambda a: jnp.pad(a, ((0, SAMPLE_TILE - a.shape[0]), (0, 0)))
    pages = lambda a: a.reshape(batch, seq // PAGE_SIZE, MOBA_HEADS, MOBA_HEAD_DIM, PAGE_SIZE).transpose(0, 1, 4, 2, 3)
    heads = lambda a: a.reshape(db, MOBA_HEADS, MOBA_HEAD_DIM)

    lb_p = jax.nn.softmax(hgrn_lb_logits.astype(f32), axis=0)
    lb_all = jnp.cumsum(lb_p, axis=0) - lb_p[0]

    peer_u16, peer_v16 = peer_u.astype(bf16), peer_v.astype(bf16)
    xp = x_prompt.reshape(n_prompt, d)
    xs = pad_sample(x_sample.reshape(db, d))
    kp_l, vp_l, ks_l, vs_l, cp_l, cs_l, hp_l, hs_l = [], [], [], [], [], [], [], []
    for l in range(DEPTH):
        if l % 2 == 0:
            e = l // 2
            w_qk, w_vag = w_in_even[e][:, :2 * MOBA_WIDTH], w_in_even[e][:, 2 * MOBA_WIDTH:]
            qk_p, k_pages = _mm(xp, w_qk, split=True, paged_chunk=1)
            vag_p, v_pages = _mm(xp, w_vag, paged_chunk=0)
            qk_s, vag_s = _mm(xs, w_qk, split=True, tm=SAMPLE_TILE)[:db], _mm(xs, w_vag, tm=SAMPLE_TILE)[:db]
            kp_l.append(pages(k_pages))
            vp_l.append(pages(v_pages))
            q_s, k_s, v_s = qk_s[:, :MOBA_WIDTH], qk_s[:, MOBA_WIDTH:], vag_s[:, :MOBA_WIDTH]
            ks_l.append(k_s.reshape(db, 1, MOBA_HEADS, MOBA_HEAD_DIM))
            vs_l.append(v_s.reshape(db, 1, MOBA_HEADS, MOBA_HEAD_DIM))
            att_p = _moba_prompt(qk_p, vag_p, batch, seq)
            att_s = _moba_sample(heads(q_s), heads(k_s), heads(v_s),
                                 cache_k[e].transpose(0, 2, 3, 1), cache_v[e].transpose(0, 2, 3, 1), page_table)
            att_s = att_s[:, :, :MOBA_HEAD_DIM].reshape(db, MOBA_WIDTH)
            cprm = (conv_w[e], conv_b[e], conv_ln_g[e], conv_ln_b[e])
            cy_p, buf_p = _conv_prompt(vag_p, batch, seq, *cprm)
            cy_s, buf_s = _conv_sample(vag_s[:, MOBA_WIDTH:MOBA_WIDTH + CONV_CH], vag_s[:, MOBA_WIDTH + CONV_CH:],
                                       state_conv[e], *cprm)
            cp_l.append(buf_p)
            cs_l.append(buf_s)
            mix_p, mix_s = [att_p, cy_p], [pad_sample(att_s), pad_sample(cy_s)]
            w_out = w_out_even[e]
        else:
            oi = l // 2
            hproj_p, hproj_s = _mm(xp, w_in_odd[oi], tm=256), _mm(xs, w_in_odd[oi], tm=SAMPLE_TILE)[:db]
            o_p, st_p = _hgrn_prompt(hproj_p, batch, seq, lb_all[l], hgrn_norm_g[oi])
            o_s, st_s = _hgrn_sample(hproj_s, state_hgrn[oi].astype(f32), lb_all[l], hgrn_norm_g[oi])
            hp_l.append(st_p)
            hs_l.append(st_s)
            mix_p, mix_s = [o_p], [pad_sample(o_s)]
            w_out = w_out_odd[oi]
        xp = _mm_res_ln(mix_p, w_out, xp, ln_g[l, 0], ln_b[l, 0])
        xs = _mm_res_ln(mix_s, w_out, xs, ln_g[l, 0], ln_b[l, 0], tm=SAMPLE_TILE)
        prm = (peer_wq[l], peer_keys[l], peer_u16, peer_v16, l, ln_g[l, 1], ln_b[l, 1])
        xp = _peer_ln(xp, *prm)
        xs = _peer_ln(xs, *prm, tt=SAMPLE_TILE)
    y_prompt = xp.reshape(batch, seq, d)
    y_sample = xs[:db].reshape(db, 1, d)
    return (y_prompt, y_sample, jnp.stack(kp_l), jnp.stack(vp_l), jnp.stack(ks_l), jnp.stack(vs_l),
            jnp.stack(cp_l), jnp.stack(cs_l), jnp.stack(hp_l), jnp.stack(hs_l))
```

```python
import functools
import math

import jax
import jax.numpy as jnp
from jax import lax
from jax.experimental import pallas as pl
from jax.experimental.pallas import tpu as pltpu

f32 = jnp.float32
bf16 = jnp.bfloat16
i32 = jnp.int32
HI = lax.Precision.HIGHEST

D_MODEL = 1024
DEPTH = 2
PAGE_SIZE = 128
MOBA_HEADS = 8
MOBA_HEAD_DIM = 64
MOBA_WIDTH = MOBA_HEADS * MOBA_HEAD_DIM
MOBA_BLOCK = 256
MOBA_TOPK = 3
MOBA_SCALE = MOBA_HEAD_DIM ** -0.5
CONV_CH = D_MODEL // 2
CONV_WIDTH = 31
HGRN_HEADS = 8
HGRN_DK = D_MODEL // HGRN_HEADS
HGRN_CHUNK = 64
HGRN_SUB = 16
PEER_HEADS = 8
PEER_NKEYS = 128
PEER_DHALF = 128
PEER_TOPK = 16
LN_EPS = 1e-5
RMS_EPS = 1e-6
DN_ALPHA = (2 * DEPTH) ** 0.25

LANES = 128
TOKEN_TILE = 512
SAMPLE_TILE = 128
MASK_NEG = -1e30
FLOOR = -3e38
EXP_CLAMP = 60.0
VMEM_LIMIT = 56 << 20

NT_DIMS = (((1,), (1,)), ((), ()))
TN_DIMS = (((0,), (0,)), ((), ()))


def _params(semantics, vmem=VMEM_LIMIT):
    return pltpu.CompilerParams(dimension_semantics=semantics, vmem_limit_bytes=vmem)


def _layer_norm(y, g, b):
    mu = jnp.mean(y, axis=-1, keepdims=True)
    d = y - mu
    var = jnp.mean(d * d, axis=-1, keepdims=True)
    return d * lax.rsqrt(var + LN_EPS) * g + b


def _sigmoid(x):
    return 1.0 / (1.0 + jnp.exp(-x))


MM_CHUNK = 512


def _mm_kernel(x_ref, *refs, split, paged_chunk):
    refs = list(refs)
    t_ref = refs.pop() if paged_chunk is not None else None
    o_ref = refs.pop()
    wh_ref = refs[0]
    x = x_ref[...]
    xh = x.astype(bf16)
    if split:
        xl = (x - xh.astype(f32)).astype(bf16)
    for c, j in enumerate(range(0, o_ref.shape[1], MM_CHUNK)):
        wh = wh_ref[:, j:j + MM_CHUNK]
        acc = jnp.dot(xh, wh, preferred_element_type=f32)
        if split:
            acc = acc + jnp.dot(xl, wh, preferred_element_type=f32)
            acc = acc + jnp.dot(xh, refs[1][:, j:j + MM_CHUNK], preferred_element_type=f32)
        o_ref[:, j:j + MM_CHUNK] = acc
        if c == paged_chunk:
            for pg in range(o_ref.shape[0] // PAGE_SIZE):
                t_ref[pg] = acc[pg * PAGE_SIZE:(pg + 1) * PAGE_SIZE, :].T


def _mm(x, w, *, split=False, tm=TOKEN_TILE, paged_chunk=None):
    m, k = x.shape
    n = w.shape[1]
    wh = w.astype(bf16)
    ws = [wh] + ([(w - wh.astype(f32)).astype(bf16)] if split else [])
    out_specs = [pl.BlockSpec((tm, n), lambda i: (i, 0))]
    out_shape = [jax.ShapeDtypeStruct((m, n), f32)]
    if paged_chunk is not None:
        out_specs.append(pl.BlockSpec((tm // PAGE_SIZE, MM_CHUNK, PAGE_SIZE), lambda i: (i, 0, 0)))
        out_shape.append(jax.ShapeDtypeStruct((m // PAGE_SIZE, MM_CHUNK, PAGE_SIZE), f32))
    outs = pl.pallas_call(
        functools.partial(_mm_kernel, split=split, paged_chunk=paged_chunk),
        grid=(m // tm,),
        in_specs=[pl.BlockSpec((tm, k), lambda i: (i, 0))] + [pl.BlockSpec((k, n), lambda i: (0, 0))] * len(ws),
        out_specs=out_specs,
        out_shape=out_shape,
        compiler_params=_params(("parallel",)),
        name="proj_split" if split else "proj",
    )(x, *ws)
    return outs[0] if paged_chunk is None else outs


def _mm_res_ln_kernel(*refs):
    *a_refs, w_ref, x_ref, g_ref, b_ref, o_ref = refs
    acc = DN_ALPHA * x_ref[...]
    row = 0
    for a_ref in a_refs:
        k = a_ref.shape[1]
        acc = acc + jnp.dot(a_ref[...].astype(bf16), w_ref[row:row + k, :], preferred_element_type=f32)
        row += k
    o_ref[...] = _layer_norm(acc, g_ref[...], b_ref[...])


def _mm_res_ln(parts, w, x, g, b, *, tm=TOKEN_TILE):
    m, n = x.shape
    k = w.shape[0]
    return pl.pallas_call(
        _mm_res_ln_kernel,
        grid=(m // tm,),
        in_specs=[pl.BlockSpec((tm, a.shape[1]), lambda i: (i, 0)) for a in parts]
                 + [pl.BlockSpec((k, n), lambda i: (0, 0)),
                    pl.BlockSpec((tm, n), lambda i: (i, 0)), pl.BlockSpec((1, n), lambda i: (0, 0)),
                    pl.BlockSpec((1, n), lambda i: (0, 0))],
        out_specs=pl.BlockSpec((tm, n), lambda i: (i, 0)),
        out_shape=jax.ShapeDtypeStruct((m, n), f32),
        compiler_params=_params(("parallel",)),
        name="out_proj_ln",
    )(*parts, w.astype(bf16), x, g.reshape(1, n), b.reshape(1, n))


CONV_TILE = 256
CONV_ROWS = 64
CONV_HIST = 32
SUBLANES = 8
CONV_SHIFT_ROWS = CONV_TILE + CONV_HIST - SUBLANES


def _conv_prompt_kernel(a_ref, gt_ref, w_ref, cb_ref, g_ref, b_ref, y_ref, st_ref, buf_ref, shift_ref):
    t = pl.program_id(1)

    @pl.when(t == 0)
    def _():
        buf_ref[0:CONV_HIST, :] = jnp.zeros((CONV_HIST, CONV_CH), f32)

    buf_ref[CONV_HIST:CONV_HIST + CONV_TILE, :] = a_ref[...] * _sigmoid(gt_ref[...])
    for q in range(1, SUBLANES):
        shift_ref[q - 1] = buf_ref[q:q + CONV_SHIFT_ROWS, :]
    first = CONV_HIST - (CONV_WIDTH - 1)
    for r in range(0, CONV_TILE, CONV_ROWS):
        acc = jnp.zeros((CONV_ROWS, CONV_CH), f32) + cb_ref[...]
        for j in range(CONV_WIDTH):
            q = (first + j) % SUBLANES
            base = r + first + j - q
            rows = buf_ref[base:base + CONV_ROWS, :] if q == 0 else shift_ref[q - 1, base:base + CONV_ROWS, :]
            acc = acc + w_ref[j:j + 1, :] * rows
        y = _layer_norm(acc, g_ref[...], b_ref[...])
        y_ref[r:r + CONV_ROWS, :] = y * _sigmoid(y)
    tail = buf_ref[CONV_TILE:CONV_TILE + CONV_HIST, :]
    buf_ref[0:CONV_HIST, :] = tail

    @pl.when(t == pl.num_programs(1) - 1)
    def _():
        st_ref[0] = tail


def _conv_prompt(vag, batch, seq, w, cb, g, b):
    nt = seq // CONV_TILE
    wpad = jnp.zeros((CONV_HIST, CONV_CH), f32).at[:CONV_WIDTH].set(w)
    row = lambda v: v.reshape(1, CONV_CH)
    y, st = pl.pallas_call(
        _conv_prompt_kernel,
        grid=(batch, nt),
        in_specs=[pl.BlockSpec((CONV_TILE, CONV_CH), lambda bi, t: (bi * nt + t, 1)),
                  pl.BlockSpec((CONV_TILE, CONV_CH), lambda bi, t: (bi * nt + t, 2)),
                  pl.BlockSpec((CONV_HIST, CONV_CH), lambda bi, t: (0, 0))]
                 + [pl.BlockSpec((1, CONV_CH), lambda bi, t: (0, 0))] * 3,
        out_specs=[pl.BlockSpec((CONV_TILE, CONV_CH), lambda bi, t: (bi * nt + t, 0)),
                   pl.BlockSpec((1, CONV_HIST, CONV_CH), lambda bi, t: (bi, 0, 0))],
        out_shape=[jax.ShapeDtypeStruct((batch * seq, CONV_CH), f32),
                   jax.ShapeDtypeStruct((batch, CONV_HIST, CONV_CH), f32)],
        scratch_shapes=[pltpu.VMEM((CONV_TILE + CONV_HIST, CONV_CH), f32),
                        pltpu.VMEM((SUBLANES - 1, CONV_SHIFT_ROWS, CONV_CH), f32)],
        compiler_params=_params(("arbitrary", "arbitrary")),
        name="conv_prompt",
    )(vag, vag, wpad, row(cb), row(g), row(b))
    return y, st[:, CONV_HIST - (CONV_WIDTH - 1):]


def _conv_sample_kernel(a_ref, gt_ref, st_ref, w_ref, cb_ref, g_ref, b_ref, y_ref, nst_ref):
    u = a_ref[...] * _sigmoid(gt_ref[...])
    acc = cb_ref[...] + w_ref[CONV_WIDTH - 1:CONV_WIDTH, :] * u
    for j in range(CONV_WIDTH - 1):
        acc = acc + w_ref[j:j + 1, :] * st_ref[j]
    y = _layer_norm(acc, g_ref[...], b_ref[...])
    y_ref[...] = y * _sigmoid(y)
    for j in range(CONV_WIDTH - 2):
        nst_ref[j] = st_ref[j + 1]
    nst_ref[CONV_WIDTH - 2] = u


def _conv_sample(a, gt, state, w, cb, g, b):
    db = a.shape[0]
    wpad = jnp.zeros((CONV_HIST, CONV_CH), f32).at[:CONV_WIDTH].set(w)
    row = lambda v: v.reshape(1, CONV_CH)
    y, nst = pl.pallas_call(
        _conv_sample_kernel,
        out_shape=[jax.ShapeDtypeStruct((db, CONV_CH), f32),
                   jax.ShapeDtypeStruct((CONV_WIDTH - 1, db, CONV_CH), f32)],
        name="conv_sample",
    )(a, gt, state.transpose(1, 0, 2), wpad, row(cb), row(g), row(b))
    return y, nst.transpose(1, 0, 2)


GATE_ROWS = 1024
ATTN_KEYS = 2 * MOBA_BLOCK


def _block_sum_kernel(k_ref, o_ref):
    o_ref[0] = jnp.sum(k_ref[...], axis=0, keepdims=True)


def _block_sums(qk, batch, seq):
    nb = seq // MOBA_BLOCK
    return pl.pallas_call(
        _block_sum_kernel,
        grid=(batch * nb,),
        in_specs=[pl.BlockSpec((MOBA_BLOCK, MOBA_WIDTH), lambda i: (i, 1))],
        out_specs=pl.BlockSpec((1, 1, MOBA_WIDTH), lambda i: (i, 0, 0)),
        out_shape=jax.ShapeDtypeStruct((batch * nb, 1, MOBA_WIDTH), f32),
        compiler_params=_params(("parallel",)),
        name="moba_block_sums",
    )(qk)


def _moba_gate_kernel(q_ref, k_ref, v_ref, ks_ref, qa_ref, ka_ref, va_ref, *, nb):
    t = pl.program_id(2)
    lane = lax.broadcasted_iota(i32, (MOBA_BLOCK, LANES), 1)
    blk = lane - MOBA_HEAD_DIM
    km_lane = lax.broadcasted_iota(i32, (nb, LANES), 1)
    ksum = ks_ref[0] * (1.0 / MOBA_BLOCK)
    pad_top = jnp.zeros((MOBA_HEAD_DIM, LANES), f32)
    pad_bot = jnp.zeros((LANES - MOBA_HEAD_DIM - nb, LANES), f32)
    for c in range(GATE_ROWS // MOBA_BLOCK):
        own = t * (GATE_ROWS // MOBA_BLOCK) + c
        rows = slice(c * MOBA_BLOCK, (c + 1) * MOBA_BLOCK)
        q2 = q_ref[rows, :]
        k2 = k_ref[rows, :]
        v2 = v_ref[rows, :]
        cand = (blk >= 0) & (blk < own)
        for j in range(2):
            head = (km_lane >= j * MOBA_HEAD_DIM) & (km_lane < (j + 1) * MOBA_HEAD_DIM)
            km = jnp.concatenate([pad_top, jnp.where(head, ksum, 0.0), pad_bot], axis=0)
            gate = lax.dot_general(q2, km, NT_DIMS, precision=HI, preferred_element_type=f32)
            masked = jnp.where(cand, gate, FLOOR)
            sel = jnp.zeros((MOBA_BLOCK, LANES), f32)
            for _ in range(MOBA_TOPK):
                mx = jnp.max(masked, axis=1, keepdims=True)
                hit = (masked == mx) & cand
                sel = jnp.where(hit, 1.0, sel)
                masked = jnp.where(hit, FLOOR, masked)
            bias = jnp.where((sel > 0.0) | (blk == own), 0.0, MASK_NEG)
            qj = q2 if j == 0 else pltpu.roll(q2, MOBA_HEAD_DIM, 1)
            kj = k2 if j == 0 else pltpu.roll(k2, MOBA_HEAD_DIM, 1)
            vj = v2 if j == 0 else pltpu.roll(v2, MOBA_HEAD_DIM, 1)
            in_head = lane < MOBA_HEAD_DIM
            in_bias = lane < MOBA_HEAD_DIM + nb
            qa = jnp.where(in_head, qj * MOBA_SCALE, jnp.where(in_bias, bias, 0.0))
            ka = jnp.where(in_head, kj, jnp.where(blk == own, 1.0, 0.0))
            va = jnp.where(in_head, vj, jnp.where(blk == 0, 1.0, 0.0))
            qa_ref[0, j, rows, :] = qa.astype(bf16)
            ka_ref[0, j, rows, :] = ka.astype(bf16)
            va_ref[0, j, rows, :] = va.astype(bf16)


def _moba_attn_kernel(qa_ref, ka_ref, va_ref, o_ref):
    i = pl.program_id(1)
    last = i // 2
    row = lax.broadcasted_iota(i32, (MOBA_BLOCK, ATTN_KEYS), 0)
    col = lax.broadcasted_iota(i32, (MOBA_BLOCK, ATTN_KEYS), 1)
    lane = lax.broadcasted_iota(i32, (MOBA_BLOCK, LANES), 1)

    def tile(h, k0, mask):
        s = lax.dot_general(qa_ref[0, h], ka_ref[0, h, pl.ds(k0, ATTN_KEYS), :], NT_DIMS, preferred_element_type=f32)
        return s if mask is None else jnp.where(mask, s, MASK_NEG)

    def weighted(p, h, k0):
        return jnp.dot(p.astype(bf16), va_ref[0, h, pl.ds(k0, ATTN_KEYS), :], preferred_element_type=f32)

    k_last = pl.multiple_of(last * ATTN_KEYS, ATTN_KEYS)
    visible = (k_last + col) <= (i * MOBA_BLOCK + row)
    state = []
    for h in range(MOBA_HEADS):
        s = tile(h, k_last, visible)
        m = jnp.max(s, axis=1, keepdims=True)
        state += [m, weighted(jnp.exp(s - m), h, k_last)]

    def body(n, carry):
        k0 = pl.multiple_of(n * ATTN_KEYS, ATTN_KEYS)
        out = []
        for h in range(MOBA_HEADS):
            m, acc = carry[2 * h:2 * h + 2]
            s = tile(h, k0, None)
            mn = jnp.maximum(m, jnp.max(s, axis=1, keepdims=True))
            out += [mn, jnp.exp(m - mn) * acc + weighted(jnp.exp(s - mn), h, k0)]
        return tuple(out)

    final = lax.fori_loop(0, last, body, tuple(state))
    for pair in range(MOBA_HEADS // 2):
        acc0, acc1 = final[4 * pair + 1], final[4 * pair + 3]
        out0 = acc0 / acc0[:, MOBA_HEAD_DIM:MOBA_HEAD_DIM + 1]
        out1 = acc1 / acc1[:, MOBA_HEAD_DIM:MOBA_HEAD_DIM + 1]
        o_ref[:, pair * LANES:(pair + 1) * LANES] = jnp.where(lane < MOBA_HEAD_DIM, out0,
                                                              pltpu.roll(out1, MOBA_HEAD_DIM, 1))


def _moba_prompt(qk, vag, batch, seq):
    nb = seq // MOBA_BLOCK
    npair = MOBA_WIDTH // LANES
    nt = seq // GATE_ROWS
    ksums = _block_sums(qk, batch, seq).reshape(batch, nb, MOBA_WIDTH)
    per_head = pl.BlockSpec((1, 2, GATE_ROWS, LANES), lambda b, p, t: (b, p, t, 0))
    qa, ka, va = pl.pallas_call(
        functools.partial(_moba_gate_kernel, nb=nb),
        grid=(batch, npair, nt),
        in_specs=[pl.BlockSpec((GATE_ROWS, LANES), lambda b, p, t: (b * nt + t, p)),
                  pl.BlockSpec((GATE_ROWS, LANES), lambda b, p, t: (b * nt + t, npair + p)),
                  pl.BlockSpec((GATE_ROWS, LANES), lambda b, p, t: (b * nt + t, p)),
                  pl.BlockSpec((1, nb, LANES), lambda b, p, t: (b, 0, p))],
        out_specs=[per_head] * 3,
        out_shape=[jax.ShapeDtypeStruct((batch, MOBA_HEADS, seq, LANES), bf16)] * 3,
        compiler_params=_params(("parallel", "parallel", "parallel")),
        name="moba_gate",
    )(qk, qk, vag, ksums)
    return pl.pallas_call(
        _moba_attn_kernel,
        grid=(batch, nb),
        in_specs=[pl.BlockSpec((1, MOBA_HEADS, MOBA_BLOCK, LANES), lambda b, i: (b, 0, i, 0)),
                  pl.BlockSpec((1, MOBA_HEADS, seq, LANES), lambda b, i: (b, 0, 0, 0),
                               pipeline_mode=pl.Buffered(1)),
                  pl.BlockSpec((1, MOBA_HEADS, seq, LANES), lambda b, i: (b, 0, 0, 0),
                               pipeline_mode=pl.Buffered(1))],
        out_specs=pl.BlockSpec((MOBA_BLOCK, MOBA_WIDTH), lambda b, i: (b * nb + i, 0)),
        out_shape=jax.ShapeDtypeStruct((batch * seq, MOBA_WIDTH), f32),
        compiler_params=_params(("parallel", "arbitrary")),
        name="moba_attn",
    )(qa, ka, va)


PAGES_PER_STEP = 8
PAGES_PER_BLOCK = MOBA_BLOCK // PAGE_SIZE
BLOCKS_PER_STEP = PAGES_PER_STEP // PAGES_PER_BLOCK


def _moba_sample_stream_kernel(pt_ref, qb_ref, *refs):
    del pt_ref
    k_refs = refs[:PAGES_PER_STEP]
    v_refs = refs[PAGES_PER_STEP:2 * PAGES_PER_STEP]
    gate_ref, m_ref, l_ref, acc_ref = refs[2 * PAGES_PER_STEP:]
    qb = qb_ref[0]
    head = lax.broadcasted_iota(i32, (MOBA_HEADS, LANES), 0)
    tile = (MOBA_HEADS, LANES)
    for jj in range(BLOCKS_PER_STEP):
        pages = range(PAGES_PER_BLOCK * jj, PAGES_PER_BLOCK * (jj + 1))
        raw = [jnp.sum(k_refs[x][0] * qb, axis=1) for x in pages]
        gate = functools.reduce(jnp.add, [jnp.sum(r, axis=1, keepdims=True) for r in raw]) * (1.0 / MOBA_BLOCK)
        m = functools.reduce(jnp.maximum, [jnp.max(r, axis=1, keepdims=True) for r in raw]) * MOBA_SCALE
        l = jnp.zeros((MOBA_HEADS, 1), f32)
        pv = jnp.zeros((MOBA_HEADS, MOBA_WIDTH), f32)
        for r, x in zip(raw, pages):
            p = jnp.exp(r * MOBA_SCALE - m)
            l = l + jnp.sum(p, axis=1, keepdims=True)
            vt = v_refs[x][0].reshape(MOBA_WIDTH, PAGE_SIZE).astype(bf16)
            pv = pv + lax.dot_general(p.astype(bf16), vt, NT_DIMS, preferred_element_type=f32)
        acc = jnp.zeros(tile, f32)
        for pair in range(MOBA_WIDTH // LANES):
            both = pv[:, pair * LANES:(pair + 1) * LANES]
            acc = acc + jnp.where(head == 2 * pair, both, 0.0)
            acc = acc + jnp.where(head == 2 * pair + 1, pltpu.roll(both, MOBA_HEAD_DIM, 1), 0.0)
        gate_ref[0, 0, jj] = jnp.broadcast_to(gate, tile)
        m_ref[0, 0, jj] = jnp.broadcast_to(m, tile)
        l_ref[0, 0, jj] = jnp.broadcast_to(l, tile)
        acc_ref[0, 0, jj] = acc


def _moba_sample_combine_kernel(gate_ref, m_ref, l_ref, acc_ref, q_ref, kn_ref, vn_ref, o_ref):
    db, nb, nh, width = gate_ref.shape
    masked = gate_ref[...]
    sel = jnp.zeros((db, nb, nh, width), f32)
    for _ in range(MOBA_TOPK):
        mx = jnp.max(masked, axis=1, keepdims=True)
        hit = masked == mx
        sel = jnp.where(hit, 1.0, sel)
        masked = jnp.where(hit, FLOOR, masked)
    chosen = sel > 0.0
    s_self = jnp.sum(q_ref[...] * kn_ref[...], axis=-1, keepdims=True) * MOBA_SCALE
    m = m_ref[...]
    top = jnp.maximum(jnp.max(jnp.where(chosen, m, FLOOR), axis=1), s_self)
    w = jnp.where(chosen, jnp.exp(m - top[:, None]), 0.0)
    w_self = jnp.exp(s_self - top)
    denom = jnp.sum(w * l_ref[...], axis=1) + w_self
    num = jnp.sum(w * acc_ref[...], axis=1) + w_self * vn_ref[...]
    o_ref[...] = num / denom


def _moba_sample(q, kn, vn, k_pool, v_pool, page_table):
    db, n_pages = page_table.shape
    nb = n_pages // PAGES_PER_BLOCK
    ng = n_pages // PAGES_PER_STEP
    page_block = (1, MOBA_HEADS, MOBA_HEAD_DIM, PAGE_SIZE)
    page_spec = lambda x: pl.BlockSpec(page_block, lambda b, g, pt, x=x: (pt[b, g * PAGES_PER_STEP + x], 0, 0, 0))
    stat = pl.BlockSpec((1, 1, BLOCKS_PER_STEP, MOBA_HEADS, LANES), lambda b, g, pt: (b, g, 0, 0, 0))
    stat_shape = jax.ShapeDtypeStruct((db, ng, BLOCKS_PER_STEP, MOBA_HEADS, LANES), f32)
    qb = jnp.broadcast_to(q[..., None], (db,) + page_block[1:])
    stats = pl.pallas_call(
        _moba_sample_stream_kernel,
        grid_spec=pltpu.PrefetchScalarGridSpec(
            num_scalar_prefetch=1,
            grid=(db, ng),
            in_specs=[pl.BlockSpec(page_block, lambda b, g, pt: (b, 0, 0, 0))]
                     + [page_spec(x) for x in range(PAGES_PER_STEP)] * 2,
            out_specs=[stat] * 4),
        out_shape=[stat_shape] * 4,
        compiler_params=_params(("parallel", "parallel")),
        name="moba_sample_stream",
    )(page_table, qb, *([k_pool] * PAGES_PER_STEP), *([v_pool] * PAGES_PER_STEP))
    gate, m, l, acc = (a.reshape(db, nb, MOBA_HEADS, LANES) for a in stats)
    vn_wide = jnp.pad(vn, ((0, 0), (0, 0), (0, LANES - MOBA_HEAD_DIM)))
    return pl.pallas_call(
        _moba_sample_combine_kernel,
        out_shape=jax.ShapeDtypeStruct((db, MOBA_HEADS, LANES), f32),
        compiler_params=pltpu.CompilerParams(vmem_limit_bytes=VMEM_LIMIT),
        name="moba_sample_combine",
    )(gate, m, l, acc, q, kn, vn_wide)


HGRN_TILE = 512


def _hgrn_prompt_kernel(q_ref, fz_ref, i_ref, g_ref, lb_ref, ng_ref, o_ref, st_ref, state_ref):
    t = pl.program_id(1)
    c_rows = HGRN_CHUNK

    @pl.when(t == 0)
    def _():
        state_ref[...] = jnp.zeros(state_ref.shape, f32)

    row = lax.broadcasted_iota(i32, (c_rows, c_rows), 0)
    col = lax.broadcasted_iota(i32, (c_rows, c_rows), 1)
    causal = col <= row
    cum = jnp.where(causal, 1.0, 0.0)
    cum_sub = jnp.where(col < (row // HGRN_SUB) * HGRN_SUB, 1.0, 0.0)
    cum_both = jnp.concatenate([cum, cum_sub], axis=0)

    def chunk(c, carry):
        rows = pl.ds(pl.multiple_of(c * c_rows, c_rows), c_rows)
        lb_all = lb_ref[...]
        f_all = lb_all + (1.0 - lb_all) * _sigmoid(fz_ref[rows, :])
        both_all = jnp.dot(cum_both, jnp.log(f_all), precision=HI, preferred_element_type=f32)
        for h in range(HGRN_HEADS):
            lanes = slice(h * HGRN_DK, (h + 1) * HGRN_DK)
            ng = ng_ref[:, lanes]
            q = q_ref[rows, lanes]
            v = i_ref[rows, lanes]
            gg = g_ref[rows, lanes]
            kk = 1.0 - f_all[:, lanes]
            b = both_all[:c_rows, lanes]
            ref_row = both_all[c_rows:, lanes]
            st = state_ref[h]
            o = lax.dot_general((q * jnp.exp(b)).astype(bf16), st.astype(bf16), NT_DIMS, preferred_element_type=f32)
            qh = (q * jnp.exp(b - ref_row)).astype(bf16)
            parts = []
            for s in range(c_rows // HGRN_SUB):
                ref_s = ref_row[s * HGRN_SUB:s * HGRN_SUB + 1, :]
                kh = (kk * jnp.exp(jnp.minimum(ref_s - b, EXP_CLAMP))).astype(bf16)
                parts.append(lax.dot_general(qh[s * HGRN_SUB:(s + 1) * HGRN_SUB], kh, NT_DIMS,
                                             preferred_element_type=f32))
            att = jnp.where(causal, jnp.concatenate(parts, axis=0), 0.0)
            o = o + jnp.dot(att.astype(bf16), v.astype(bf16), preferred_element_type=f32)
            b_last = b[c_rows - 1:c_rows, :]
            kd = (kk * jnp.exp(b_last - b)).astype(bf16)
            state_ref[h] = st * jnp.exp(b_last) + lax.dot_general(v.astype(bf16), kd, TN_DIMS,
                                                                   preferred_element_type=f32)
            ms = jnp.mean(o * o, axis=1, keepdims=True)
            o_ref[rows, lanes] = o * lax.rsqrt(ms + RMS_EPS) * ng * (gg * _sigmoid(gg))
        return carry

    lax.fori_loop(0, HGRN_TILE // c_rows, chunk, 0)

    @pl.when(t == pl.num_programs(1) - 1)
    def _():
        for h in range(HGRN_HEADS):
            st_ref[0, h] = state_ref[h].T


def _hgrn_prompt(hproj, batch, seq, lb, norm_g):
    nt = seq // HGRN_TILE
    width = HGRN_HEADS * HGRN_DK
    col = lambda j: pl.BlockSpec((HGRN_TILE, width), lambda b, t, j=j: (b * nt + t, j))
    vec = pl.BlockSpec((1, width), lambda b, t: (0, 0))
    return pl.pallas_call(
        _hgrn_prompt_kernel,
        grid=(batch, nt),
        in_specs=[col(0), col(1), col(2), col(3), vec, vec],
        out_specs=[pl.BlockSpec((HGRN_TILE, width), lambda b, t: (b * nt + t, 0)),
                   pl.BlockSpec((1, HGRN_HEADS, HGRN_DK, HGRN_DK), lambda b, t: (b, 0, 0, 0))],
        out_shape=[jax.ShapeDtypeStruct((batch * seq, width), f32),
                   jax.ShapeDtypeStruct((batch, HGRN_HEADS, HGRN_DK, HGRN_DK), f32)],
        scratch_shapes=[pltpu.VMEM((HGRN_HEADS, HGRN_DK, HGRN_DK), f32)],
        compiler_params=_params(("parallel", "arbitrary")),
        name="hgrn_prompt",
    )(hproj, hproj, hproj, hproj, lb.reshape(1, width), norm_g.reshape(1, width))


def _hgrn_sample_kernel(qc_ref, fzc_ref, lbc_ref, i_ref, g_ref, ng_ref, st_ref, o_ref, nst_ref):
    for h in range(HGRN_HEADS):
        lb = lbc_ref[h]
        f = lb + (1.0 - lb) * _sigmoid(fzc_ref[0, h])
        st = f * st_ref[0, h] + (1.0 - f) * i_ref[0, h]
        nst_ref[0, h] = st
        o = jnp.sum(qc_ref[0, h] * st, axis=0, keepdims=True)
        ms = jnp.mean(o * o, axis=1, keepdims=True)
        gg = g_ref[0, h]
        o_ref[0, h] = o * lax.rsqrt(ms + RMS_EPS) * ng_ref[h] * (gg * _sigmoid(gg))


def _hgrn_sample(hs, state, lb, norm_g):
    db = hs.shape[0]
    width = HGRN_HEADS * HGRN_DK
    colv = lambda x: x.reshape(db, HGRN_HEADS, HGRN_DK, 1)
    rowv = lambda x: x.reshape(db, HGRN_HEADS, 1, HGRN_DK)
    cspec = pl.BlockSpec((1, HGRN_HEADS, HGRN_DK, 1), lambda b: (b, 0, 0, 0))
    rspec = pl.BlockSpec((1, HGRN_HEADS, 1, HGRN_DK), lambda b: (b, 0, 0, 0))
    sspec = pl.BlockSpec((1, HGRN_HEADS, HGRN_DK, HGRN_DK), lambda b: (b, 0, 0, 0))
    o, nst = pl.pallas_call(
        _hgrn_sample_kernel,
        grid=(db,),
        in_specs=[cspec, cspec, pl.BlockSpec((HGRN_HEADS, HGRN_DK, 1), lambda b: (0, 0, 0)), rspec, rspec,
                  pl.BlockSpec((HGRN_HEADS, 1, HGRN_DK), lambda b: (0, 0, 0)), sspec],
        out_specs=[rspec, sspec],
        out_shape=[jax.ShapeDtypeStruct((db, HGRN_HEADS, 1, HGRN_DK), f32),
                   jax.ShapeDtypeStruct((db, HGRN_HEADS, HGRN_DK, HGRN_DK), f32)],
        compiler_params=_params(("parallel",)),
        name="hgrn_sample",
    )(colv(hs[:, :width]), colv(hs[:, width:2 * width]), lb.reshape(HGRN_HEADS, HGRN_DK, 1),
      rowv(hs[:, 2 * width:3 * width]), rowv(hs[:, 3 * width:]), norm_g.reshape(HGRN_HEADS, 1, HGRN_DK), state)
    return o.reshape(db, width), nst


PEER_EBLK = 1024
PEER_RANKS = PEER_TOPK + 1
PEER_VROWS = 24
PEER_SUB = 8
PEER_ROWS = 16


def _extract_max(tiles):
    mx = functools.reduce(jnp.maximum, tiles)
    mx = jnp.max(mx, axis=0, keepdims=True)
    return mx, [jnp.where(x == mx, FLOOR, x) for x in tiles]


def _peer_kernel(x_ref, q_ref, k1_ref, k2_ref, u_ref, v_ref, g_ref, b_ref, o_ref,
                 xt_ref, acc_ref, act_ref, p_ref, a_ref, bw_ref, c_ref, s2_ref, v1_ref, v2_ref, ab_ref, cb_ref):
    e = pl.program_id(1)
    tt = x_ref.shape[0]
    n_lane_chunks = tt // LANES
    sub = PEER_SUB
    n_sub = PEER_NKEYS // sub
    keys_per_step = PEER_EBLK // PEER_NKEYS

    @pl.when(e == 0)
    def _():
        xt_ref[...] = x_ref[...].T.astype(bf16)
        acc_ref[...] = jnp.zeros(acc_ref.shape, f32)
        floor_rows = jnp.full((PEER_VROWS, LANES), FLOOR, f32)
        for h in range(PEER_HEADS):
            q1 = q_ref[:, (2 * h) * PEER_DHALF:(2 * h + 1) * PEER_DHALF]
            q2 = q_ref[:, (2 * h + 1) * PEER_DHALF:(2 * h + 2) * PEER_DHALF]
            s1 = lax.dot_general(k1_ref[h], q1, NT_DIMS, precision=HI, preferred_element_type=f32)
            c_ref[h] = s1.reshape(n_sub, sub, tt)
            s2_ref[h] = lax.dot_general(k2_ref[h], q2, NT_DIMS, precision=HI, preferred_element_type=f32)

        def per_head(h, carry):
            for lc in range(n_lane_chunks):
                lanes = slice(lc * LANES, (lc + 1) * LANES)
                for side, vals_ref in enumerate((v1_ref, v2_ref)):
                    vals_ref[...] = floor_rows
                    if side == 0:
                        tiles = [c_ref[h, k, :, lanes] for k in range(n_sub)]
                    else:
                        tiles = [s2_ref[h, k * sub:(k + 1) * sub, lanes] for k in range(n_sub)]
                    for r in range(PEER_RANKS):
                        mx, tiles = _extract_max(tiles)
                        vals_ref[r:r + 1, :] = mx
                cands = [v1_ref[0:1, :] + v2_ref[r:r + sub, :] for r in range(0, PEER_VROWS, sub)]
                cands += [v1_ref[r:r + 1, :] + v2_ref[0:sub, :] for r in range(1, sub)]
                cands += [v1_ref[r:r + sub, :] + v2_ref[0:1, :] for r in range(sub, PEER_VROWS, sub)]
                best = v1_ref[0:1, :] + v2_ref[0:1, :]
                zsum = jnp.zeros((1, LANES), f32)
                kth = best
                for r in range(PEER_TOPK):
                    kth, cands = _extract_max(cands)
                    zsum = zsum + jnp.exp(kth - best)
                nxt, _ = _extract_max(cands)
                thresh = 0.5 * (kth + nxt)
                s1 = c_ref[h, :, :, lanes]
                a_ref[h, :, :, lanes] = jnp.exp(s1 - v1_ref[0:1, :]) / zsum
                bw_ref[h, :, lanes] = jnp.exp(s2_ref[h, :, lanes] - v2_ref[0:1, :])
                c_ref[h, :, :, lanes] = thresh - s1
            return carry

        lax.fori_loop(0, PEER_HEADS, per_head, 0)

    act_ref[...] = jnp.dot(u_ref[0], xt_ref[...], preferred_element_type=f32)
    for h in range(PEER_HEADS):
        for ii in range(keys_per_step):
            ab_ref[ii, h] = jnp.broadcast_to(a_ref[h, e, ii:ii + 1, :], (sub, tt))
            cb_ref[ii, h] = jnp.broadcast_to(c_ref[h, e, ii:ii + 1, :], (sub, tt))

    def per_second_keys(k, carry):
        r0 = pl.multiple_of(k * PEER_ROWS, PEER_ROWS)
        for lc in range(n_lane_chunks):
            lanes = slice(lc * LANES, (lc + 1) * LANES)
            w = [None] * keys_per_step
            for h in range(PEER_HEADS):
                s2 = s2_ref[h, pl.ds(r0, PEER_ROWS), lanes].reshape(PEER_ROWS // sub, sub, LANES)
                bw = bw_ref[h, pl.ds(r0, PEER_ROWS), lanes].reshape(PEER_ROWS // sub, sub, LANES)
                for ii in range(keys_per_step):
                    hit = s2 >= cb_ref[ii, h, :, lanes][None]
                    term = ab_ref[ii, h, :, lanes][None] * jnp.where(hit, bw, 0.0)
                    w[ii] = term if h == 0 else w[ii] + term
            for ii in range(keys_per_step):
                rows = pl.ds(pl.multiple_of(ii * PEER_NKEYS + r0, PEER_ROWS), PEER_ROWS)
                act = act_ref[rows, lanes]
                gelu = 0.5 * act * (1.0 + lax.erf(act * (2.0 ** -0.5)))
                p_ref[rows, lanes] = (w[ii].reshape(PEER_ROWS, LANES) * gelu).astype(bf16)
        return carry

    lax.fori_loop(0, PEER_NKEYS // PEER_ROWS, per_second_keys, 0)
    acc_ref[...] += lax.dot_general(p_ref[...], v_ref[0], TN_DIMS, preferred_element_type=f32)

    @pl.when(e == pl.num_programs(1) - 1)
    def _():
        o_ref[...] = _layer_norm(DN_ALPHA * x_ref[...] + acc_ref[...], g_ref[...], b_ref[...])


def _peer_ln(x, wq, keys, u, v, layer, g, b, *, tt=TOKEN_TILE):
    m, d = x.shape
    n_exp = u.shape[1]
    q = _mm(x, wq, tm=tt)
    nq = q.shape[1]
    return pl.pallas_call(
        _peer_kernel,
        grid=(m // tt, n_exp // PEER_EBLK),
        in_specs=[pl.BlockSpec((tt, d), lambda t, e: (t, 0)),
                  pl.BlockSpec((tt, nq), lambda t, e: (t, 0)),
                  pl.BlockSpec((PEER_HEADS, PEER_NKEYS, PEER_DHALF), lambda t, e: (0, 0, 0)),
                  pl.BlockSpec((PEER_HEADS, PEER_NKEYS, PEER_DHALF), lambda t, e: (0, 0, 0)),
                  pl.BlockSpec((1, PEER_EBLK, d), lambda t, e: (layer, e, 0)),
                  pl.BlockSpec((1, PEER_EBLK, d), lambda t, e: (layer, e, 0)),
                  pl.BlockSpec((1, d), lambda t, e: (0, 0)),
                  pl.BlockSpec((1, d), lambda t, e: (0, 0))],
        out_specs=pl.BlockSpec((tt, d), lambda t, e: (t, 0)),
        out_shape=jax.ShapeDtypeStruct((m, d), f32),
        scratch_shapes=[pltpu.VMEM((d, tt), bf16), pltpu.VMEM((tt, d), f32),
                        pltpu.VMEM((PEER_EBLK, tt), f32), pltpu.VMEM((PEER_EBLK, tt), bf16)]
                       + [pltpu.VMEM((PEER_HEADS, PEER_NKEYS // PEER_SUB, PEER_SUB, tt), f32),
                          pltpu.VMEM((PEER_HEADS, PEER_NKEYS, tt), f32),
                          pltpu.VMEM((PEER_HEADS, PEER_NKEYS // PEER_SUB, PEER_SUB, tt), f32),
                          pltpu.VMEM((PEER_HEADS, PEER_NKEYS, tt), f32)]
                       + [pltpu.VMEM((PEER_VROWS, LANES), f32)] * 2
                       + [pltpu.VMEM((PEER_EBLK // PEER_NKEYS, PEER_HEADS, PEER_SUB, tt), f32)] * 2,
        compiler_params=_params(("parallel", "arbitrary")),
        name="peer",
    )(x, q, keys[0], keys[1], u, v, g.reshape(1, d), b.reshape(1, d))


def kernel(x_prompt, x_sample, cache_k, cache_v, state_conv, state_hgrn, page_table, w_in_even, conv_w, conv_b, conv_ln_g, conv_ln_b, w_out_even, w_in_odd, hgrn_lb_logits, hgrn_norm_g, w_out_odd, ln_g, ln_b, peer_wq, peer_keys, peer_u, peer_v):
    batch, seq, d = x_prompt.shape
    db = x_sample.shape[0]
    n_prompt = batch * seq
    pad_sample = lambda a: jnp.pad(a, ((0, SAMPLE_TILE - a.shape[0]), (0, 0)))
    pages = lambda a: a.reshape(batch, seq // PAGE_SIZE, MOBA_HEADS, MOBA_HEAD_DIM, PAGE_SIZE).transpose(0, 1, 4, 2, 3)
    heads = lambda a: a.reshape(db, MOBA_HEADS, MOBA_HEAD_DIM)

    lb_p = jax.nn.softmax(hgrn_lb_logits.astype(f32), axis=0)
    lb_all = jnp.cumsum(lb_p, axis=0) - lb_p[0]

    peer_u16, peer_v16 = peer_u.astype(bf16), peer_v.astype(bf16)
    xp = x_prompt.reshape(n_prompt, d)
    xs = pad_sample(x_sample.reshape(db, d))
    kp_l, vp_l, ks_l, vs_l, cp_l, cs_l, hp_l, hs_l = [], [], [], [], [], [], [], []
    for l in range(DEPTH):
        if l % 2 == 0:
            e = l // 2
            w_qk, w_vag = w_in_even[e][:, :2 * MOBA_WIDTH], w_in_even[e][:, 2 * MOBA_WIDTH:]
            qk_p, k_pages = _mm(xp, w_qk, split=True, paged_chunk=1)
            vag_p, v_pages = _mm(xp, w_vag, paged_chunk=0)
            qk_s, vag_s = _mm(xs, w_qk, split=True, tm=SAMPLE_TILE)[:db], _mm(xs, w_vag, tm=SAMPLE_TILE)[:db]
            kp_l.append(pages(k_pages))
            vp_l.append(pages(v_pages))
            q_s, k_s, v_s = qk_s[:, :MOBA_WIDTH], qk_s[:, MOBA_WIDTH:], vag_s[:, :MOBA_WIDTH]
            ks_l.append(k_s.reshape(db, 1, MOBA_HEADS, MOBA_HEAD_DIM))
            vs_l.append(v_s.reshape(db, 1, MOBA_HEADS, MOBA_HEAD_DIM))
            att_p = _moba_prompt(qk_p, vag_p, batch, seq)
            att_s = _moba_sample(heads(q_s), heads(k_s), heads(v_s),
                                 cache_k[e].transpose(0, 2, 3, 1), cache_v[e].transpose(0, 2, 3, 1), page_table)
            att_s = att_s[:, :, :MOBA_HEAD_DIM].reshape(db, MOBA_WIDTH)
            cprm = (conv_w[e], conv_b[e], conv_ln_g[e], conv_ln_b[e])
            cy_p, buf_p = _conv_prompt(vag_p, batch, seq, *cprm)
            cy_s, buf_s = _conv_sample(vag_s[:, MOBA_WIDTH:MOBA_WIDTH + CONV_CH], vag_s[:, MOBA_WIDTH + CONV_CH:],
                                       state_conv[e], *cprm)
            cp_l.append(buf_p)
            cs_l.append(buf_s)
            mix_p, mix_s = [att_p, cy_p], [pad_sample(att_s), pad_sample(cy_s)]
            w_out = w_out_even[e]
        else:
            oi = l // 2
            hproj_p, hproj_s = _mm(xp, w_in_odd[oi], tm=256), _mm(xs, w_in_odd[oi], tm=SAMPLE_TILE)[:db]
            o_p, st_p = _hgrn_prompt(hproj_p, batch, seq, lb_all[l], hgrn_norm_g[oi])
            o_s, st_s = _hgrn_sample(hproj_s, state_hgrn[oi].astype(f32), lb_all[l], hgrn_norm_g[oi])
            hp_l.append(st_p)
            hs_l.append(st_s)
            mix_p, mix_s = [o_p], [pad_sample(o_s)]
            w_out = w_out_odd[oi]
        xp = _mm_res_ln(mix_p, w_out, xp, ln_g[l, 0], ln_b[l, 0])
        xs = _mm_res_ln(mix_s, w_out, xs, ln_g[l, 0], ln_b[l, 0], tm=SAMPLE_TILE)
        prm = (peer_wq[l], peer_keys[l], peer_u16, peer_v16, l, ln_g[l, 1], ln_b[l, 1])
        xp = _peer_ln(xp, *prm)
        xs = _peer_ln(xs, *prm, tt=SAMPLE_TILE)
    y_prompt = xp.reshape(batch, seq, d)
    y_sample = xs[:db].reshape(db, 1, d)
    return (y_prompt, y_sample, jnp.stack(kp_l), jnp.stack(vp_l), jnp.stack(ks_l), jnp.stack(vs_l),
            jnp.stack(cp_l), jnp.stack(cs_l), jnp.stack(hp_l), jnp.stack(hs_l))
```

```python
import functools

import jax
import jax.numpy as jnp
from jax import lax
from jax.experimental import pallas as pl
from jax.experimental.pallas import tpu as pltpu

f32 = jnp.float32
bf16 = jnp.bfloat16
i32 = jnp.int32
HI = lax.Precision.HIGHEST

D_MODEL = 1024
DEPTH = 2
PAGE_SIZE = 128
MOBA_HEADS = 8
MOBA_HEAD_DIM = 64
MOBA_WIDTH = MOBA_HEADS * MOBA_HEAD_DIM
MOBA_BLOCK = 256
MOBA_TOPK = 3
MOBA_SCALE = MOBA_HEAD_DIM ** -0.5
CONV_CH = D_MODEL // 2
CONV_WIDTH = 31
HGRN_HEADS = 8
HGRN_DK = D_MODEL // HGRN_HEADS
HGRN_CHUNK = 64
HGRN_SUB = 16
PEER_HEADS = 8
PEER_NKEYS = 128
PEER_DHALF = 128
PEER_TOPK = 16
LN_EPS = 1e-5
RMS_EPS = 1e-6
DN_ALPHA = (2 * DEPTH) ** 0.25

LANES = 128
TOKEN_TILE = 512
SAMPLE_TILE = 128
WIDE_OUT_TILE = 256
MASK_NEG = -1e30
FLOOR = -3e38
EXP_CLAMP = 60.0
VMEM_LIMIT = 56 << 20

NT_DIMS = (((1,), (1,)), ((), ()))
TN_DIMS = (((0,), (0,)), ((), ()))


def _params(semantics, vmem=VMEM_LIMIT):
    return pltpu.CompilerParams(dimension_semantics=semantics, vmem_limit_bytes=vmem)


def _layer_norm(y, g, b):
    mu = jnp.mean(y, axis=-1, keepdims=True)
    d = y - mu
    var = jnp.mean(d * d, axis=-1, keepdims=True)
    return d * lax.rsqrt(var + LN_EPS) * g + b


def _sigmoid(x):
    return 1.0 / (1.0 + jnp.exp(-x))


MM_CHUNK = 512


def _mm_kernel(x_ref, *refs, split, paged_chunk):
    refs = list(refs)
    t_ref = refs.pop() if paged_chunk is not None else None
    o_ref = refs.pop()
    wh_ref = refs[0]
    x = x_ref[...]
    xh = x.astype(bf16)
    if split:
        xl = (x - xh.astype(f32)).astype(bf16)
    for c, j in enumerate(range(0, o_ref.shape[1], MM_CHUNK)):
        wh = wh_ref[:, j:j + MM_CHUNK]
        acc = jnp.dot(xh, wh, preferred_element_type=f32)
        if split:
            acc = acc + jnp.dot(xl, wh, preferred_element_type=f32)
            acc = acc + jnp.dot(xh, refs[1][:, j:j + MM_CHUNK], preferred_element_type=f32)
        o_ref[:, j:j + MM_CHUNK] = acc
        if c == paged_chunk:
            for pg in range(o_ref.shape[0] // PAGE_SIZE):
                t_ref[pg] = acc[pg * PAGE_SIZE:(pg + 1) * PAGE_SIZE, :].T


def _mm(x, w, *, split=False, tm=TOKEN_TILE, paged_chunk=None):
    m, k = x.shape
    n = w.shape[1]
    wh = w.astype(bf16)
    ws = [wh] + ([(w - wh.astype(f32)).astype(bf16)] if split else [])
    out_specs = [pl.BlockSpec((tm, n), lambda i: (i, 0))]
    out_shape = [jax.ShapeDtypeStruct((m, n), f32)]
    if paged_chunk is not None:
        out_specs.append(pl.BlockSpec((tm // PAGE_SIZE, MM_CHUNK, PAGE_SIZE), lambda i: (i, 0, 0)))
        out_shape.append(jax.ShapeDtypeStruct((m // PAGE_SIZE, MM_CHUNK, PAGE_SIZE), f32))
    outs = pl.pallas_call(
        functools.partial(_mm_kernel, split=split, paged_chunk=paged_chunk),
        grid=(m // tm,),
        in_specs=[pl.BlockSpec((tm, k), lambda i: (i, 0))] + [pl.BlockSpec((k, n), lambda i: (0, 0))] * len(ws),
        out_specs=out_specs,
        out_shape=out_shape,
        compiler_params=_params(("parallel",)),
        name="proj_split" if split else "proj",
    )(x, *ws)
    return outs[0] if paged_chunk is None else outs


def _mm_res_ln_kernel(*refs):
    *a_refs, w_ref, x_ref, g_ref, b_ref, o_ref = refs
    acc = DN_ALPHA * x_ref[...]
    row = 0
    for a_ref in a_refs:
        k = a_ref.shape[1]
        acc = acc + jnp.dot(a_ref[...].astype(bf16), w_ref[row:row + k, :], preferred_element_type=f32)
        row += k
    o_ref[...] = _layer_norm(acc, g_ref[...], b_ref[...])


def _mm_res_ln(parts, w, x, g, b, *, tm=TOKEN_TILE):
    m, n = x.shape
    k = w.shape[0]
    return pl.pallas_call(
        _mm_res_ln_kernel,
        grid=(m // tm,),
        in_specs=[pl.BlockSpec((tm, a.shape[1]), lambda i: (i, 0)) for a in parts]
                 + [pl.BlockSpec((k, n), lambda i: (0, 0)),
                    pl.BlockSpec((tm, n), lambda i: (i, 0)), pl.BlockSpec((1, n), lambda i: (0, 0)),
                    pl.BlockSpec((1, n), lambda i: (0, 0))],
        out_specs=pl.BlockSpec((tm, n), lambda i: (i, 0)),
        out_shape=jax.ShapeDtypeStruct((m, n), f32),
        compiler_params=_params(("parallel",)),
        name="out_proj_ln",
    )(*parts, w.astype(bf16), x, g.reshape(1, n), b.reshape(1, n))


CONV_TILE = 256
CONV_ROWS = 64
CONV_HIST = 32
SUBLANES = 8
CONV_SHIFT_ROWS = CONV_TILE + CONV_HIST - SUBLANES


def _conv_prompt_kernel(a_ref, gt_ref, w_ref, cb_ref, g_ref, b_ref, y_ref, st_ref, buf_ref, shift_ref):
    t = pl.program_id(1)

    @pl.when(t == 0)
    def _():
        buf_ref[0:CONV_HIST, :] = jnp.zeros((CONV_HIST, CONV_CH), f32)

    buf_ref[CONV_HIST:CONV_HIST + CONV_TILE, :] = a_ref[...] * _sigmoid(gt_ref[...])
    for q in range(1, SUBLANES):
        shift_ref[q - 1] = buf_ref[q:q + CONV_SHIFT_ROWS, :]
    first = CONV_HIST - (CONV_WIDTH - 1)
    for r in range(0, CONV_TILE, CONV_ROWS):
        acc = jnp.zeros((CONV_ROWS, CONV_CH), f32) + cb_ref[...]
        for j in range(CONV_WIDTH):
            q = (first + j) % SUBLANES
            base = r + first + j - q
            rows = buf_ref[base:base + CONV_ROWS, :] if q == 0 else shift_ref[q - 1, base:base + CONV_ROWS, :]
            acc = acc + w_ref[j:j + 1, :] * rows
        y = _layer_norm(acc, g_ref[...], b_ref[...])
        y_ref[r:r + CONV_ROWS, :] = y * _sigmoid(y)
    tail = buf_ref[CONV_TILE:CONV_TILE + CONV_HIST, :]
    buf_ref[0:CONV_HIST, :] = tail

    @pl.when(t == pl.num_programs(1) - 1)
    def _():
        st_ref[0] = tail


def _conv_prompt(vag, batch, seq, w, cb, g, b):
    nt = seq // CONV_TILE
    wpad = jnp.zeros((CONV_HIST, CONV_CH), f32).at[:CONV_WIDTH].set(w)
    row = lambda v: v.reshape(1, CONV_CH)
    y, st = pl.pallas_call(
        _conv_prompt_kernel,
        grid=(batch, nt),
        in_specs=[pl.BlockSpec((CONV_TILE, CONV_CH), lambda bi, t: (bi * nt + t, 1)),
                  pl.BlockSpec((CONV_TILE, CONV_CH), lambda bi, t: (bi * nt + t, 2)),
                  pl.BlockSpec((CONV_HIST, CONV_CH), lambda bi, t: (0, 0))]
                 + [pl.BlockSpec((1, CONV_CH), lambda bi, t: (0, 0))] * 3,
        out_specs=[pl.BlockSpec((CONV_TILE, CONV_CH), lambda bi, t: (bi * nt + t, 0)),
                   pl.BlockSpec((1, CONV_HIST, CONV_CH), lambda bi, t: (bi, 0, 0))],
        out_shape=[jax.ShapeDtypeStruct((batch * seq, CONV_CH), f32),
                   jax.ShapeDtypeStruct((batch, CONV_HIST, CONV_CH), f32)],
        scratch_shapes=[pltpu.VMEM((CONV_TILE + CONV_HIST, CONV_CH), f32),
                        pltpu.VMEM((SUBLANES - 1, CONV_SHIFT_ROWS, CONV_CH), f32)],
        compiler_params=_params(("arbitrary", "arbitrary")),
        name="conv_prompt",
    )(vag, vag, wpad, row(cb), row(g), row(b))
    return y, st[:, CONV_HIST - (CONV_WIDTH - 1):]


def _conv_sample_kernel(a_ref, gt_ref, st_ref, w_ref, cb_ref, g_ref, b_ref, y_ref, nst_ref):
    u = a_ref[...] * _sigmoid(gt_ref[...])
    acc = cb_ref[...] + w_ref[CONV_WIDTH - 1:CONV_WIDTH, :] * u
    for j in range(CONV_WIDTH - 1):
        acc = acc + w_ref[j:j + 1, :] * st_ref[j]
    y = _layer_norm(acc, g_ref[...], b_ref[...])
    y_ref[...] = y * _sigmoid(y)
    for j in range(CONV_WIDTH - 2):
        nst_ref[j] = st_ref[j + 1]
    nst_ref[CONV_WIDTH - 2] = u


def _conv_sample(a, gt, state, w, cb, g, b):
    db = a.shape[0]
    wpad = jnp.zeros((CONV_HIST, CONV_CH), f32).at[:CONV_WIDTH].set(w)
    row = lambda v: v.reshape(1, CONV_CH)
    y, nst = pl.pallas_call(
        _conv_sample_kernel,
        out_shape=[jax.ShapeDtypeStruct((db, CONV_CH), f32),
                   jax.ShapeDtypeStruct((CONV_WIDTH - 1, db, CONV_CH), f32)],
        name="conv_sample",
    )(a, gt, state.transpose(1, 0, 2), wpad, row(cb), row(g), row(b))
    return y, nst.transpose(1, 0, 2)


GATE_ROWS = 1024
ATTN_KEYS = 2 * MOBA_BLOCK


def _block_sum_kernel(k_ref, o_ref):
    o_ref[0] = jnp.sum(k_ref[...], axis=0, keepdims=True)


def _block_sums(qk, batch, seq):
    nb = seq // MOBA_BLOCK
    return pl.pallas_call(
        _block_sum_kernel,
        grid=(batch * nb,),
        in_specs=[pl.BlockSpec((MOBA_BLOCK, MOBA_WIDTH), lambda i: (i, 1))],
        out_specs=pl.BlockSpec((1, 1, MOBA_WIDTH), lambda i: (i, 0, 0)),
        out_shape=jax.ShapeDtypeStruct((batch * nb, 1, MOBA_WIDTH), f32),
        compiler_params=_params(("parallel",)),
        name="moba_block_sums",
    )(qk)


def _moba_gate_kernel(q_ref, k_ref, v_ref, ks_ref, qa_ref, ka_ref, va_ref, *, nb):
    t = pl.program_id(2)
    lane = lax.broadcasted_iota(i32, (MOBA_BLOCK, LANES), 1)
    blk = lane - MOBA_HEAD_DIM
    km_lane = lax.broadcasted_iota(i32, (nb, LANES), 1)
    ksum = ks_ref[0] * (1.0 / MOBA_BLOCK)
    pad_top = jnp.zeros((MOBA_HEAD_DIM, LANES), f32)
    pad_bot = jnp.zeros((LANES - MOBA_HEAD_DIM - nb, LANES), f32)
    for c in range(GATE_ROWS // MOBA_BLOCK):
        own = t * (GATE_ROWS // MOBA_BLOCK) + c
        rows = slice(c * MOBA_BLOCK, (c + 1) * MOBA_BLOCK)
        q2 = q_ref[rows, :]
        k2 = k_ref[rows, :]
        v2 = v_ref[rows, :]
        cand = (blk >= 0) & (blk < own)
        for j in range(2):
            head = (km_lane >= j * MOBA_HEAD_DIM) & (km_lane < (j + 1) * MOBA_HEAD_DIM)
            km = jnp.concatenate([pad_top, jnp.where(head, ksum, 0.0), pad_bot], axis=0)
            gate = lax.dot_general(q2, km, NT_DIMS, precision=HI, preferred_element_type=f32)
            masked = jnp.where(cand, gate, FLOOR)
            sel = jnp.zeros((MOBA_BLOCK, LANES), f32)
            for _ in range(MOBA_TOPK):
                mx = jnp.max(masked, axis=1, keepdims=True)
                hit = (masked == mx) & cand
                sel = jnp.where(hit, 1.0, sel)
                masked = jnp.where(hit, FLOOR, masked)
            bias = jnp.where((sel > 0.0) | (blk == own), 0.0, MASK_NEG)
            qj = q2 if j == 0 else pltpu.roll(q2, MOBA_HEAD_DIM, 1)
            kj = k2 if j == 0 else pltpu.roll(k2, MOBA_HEAD_DIM, 1)
            vj = v2 if j == 0 else pltpu.roll(v2, MOBA_HEAD_DIM, 1)
            in_head = lane < MOBA_HEAD_DIM
            in_bias = lane < MOBA_HEAD_DIM + nb
            qa = jnp.where(in_head, qj * MOBA_SCALE, jnp.where(in_bias, bias, 0.0))
            ka = jnp.where(in_head, kj, jnp.where(blk == own, 1.0, 0.0))
            va = jnp.where(in_head, vj, jnp.where(blk == 0, 1.0, 0.0))
            qa_ref[0, j, rows, :] = qa.astype(bf16)
            ka_ref[0, j, rows, :] = ka.astype(bf16)
            va_ref[0, j, rows, :] = va.astype(bf16)


def _moba_attn_kernel(qa_ref, ka_ref, va_ref, o_ref):
    i = pl.program_id(1)
    last = i // 2
    row = lax.broadcasted_iota(i32, (MOBA_BLOCK, ATTN_KEYS), 0)
    col = lax.broadcasted_iota(i32, (MOBA_BLOCK, ATTN_KEYS), 1)
    lane = lax.broadcasted_iota(i32, (MOBA_BLOCK, LANES), 1)

    def tile(h, k0, mask):
        s = lax.dot_general(qa_ref[0, h], ka_ref[0, h, pl.ds(k0, ATTN_KEYS), :], NT_DIMS, preferred_element_type=f32)
        return s if mask is None else jnp.where(mask, s, MASK_NEG)

    def weighted(p, h, k0):
        return jnp.dot(p.astype(bf16), va_ref[0, h, pl.ds(k0, ATTN_KEYS), :], preferred_element_type=f32)

    k_last = pl.multiple_of(last * ATTN_KEYS, ATTN_KEYS)
    visible = (k_last + col) <= (i * MOBA_BLOCK + row)
    state = []
    for h in range(MOBA_HEADS):
        s = tile(h, k_last, visible)
        m = jnp.max(s, axis=1, keepdims=True)
        state += [m, weighted(jnp.exp(s - m), h, k_last)]

    def body(n, carry):
        k0 = pl.multiple_of(n * ATTN_KEYS, ATTN_KEYS)
        out = []
        for h in range(MOBA_HEADS):
            m, acc = carry[2 * h:2 * h + 2]
            s = tile(h, k0, None)
            mn = jnp.maximum(m, jnp.max(s, axis=1, keepdims=True))
            out += [mn, jnp.exp(m - mn) * acc + weighted(jnp.exp(s - mn), h, k0)]
        return tuple(out)

    final = lax.fori_loop(0, last, body, tuple(state))
    for pair in range(MOBA_HEADS // 2):
        acc0, acc1 = final[4 * pair + 1], final[4 * pair + 3]
        out0 = acc0 / acc0[:, MOBA_HEAD_DIM:MOBA_HEAD_DIM + 1]
        out1 = acc1 / acc1[:, MOBA_HEAD_DIM:MOBA_HEAD_DIM + 1]
        o_ref[:, pair * LANES:(pair + 1) * LANES] = jnp.where(lane < MOBA_HEAD_DIM, out0,
                                                              pltpu.roll(out1, MOBA_HEAD_DIM, 1))


def _moba_prompt(qk, vag, batch, seq):
    nb = seq // MOBA_BLOCK
    npair = MOBA_WIDTH // LANES
    nt = seq // GATE_ROWS
    ksums = _block_sums(qk, batch, seq).reshape(batch, nb, MOBA_WIDTH)
    per_head = pl.BlockSpec((1, 2, GATE_ROWS, LANES), lambda b, p, t: (b, p, t, 0))
    qa, ka, va = pl.pallas_call(
        functools.partial(_moba_gate_kernel, nb=nb),
        grid=(batch, npair, nt),
        in_specs=[pl.BlockSpec((GATE_ROWS, LANES), lambda b, p, t: (b * nt + t, p)),
                  pl.BlockSpec((GATE_ROWS, LANES), lambda b, p, t: (b * nt + t, npair + p)),
                  pl.BlockSpec((GATE_ROWS, LANES), lambda b, p, t: (b * nt + t, p)),
                  pl.BlockSpec((1, nb, LANES), lambda b, p, t: (b, 0, p))],
        out_specs=[per_head] * 3,
        out_shape=[jax.ShapeDtypeStruct((batch, MOBA_HEADS, seq, LANES), bf16)] * 3,
        compiler_params=_params(("parallel", "parallel", "parallel")),
        name="moba_gate",
    )(qk, qk, vag, ksums)
    return pl.pallas_call(
        _moba_attn_kernel,
        grid=(batch, nb),
        in_specs=[pl.BlockSpec((1, MOBA_HEADS, MOBA_BLOCK, LANES), lambda b, i: (b, 0, i, 0)),
                  pl.BlockSpec((1, MOBA_HEADS, seq, LANES), lambda b, i: (b, 0, 0, 0),
                               pipeline_mode=pl.Buffered(1)),
                  pl.BlockSpec((1, MOBA_HEADS, seq, LANES), lambda b, i: (b, 0, 0, 0),
                               pipeline_mode=pl.Buffered(1))],
        out_specs=pl.BlockSpec((MOBA_BLOCK, MOBA_WIDTH), lambda b, i: (b * nb + i, 0)),
        out_shape=jax.ShapeDtypeStruct((batch * seq, MOBA_WIDTH), f32),
        compiler_params=_params(("parallel", "arbitrary")),
        name="moba_attn",
    )(qa, ka, va)


PAGES_PER_STEP = 8
PAGES_PER_BLOCK = MOBA_BLOCK // PAGE_SIZE
BLOCKS_PER_STEP = PAGES_PER_STEP // PAGES_PER_BLOCK


def _moba_sample_stream_kernel(pt_ref, qb_ref, *refs):
    del pt_ref
    k_refs = refs[:PAGES_PER_STEP]
    v_refs = refs[PAGES_PER_STEP:2 * PAGES_PER_STEP]
    gate_ref, m_ref, l_ref, acc_ref = refs[2 * PAGES_PER_STEP:]
    qb = qb_ref[0]
    head = lax.broadcasted_iota(i32, (MOBA_HEADS, LANES), 0)
    tile = (MOBA_HEADS, LANES)
    for jj in range(BLOCKS_PER_STEP):
        pages = range(PAGES_PER_BLOCK * jj, PAGES_PER_BLOCK * (jj + 1))
        raw = [jnp.sum(k_refs[x][0] * qb, axis=1) for x in pages]
        gate = functools.reduce(jnp.add, [jnp.sum(r, axis=1, keepdims=True) for r in raw]) * (1.0 / MOBA_BLOCK)
        m = functools.reduce(jnp.maximum, [jnp.max(r, axis=1, keepdims=True) for r in raw]) * MOBA_SCALE
        l = jnp.zeros((MOBA_HEADS, 1), f32)
        pv = jnp.zeros((MOBA_HEADS, MOBA_WIDTH), f32)
        for r, x in zip(raw, pages):
            p = jnp.exp(r * MOBA_SCALE - m)
            l = l + jnp.sum(p, axis=1, keepdims=True)
            vt = v_refs[x][0].reshape(MOBA_WIDTH, PAGE_SIZE).astype(bf16)
            pv = pv + lax.dot_general(p.astype(bf16), vt, NT_DIMS, preferred_element_type=f32)
        acc = jnp.zeros(tile, f32)
        for pair in range(MOBA_WIDTH // LANES):
            both = pv[:, pair * LANES:(pair + 1) * LANES]
            acc = acc + jnp.where(head == 2 * pair, both, 0.0)
            acc = acc + jnp.where(head == 2 * pair + 1, pltpu.roll(both, MOBA_HEAD_DIM, 1), 0.0)
        gate_ref[0, 0, jj] = jnp.broadcast_to(gate, tile)
        m_ref[0, 0, jj] = jnp.broadcast_to(m, tile)
        l_ref[0, 0, jj] = jnp.broadcast_to(l, tile)
        acc_ref[0, 0, jj] = acc


def _moba_sample_combine_kernel(gate_ref, m_ref, l_ref, acc_ref, q_ref, kn_ref, vn_ref, o_ref):
    db, nb, nh, width = gate_ref.shape
    masked = gate_ref[...]
    sel = jnp.zeros((db, nb, nh, width), f32)
    for _ in range(MOBA_TOPK):
        mx = jnp.max(masked, axis=1, keepdims=True)
        hit = masked == mx
        sel = jnp.where(hit, 1.0, sel)
        masked = jnp.where(hit, FLOOR, masked)
    chosen = sel > 0.0
    s_self = jnp.sum(q_ref[...] * kn_ref[...], axis=-1, keepdims=True) * MOBA_SCALE
    m = m_ref[...]
    top = jnp.maximum(jnp.max(jnp.where(chosen, m, FLOOR), axis=1), s_self)
    w = jnp.where(chosen, jnp.exp(m - top[:, None]), 0.0)
    w_self = jnp.exp(s_self - top)
    denom = jnp.sum(w * l_ref[...], axis=1) + w_self
    num = jnp.sum(w * acc_ref[...], axis=1) + w_self * vn_ref[...]
    o_ref[...] = num / denom


def _moba_sample(q, kn, vn, k_pool, v_pool, page_table):
    db, n_pages = page_table.shape
    nb = n_pages // PAGES_PER_BLOCK
    ng = n_pages // PAGES_PER_STEP
    page_block = (1, MOBA_HEADS, MOBA_HEAD_DIM, PAGE_SIZE)
    page_spec = lambda x: pl.BlockSpec(page_block, lambda b, g, pt, x=x: (pt[b, g * PAGES_PER_STEP + x], 0, 0, 0))
    stat = pl.BlockSpec((1, 1, BLOCKS_PER_STEP, MOBA_HEADS, LANES), lambda b, g, pt: (b, g, 0, 0, 0))
    stat_shape = jax.ShapeDtypeStruct((db, ng, BLOCKS_PER_STEP, MOBA_HEADS, LANES), f32)
    qb = jnp.broadcast_to(q[..., None], (db,) + page_block[1:])
    stats = pl.pallas_call(
        _moba_sample_stream_kernel,
        grid_spec=pltpu.PrefetchScalarGridSpec(
            num_scalar_prefetch=1,
            grid=(db, ng),
            in_specs=[pl.BlockSpec(page_block, lambda b, g, pt: (b, 0, 0, 0))]
                     + [page_spec(x) for x in range(PAGES_PER_STEP)] * 2,
            out_specs=[stat] * 4),
        out_shape=[stat_shape] * 4,
        compiler_params=_params(("parallel", "parallel")),
        name="moba_sample_stream",
    )(page_table, qb, *([k_pool] * PAGES_PER_STEP), *([v_pool] * PAGES_PER_STEP))
    gate, m, l, acc = (a.reshape(db, nb, MOBA_HEADS, LANES) for a in stats)
    vn_wide = jnp.pad(vn, ((0, 0), (0, 0), (0, LANES - MOBA_HEAD_DIM)))
    return pl.pallas_call(
        _moba_sample_combine_kernel,
        out_shape=jax.ShapeDtypeStruct((db, MOBA_HEADS, LANES), f32),
        compiler_params=pltpu.CompilerParams(vmem_limit_bytes=VMEM_LIMIT),
        name="moba_sample_combine",
    )(gate, m, l, acc, q, kn, vn_wide)


HGRN_TILE = 512


def _hgrn_prompt_kernel(q_ref, fz_ref, i_ref, g_ref, lb_ref, ng_ref, o_ref, st_ref, state_ref):
    t = pl.program_id(1)
    c_rows = HGRN_CHUNK

    @pl.when(t == 0)
    def _():
        state_ref[...] = jnp.zeros(state_ref.shape, f32)

    row = lax.broadcasted_iota(i32, (c_rows, c_rows), 0)
    col = lax.broadcasted_iota(i32, (c_rows, c_rows), 1)
    causal = col <= row
    cum = jnp.where(causal, 1.0, 0.0)
    cum_sub = jnp.where(col < (row // HGRN_SUB) * HGRN_SUB, 1.0, 0.0)
    cum_both = jnp.concatenate([cum, cum_sub], axis=0)

    def chunk(c, carry):
        rows = pl.ds(pl.multiple_of(c * c_rows, c_rows), c_rows)
        lb_all = lb_ref[...]
        f_all = lb_all + (1.0 - lb_all) * _sigmoid(fz_ref[rows, :])
        both_all = jnp.dot(cum_both, jnp.log(f_all), precision=HI, preferred_element_type=f32)
        for h in range(HGRN_HEADS):
            lanes = slice(h * HGRN_DK, (h + 1) * HGRN_DK)
            ng = ng_ref[:, lanes]
            q = q_ref[rows, lanes]
            v = i_ref[rows, lanes]
            gg = g_ref[rows, lanes]
            kk = 1.0 - f_all[:, lanes]
            b = both_all[:c_rows, lanes]
            ref_row = both_all[c_rows:, lanes]
            st = state_ref[h]
            o = lax.dot_general((q * jnp.exp(b)).astype(bf16), st.astype(bf16), NT_DIMS, preferred_element_type=f32)
            qh = (q * jnp.exp(b - ref_row)).astype(bf16)
            parts = []
            for s in range(c_rows // HGRN_SUB):
                ref_s = ref_row[s * HGRN_SUB:s * HGRN_SUB + 1, :]
                kh = (kk * jnp.exp(jnp.minimum(ref_s - b, EXP_CLAMP))).astype(bf16)
                parts.append(lax.dot_general(qh[s * HGRN_SUB:(s + 1) * HGRN_SUB], kh, NT_DIMS,
                                             preferred_element_type=f32))
            att = jnp.where(causal, jnp.concatenate(parts, axis=0), 0.0)
            o = o + jnp.dot(att.astype(bf16), v.astype(bf16), preferred_element_type=f32)
            b_last = b[c_rows - 1:c_rows, :]
            kd = (kk * jnp.exp(b_last - b)).astype(bf16)
            state_ref[h] = st * jnp.exp(b_last) + lax.dot_general(v.astype(bf16), kd, TN_DIMS,
                                                                   preferred_element_type=f32)
            ms = jnp.mean(o * o, axis=1, keepdims=True)
            o_ref[rows, lanes] = o * lax.rsqrt(ms + RMS_EPS) * ng * (gg * _sigmoid(gg))
        return carry

    lax.fori_loop(0, HGRN_TILE // c_rows, chunk, 0)

    @pl.when(t == pl.num_programs(1) - 1)
    def _():
        for h in range(HGRN_HEADS):
            st_ref[0, h] = state_ref[h].T


def _hgrn_prompt(hproj, batch, seq, lb, norm_g):
    nt = seq // HGRN_TILE
    width = HGRN_HEADS * HGRN_DK
    col = lambda j: pl.BlockSpec((HGRN_TILE, width), lambda b, t, j=j: (b * nt + t, j))
    vec = pl.BlockSpec((1, width), lambda b, t: (0, 0))
    return pl.pallas_call(
        _hgrn_prompt_kernel,
        grid=(batch, nt),
        in_specs=[col(0), col(1), col(2), col(3), vec, vec],
        out_specs=[pl.BlockSpec((HGRN_TILE, width), lambda b, t: (b * nt + t, 0)),
                   pl.BlockSpec((1, HGRN_HEADS, HGRN_DK, HGRN_DK), lambda b, t: (b, 0, 0, 0))],
        out_shape=[jax.ShapeDtypeStruct((batch * seq, width), f32),
                   jax.ShapeDtypeStruct((batch, HGRN_HEADS, HGRN_DK, HGRN_DK), f32)],
        scratch_shapes=[pltpu.VMEM((HGRN_HEADS, HGRN_DK, HGRN_DK), f32)],
        compiler_params=_params(("parallel", "arbitrary")),
        name="hgrn_prompt",
    )(hproj, hproj, hproj, hproj, lb.reshape(1, width), norm_g.reshape(1, width))


def _hgrn_sample_kernel(qc_ref, fzc_ref, lbc_ref, i_ref, g_ref, ng_ref, st_ref, o_ref, nst_ref):
    for h in range(HGRN_HEADS):
        lb = lbc_ref[h]
        f = lb + (1.0 - lb) * _sigmoid(fzc_ref[0, h])
        st = f * st_ref[0, h] + (1.0 - f) * i_ref[0, h]
        nst_ref[0, h] = st
        o = jnp.sum(qc_ref[0, h] * st, axis=0, keepdims=True)
        ms = jnp.mean(o * o, axis=1, keepdims=True)
        gg = g_ref[0, h]
        o_ref[0, h] = o * lax.rsqrt(ms + RMS_EPS) * ng_ref[h] * (gg * _sigmoid(gg))


def _hgrn_sample(hs, state, lb, norm_g):
    db = hs.shape[0]
    width = HGRN_HEADS * HGRN_DK
    colv = lambda x: x.reshape(db, HGRN_HEADS, HGRN_DK, 1)
    rowv = lambda x: x.reshape(db, HGRN_HEADS, 1, HGRN_DK)
    cspec = pl.BlockSpec((1, HGRN_HEADS, HGRN_DK, 1), lambda b: (b, 0, 0, 0))
    rspec = pl.BlockSpec((1, HGRN_HEADS, 1, HGRN_DK), lambda b: (b, 0, 0, 0))
    sspec = pl.BlockSpec((1, HGRN_HEADS, HGRN_DK, HGRN_DK), lambda b: (b, 0, 0, 0))
    o, nst = pl.pallas_call(
        _hgrn_sample_kernel,
        grid=(db,),
        in_specs=[cspec, cspec, pl.BlockSpec((HGRN_HEADS, HGRN_DK, 1), lambda b: (0, 0, 0)), rspec, rspec,
                  pl.BlockSpec((HGRN_HEADS, 1, HGRN_DK), lambda b: (0, 0, 0)), sspec],
        out_specs=[rspec, sspec],
        out_shape=[jax.ShapeDtypeStruct((db, HGRN_HEADS, 1, HGRN_DK), f32),
                   jax.ShapeDtypeStruct((db, HGRN_HEADS, HGRN_DK, HGRN_DK), f32)],
        compiler_params=_params(("parallel",)),
        name="hgrn_sample",
    )(colv(hs[:, :width]), colv(hs[:, width:2 * width]), lb.reshape(HGRN_HEADS, HGRN_DK, 1),
      rowv(hs[:, 2 * width:3 * width]), rowv(hs[:, 3 * width:]), norm_g.reshape(HGRN_HEADS, 1, HGRN_DK), state)
    return o.reshape(db, width), nst


PEER_EBLK = 1024
PEER_RANKS = PEER_TOPK + 1
PEER_VROWS = 24
PEER_SUB = SUBLANES
PEER_ROWS = 16


def _extract_max(tiles):
    mx = functools.reduce(jnp.maximum, tiles)
    mx = jnp.max(mx, axis=0, keepdims=True)
    return mx, [jnp.where(x == mx, FLOOR, x) for x in tiles]


def _peer_kernel(x_ref, q_ref, k1_ref, k2_ref, u_ref, v_ref, g_ref, b_ref, o_ref,
                 xt_ref, acc_ref, act_ref, p_ref, a_ref, bw_ref, c_ref, s2_ref, v1_ref, v2_ref, ab_ref, cb_ref):
    e = pl.program_id(1)
    tt = x_ref.shape[0]
    n_lane_chunks = tt // LANES
    sub = PEER_SUB
    n_sub = PEER_NKEYS // sub
    keys_per_step = PEER_EBLK // PEER_NKEYS

    @pl.when(e == 0)
    def _():
        xt_ref[...] = x_ref[...].T.astype(bf16)
        acc_ref[...] = jnp.zeros(acc_ref.shape, f32)
        floor_rows = jnp.full((PEER_VROWS, LANES), FLOOR, f32)
        for h in range(PEER_HEADS):
            q1 = q_ref[:, (2 * h) * PEER_DHALF:(2 * h + 1) * PEER_DHALF]
            q2 = q_ref[:, (2 * h + 1) * PEER_DHALF:(2 * h + 2) * PEER_DHALF]
            s1 = lax.dot_general(k1_ref[h], q1, NT_DIMS, precision=HI, preferred_element_type=f32)
            c_ref[h] = s1.reshape(n_sub, sub, tt)
            s2_ref[h] = lax.dot_general(k2_ref[h], q2, NT_DIMS, precision=HI, preferred_element_type=f32)

        def per_head(h, carry):
            for lc in range(n_lane_chunks):
                lanes = slice(lc * LANES, (lc + 1) * LANES)
                for side, vals_ref in enumerate((v1_ref, v2_ref)):
                    vals_ref[...] = floor_rows
                    if side == 0:
                        tiles = [c_ref[h, k, :, lanes] for k in range(n_sub)]
                    else:
                        tiles = [s2_ref[h, k * sub:(k + 1) * sub, lanes] for k in range(n_sub)]
                    for r in range(PEER_RANKS):
                        mx, tiles = _extract_max(tiles)
                        vals_ref[r:r + 1, :] = mx
                cands = [v1_ref[0:1, :] + v2_ref[r:r + sub, :] for r in range(0, PEER_VROWS, sub)]
                cands += [v1_ref[r:r + 1, :] + v2_ref[0:sub, :] for r in range(1, sub)]
                cands += [v1_ref[r:r + sub, :] + v2_ref[0:1, :] for r in range(sub, PEER_VROWS, sub)]
                best = v1_ref[0:1, :] + v2_ref[0:1, :]
                zsum = jnp.zeros((1, LANES), f32)
                kth = best
                for r in range(PEER_TOPK):
                    kth, cands = _extract_max(cands)
                    zsum = zsum + jnp.exp(kth - best)
                nxt, _ = _extract_max(cands)
                thresh = 0.5 * (kth + nxt)
                s1 = c_ref[h, :, :, lanes]
                a_ref[h, :, :, lanes] = jnp.exp(s1 - v1_ref[0:1, :]) / zsum
                bw_ref[h, :, lanes] = jnp.exp(s2_ref[h, :, lanes] - v2_ref[0:1, :])
                c_ref[h, :, :, lanes] = thresh - s1
            return carry

        lax.fori_loop(0, PEER_HEADS, per_head, 0)

    act_ref[...] = jnp.dot(u_ref[0], xt_ref[...], preferred_element_type=f32)
    for h in range(PEER_HEADS):
        for ii in range(keys_per_step):
            ab_ref[ii, h] = jnp.broadcast_to(a_ref[h, e, ii:ii + 1, :], (sub, tt))
            cb_ref[ii, h] = jnp.broadcast_to(c_ref[h, e, ii:ii + 1, :], (sub, tt))

    def per_second_keys(k, carry):
        r0 = pl.multiple_of(k * PEER_ROWS, PEER_ROWS)
        for lc in range(n_lane_chunks):
            lanes = slice(lc * LANES, (lc + 1) * LANES)
            w = [None] * keys_per_step
            for h in range(PEER_HEADS):
                s2 = s2_ref[h, pl.ds(r0, PEER_ROWS), lanes].reshape(PEER_ROWS // sub, sub, LANES)
                bw = bw_ref[h, pl.ds(r0, PEER_ROWS), lanes].reshape(PEER_ROWS // sub, sub, LANES)
                for ii in range(keys_per_step):
                    hit = s2 >= cb_ref[ii, h, :, lanes][None]
                    term = ab_ref[ii, h, :, lanes][None] * jnp.where(hit, bw, 0.0)
                    w[ii] = term if h == 0 else w[ii] + term
            for ii in range(keys_per_step):
                rows = pl.ds(pl.multiple_of(ii * PEER_NKEYS + r0, PEER_ROWS), PEER_ROWS)
                act = act_ref[rows, lanes]
                gelu = 0.5 * act * (1.0 + lax.erf(act * (2.0 ** -0.5)))
                p_ref[rows, lanes] = (w[ii].reshape(PEER_ROWS, LANES) * gelu).astype(bf16)
        return carry

    lax.fori_loop(0, PEER_NKEYS // PEER_ROWS, per_second_keys, 0)
    acc_ref[...] += lax.dot_general(p_ref[...], v_ref[0], TN_DIMS, preferred_element_type=f32)

    @pl.when(e == pl.num_programs(1) - 1)
    def _():
        o_ref[...] = _layer_norm(DN_ALPHA * x_ref[...] + acc_ref[...], g_ref[...], b_ref[...])


def _peer_ln(x, wq, keys, u, v, layer, g, b, *, tt=TOKEN_TILE):
    m, d = x.shape
    n_exp = u.shape[1]
    q = _mm(x, wq, tm=tt)
    nq = q.shape[1]
    return pl.pallas_call(
        _peer_kernel,
        grid=(m // tt, n_exp // PEER_EBLK),
        in_specs=[pl.BlockSpec((tt, d), lambda t, e: (t, 0)),
                  pl.BlockSpec((tt, nq), lambda t, e: (t, 0)),
                  pl.BlockSpec((PEER_HEADS, PEER_NKEYS, PEER_DHALF), lambda t, e: (0, 0, 0)),
                  pl.BlockSpec((PEER_HEADS, PEER_NKEYS, PEER_DHALF), lambda t, e: (0, 0, 0)),
                  pl.BlockSpec((1, PEER_EBLK, d), lambda t, e: (layer, e, 0)),
                  pl.BlockSpec((1, PEER_EBLK, d), lambda t, e: (layer, e, 0)),
                  pl.BlockSpec((1, d), lambda t, e: (0, 0)),
                  pl.BlockSpec((1, d), lambda t, e: (0, 0))],
        out_specs=pl.BlockSpec((tt, d), lambda t, e: (t, 0)),
        out_shape=jax.ShapeDtypeStruct((m, d), f32),
        scratch_shapes=[pltpu.VMEM((d, tt), bf16), pltpu.VMEM((tt, d), f32),
                        pltpu.VMEM((PEER_EBLK, tt), f32), pltpu.VMEM((PEER_EBLK, tt), bf16)]
                       + [pltpu.VMEM((PEER_HEADS, PEER_NKEYS // PEER_SUB, PEER_SUB, tt), f32),
                          pltpu.VMEM((PEER_HEADS, PEER_NKEYS, tt), f32),
                          pltpu.VMEM((PEER_HEADS, PEER_NKEYS // PEER_SUB, PEER_SUB, tt), f32),
                          pltpu.VMEM((PEER_HEADS, PEER_NKEYS, tt), f32)]
                       + [pltpu.VMEM((PEER_VROWS, LANES), f32)] * 2
                       + [pltpu.VMEM((PEER_EBLK // PEER_NKEYS, PEER_HEADS, PEER_SUB, tt), f32)] * 2,
        compiler_params=_params(("parallel", "arbitrary")),
        name="peer",
    )(x, q, keys[0], keys[1], u, v, g.reshape(1, d), b.reshape(1, d))


def kernel(x_prompt, x_sample, cache_k, cache_v, state_conv, state_hgrn, page_table, w_in_even, conv_w, conv_b, conv_ln_g, conv_ln_b, w_out_even, w_in_odd, hgrn_lb_logits, hgrn_norm_g, w_out_odd, ln_g, ln_b, peer_wq, peer_keys, peer_u, peer_v):
    batch, seq, d = x_prompt.shape
    db = x_sample.shape[0]
    n_prompt = batch * seq
    pad_sample = lambda a: jnp.pad(a, ((0, SAMPLE_TILE - a.shape[0]), (0, 0)))
    pages = lambda a: a.reshape(batch, seq // PAGE_SIZE, MOBA_HEADS, MOBA_HEAD_DIM, PAGE_SIZE).transpose(0, 1, 4, 2, 3)
    heads = lambda a: a.reshape(db, MOBA_HEADS, MOBA_HEAD_DIM)

    lb_p = jax.nn.softmax(hgrn_lb_logits.astype(f32), axis=0)
    lb_all = jnp.cumsum(lb_p, axis=0) - lb_p[0]

    peer_u16, peer_v16 = peer_u.astype(bf16), peer_v.astype(bf16)
    xp = x_prompt.reshape(n_prompt, d)
    xs = pad_sample(x_sample.reshape(db, d))
    kp_l, vp_l, ks_l, vs_l, cp_l, cs_l, hp_l, hs_l = [], [], [], [], [], [], [], []
    for l in range(DEPTH):
        if l % 2 == 0:
            e = l // 2
            w_qk, w_vag = w_in_even[e][:, :2 * MOBA_WIDTH], w_in_even[e][:, 2 * MOBA_WIDTH:]
            qk_p, k_pages = _mm(xp, w_qk, split=True, paged_chunk=1)
            vag_p, v_pages = _mm(xp, w_vag, paged_chunk=0)
            qk_s, vag_s = _mm(xs, w_qk, split=True, tm=SAMPLE_TILE)[:db], _mm(xs, w_vag, tm=SAMPLE_TILE)[:db]
            kp_l.append(pages(k_pages))
            vp_l.append(pages(v_pages))
            q_s, k_s, v_s = qk_s[:, :MOBA_WIDTH], qk_s[:, MOBA_WIDTH:], vag_s[:, :MOBA_WIDTH]
            ks_l.append(k_s.reshape(db, 1, MOBA_HEADS, MOBA_HEAD_DIM))
            vs_l.append(v_s.reshape(db, 1, MOBA_HEADS, MOBA_HEAD_DIM))
            att_p = _moba_prompt(qk_p, vag_p, batch, seq)
            att_s = _moba_sample(heads(q_s), heads(k_s), heads(v_s),
                                 cache_k[e].transpose(0, 2, 3, 1), cache_v[e].transpose(0, 2, 3, 1), page_table)
            att_s = att_s[:, :, :MOBA_HEAD_DIM].reshape(db, MOBA_WIDTH)
            cprm = (conv_w[e], conv_b[e], conv_ln_g[e], conv_ln_b[e])
            cy_p, buf_p = _conv_prompt(vag_p, batch, seq, *cprm)
            cy_s, buf_s = _conv_sample(vag_s[:, MOBA_WIDTH:MOBA_WIDTH + CONV_CH], vag_s[:, MOBA_WIDTH + CONV_CH:],
                                       state_conv[e], *cprm)
            cp_l.append(buf_p)
            cs_l.append(buf_s)
            mix_p, mix_s = [att_p, cy_p], [pad_sample(att_s), pad_sample(cy_s)]
            w_out = w_out_even[e]
        else:
            oi = l // 2
            hproj_p = _mm(xp, w_in_odd[oi], tm=WIDE_OUT_TILE)
            hproj_s = _mm(xs, w_in_odd[oi], tm=SAMPLE_TILE)[:db]
            o_p, st_p = _hgrn_prompt(hproj_p, batch, seq, lb_all[l], hgrn_norm_g[oi])
            o_s, st_s = _hgrn_sample(hproj_s, state_hgrn[oi].astype(f32), lb_all[l], hgrn_norm_g[oi])
            hp_l.append(st_p)
            hs_l.append(st_s)
            mix_p, mix_s = [o_p], [pad_sample(o_s)]
            w_out = w_out_odd[oi]
        xp = _mm_res_ln(mix_p, w_out, xp, ln_g[l, 0], ln_b[l, 0])
        xs = _mm_res_ln(mix_s, w_out, xs, ln_g[l, 0], ln_b[l, 0], tm=SAMPLE_TILE)
        prm = (peer_wq[l], peer_keys[l], peer_u16, peer_v16, l, ln_g[l, 1], ln_b[l, 1])
        xp = _peer_ln(xp, *prm)
        xs = _peer_ln(xs, *prm, tt=SAMPLE_TILE)
    y_prompt = xp.reshape(batch, seq, d)
    y_sample = xs[:db].reshape(db, 1, d)
    return (y_prompt, y_sample, jnp.stack(kp_l), jnp.stack(vp_l), jnp.stack(ks_l), jnp.stack(vs_l),
            jnp.stack(cp_l), jnp.stack(cs_l), jnp.stack(hp_l), jnp.stack(hs_l))
```

```python
import functools

import jax
import jax.numpy as jnp
from jax import lax
from jax.experimental import pallas as pl
from jax.experimental.pallas import tpu as pltpu

f32 = jnp.float32
bf16 = jnp.bfloat16
i32 = jnp.int32
HI = lax.Precision.HIGHEST

D_MODEL = 1024
DEPTH = 2
PAGE_SIZE = 128
MOBA_HEADS = 8
MOBA_HEAD_DIM = 64
MOBA_WIDTH = MOBA_HEADS * MOBA_HEAD_DIM
MOBA_BLOCK = 256
MOBA_TOPK = 3
MOBA_SCALE = MOBA_HEAD_DIM ** -0.5
CONV_CH = D_MODEL // 2
CONV_WIDTH = 31
HGRN_HEADS = 8
HGRN_DK = D_MODEL // HGRN_HEADS
HGRN_CHUNK = 128
HGRN_SUB = 16
PEER_HEADS = 8
PEER_NKEYS = 128
PEER_DHALF = 128
PEER_TOPK = 16
LN_EPS = 1e-5
RMS_EPS = 1e-6
DN_ALPHA = (2 * DEPTH) ** 0.25

LANES = 128
TOKEN_TILE = 512
SAMPLE_TILE = 128
WIDE_OUT_TILE = 256
MASK_NEG = -1e30
FLOOR = -3e38
EXP_CLAMP = 60.0
VMEM_LIMIT = 56 << 20

NT_DIMS = (((1,), (1,)), ((), ()))
TN_DIMS = (((0,), (0,)), ((), ()))


def _params(semantics, vmem=VMEM_LIMIT):
    return pltpu.CompilerParams(dimension_semantics=semantics, vmem_limit_bytes=vmem)


def _layer_norm(y, g, b):
    mu = jnp.mean(y, axis=-1, keepdims=True)
    d = y - mu
    var = jnp.mean(d * d, axis=-1, keepdims=True)
    return d * lax.rsqrt(var + LN_EPS) * g + b


def _sigmoid(x):
    return 1.0 / (1.0 + jnp.exp(-x))


MM_CHUNK = 512


def _mm_kernel(x_ref, *refs, split, paged_chunk):
    refs = list(refs)
    t_ref = refs.pop() if paged_chunk is not None else None
    o_ref = refs.pop()
    wh_ref = refs[0]
    x = x_ref[...]
    xh = x.astype(bf16)
    if split:
        xl = (x - xh.astype(f32)).astype(bf16)
    for c, j in enumerate(range(0, o_ref.shape[1], MM_CHUNK)):
        wh = wh_ref[:, j:j + MM_CHUNK]
        acc = jnp.dot(xh, wh, preferred_element_type=f32)
        if split:
            acc = acc + jnp.dot(xl, wh, preferred_element_type=f32)
            acc = acc + jnp.dot(xh, refs[1][:, j:j + MM_CHUNK], preferred_element_type=f32)
        o_ref[:, j:j + MM_CHUNK] = acc
        if c == paged_chunk:
            for pg in range(o_ref.shape[0] // PAGE_SIZE):
                t_ref[pg] = acc[pg * PAGE_SIZE:(pg + 1) * PAGE_SIZE, :].T


def _mm(x, w, *, split=False, tm=TOKEN_TILE, paged_chunk=None):
    m, k = x.shape
    n = w.shape[1]
    wh = w.astype(bf16)
    ws = [wh] + ([(w - wh.astype(f32)).astype(bf16)] if split else [])
    out_specs = [pl.BlockSpec((tm, n), lambda i: (i, 0))]
    out_shape = [jax.ShapeDtypeStruct((m, n), f32)]
    if paged_chunk is not None:
        out_specs.append(pl.BlockSpec((tm // PAGE_SIZE, MM_CHUNK, PAGE_SIZE), lambda i: (i, 0, 0)))
        out_shape.append(jax.ShapeDtypeStruct((m // PAGE_SIZE, MM_CHUNK, PAGE_SIZE), f32))
    outs = pl.pallas_call(
        functools.partial(_mm_kernel, split=split, paged_chunk=paged_chunk),
        grid=(m // tm,),
        in_specs=[pl.BlockSpec((tm, k), lambda i: (i, 0))] + [pl.BlockSpec((k, n), lambda i: (0, 0))] * len(ws),
        out_specs=out_specs,
        out_shape=out_shape,
        compiler_params=_params(("parallel",)),
        name="proj_split" if split else "proj",
    )(x, *ws)
    return outs[0] if paged_chunk is None else outs


def _mm_res_ln_kernel(*refs):
    *a_refs, w_ref, x_ref, g_ref, b_ref, o_ref = refs
    acc = DN_ALPHA * x_ref[...]
    row = 0
    for a_ref in a_refs:
        k = a_ref.shape[1]
        acc = acc + jnp.dot(a_ref[...].astype(bf16), w_ref[row:row + k, :], preferred_element_type=f32)
        row += k
    o_ref[...] = _layer_norm(acc, g_ref[...], b_ref[...])


def _mm_res_ln(parts, w, x, g, b, *, tm=TOKEN_TILE):
    m, n = x.shape
    k = w.shape[0]
    return pl.pallas_call(
        _mm_res_ln_kernel,
        grid=(m // tm,),
        in_specs=[pl.BlockSpec((tm, a.shape[1]), lambda i: (i, 0)) for a in parts]
                 + [pl.BlockSpec((k, n), lambda i: (0, 0)),
                    pl.BlockSpec((tm, n), lambda i: (i, 0)), pl.BlockSpec((1, n), lambda i: (0, 0)),
                    pl.BlockSpec((1, n), lambda i: (0, 0))],
        out_specs=pl.BlockSpec((tm, n), lambda i: (i, 0)),
        out_shape=jax.ShapeDtypeStruct((m, n), f32),
        compiler_params=_params(("parallel",)),
        name="out_proj_ln",
    )(*parts, w.astype(bf16), x, g.reshape(1, n), b.reshape(1, n))


CONV_TILE = 256
CONV_ROWS = 64
CONV_HIST = 32
SUBLANES = 8
CONV_SHIFT_ROWS = CONV_TILE + CONV_HIST - SUBLANES


def _conv_prompt_kernel(a_ref, gt_ref, w_ref, cb_ref, g_ref, b_ref, y_ref, st_ref, buf_ref, shift_ref):
    t = pl.program_id(1)

    @pl.when(t == 0)
    def _():
        buf_ref[0:CONV_HIST, :] = jnp.zeros((CONV_HIST, CONV_CH), f32)

    buf_ref[CONV_HIST:CONV_HIST + CONV_TILE, :] = a_ref[...] * _sigmoid(gt_ref[...])
    for q in range(1, SUBLANES):
        shift_ref[q - 1] = buf_ref[q:q + CONV_SHIFT_ROWS, :]
    first = CONV_HIST - (CONV_WIDTH - 1)
    for r in range(0, CONV_TILE, CONV_ROWS):
        acc = jnp.zeros((CONV_ROWS, CONV_CH), f32) + cb_ref[...]
        for j in range(CONV_WIDTH):
            q = (first + j) % SUBLANES
            base = r + first + j - q
            rows = buf_ref[base:base + CONV_ROWS, :] if q == 0 else shift_ref[q - 1, base:base + CONV_ROWS, :]
            acc = acc + w_ref[j:j + 1, :] * rows
        y = _layer_norm(acc, g_ref[...], b_ref[...])
        y_ref[r:r + CONV_ROWS, :] = y * _sigmoid(y)
    tail = buf_ref[CONV_TILE:CONV_TILE + CONV_HIST, :]
    buf_ref[0:CONV_HIST, :] = tail

    @pl.when(t == pl.num_programs(1) - 1)
    def _():
        st_ref[0] = tail


def _conv_prompt(vag, batch, seq, w, cb, g, b):
    nt = seq // CONV_TILE
    wpad = jnp.zeros((CONV_HIST, CONV_CH), f32).at[:CONV_WIDTH].set(w)
    row = lambda v: v.reshape(1, CONV_CH)
    y, st = pl.pallas_call(
        _conv_prompt_kernel,
        grid=(batch, nt),
        in_specs=[pl.BlockSpec((CONV_TILE, CONV_CH), lambda bi, t: (bi * nt + t, 1)),
                  pl.BlockSpec((CONV_TILE, CONV_CH), lambda bi, t: (bi * nt + t, 2)),
                  pl.BlockSpec((CONV_HIST, CONV_CH), lambda bi, t: (0, 0))]
                 + [pl.BlockSpec((1, CONV_CH), lambda bi, t: (0, 0))] * 3,
        out_specs=[pl.BlockSpec((CONV_TILE, CONV_CH), lambda bi, t: (bi * nt + t, 0)),
                   pl.BlockSpec((1, CONV_HIST, CONV_CH), lambda bi, t: (bi, 0, 0))],
        out_shape=[jax.ShapeDtypeStruct((batch * seq, CONV_CH), f32),
                   jax.ShapeDtypeStruct((batch, CONV_HIST, CONV_CH), f32)],
        scratch_shapes=[pltpu.VMEM((CONV_TILE + CONV_HIST, CONV_CH), f32),
                        pltpu.VMEM((SUBLANES - 1, CONV_SHIFT_ROWS, CONV_CH), f32)],
        compiler_params=_params(("arbitrary", "arbitrary")),
        name="conv_prompt",
    )(vag, vag, wpad, row(cb), row(g), row(b))
    return y, st[:, CONV_HIST - (CONV_WIDTH - 1):]


def _conv_sample_kernel(a_ref, gt_ref, st_ref, w_ref, cb_ref, g_ref, b_ref, y_ref, nst_ref):
    u = a_ref[...] * _sigmoid(gt_ref[...])
    acc = cb_ref[...] + w_ref[CONV_WIDTH - 1:CONV_WIDTH, :] * u
    for j in range(CONV_WIDTH - 1):
        acc = acc + w_ref[j:j + 1, :] * st_ref[j]
    y = _layer_norm(acc, g_ref[...], b_ref[...])
    y_ref[...] = y * _sigmoid(y)
    for j in range(CONV_WIDTH - 2):
        nst_ref[j] = st_ref[j + 1]
    nst_ref[CONV_WIDTH - 2] = u


def _conv_sample(a, gt, state, w, cb, g, b):
    db = a.shape[0]
    wpad = jnp.zeros((CONV_HIST, CONV_CH), f32).at[:CONV_WIDTH].set(w)
    row = lambda v: v.reshape(1, CONV_CH)
    y, nst = pl.pallas_call(
        _conv_sample_kernel,
        out_shape=[jax.ShapeDtypeStruct((db, CONV_CH), f32),
                   jax.ShapeDtypeStruct((CONV_WIDTH - 1, db, CONV_CH), f32)],
        name="conv_sample",
    )(a, gt, state.transpose(1, 0, 2), wpad, row(cb), row(g), row(b))
    return y, nst.transpose(1, 0, 2)


GATE_ROWS = 1024
ATTN_KEYS = 2 * MOBA_BLOCK


def _block_sum_kernel(k_ref, o_ref):
    o_ref[0] = jnp.sum(k_ref[...], axis=0, keepdims=True)


def _block_sums(qk, batch, seq):
    nb = seq // MOBA_BLOCK
    return pl.pallas_call(
        _block_sum_kernel,
        grid=(batch * nb,),
        in_specs=[pl.BlockSpec((MOBA_BLOCK, MOBA_WIDTH), lambda i: (i, 1))],
        out_specs=pl.BlockSpec((1, 1, MOBA_WIDTH), lambda i: (i, 0, 0)),
        out_shape=jax.ShapeDtypeStruct((batch * nb, 1, MOBA_WIDTH), f32),
        compiler_params=_params(("parallel",)),
        name="moba_block_sums",
    )(qk)


def _moba_gate_kernel(q_ref, k_ref, v_ref, ks_ref, qa_ref, ka_ref, va_ref, *, nb):
    t = pl.program_id(2)
    lane = lax.broadcasted_iota(i32, (MOBA_BLOCK, LANES), 1)
    blk = lane - MOBA_HEAD_DIM
    km_lane = lax.broadcasted_iota(i32, (nb, LANES), 1)
    ksum = ks_ref[0] * (1.0 / MOBA_BLOCK)
    pad_top = jnp.zeros((MOBA_HEAD_DIM, LANES), f32)
    pad_bot = jnp.zeros((LANES - MOBA_HEAD_DIM - nb, LANES), f32)
    for c in range(GATE_ROWS // MOBA_BLOCK):
        own = t * (GATE_ROWS // MOBA_BLOCK) + c
        rows = slice(c * MOBA_BLOCK, (c + 1) * MOBA_BLOCK)
        q2 = q_ref[rows, :]
        k2 = k_ref[rows, :]
        v2 = v_ref[rows, :]
        cand = (blk >= 0) & (blk < own)
        for j in range(2):
            head = (km_lane >= j * MOBA_HEAD_DIM) & (km_lane < (j + 1) * MOBA_HEAD_DIM)
            km = jnp.concatenate([pad_top, jnp.where(head, ksum, 0.0), pad_bot], axis=0)
            gate = lax.dot_general(q2, km, NT_DIMS, precision=HI, preferred_element_type=f32)
            masked = jnp.where(cand, gate, FLOOR)
            sel = jnp.zeros((MOBA_BLOCK, LANES), f32)
            for _ in range(MOBA_TOPK):
                mx = jnp.max(masked, axis=1, keepdims=True)
                hit = (masked == mx) & cand
                sel = jnp.where(hit, 1.0, sel)
                masked = jnp.where(hit, FLOOR, masked)
            bias = jnp.where((sel > 0.0) | (blk == own), 0.0, MASK_NEG)
            qj = q2 if j == 0 else pltpu.roll(q2, MOBA_HEAD_DIM, 1)
            kj = k2 if j == 0 else pltpu.roll(k2, MOBA_HEAD_DIM, 1)
            vj = v2 if j == 0 else pltpu.roll(v2, MOBA_HEAD_DIM, 1)
            in_head = lane < MOBA_HEAD_DIM
            in_bias = lane < MOBA_HEAD_DIM + nb
            qa = jnp.where(in_head, qj * MOBA_SCALE, jnp.where(in_bias, bias, 0.0))
            ka = jnp.where(in_head, kj, jnp.where(blk == own, 1.0, 0.0))
            va = jnp.where(in_head, vj, jnp.where(blk == 0, 1.0, 0.0))
            qa_ref[0, j, rows, :] = qa.astype(bf16)
            ka_ref[0, j, rows, :] = ka.astype(bf16)
            va_ref[0, j, rows, :] = va.astype(bf16)


def _moba_attn_kernel(qa_ref, ka_ref, va_ref, o_ref):
    i = pl.program_id(1)
    last = i // 2
    row = lax.broadcasted_iota(i32, (MOBA_BLOCK, ATTN_KEYS), 0)
    col = lax.broadcasted_iota(i32, (MOBA_BLOCK, ATTN_KEYS), 1)
    lane = lax.broadcasted_iota(i32, (MOBA_BLOCK, LANES), 1)

    def tile(h, k0, mask):
        s = lax.dot_general(qa_ref[0, h], ka_ref[0, h, pl.ds(k0, ATTN_KEYS), :], NT_DIMS, preferred_element_type=f32)
        return s if mask is None else jnp.where(mask, s, MASK_NEG)

    def weighted(p, h, k0):
        return jnp.dot(p.astype(bf16), va_ref[0, h, pl.ds(k0, ATTN_KEYS), :], preferred_element_type=f32)

    k_last = pl.multiple_of(last * ATTN_KEYS, ATTN_KEYS)
    visible = (k_last + col) <= (i * MOBA_BLOCK + row)
    state = []
    for h in range(MOBA_HEADS):
        s = tile(h, k_last, visible)
        m = jnp.max(s, axis=1, keepdims=True)
        state += [m, weighted(jnp.exp(s - m), h, k_last)]

    def body(n, carry):
        k0 = pl.multiple_of(n * ATTN_KEYS, ATTN_KEYS)
        out = []
        for h in range(MOBA_HEADS):
            m, acc = carry[2 * h:2 * h + 2]
            s = tile(h, k0, None)
            mn = jnp.maximum(m, jnp.max(s, axis=1, keepdims=True))
            out += [mn, jnp.exp(m - mn) * acc + weighted(jnp.exp(s - mn), h, k0)]
        return tuple(out)

    final = lax.fori_loop(0, last, body, tuple(state))
    for pair in range(MOBA_HEADS // 2):
        acc0, acc1 = final[4 * pair + 1], final[4 * pair + 3]
        out0 = acc0 / acc0[:, MOBA_HEAD_DIM:MOBA_HEAD_DIM + 1]
        out1 = acc1 / acc1[:, MOBA_HEAD_DIM:MOBA_HEAD_DIM + 1]
        o_ref[:, pair * LANES:(pair + 1) * LANES] = jnp.where(lane < MOBA_HEAD_DIM, out0,
                                                              pltpu.roll(out1, MOBA_HEAD_DIM, 1))


def _moba_prompt(qk, vag, batch, seq):
    nb = seq // MOBA_BLOCK
    npair = MOBA_WIDTH // LANES
    nt = seq // GATE_ROWS
    ksums = _block_sums(qk, batch, seq).reshape(batch, nb, MOBA_WIDTH)
    per_head = pl.BlockSpec((1, 2, GATE_ROWS, LANES), lambda b, p, t: (b, p, t, 0))
    qa, ka, va = pl.pallas_call(
        functools.partial(_moba_gate_kernel, nb=nb),
        grid=(batch, npair, nt),
        in_specs=[pl.BlockSpec((GATE_ROWS, LANES), lambda b, p, t: (b * nt + t, p)),
                  pl.BlockSpec((GATE_ROWS, LANES), lambda b, p, t: (b * nt + t, npair + p)),
                  pl.BlockSpec((GATE_ROWS, LANES), lambda b, p, t: (b * nt + t, p)),
                  pl.BlockSpec((1, nb, LANES), lambda b, p, t: (b, 0, p))],
        out_specs=[per_head] * 3,
        out_shape=[jax.ShapeDtypeStruct((batch, MOBA_HEADS, seq, LANES), bf16)] * 3,
        compiler_params=_params(("parallel", "parallel", "parallel")),
        name="moba_gate",
    )(qk, qk, vag, ksums)
    return pl.pallas_call(
        _moba_attn_kernel,
        grid=(batch, nb),
        in_specs=[pl.BlockSpec((1, MOBA_HEADS, MOBA_BLOCK, LANES), lambda b, i: (b, 0, i, 0)),
                  pl.BlockSpec((1, MOBA_HEADS, seq, LANES), lambda b, i: (b, 0, 0, 0),
                               pipeline_mode=pl.Buffered(1)),
                  pl.BlockSpec((1, MOBA_HEADS, seq, LANES), lambda b, i: (b, 0, 0, 0),
                               pipeline_mode=pl.Buffered(1))],
        out_specs=pl.BlockSpec((MOBA_BLOCK, MOBA_WIDTH), lambda b, i: (b * nb + i, 0)),
        out_shape=jax.ShapeDtypeStruct((batch * seq, MOBA_WIDTH), f32),
        compiler_params=_params(("parallel", "arbitrary")),
        name="moba_attn",
    )(qa, ka, va)


PAGES_PER_STEP = 8
PAGES_PER_BLOCK = MOBA_BLOCK // PAGE_SIZE
BLOCKS_PER_STEP = PAGES_PER_STEP // PAGES_PER_BLOCK


def _moba_sample_stream_kernel(pt_ref, qb_ref, *refs):
    del pt_ref
    k_refs = refs[:PAGES_PER_STEP]
    v_refs = refs[PAGES_PER_STEP:2 * PAGES_PER_STEP]
    gate_ref, m_ref, l_ref, acc_ref = refs[2 * PAGES_PER_STEP:]
    qb = qb_ref[0]
    head = lax.broadcasted_iota(i32, (MOBA_HEADS, LANES), 0)
    tile = (MOBA_HEADS, LANES)
    for jj in range(BLOCKS_PER_STEP):
        pages = range(PAGES_PER_BLOCK * jj, PAGES_PER_BLOCK * (jj + 1))
        raw = [jnp.sum(k_refs[x][0] * qb, axis=1) for x in pages]
        gate = functools.reduce(jnp.add, [jnp.sum(r, axis=1, keepdims=True) for r in raw]) * (1.0 / MOBA_BLOCK)
        m = functools.reduce(jnp.maximum, [jnp.max(r, axis=1, keepdims=True) for r in raw]) * MOBA_SCALE
        l = jnp.zeros((MOBA_HEADS, 1), f32)
        pv = jnp.zeros((MOBA_HEADS, MOBA_WIDTH), f32)
        for r, x in zip(raw, pages):
            p = jnp.exp(r * MOBA_SCALE - m)
            l = l + jnp.sum(p, axis=1, keepdims=True)
            vt = v_refs[x][0].reshape(MOBA_WIDTH, PAGE_SIZE).astype(bf16)
            pv = pv + lax.dot_general(p.astype(bf16), vt, NT_DIMS, preferred_element_type=f32)
        acc = jnp.zeros(tile, f32)
        for pair in range(MOBA_WIDTH // LANES):
            both = pv[:, pair * LANES:(pair + 1) * LANES]
            acc = acc + jnp.where(head == 2 * pair, both, 0.0)
            acc = acc + jnp.where(head == 2 * pair + 1, pltpu.roll(both, MOBA_HEAD_DIM, 1), 0.0)
        gate_ref[0, 0, jj] = jnp.broadcast_to(gate, tile)
        m_ref[0, 0, jj] = jnp.broadcast_to(m, tile)
        l_ref[0, 0, jj] = jnp.broadcast_to(l, tile)
        acc_ref[0, 0, jj] = acc


def _moba_sample_combine_kernel(gate_ref, m_ref, l_ref, acc_ref, q_ref, kn_ref, vn_ref, o_ref):
    db, nb, nh, width = gate_ref.shape
    masked = gate_ref[...]
    sel = jnp.zeros((db, nb, nh, width), f32)
    for _ in range(MOBA_TOPK):
        mx = jnp.max(masked, axis=1, keepdims=True)
        hit = masked == mx
        sel = jnp.where(hit, 1.0, sel)
        masked = jnp.where(hit, FLOOR, masked)
    chosen = sel > 0.0
    s_self = jnp.sum(q_ref[...] * kn_ref[...], axis=-1, keepdims=True) * MOBA_SCALE
    m = m_ref[...]
    top = jnp.maximum(jnp.max(jnp.where(chosen, m, FLOOR), axis=1), s_self)
    w = jnp.where(chosen, jnp.exp(m - top[:, None]), 0.0)
    w_self = jnp.exp(s_self - top)
    denom = jnp.sum(w * l_ref[...], axis=1) + w_self
    num = jnp.sum(w * acc_ref[...], axis=1) + w_self * vn_ref[...]
    o_ref[...] = num / denom


def _moba_sample(q, kn, vn, k_pool, v_pool, page_table):
    db, n_pages = page_table.shape
    nb = n_pages // PAGES_PER_BLOCK
    ng = n_pages // PAGES_PER_STEP
    page_block = (1, MOBA_HEADS, MOBA_HEAD_DIM, PAGE_SIZE)
    page_spec = lambda x: pl.BlockSpec(page_block, lambda b, g, pt, x=x: (pt[b, g * PAGES_PER_STEP + x], 0, 0, 0))
    stat = pl.BlockSpec((1, 1, BLOCKS_PER_STEP, MOBA_HEADS, LANES), lambda b, g, pt: (b, g, 0, 0, 0))
    stat_shape = jax.ShapeDtypeStruct((db, ng, BLOCKS_PER_STEP, MOBA_HEADS, LANES), f32)
    qb = jnp.broadcast_to(q[..., None], (db,) + page_block[1:])
    stats = pl.pallas_call(
        _moba_sample_stream_kernel,
        grid_spec=pltpu.PrefetchScalarGridSpec(
            num_scalar_prefetch=1,
            grid=(db, ng),
            in_specs=[pl.BlockSpec(page_block, lambda b, g, pt: (b, 0, 0, 0))]
                     + [page_spec(x) for x in range(PAGES_PER_STEP)] * 2,
            out_specs=[stat] * 4),
        out_shape=[stat_shape] * 4,
        compiler_params=_params(("parallel", "parallel")),
        name="moba_sample_stream",
    )(page_table, qb, *([k_pool] * PAGES_PER_STEP), *([v_pool] * PAGES_PER_STEP))
    gate, m, l, acc = (a.reshape(db, nb, MOBA_HEADS, LANES) for a in stats)
    vn_wide = jnp.pad(vn, ((0, 0), (0, 0), (0, LANES - MOBA_HEAD_DIM)))
    return pl.pallas_call(
        _moba_sample_combine_kernel,
        out_shape=jax.ShapeDtypeStruct((db, MOBA_HEADS, LANES), f32),
        compiler_params=pltpu.CompilerParams(vmem_limit_bytes=VMEM_LIMIT),
        name="moba_sample_combine",
    )(gate, m, l, acc, q, kn, vn_wide)


HGRN_TILE = 512


def _hgrn_prompt_kernel(q_ref, fz_ref, i_ref, g_ref, lb_ref, ng_ref, o_ref, st_ref, state_ref):
    t = pl.program_id(1)
    c_rows = HGRN_CHUNK

    @pl.when(t == 0)
    def _():
        state_ref[...] = jnp.zeros(state_ref.shape, f32)

    row = lax.broadcasted_iota(i32, (c_rows, c_rows), 0)
    col = lax.broadcasted_iota(i32, (c_rows, c_rows), 1)
    causal = col <= row
    cum = jnp.where(causal, 1.0, 0.0)
    cum_sub = jnp.where(col < (row // HGRN_SUB) * HGRN_SUB, 1.0, 0.0)
    cum_both16 = jnp.concatenate([cum, cum_sub], axis=0).astype(bf16)

    def chunk(c, carry):
        rows = pl.ds(pl.multiple_of(c * c_rows, c_rows), c_rows)
        lb_all = lb_ref[...]
        f_all = lb_all + (1.0 - lb_all) * _sigmoid(fz_ref[rows, :])
        log_f = jnp.log(f_all)
        both_all = jnp.zeros((2 * c_rows, log_f.shape[1]), f32)
        rest = log_f
        for _ in range(3):
            piece = rest.astype(bf16)
            both_all = both_all + jnp.dot(cum_both16, piece, preferred_element_type=f32)
            rest = rest - piece.astype(f32)
        for h in range(HGRN_HEADS):
            lanes = slice(h * HGRN_DK, (h + 1) * HGRN_DK)
            ng = ng_ref[:, lanes]
            q = q_ref[rows, lanes]
            v = i_ref[rows, lanes]
            gg = g_ref[rows, lanes]
            kk = 1.0 - f_all[:, lanes]
            b = both_all[:c_rows, lanes]
            ref_row = both_all[c_rows:, lanes]
            st = state_ref[h]
            o = lax.dot_general((q * jnp.exp(b)).astype(bf16), st.astype(bf16), NT_DIMS, preferred_element_type=f32)
            qh = (q * jnp.exp(b - ref_row)).astype(bf16)
            parts = []
            for s in range(c_rows // HGRN_SUB):
                ref_s = ref_row[s * HGRN_SUB:s * HGRN_SUB + 1, :]
                kh = (kk * jnp.exp(jnp.minimum(ref_s - b, EXP_CLAMP))).astype(bf16)
                parts.append(lax.dot_general(qh[s * HGRN_SUB:(s + 1) * HGRN_SUB], kh, NT_DIMS,
                                             preferred_element_type=f32))
            att = jnp.where(causal, jnp.concatenate(parts, axis=0), 0.0)
            o = o + jnp.dot(att.astype(bf16), v.astype(bf16), preferred_element_type=f32)
            b_last = b[c_rows - 1:c_rows, :]
            kd = (kk * jnp.exp(b_last - b)).astype(bf16)
            state_ref[h] = st * jnp.exp(b_last) + lax.dot_general(v.astype(bf16), kd, TN_DIMS,
                                                                   preferred_element_type=f32)
            ms = jnp.mean(o * o, axis=1, keepdims=True)
            o_ref[rows, lanes] = o * lax.rsqrt(ms + RMS_EPS) * ng * (gg * _sigmoid(gg))
        return carry

    lax.fori_loop(0, HGRN_TILE // c_rows, chunk, 0)

    @pl.when(t == pl.num_programs(1) - 1)
    def _():
        for h in range(HGRN_HEADS):
            st_ref[0, h] = state_ref[h].T


def _hgrn_prompt(hproj, batch, seq, lb, norm_g):
    nt = seq // HGRN_TILE
    width = HGRN_HEADS * HGRN_DK
    col = lambda j: pl.BlockSpec((HGRN_TILE, width), lambda b, t, j=j: (b * nt + t, j))
    vec = pl.BlockSpec((1, width), lambda b, t: (0, 0))
    return pl.pallas_call(
        _hgrn_prompt_kernel,
        grid=(batch, nt),
        in_specs=[col(0), col(1), col(2), col(3), vec, vec],
        out_specs=[pl.BlockSpec((HGRN_TILE, width), lambda b, t: (b * nt + t, 0)),
                   pl.BlockSpec((1, HGRN_HEADS, HGRN_DK, HGRN_DK), lambda b, t: (b, 0, 0, 0))],
        out_shape=[jax.ShapeDtypeStruct((batch * seq, width), f32),
                   jax.ShapeDtypeStruct((batch, HGRN_HEADS, HGRN_DK, HGRN_DK), f32)],
        scratch_shapes=[pltpu.VMEM((HGRN_HEADS, HGRN_DK, HGRN_DK), f32)],
        compiler_params=_params(("parallel", "arbitrary")),
        name="hgrn_prompt",
    )(hproj, hproj, hproj, hproj, lb.reshape(1, width), norm_g.reshape(1, width))


def _hgrn_sample_kernel(qc_ref, fzc_ref, lbc_ref, i_ref, g_ref, ng_ref, st_ref, o_ref, nst_ref):
    for h in range(HGRN_HEADS):
        lb = lbc_ref[h]
        f = lb + (1.0 - lb) * _sigmoid(fzc_ref[0, h])
        st = f * st_ref[0, h] + (1.0 - f) * i_ref[0, h]
        nst_ref[0, h] = st
        o = jnp.sum(qc_ref[0, h] * st, axis=0, keepdims=True)
        ms = jnp.mean(o * o, axis=1, keepdims=True)
        gg = g_ref[0, h]
        o_ref[0, h] = o * lax.rsqrt(ms + RMS_EPS) * ng_ref[h] * (gg * _sigmoid(gg))


def _hgrn_sample(hs, state, lb, norm_g):
    db = hs.shape[0]
    width = HGRN_HEADS * HGRN_DK
    colv = lambda x: x.reshape(db, HGRN_HEADS, HGRN_DK, 1)
    rowv = lambda x: x.reshape(db, HGRN_HEADS, 1, HGRN_DK)
    cspec = pl.BlockSpec((1, HGRN_HEADS, HGRN_DK, 1), lambda b: (b, 0, 0, 0))
    rspec = pl.BlockSpec((1, HGRN_HEADS, 1, HGRN_DK), lambda b: (b, 0, 0, 0))
    sspec = pl.BlockSpec((1, HGRN_HEADS, HGRN_DK, HGRN_DK), lambda b: (b, 0, 0, 0))
    o, nst = pl.pallas_call(
        _hgrn_sample_kernel,
        grid=(db,),
        in_specs=[cspec, cspec, pl.BlockSpec((HGRN_HEADS, HGRN_DK, 1), lambda b: (0, 0, 0)), rspec, rspec,
                  pl.BlockSpec((HGRN_HEADS, 1, HGRN_DK), lambda b: (0, 0, 0)), sspec],
        out_specs=[rspec, sspec],
        out_shape=[jax.ShapeDtypeStruct((db, HGRN_HEADS, 1, HGRN_DK), f32),
                   jax.ShapeDtypeStruct((db, HGRN_HEADS, HGRN_DK, HGRN_DK), f32)],
        compiler_params=_params(("parallel",)),
        name="hgrn_sample",
    )(colv(hs[:, :width]), colv(hs[:, width:2 * width]), lb.reshape(HGRN_HEADS, HGRN_DK, 1),
      rowv(hs[:, 2 * width:3 * width]), rowv(hs[:, 3 * width:]), norm_g.reshape(HGRN_HEADS, 1, HGRN_DK), state)
    return o.reshape(db, width), nst


PEER_EBLK = 1024
PEER_RANKS = PEER_TOPK + 1
PEER_VROWS = 24
PEER_SUB = SUBLANES
PEER_ROWS = 16


def _extract_max(tiles):
    mx = functools.reduce(jnp.maximum, tiles)
    mx = jnp.max(mx, axis=0, keepdims=True)
    return mx, [jnp.where(x == mx, FLOOR, x) for x in tiles]


def _peer_kernel(x_ref, q_ref, k1_ref, k2_ref, u_ref, v_ref, g_ref, b_ref, o_ref,
                 xt_ref, acc_ref, act_ref, p_ref, a_ref, bw_ref, c_ref, s2_ref, v1_ref, v2_ref, ab_ref, cb_ref):
    e = pl.program_id(1)
    tt = x_ref.shape[0]
    n_lane_chunks = tt // LANES
    sub = PEER_SUB
    n_sub = PEER_NKEYS // sub
    keys_per_step = PEER_EBLK // PEER_NKEYS

    @pl.when(e == 0)
    def _():
        xt_ref[...] = x_ref[...].T.astype(bf16)
        acc_ref[...] = jnp.zeros(acc_ref.shape, f32)
        floor_rows = jnp.full((PEER_VROWS, LANES), FLOOR, f32)
        for h in range(PEER_HEADS):
            q1 = q_ref[:, (2 * h) * PEER_DHALF:(2 * h + 1) * PEER_DHALF]
            q2 = q_ref[:, (2 * h + 1) * PEER_DHALF:(2 * h + 2) * PEER_DHALF]
            s1 = lax.dot_general(k1_ref[h], q1, NT_DIMS, precision=HI, preferred_element_type=f32)
            c_ref[h] = s1.reshape(n_sub, sub, tt)
            s2_ref[h] = lax.dot_general(k2_ref[h], q2, NT_DIMS, precision=HI, preferred_element_type=f32)

        def per_head(h, carry):
            for lc in range(n_lane_chunks):
                lanes = slice(lc * LANES, (lc + 1) * LANES)
                for side, vals_ref in enumerate((v1_ref, v2_ref)):
                    vals_ref[...] = floor_rows
                    if side == 0:
                        tiles = [c_ref[h, k, :, lanes] for k in range(n_sub)]
                    else:
                        tiles = [s2_ref[h, k * sub:(k + 1) * sub, lanes] for k in range(n_sub)]
                    for r in range(PEER_RANKS):
                        mx, tiles = _extract_max(tiles)
                        vals_ref[r:r + 1, :] = mx
                cands = [v1_ref[0:1, :] + v2_ref[r:r + sub, :] for r in range(0, PEER_VROWS, sub)]
                cands += [v1_ref[r:r + 1, :] + v2_ref[0:sub, :] for r in range(1, sub)]
                cands += [v1_ref[r:r + sub, :] + v2_ref[0:1, :] for r in range(sub, PEER_VROWS, sub)]
                best = v1_ref[0:1, :] + v2_ref[0:1, :]
                zsum = jnp.zeros((1, LANES), f32)
                kth = best
                for r in range(PEER_TOPK):
                    kth, cands = _extract_max(cands)
                    zsum = zsum + jnp.exp(kth - best)
                nxt, _ = _extract_max(cands)
                thresh = 0.5 * (kth + nxt)
                s1 = c_ref[h, :, :, lanes]
                a_ref[h, :, :, lanes] = jnp.exp(s1 - v1_ref[0:1, :]) / zsum
                bw_ref[h, :, lanes] = jnp.exp(s2_ref[h, :, lanes] - v2_ref[0:1, :])
                c_ref[h, :, :, lanes] = thresh - s1
            return carry

        lax.fori_loop(0, PEER_HEADS, per_head, 0)

    act_ref[...] = jnp.dot(u_ref[0], xt_ref[...], preferred_element_type=f32)
    for h in range(PEER_HEADS):
        for ii in range(keys_per_step):
            ab_ref[ii, h] = jnp.broadcast_to(a_ref[h, e, ii:ii + 1, :], (sub, tt))
            cb_ref[ii, h] = jnp.broadcast_to(c_ref[h, e, ii:ii + 1, :], (sub, tt))

    def per_second_keys(k, carry):
        r0 = pl.multiple_of(k * PEER_ROWS, PEER_ROWS)
        for lc in range(n_lane_chunks):
            lanes = slice(lc * LANES, (lc + 1) * LANES)
            w = [None] * keys_per_step
            for h in range(PEER_HEADS):
                s2 = s2_ref[h, pl.ds(r0, PEER_ROWS), lanes].reshape(PEER_ROWS // sub, sub, LANES)
                bw = bw_ref[h, pl.ds(r0, PEER_ROWS), lanes].reshape(PEER_ROWS // sub, sub, LANES)
                for ii in range(keys_per_step):
                    hit = s2 >= cb_ref[ii, h, :, lanes][None]
                    term = ab_ref[ii, h, :, lanes][None] * jnp.where(hit, bw, 0.0)
                    w[ii] = term if h == 0 else w[ii] + term
            for ii in range(keys_per_step):
                rows = pl.ds(pl.multiple_of(ii * PEER_NKEYS + r0, PEER_ROWS), PEER_ROWS)
                act = act_ref[rows, lanes]
                gelu = 0.5 * act * (1.0 + lax.erf(act * (2.0 ** -0.5)))
                p_ref[rows, lanes] = (w[ii].reshape(PEER_ROWS, LANES) * gelu).astype(bf16)
        return carry

    lax.fori_loop(0, PEER_NKEYS // PEER_ROWS, per_second_keys, 0)
    acc_ref[...] += lax.dot_general(p_ref[...], v_ref[0], TN_DIMS, preferred_element_type=f32)

    @pl.when(e == pl.num_programs(1) - 1)
    def _():
        o_ref[...] = _layer_norm(DN_ALPHA * x_ref[...] + acc_ref[...], g_ref[...], b_ref[...])


def _peer_ln(x, wq, keys, u, v, layer, g, b, *, tt=TOKEN_TILE):
    m, d = x.shape
    n_exp = u.shape[1]
    q = _mm(x, wq, tm=tt)
    nq = q.shape[1]
    return pl.pallas_call(
        _peer_kernel,
        grid=(m // tt, n_exp // PEER_EBLK),
        in_specs=[pl.BlockSpec((tt, d), lambda t, e: (t, 0)),
                  pl.BlockSpec((tt, nq), lambda t, e: (t, 0)),
                  pl.BlockSpec((PEER_HEADS, PEER_NKEYS, PEER_DHALF), lambda t, e: (0, 0, 0)),
                  pl.BlockSpec((PEER_HEADS, PEER_NKEYS, PEER_DHALF), lambda t, e: (0, 0, 0)),
                  pl.BlockSpec((1, PEER_EBLK, d), lambda t, e: (layer, e, 0)),
                  pl.BlockSpec((1, PEER_EBLK, d), lambda t, e: (layer, e, 0)),
                  pl.BlockSpec((1, d), lambda t, e: (0, 0)),
                  pl.BlockSpec((1, d), lambda t, e: (0, 0))],
        out_specs=pl.BlockSpec((tt, d), lambda t, e: (t, 0)),
        out_shape=jax.ShapeDtypeStruct((m, d), f32),
        scratch_shapes=[pltpu.VMEM((d, tt), bf16), pltpu.VMEM((tt, d), f32),
                        pltpu.VMEM((PEER_EBLK, tt), f32), pltpu.VMEM((PEER_EBLK, tt), bf16)]
                       + [pltpu.VMEM((PEER_HEADS, PEER_NKEYS // PEER_SUB, PEER_SUB, tt), f32),
                          pltpu.VMEM((PEER_HEADS, PEER_NKEYS, tt), f32),
                          pltpu.VMEM((PEER_HEADS, PEER_NKEYS // PEER_SUB, PEER_SUB, tt), f32),
                          pltpu.VMEM((PEER_HEADS, PEER_NKEYS, tt), f32)]
                       + [pltpu.VMEM((PEER_VROWS, LANES), f32)] * 2
                       + [pltpu.VMEM((PEER_EBLK // PEER_NKEYS, PEER_HEADS, PEER_SUB, tt), f32)] * 2,
        compiler_params=_params(("parallel", "arbitrary")),
        name="peer",
    )(x, q, keys[0], keys[1], u, v, g.reshape(1, d), b.reshape(1, d))


def kernel(x_prompt, x_sample, cache_k, cache_v, state_conv, state_hgrn, page_table, w_in_even, conv_w, conv_b, conv_ln_g, conv_ln_b, w_out_even, w_in_odd, hgrn_lb_logits, hgrn_norm_g, w_out_odd, ln_g, ln_b, peer_wq, peer_keys, peer_u, peer_v):
    batch, seq, d = x_prompt.shape
    db = x_sample.shape[0]
    n_prompt = batch * seq
    pad_sample = lambda a: jnp.pad(a, ((0, SAMPLE_TILE - a.shape[0]), (0, 0)))
    pages = lambda a: a.reshape(batch, seq // PAGE_SIZE, MOBA_HEADS, MOBA_HEAD_DIM, PAGE_SIZE).transpose(0, 1, 4, 2, 3)
    heads = lambda a: a.reshape(db, MOBA_HEADS, MOBA_HEAD_DIM)

    lb_p = jax.nn.softmax(hgrn_lb_logits.astype(f32), axis=0)
    lb_all = jnp.cumsum(lb_p, axis=0) - lb_p[0]

    peer_u16, peer_v16 = peer_u.astype(bf16), peer_v.astype(bf16)
    xp = x_prompt.reshape(n_prompt, d)
    xs = pad_sample(x_sample.reshape(db, d))
    kp_l, vp_l, ks_l, vs_l, cp_l, cs_l, hp_l, hs_l = [], [], [], [], [], [], [], []
    for l in range(DEPTH):
        if l % 2 == 0:
            e = l // 2
            w_qk, w_vag = w_in_even[e][:, :2 * MOBA_WIDTH], w_in_even[e][:, 2 * MOBA_WIDTH:]
            qk_p, k_pages = _mm(xp, w_qk, split=True, paged_chunk=1)
            vag_p, v_pages = _mm(xp, w_vag, paged_chunk=0)
            qk_s, vag_s = _mm(xs, w_qk, split=True, tm=SAMPLE_TILE)[:db], _mm(xs, w_vag, tm=SAMPLE_TILE)[:db]
            kp_l.append(pages(k_pages))
            vp_l.append(pages(v_pages))
            q_s, k_s, v_s = qk_s[:, :MOBA_WIDTH], qk_s[:, MOBA_WIDTH:], vag_s[:, :MOBA_WIDTH]
            ks_l.append(k_s.reshape(db, 1, MOBA_HEADS, MOBA_HEAD_DIM))
            vs_l.append(v_s.reshape(db, 1, MOBA_HEADS, MOBA_HEAD_DIM))
            att_p = _moba_prompt(qk_p, vag_p, batch, seq)
            att_s = _moba_sample(heads(q_s), heads(k_s), heads(v_s),
                                 cache_k[e].transpose(0, 2, 3, 1), cache_v[e].transpose(0, 2, 3, 1), page_table)
            att_s = att_s[:, :, :MOBA_HEAD_DIM].reshape(db, MOBA_WIDTH)
            cprm = (conv_w[e], conv_b[e], conv_ln_g[e], conv_ln_b[e])
            cy_p, buf_p = _conv_prompt(vag_p, batch, seq, *cprm)
            cy_s, buf_s = _conv_sample(vag_s[:, MOBA_WIDTH:MOBA_WIDTH + CONV_CH], vag_s[:, MOBA_WIDTH + CONV_CH:],
                                       state_conv[e], *cprm)
            cp_l.append(buf_p)
            cs_l.append(buf_s)
            mix_p, mix_s = [att_p, cy_p], [pad_sample(att_s), pad_sample(cy_s)]
            w_out = w_out_even[e]
        else:
            oi = l // 2
            hproj_p = _mm(xp, w_in_odd[oi], tm=WIDE_OUT_TILE)
            hproj_s = _mm(xs, w_in_odd[oi], tm=SAMPLE_TILE)[:db]
            o_p, st_p = _hgrn_prompt(hproj_p, batch, seq, lb_all[l], hgrn_norm_g[oi])
            o_s, st_s = _hgrn_sample(hproj_s, state_hgrn[oi].astype(f32), lb_all[l], hgrn_norm_g[oi])
            hp_l.append(st_p)
            hs_l.append(st_s)
            mix_p, mix_s = [o_p], [pad_sample(o_s)]
            w_out = w_out_odd[oi]
        xp = _mm_res_ln(mix_p, w_out, xp, ln_g[l, 0], ln_b[l, 0])
        xs = _mm_res_ln(mix_s, w_out, xs, ln_g[l, 0], ln_b[l, 0], tm=SAMPLE_TILE)
        prm = (peer_wq[l], peer_keys[l], peer_u16, peer_v16, l, ln_g[l, 1], ln_b[l, 1])
        xp = _peer_ln(xp, *prm)
        xs = _peer_ln(xs, *prm, tt=SAMPLE_TILE)
    y_prompt = xp.reshape(batch, seq, d)
    y_sample = xs[:db].reshape(db, 1, d)
    return (y_prompt, y_sample, jnp.stack(kp_l), jnp.stack(vp_l), jnp.stack(ks_l), jnp.stack(vs_l),
            jnp.stack(cp_l), jnp.stack(cs_l), jnp.stack(hp_l), jnp.stack(hs_l))
```

```python
import functools

import jax
import jax.numpy as jnp
from jax import lax
from jax.experimental import pallas as pl
from jax.experimental.pallas import tpu as pltpu

f32 = jnp.float32
bf16 = jnp.bfloat16
i32 = jnp.int32
HI = lax.Precision.HIGHEST

D_MODEL = 1024
DEPTH = 2
PAGE_SIZE = 128
MOBA_HEADS = 8
MOBA_HEAD_DIM = 64
MOBA_WIDTH = MOBA_HEADS * MOBA_HEAD_DIM
MOBA_BLOCK = 256
MOBA_TOPK = 3
MOBA_SCALE = MOBA_HEAD_DIM ** -0.5
CONV_CH = D_MODEL // 2
CONV_WIDTH = 31
HGRN_HEADS = 8
HGRN_DK = D_MODEL // HGRN_HEADS
HGRN_CHUNK = 128
HGRN_SUB = 16
PEER_HEADS = 8
PEER_NKEYS = 128
PEER_DHALF = 128
PEER_TOPK = 16
LN_EPS = 1e-5
RMS_EPS = 1e-6
DN_ALPHA = (2 * DEPTH) ** 0.25

LANES = 128
TOKEN_TILE = 512
SAMPLE_TILE = 128
WIDE_OUT_TILE = 256
MASK_NEG = -1e30
FLOOR = -3e38
EXP_CLAMP = 60.0
VMEM_LIMIT = 56 << 20

NT_DIMS = (((1,), (1,)), ((), ()))
TN_DIMS = (((0,), (0,)), ((), ()))


def _params(semantics, vmem=VMEM_LIMIT):
    return pltpu.CompilerParams(dimension_semantics=semantics, vmem_limit_bytes=vmem)


def _layer_norm(y, g, b):
    mu = jnp.mean(y, axis=-1, keepdims=True)
    d = y - mu
    var = jnp.mean(d * d, axis=-1, keepdims=True)
    return d * lax.rsqrt(var + LN_EPS) * g + b


def _sigmoid(x):
    return 1.0 / (1.0 + jnp.exp(-x))


MM_CHUNK = 512


def _mm_kernel(x_ref, *refs, split, paged_chunk):
    refs = list(refs)
    t_ref = refs.pop() if paged_chunk is not None else None
    o_ref = refs.pop()
    wh_ref = refs[0]
    x = x_ref[...]
    xh = x.astype(bf16)
    if split:
        xl = (x - xh.astype(f32)).astype(bf16)
    for c, j in enumerate(range(0, o_ref.shape[1], MM_CHUNK)):
        wh = wh_ref[:, j:j + MM_CHUNK]
        acc = jnp.dot(xh, wh, preferred_element_type=f32)
        if split:
            acc = acc + jnp.dot(xl, wh, preferred_element_type=f32)
            acc = acc + jnp.dot(xh, refs[1][:, j:j + MM_CHUNK], preferred_element_type=f32)
        o_ref[:, j:j + MM_CHUNK] = acc
        if c == paged_chunk:
            for pg in range(o_ref.shape[0] // PAGE_SIZE):
                t_ref[pg] = acc[pg * PAGE_SIZE:(pg + 1) * PAGE_SIZE, :].T


def _mm(x, w, *, split=False, tm=TOKEN_TILE, paged_chunk=None):
    m, k = x.shape
    n = w.shape[1]
    wh = w.astype(bf16)
    ws = [wh] + ([(w - wh.astype(f32)).astype(bf16)] if split else [])
    out_specs = [pl.BlockSpec((tm, n), lambda i: (i, 0))]
    out_shape = [jax.ShapeDtypeStruct((m, n), f32)]
    if paged_chunk is not None:
        out_specs.append(pl.BlockSpec((tm // PAGE_SIZE, MM_CHUNK, PAGE_SIZE), lambda i: (i, 0, 0)))
        out_shape.append(jax.ShapeDtypeStruct((m // PAGE_SIZE, MM_CHUNK, PAGE_SIZE), f32))
    outs = pl.pallas_call(
        functools.partial(_mm_kernel, split=split, paged_chunk=paged_chunk),
        grid=(m // tm,),
        in_specs=[pl.BlockSpec((tm, k), lambda i: (i, 0))] + [pl.BlockSpec((k, n), lambda i: (0, 0))] * len(ws),
        out_specs=out_specs,
        out_shape=out_shape,
        compiler_params=_params(("parallel",)),
        name="proj_split" if split else "proj",
    )(x, *ws)
    return outs[0] if paged_chunk is None else outs


def _mm_res_ln_kernel(*refs):
    *a_refs, w_ref, x_ref, g_ref, b_ref, o_ref = refs
    acc = DN_ALPHA * x_ref[...]
    row = 0
    for a_ref in a_refs:
        k = a_ref.shape[1]
        acc = acc + jnp.dot(a_ref[...].astype(bf16), w_ref[row:row + k, :], preferred_element_type=f32)
        row += k
    o_ref[...] = _layer_norm(acc, g_ref[...], b_ref[...])


def _mm_res_ln(parts, w, x, g, b, *, tm=TOKEN_TILE):
    m, n = x.shape
    k = w.shape[0]
    return pl.pallas_call(
        _mm_res_ln_kernel,
        grid=(m // tm,),
        in_specs=[pl.BlockSpec((tm, a.shape[1]), lambda i: (i, 0)) for a in parts]
                 + [pl.BlockSpec((k, n), lambda i: (0, 0)),
                    pl.BlockSpec((tm, n), lambda i: (i, 0)), pl.BlockSpec((1, n), lambda i: (0, 0)),
                    pl.BlockSpec((1, n), lambda i: (0, 0))],
        out_specs=pl.BlockSpec((tm, n), lambda i: (i, 0)),
        out_shape=jax.ShapeDtypeStruct((m, n), f32),
        compiler_params=_params(("parallel",)),
        name="out_proj_ln",
    )(*parts, w.astype(bf16), x, g.reshape(1, n), b.reshape(1, n))


CONV_TILE = 256
CONV_ROWS = 64
CONV_HIST = 32
SUBLANES = 8
CONV_SHIFT_ROWS = CONV_TILE + CONV_HIST - SUBLANES


def _conv_prompt_kernel(a_ref, gt_ref, w_ref, cb_ref, g_ref, b_ref, y_ref, st_ref, buf_ref, shift_ref):
    t = pl.program_id(1)

    @pl.when(t == 0)
    def _():
        buf_ref[0:CONV_HIST, :] = jnp.zeros((CONV_HIST, CONV_CH), f32)

    buf_ref[CONV_HIST:CONV_HIST + CONV_TILE, :] = a_ref[...] * _sigmoid(gt_ref[...])
    for q in range(1, SUBLANES):
        shift_ref[q - 1] = buf_ref[q:q + CONV_SHIFT_ROWS, :]
    first = CONV_HIST - (CONV_WIDTH - 1)
    for r in range(0, CONV_TILE, CONV_ROWS):
        acc = jnp.zeros((CONV_ROWS, CONV_CH), f32) + cb_ref[...]
        for j in range(CONV_WIDTH):
            q = (first + j) % SUBLANES
            base = r + first + j - q
            rows = buf_ref[base:base + CONV_ROWS, :] if q == 0 else shift_ref[q - 1, base:base + CONV_ROWS, :]
            acc = acc + w_ref[j:j + 1, :] * rows
        y = _layer_norm(acc, g_ref[...], b_ref[...])
        y_ref[r:r + CONV_ROWS, :] = y * _sigmoid(y)
    tail = buf_ref[CONV_TILE:CONV_TILE + CONV_HIST, :]
    buf_ref[0:CONV_HIST, :] = tail

    @pl.when(t == pl.num_programs(1) - 1)
    def _():
        st_ref[0] = tail


def _conv_prompt(vag, batch, seq, w, cb, g, b):
    nt = seq // CONV_TILE
    wpad = jnp.zeros((CONV_HIST, CONV_CH), f32).at[:CONV_WIDTH].set(w)
    row = lambda v: v.reshape(1, CONV_CH)
    y, st = pl.pallas_call(
        _conv_prompt_kernel,
        grid=(batch, nt),
        in_specs=[pl.BlockSpec((CONV_TILE, CONV_CH), lambda bi, t: (bi * nt + t, 1)),
                  pl.BlockSpec((CONV_TILE, CONV_CH), lambda bi, t: (bi * nt + t, 2)),
                  pl.BlockSpec((CONV_HIST, CONV_CH), lambda bi, t: (0, 0))]
                 + [pl.BlockSpec((1, CONV_CH), lambda bi, t: (0, 0))] * 3,
        out_specs=[pl.BlockSpec((CONV_TILE, CONV_CH), lambda bi, t: (bi * nt + t, 0)),
                   pl.BlockSpec((1, CONV_HIST, CONV_CH), lambda bi, t: (bi, 0, 0))],
        out_shape=[jax.ShapeDtypeStruct((batch * seq, CONV_CH), f32),
                   jax.ShapeDtypeStruct((batch, CONV_HIST, CONV_CH), f32)],
        scratch_shapes=[pltpu.VMEM((CONV_TILE + CONV_HIST, CONV_CH), f32),
                        pltpu.VMEM((SUBLANES - 1, CONV_SHIFT_ROWS, CONV_CH), f32)],
        compiler_params=_params(("arbitrary", "arbitrary")),
        name="conv_prompt",
    )(vag, vag, wpad, row(cb), row(g), row(b))
    return y, st[:, CONV_HIST - (CONV_WIDTH - 1):]


def _conv_sample_kernel(a_ref, gt_ref, st_ref, w_ref, cb_ref, g_ref, b_ref, y_ref, nst_ref):
    u = a_ref[...] * _sigmoid(gt_ref[...])
    acc = cb_ref[...] + w_ref[CONV_WIDTH - 1:CONV_WIDTH, :] * u
    for j in range(CONV_WIDTH - 1):
        acc = acc + w_ref[j:j + 1, :] * st_ref[j]
    y = _layer_norm(acc, g_ref[...], b_ref[...])
    y_ref[...] = y * _sigmoid(y)
    for j in range(CONV_WIDTH - 2):
        nst_ref[j] = st_ref[j + 1]
    nst_ref[CONV_WIDTH - 2] = u


def _conv_sample(a, gt, state, w, cb, g, b):
    db = a.shape[0]
    wpad = jnp.zeros((CONV_HIST, CONV_CH), f32).at[:CONV_WIDTH].set(w)
    row = lambda v: v.reshape(1, CONV_CH)
    y, nst = pl.pallas_call(
        _conv_sample_kernel,
        out_shape=[jax.ShapeDtypeStruct((db, CONV_CH), f32),
                   jax.ShapeDtypeStruct((CONV_WIDTH - 1, db, CONV_CH), f32)],
        name="conv_sample",
    )(a, gt, state.transpose(1, 0, 2), wpad, row(cb), row(g), row(b))
    return y, nst.transpose(1, 0, 2)


GATE_ROWS = 1024
ATTN_KEYS = 2 * MOBA_BLOCK


def _block_sum_kernel(k_ref, o_ref):
    o_ref[0] = jnp.sum(k_ref[...], axis=0, keepdims=True)


def _block_sums(qk, batch, seq):
    nb = seq // MOBA_BLOCK
    return pl.pallas_call(
        _block_sum_kernel,
        grid=(batch * nb,),
        in_specs=[pl.BlockSpec((MOBA_BLOCK, MOBA_WIDTH), lambda i: (i, 1))],
        out_specs=pl.BlockSpec((1, 1, MOBA_WIDTH), lambda i: (i, 0, 0)),
        out_shape=jax.ShapeDtypeStruct((batch * nb, 1, MOBA_WIDTH), f32),
        compiler_params=_params(("parallel",)),
        name="moba_block_sums",
    )(qk)


def _moba_gate_kernel(q_ref, k_ref, v_ref, ks_ref, qa_ref, ka_ref, va_ref, *, nb):
    t = pl.program_id(2)
    lane = lax.broadcasted_iota(i32, (MOBA_BLOCK, LANES), 1)
    blk = lane - MOBA_HEAD_DIM
    km_lane = lax.broadcasted_iota(i32, (nb, LANES), 1)
    ksum = ks_ref[0] * (1.0 / MOBA_BLOCK)
    pad_top = jnp.zeros((MOBA_HEAD_DIM, LANES), f32)
    pad_bot = jnp.zeros((LANES - MOBA_HEAD_DIM - nb, LANES), f32)
    for c in range(GATE_ROWS // MOBA_BLOCK):
        own = t * (GATE_ROWS // MOBA_BLOCK) + c
        rows = slice(c * MOBA_BLOCK, (c + 1) * MOBA_BLOCK)
        q2 = q_ref[rows, :]
        k2 = k_ref[rows, :]
        v2 = v_ref[rows, :]
        cand = (blk >= 0) & (blk < own)
        for j in range(2):
            head = (km_lane >= j * MOBA_HEAD_DIM) & (km_lane < (j + 1) * MOBA_HEAD_DIM)
            km = jnp.concatenate([pad_top, jnp.where(head, ksum, 0.0), pad_bot], axis=0)
            gate = lax.dot_general(q2, km, NT_DIMS, precision=HI, preferred_element_type=f32)
            masked = jnp.where(cand, gate, FLOOR)
            sel = jnp.zeros((MOBA_BLOCK, LANES), f32)
            for _ in range(MOBA_TOPK):
                mx = jnp.max(masked, axis=1, keepdims=True)
                hit = (masked == mx) & cand
                sel = jnp.where(hit, 1.0, sel)
                masked = jnp.where(hit, FLOOR, masked)
            bias = jnp.where((sel > 0.0) | (blk == own), 0.0, MASK_NEG)
            qj = q2 if j == 0 else pltpu.roll(q2, MOBA_HEAD_DIM, 1)
            kj = k2 if j == 0 else pltpu.roll(k2, MOBA_HEAD_DIM, 1)
            vj = v2 if j == 0 else pltpu.roll(v2, MOBA_HEAD_DIM, 1)
            in_head = lane < MOBA_HEAD_DIM
            in_bias = lane < MOBA_HEAD_DIM + nb
            qa = jnp.where(in_head, qj * MOBA_SCALE, jnp.where(in_bias, bias, 0.0))
            ka = jnp.where(in_head, kj, jnp.where(blk == own, 1.0, 0.0))
            va = jnp.where(in_head, vj, jnp.where(blk == 0, 1.0, 0.0))
            qa_ref[0, j, rows, :] = qa.astype(bf16)
            ka_ref[0, j, rows, :] = ka.astype(bf16)
            va_ref[0, j, rows, :] = va.astype(bf16)


def _moba_attn_kernel(qa_ref, ka_ref, va_ref, o_ref):
    i = pl.program_id(1)
    last = i // 2
    row = lax.broadcasted_iota(i32, (MOBA_BLOCK, ATTN_KEYS), 0)
    col = lax.broadcasted_iota(i32, (MOBA_BLOCK, ATTN_KEYS), 1)
    lane = lax.broadcasted_iota(i32, (MOBA_BLOCK, LANES), 1)

    def tile(h, k0, mask):
        s = lax.dot_general(qa_ref[0, h], ka_ref[0, h, pl.ds(k0, ATTN_KEYS), :], NT_DIMS, preferred_element_type=f32)
        return s if mask is None else jnp.where(mask, s, MASK_NEG)

    def weighted(p, h, k0):
        return jnp.dot(p.astype(bf16), va_ref[0, h, pl.ds(k0, ATTN_KEYS), :], preferred_element_type=f32)

    k_last = pl.multiple_of(last * ATTN_KEYS, ATTN_KEYS)
    visible = (k_last + col) <= (i * MOBA_BLOCK + row)
    state = []
    for h in range(MOBA_HEADS):
        s = tile(h, k_last, visible)
        m = jnp.max(s, axis=1, keepdims=True)
        state += [m, weighted(jnp.exp(s - m), h, k_last)]

    def body(n, carry):
        k0 = pl.multiple_of(n * ATTN_KEYS, ATTN_KEYS)
        out = []
        for h in range(MOBA_HEADS):
            m, acc = carry[2 * h:2 * h + 2]
            s = tile(h, k0, None)
            mn = jnp.maximum(m, jnp.max(s, axis=1, keepdims=True))
            out += [mn, jnp.exp(m - mn) * acc + weighted(jnp.exp(s - mn), h, k0)]
        return tuple(out)

    final = lax.fori_loop(0, last, body, tuple(state))
    for pair in range(MOBA_HEADS // 2):
        acc0, acc1 = final[4 * pair + 1], final[4 * pair + 3]
        out0 = acc0 / acc0[:, MOBA_HEAD_DIM:MOBA_HEAD_DIM + 1]
        out1 = acc1 / acc1[:, MOBA_HEAD_DIM:MOBA_HEAD_DIM + 1]
        o_ref[:, pair * LANES:(pair + 1) * LANES] = jnp.where(lane < MOBA_HEAD_DIM, out0,
                                                              pltpu.roll(out1, MOBA_HEAD_DIM, 1))


def _moba_prompt(qk, vag, batch, seq):
    nb = seq // MOBA_BLOCK
    npair = MOBA_WIDTH // LANES
    nt = seq // GATE_ROWS
    ksums = _block_sums(qk, batch, seq).reshape(batch, nb, MOBA_WIDTH)
    per_head = pl.BlockSpec((1, 2, GATE_ROWS, LANES), lambda b, p, t: (b, p, t, 0))
    qa, ka, va = pl.pallas_call(
        functools.partial(_moba_gate_kernel, nb=nb),
        grid=(batch, npair, nt),
        in_specs=[pl.BlockSpec((GATE_ROWS, LANES), lambda b, p, t: (b * nt + t, p)),
                  pl.BlockSpec((GATE_ROWS, LANES), lambda b, p, t: (b * nt + t, npair + p)),
                  pl.BlockSpec((GATE_ROWS, LANES), lambda b, p, t: (b * nt + t, p)),
                  pl.BlockSpec((1, nb, LANES), lambda b, p, t: (b, 0, p))],
        out_specs=[per_head] * 3,
        out_shape=[jax.ShapeDtypeStruct((batch, MOBA_HEADS, seq, LANES), bf16)] * 3,
        compiler_params=_params(("parallel", "parallel", "parallel")),
        name="moba_gate",
    )(qk, qk, vag, ksums)
    return pl.pallas_call(
        _moba_attn_kernel,
        grid=(batch, nb),
        in_specs=[pl.BlockSpec((1, MOBA_HEADS, MOBA_BLOCK, LANES), lambda b, i: (b, 0, i, 0)),
                  pl.BlockSpec((1, MOBA_HEADS, seq, LANES), lambda b, i: (b, 0, 0, 0),
                               pipeline_mode=pl.Buffered(1)),
                  pl.BlockSpec((1, MOBA_HEADS, seq, LANES), lambda b, i: (b, 0, 0, 0),
                               pipeline_mode=pl.Buffered(1))],
        out_specs=pl.BlockSpec((MOBA_BLOCK, MOBA_WIDTH), lambda b, i: (b * nb + i, 0)),
        out_shape=jax.ShapeDtypeStruct((batch * seq, MOBA_WIDTH), f32),
        compiler_params=_params(("parallel", "arbitrary")),
        name="moba_attn",
    )(qa, ka, va)


PAGES_PER_STEP = 8
PAGES_PER_BLOCK = MOBA_BLOCK // PAGE_SIZE
BLOCKS_PER_STEP = PAGES_PER_STEP // PAGES_PER_BLOCK


def _moba_sample_stream_kernel(pt_ref, qb_ref, *refs):
    del pt_ref
    k_refs = refs[:PAGES_PER_STEP]
    v_refs = refs[PAGES_PER_STEP:2 * PAGES_PER_STEP]
    gate_ref, m_ref, l_ref, acc_ref = refs[2 * PAGES_PER_STEP:]
    qb = qb_ref[0]
    head = lax.broadcasted_iota(i32, (MOBA_HEADS, LANES), 0)
    tile = (MOBA_HEADS, LANES)
    for jj in range(BLOCKS_PER_STEP):
        pages = range(PAGES_PER_BLOCK * jj, PAGES_PER_BLOCK * (jj + 1))
        raw = [jnp.sum(k_refs[x][0] * qb, axis=1) for x in pages]
        gate = functools.reduce(jnp.add, [jnp.sum(r, axis=1, keepdims=True) for r in raw]) * (1.0 / MOBA_BLOCK)
        m = functools.reduce(jnp.maximum, [jnp.max(r, axis=1, keepdims=True) for r in raw]) * MOBA_SCALE
        l = jnp.zeros((MOBA_HEADS, 1), f32)
        pv = jnp.zeros((MOBA_HEADS, MOBA_WIDTH), f32)
        for r, x in zip(raw, pages):
            p = jnp.exp(r * MOBA_SCALE - m)
            l = l + jnp.sum(p, axis=1, keepdims=True)
            vt = v_refs[x][0].reshape(MOBA_WIDTH, PAGE_SIZE).astype(bf16)
            pv = pv + lax.dot_general(p.astype(bf16), vt, NT_DIMS, preferred_element_type=f32)
        acc = jnp.zeros(tile, f32)
        for pair in range(MOBA_WIDTH // LANES):
            both = pv[:, pair * LANES:(pair + 1) * LANES]
            acc = acc + jnp.where(head == 2 * pair, both, 0.0)
            acc = acc + jnp.where(head == 2 * pair + 1, pltpu.roll(both, MOBA_HEAD_DIM, 1), 0.0)
        gate_ref[0, 0, jj] = jnp.broadcast_to(gate, tile)
        m_ref[0, 0, jj] = jnp.broadcast_to(m, tile)
        l_ref[0, 0, jj] = jnp.broadcast_to(l, tile)
        acc_ref[0, 0, jj] = acc


def _moba_sample_combine_kernel(gate_ref, m_ref, l_ref, acc_ref, q_ref, kn_ref, vn_ref, o_ref):
    db, nb, nh, width = gate_ref.shape
    masked = gate_ref[...]
    sel = jnp.zeros((db, nb, nh, width), f32)
    for _ in range(MOBA_TOPK):
        mx = jnp.max(masked, axis=1, keepdims=True)
        hit = masked == mx
        sel = jnp.where(hit, 1.0, sel)
        masked = jnp.where(hit, FLOOR, masked)
    chosen = sel > 0.0
    s_self = jnp.sum(q_ref[...] * kn_ref[...], axis=-1, keepdims=True) * MOBA_SCALE
    m = m_ref[...]
    top = jnp.maximum(jnp.max(jnp.where(chosen, m, FLOOR), axis=1), s_self)
    w = jnp.where(chosen, jnp.exp(m - top[:, None]), 0.0)
    w_self = jnp.exp(s_self - top)
    denom = jnp.sum(w * l_ref[...], axis=1) + w_self
    num = jnp.sum(w * acc_ref[...], axis=1) + w_self * vn_ref[...]
    o_ref[...] = num / denom


def _moba_sample(q, kn, vn, k_pool, v_pool, page_table):
    db, n_pages = page_table.shape
    nb = n_pages // PAGES_PER_BLOCK
    ng = n_pages // PAGES_PER_STEP
    page_block = (1, MOBA_HEADS, MOBA_HEAD_DIM, PAGE_SIZE)
    page_spec = lambda x: pl.BlockSpec(page_block, lambda b, g, pt, x=x: (pt[b, g * PAGES_PER_STEP + x], 0, 0, 0))
    stat = pl.BlockSpec((1, 1, BLOCKS_PER_STEP, MOBA_HEADS, LANES), lambda b, g, pt: (b, g, 0, 0, 0))
    stat_shape = jax.ShapeDtypeStruct((db, ng, BLOCKS_PER_STEP, MOBA_HEADS, LANES), f32)
    qb = jnp.broadcast_to(q[..., None], (db,) + page_block[1:])
    stats = pl.pallas_call(
        _moba_sample_stream_kernel,
        grid_spec=pltpu.PrefetchScalarGridSpec(
            num_scalar_prefetch=1,
            grid=(db, ng),
            in_specs=[pl.BlockSpec(page_block, lambda b, g, pt: (b, 0, 0, 0))]
                     + [page_spec(x) for x in range(PAGES_PER_STEP)] * 2,
            out_specs=[stat] * 4),
        out_shape=[stat_shape] * 4,
        compiler_params=_params(("parallel", "parallel")),
        name="moba_sample_stream",
    )(page_table, qb, *([k_pool] * PAGES_PER_STEP), *([v_pool] * PAGES_PER_STEP))
    gate, m, l, acc = (a.reshape(db, nb, MOBA_HEADS, LANES) for a in stats)
    vn_wide = jnp.pad(vn, ((0, 0), (0, 0), (0, LANES - MOBA_HEAD_DIM)))
    return pl.pallas_call(
        _moba_sample_combine_kernel,
        out_shape=jax.ShapeDtypeStruct((db, MOBA_HEADS, LANES), f32),
        compiler_params=pltpu.CompilerParams(vmem_limit_bytes=VMEM_LIMIT),
        name="moba_sample_combine",
    )(gate, m, l, acc, q, kn, vn_wide)


HGRN_TILE = 512


def _hgrn_prompt_kernel(q_ref, fz_ref, i_ref, g_ref, lb_ref, ng_ref, o_ref, st_ref, state_ref):
    t = pl.program_id(1)
    c_rows = HGRN_CHUNK

    @pl.when(t == 0)
    def _():
        state_ref[...] = jnp.zeros(state_ref.shape, f32)

    row = lax.broadcasted_iota(i32, (c_rows, c_rows), 0)
    col = lax.broadcasted_iota(i32, (c_rows, c_rows), 1)
    causal = col <= row
    cum = jnp.where(causal, 1.0, 0.0)
    cum_sub = jnp.where(col < (row // HGRN_SUB) * HGRN_SUB, 1.0, 0.0)
    cum_both16 = jnp.concatenate([cum, cum_sub], axis=0).astype(bf16)

    def chunk(c, carry):
        rows = pl.ds(pl.multiple_of(c * c_rows, c_rows), c_rows)
        lb_all = lb_ref[...]
        f_all = lb_all + (1.0 - lb_all) * _sigmoid(fz_ref[rows, :])
        log_f = jnp.log(f_all)
        both_all = jnp.zeros((2 * c_rows, log_f.shape[1]), f32)
        rest = log_f
        for _ in range(3):
            piece = rest.astype(bf16)
            both_all = both_all + jnp.dot(cum_both16, piece, preferred_element_type=f32)
            rest = rest - piece.astype(f32)
        for h in range(HGRN_HEADS):
            lanes = slice(h * HGRN_DK, (h + 1) * HGRN_DK)
            ng = ng_ref[:, lanes]
            q = q_ref[rows, lanes]
            v = i_ref[rows, lanes]
            gg = g_ref[rows, lanes]
            kk = 1.0 - f_all[:, lanes]
            b = both_all[:c_rows, lanes]
            ref_row = both_all[c_rows:, lanes]
            st = state_ref[h]
            o = lax.dot_general((q * jnp.exp(b)).astype(bf16), st.astype(bf16), NT_DIMS, preferred_element_type=f32)
            qh = (q * jnp.exp(b - ref_row)).astype(bf16)
            parts = []
            for s in range(c_rows // HGRN_SUB):
                ref_s = ref_row[s * HGRN_SUB:s * HGRN_SUB + 1, :]
                kh = (kk * jnp.exp(jnp.minimum(ref_s - b, EXP_CLAMP))).astype(bf16)
                parts.append(lax.dot_general(qh[s * HGRN_SUB:(s + 1) * HGRN_SUB], kh, NT_DIMS,
                                             preferred_element_type=f32))
            att = jnp.where(causal, jnp.concatenate(parts, axis=0), 0.0)
            o = o + jnp.dot(att.astype(bf16), v.astype(bf16), preferred_element_type=f32)
            b_last = b[c_rows - 1:c_rows, :]
            kd = (kk * jnp.exp(b_last - b)).astype(bf16)
            state_ref[h] = st * jnp.exp(b_last) + lax.dot_general(v.astype(bf16), kd, TN_DIMS,
                                                                   preferred_element_type=f32)
            ms = jnp.mean(o * o, axis=1, keepdims=True)
            o_ref[rows, lanes] = o * lax.rsqrt(ms + RMS_EPS) * ng * (gg * _sigmoid(gg))
        return carry

    lax.fori_loop(0, HGRN_TILE // c_rows, chunk, 0)

    @pl.when(t == pl.num_programs(1) - 1)
    def _():
        for h in range(HGRN_HEADS):
            st_ref[0, h] = state_ref[h].T


def _hgrn_prompt(hproj, batch, seq, lb, norm_g):
    nt = seq // HGRN_TILE
    width = HGRN_HEADS * HGRN_DK
    col = lambda j: pl.BlockSpec((HGRN_TILE, width), lambda b, t, j=j: (b * nt + t, j))
    vec = pl.BlockSpec((1, width), lambda b, t: (0, 0))
    return pl.pallas_call(
        _hgrn_prompt_kernel,
        grid=(batch, nt),
        in_specs=[col(0), col(1), col(2), col(3), vec, vec],
        out_specs=[pl.BlockSpec((HGRN_TILE, width), lambda b, t: (b * nt + t, 0)),
                   pl.BlockSpec((1, HGRN_HEADS, HGRN_DK, HGRN_DK), lambda b, t: (b, 0, 0, 0))],
        out_shape=[jax.ShapeDtypeStruct((batch * seq, width), f32),
                   jax.ShapeDtypeStruct((batch, HGRN_HEADS, HGRN_DK, HGRN_DK), f32)],
        scratch_shapes=[pltpu.VMEM((HGRN_HEADS, HGRN_DK, HGRN_DK), f32)],
        compiler_params=_params(("parallel", "arbitrary")),
        name="hgrn_prompt",
    )(hproj, hproj, hproj, hproj, lb.reshape(1, width), norm_g.reshape(1, width))


def _hgrn_sample_kernel(qc_ref, fzc_ref, lbc_ref, i_ref, g_ref, ng_ref, st_ref, o_ref, nst_ref):
    for h in range(HGRN_HEADS):
        lb = lbc_ref[h]
        f = lb + (1.0 - lb) * _sigmoid(fzc_ref[0, h])
        st = f * st_ref[0, h] + (1.0 - f) * i_ref[0, h]
        nst_ref[0, h] = st
        o = jnp.sum(qc_ref[0, h] * st, axis=0, keepdims=True)
        ms = jnp.mean(o * o, axis=1, keepdims=True)
        gg = g_ref[0, h]
        o_ref[0, h] = o * lax.rsqrt(ms + RMS_EPS) * ng_ref[h] * (gg * _sigmoid(gg))


def _hgrn_sample(hs, state, lb, norm_g):
    db = hs.shape[0]
    width = HGRN_HEADS * HGRN_DK
    colv = lambda x: x.reshape(db, HGRN_HEADS, HGRN_DK, 1)
    rowv = lambda x: x.reshape(db, HGRN_HEADS, 1, HGRN_DK)
    cspec = pl.BlockSpec((1, HGRN_HEADS, HGRN_DK, 1), lambda b: (b, 0, 0, 0))
    rspec = pl.BlockSpec((1, HGRN_HEADS, 1, HGRN_DK), lambda b: (b, 0, 0, 0))
    sspec = pl.BlockSpec((1, HGRN_HEADS, HGRN_DK, HGRN_DK), lambda b: (b, 0, 0, 0))
    o, nst = pl.pallas_call(
        _hgrn_sample_kernel,
        grid=(db,),
        in_specs=[cspec, cspec, pl.BlockSpec((HGRN_HEADS, HGRN_DK, 1), lambda b: (0, 0, 0)), rspec, rspec,
                  pl.BlockSpec((HGRN_HEADS, 1, HGRN_DK), lambda b: (0, 0, 0)), sspec],
        out_specs=[rspec, sspec],
        out_shape=[jax.ShapeDtypeStruct((db, HGRN_HEADS, 1, HGRN_DK), f32),
                   jax.ShapeDtypeStruct((db, HGRN_HEADS, HGRN_DK, HGRN_DK), f32)],
        compiler_params=_params(("parallel",)),
        name="hgrn_sample",
    )(colv(hs[:, :width]), colv(hs[:, width:2 * width]), lb.reshape(HGRN_HEADS, HGRN_DK, 1),
      rowv(hs[:, 2 * width:3 * width]), rowv(hs[:, 3 * width:]), norm_g.reshape(HGRN_HEADS, 1, HGRN_DK), state)
    return o.reshape(db, width), nst


PEER_EBLK = 1024
PEER_RANKS = PEER_TOPK + 1
PEER_VROWS = 24
PEER_SUB = SUBLANES
PEER_ROWS = 16


def _extract_max(tiles):
    mx = functools.reduce(jnp.maximum, tiles)
    mx = jnp.max(mx, axis=0, keepdims=True)
    return mx, [jnp.where(x == mx, FLOOR, x) for x in tiles]


def _peer_kernel(x_ref, q_ref, k1_ref, k2_ref, u_ref, v_ref, g_ref, b_ref, o_ref,
                 xt_ref, acc_ref, act_ref, p_ref, a_ref, bw_ref, c_ref, s2_ref, v1_ref, v2_ref, ab_ref, cb_ref):
    e = pl.program_id(1)
    tt = x_ref.shape[0]
    n_lane_chunks = tt // LANES
    sub = PEER_SUB
    n_sub = PEER_NKEYS // sub
    keys_per_step = PEER_EBLK // PEER_NKEYS

    @pl.when(e == 0)
    def _():
        xt_ref[...] = x_ref[...].T.astype(bf16)
        acc_ref[...] = jnp.zeros(acc_ref.shape, f32)
        floor_rows = jnp.full((PEER_VROWS, LANES), FLOOR, f32)
        for h in range(PEER_HEADS):
            q1 = q_ref[:, (2 * h) * PEER_DHALF:(2 * h + 1) * PEER_DHALF]
            q2 = q_ref[:, (2 * h + 1) * PEER_DHALF:(2 * h + 2) * PEER_DHALF]
            s1 = lax.dot_general(k1_ref[h], q1, NT_DIMS, precision=HI, preferred_element_type=f32)
            c_ref[h] = s1.reshape(n_sub, sub, tt)
            s2_ref[h] = lax.dot_general(k2_ref[h], q2, NT_DIMS, precision=HI, preferred_element_type=f32)

        def per_head(h, carry):
            for lc in range(n_lane_chunks):
                lanes = slice(lc * LANES, (lc + 1) * LANES)
                for side, vals_ref in enumerate((v1_ref, v2_ref)):
                    vals_ref[...] = floor_rows
                    if side == 0:
                        tiles = [c_ref[h, k, :, lanes] for k in range(n_sub)]
                    else:
                        tiles = [s2_ref[h, k * sub:(k + 1) * sub, lanes] for k in range(n_sub)]
                    for r in range(PEER_RANKS):
                        mx, tiles = _extract_max(tiles)
                        vals_ref[r:r + 1, :] = mx
                cands = [v1_ref[0:1, :] + v2_ref[r:r + sub, :] for r in range(0, PEER_VROWS, sub)]
                cands += [v1_ref[r:r + 1, :] + v2_ref[0:sub, :] for r in range(1, sub)]
                cands += [v1_ref[r:r + sub, :] + v2_ref[0:1, :] for r in range(sub, PEER_VROWS, sub)]
                best = v1_ref[0:1, :] + v2_ref[0:1, :]
                zsum = jnp.zeros((1, LANES), f32)
                kth = best
                for r in range(PEER_TOPK):
                    kth, cands = _extract_max(cands)
                    zsum = zsum + jnp.exp(kth - best)
                nxt, _ = _extract_max(cands)
                thresh = 0.5 * (kth + nxt)
                s1 = c_ref[h, :, :, lanes]
                a_ref[h, :, :, lanes] = jnp.exp(s1 - v1_ref[0:1, :]) / zsum
                bw_ref[h, :, lanes] = jnp.exp(s2_ref[h, :, lanes] - v2_ref[0:1, :])
                c_ref[h, :, :, lanes] = thresh - s1
            return carry

        lax.fori_loop(0, PEER_HEADS, per_head, 0)

    act_ref[...] = jnp.dot(u_ref[0], xt_ref[...], preferred_element_type=f32)
    for h in range(PEER_HEADS):
        for ii in range(keys_per_step):
            ab_ref[ii, h] = jnp.broadcast_to(a_ref[h, e, ii:ii + 1, :], (sub, tt))
            cb_ref[ii, h] = jnp.broadcast_to(c_ref[h, e, ii:ii + 1, :], (sub, tt))

    def per_second_keys(k, carry):
        r0 = pl.multiple_of(k * PEER_ROWS, PEER_ROWS)
        for lc in range(n_lane_chunks):
            lanes = slice(lc * LANES, (lc + 1) * LANES)
            w = [None] * keys_per_step
            for h in range(PEER_HEADS):
                s2 = s2_ref[h, pl.ds(r0, PEER_ROWS), lanes].reshape(PEER_ROWS // sub, sub, LANES)
                bw = bw_ref[h, pl.ds(r0, PEER_ROWS), lanes].reshape(PEER_ROWS // sub, sub, LANES)
                for ii in range(keys_per_step):
                    hit = s2 >= cb_ref[ii, h, :, lanes][None]
                    term = ab_ref[ii, h, :, lanes][None] * jnp.where(hit, bw, 0.0)
                    w[ii] = term if h == 0 else w[ii] + term
            for ii in range(keys_per_step):
                rows = pl.ds(pl.multiple_of(ii * PEER_NKEYS + r0, PEER_ROWS), PEER_ROWS)
                act = act_ref[rows, lanes]
                gelu = 0.5 * act * (1.0 + lax.erf(act * (2.0 ** -0.5)))
                p_ref[rows, lanes] = (w[ii].reshape(PEER_ROWS, LANES) * gelu).astype(bf16)
        return carry

    lax.fori_loop(0, PEER_NKEYS // PEER_ROWS, per_second_keys, 0)
    acc_ref[...] += lax.dot_general(p_ref[...], v_ref[0], TN_DIMS, preferred_element_type=f32)

    @pl.when(e == pl.num_programs(1) - 1)
    def _():
        o_ref[...] = _layer_norm(DN_ALPHA * x_ref[...] + acc_ref[...], g_ref[...], b_ref[...])


def _peer_ln(x, wq, keys, u, v, layer, g, b, *, tt=TOKEN_TILE):
    m, d = x.shape
    n_exp = u.shape[1]
    q = _mm(x, wq, split=True, tm=tt)
    nq = q.shape[1]
    return pl.pallas_call(
        _peer_kernel,
        grid=(m // tt, n_exp // PEER_EBLK),
        in_specs=[pl.BlockSpec((tt, d), lambda t, e: (t, 0)),
                  pl.BlockSpec((tt, nq), lambda t, e: (t, 0)),
                  pl.BlockSpec((PEER_HEADS, PEER_NKEYS, PEER_DHALF), lambda t, e: (0, 0, 0)),
                  pl.BlockSpec((PEER_HEADS, PEER_NKEYS, PEER_DHALF), lambda t, e: (0, 0, 0)),
                  pl.BlockSpec((1, PEER_EBLK, d), lambda t, e: (layer, e, 0)),
                  pl.BlockSpec((1, PEER_EBLK, d), lambda t, e: (layer, e, 0)),
                  pl.BlockSpec((1, d), lambda t, e: (0, 0)),
                  pl.BlockSpec((1, d), lambda t, e: (0, 0))],
        out_specs=pl.BlockSpec((tt, d), lambda t, e: (t, 0)),
        out_shape=jax.ShapeDtypeStruct((m, d), f32),
        scratch_shapes=[pltpu.VMEM((d, tt), bf16), pltpu.VMEM((tt, d), f32),
                        pltpu.VMEM((PEER_EBLK, tt), f32), pltpu.VMEM((PEER_EBLK, tt), bf16)]
                       + [pltpu.VMEM((PEER_HEADS, PEER_NKEYS // PEER_SUB, PEER_SUB, tt), f32),
                          pltpu.VMEM((PEER_HEADS, PEER_NKEYS, tt), f32),
                          pltpu.VMEM((PEER_HEADS, PEER_NKEYS // PEER_SUB, PEER_SUB, tt), f32),
                          pltpu.VMEM((PEER_HEADS, PEER_NKEYS, tt), f32)]
                       + [pltpu.VMEM((PEER_VROWS, LANES), f32)] * 2
                       + [pltpu.VMEM((PEER_EBLK // PEER_NKEYS, PEER_HEADS, PEER_SUB, tt), f32)] * 2,
        compiler_params=_params(("parallel", "arbitrary")),
        name="peer",
    )(x, q, keys[0], keys[1], u, v, g.reshape(1, d), b.reshape(1, d))


def kernel(x_prompt, x_sample, cache_k, cache_v, state_conv, state_hgrn, page_table, w_in_even, conv_w, conv_b, conv_ln_g, conv_ln_b, w_out_even, w_in_odd, hgrn_lb_logits, hgrn_norm_g, w_out_odd, ln_g, ln_b, peer_wq, peer_keys, peer_u, peer_v):
    batch, seq, d = x_prompt.shape
    db = x_sample.shape[0]
    n_prompt = batch * seq
    pad_sample = lambda a: jnp.pad(a, ((0, SAMPLE_TILE - a.shape[0]), (0, 0)))
    pages = lambda a: a.reshape(batch, seq // PAGE_SIZE, MOBA_HEADS, MOBA_HEAD_DIM, PAGE_SIZE).transpose(0, 1, 4, 2, 3)
    heads = lambda a: a.reshape(db, MOBA_HEADS, MOBA_HEAD_DIM)

    lb_p = jax.nn.softmax(hgrn_lb_logits.astype(f32), axis=0)
    lb_all = jnp.cumsum(lb_p, axis=0) - lb_p[0]

    peer_u16, peer_v16 = peer_u.astype(bf16), peer_v.astype(bf16)
    xp = x_prompt.reshape(n_prompt, d)
    xs = pad_sample(x_sample.reshape(db, d))
    kp_l, vp_l, ks_l, vs_l, cp_l, cs_l, hp_l, hs_l = [], [], [], [], [], [], [], []
    for l in range(DEPTH):
        if l % 2 == 0:
            e = l // 2
            w_qk, w_vag = w_in_even[e][:, :2 * MOBA_WIDTH], w_in_even[e][:, 2 * MOBA_WIDTH:]
            qk_p, k_pages = _mm(xp, w_qk, split=True, paged_chunk=1)
            vag_p, v_pages = _mm(xp, w_vag, paged_chunk=0)
            qk_s, vag_s = _mm(xs, w_qk, split=True, tm=SAMPLE_TILE)[:db], _mm(xs, w_vag, tm=SAMPLE_TILE)[:db]
            kp_l.append(pages(k_pages))
            vp_l.append(pages(v_pages))
            q_s, k_s, v_s = qk_s[:, :MOBA_WIDTH], qk_s[:, MOBA_WIDTH:], vag_s[:, :MOBA_WIDTH]
            ks_l.append(k_s.reshape(db, 1, MOBA_HEADS, MOBA_HEAD_DIM))
            vs_l.append(v_s.reshape(db, 1, MOBA_HEADS, MOBA_HEAD_DIM))
            att_p = _moba_prompt(qk_p, vag_p, batch, seq)
            att_s = _moba_sample(heads(q_s), heads(k_s), heads(v_s),
                                 cache_k[e].transpose(0, 2, 3, 1), cache_v[e].transpose(0, 2, 3, 1), page_table)
            att_s = att_s[:, :, :MOBA_HEAD_DIM].reshape(db, MOBA_WIDTH)
            cprm = (conv_w[e], conv_b[e], conv_ln_g[e], conv_ln_b[e])
            cy_p, buf_p = _conv_prompt(vag_p, batch, seq, *cprm)
            cy_s, buf_s = _conv_sample(vag_s[:, MOBA_WIDTH:MOBA_WIDTH + CONV_CH], vag_s[:, MOBA_WIDTH + CONV_CH:],
                                       state_conv[e], *cprm)
            cp_l.append(buf_p)
            cs_l.append(buf_s)
            mix_p, mix_s = [att_p, cy_p], [pad_sample(att_s), pad_sample(cy_s)]
            w_out = w_out_even[e]
        else:
            oi = l // 2
            hproj_p = _mm(xp, w_in_odd[oi], tm=WIDE_OUT_TILE)
            hproj_s = _mm(xs, w_in_odd[oi], tm=SAMPLE_TILE)[:db]
            o_p, st_p = _hgrn_prompt(hproj_p, batch, seq, lb_all[l], hgrn_norm_g[oi])
            o_s, st_s = _hgrn_sample(hproj_s, state_hgrn[oi].astype(f32), lb_all[l], hgrn_norm_g[oi])
            hp_l.append(st_p)
            hs_l.append(st_s)
            mix_p, mix_s = [o_p], [pad_sample(o_s)]
            w_out = w_out_odd[oi]
        xp = _mm_res_ln(mix_p, w_out, xp, ln_g[l, 0], ln_b[l, 0])
        xs = _mm_res_ln(mix_s, w_out, xs, ln_g[l, 0], ln_b[l, 0], tm=SAMPLE_TILE)
        prm = (peer_wq[l], peer_keys[l], peer_u16, peer_v16, l, ln_g[l, 1], ln_b[l, 1])
        xp = _peer_ln(xp, *prm)
        xs = _peer_ln(xs, *prm, tt=SAMPLE_TILE)
    y_prompt = xp.reshape(batch, seq, d)
    y_sample = xs[:db].reshape(db, 1, d)
    return (y_prompt, y_sample, jnp.stack(kp_l), jnp.stack(vp_l), jnp.stack(ks_l), jnp.stack(vs_l),
            jnp.stack(cp_l), jnp.stack(cs_l), jnp.stack(hp_l), jnp.stack(hs_l))
```

```python
import functools

import jax
import jax.numpy as jnp
from jax import lax
from jax.experimental import pallas as pl
from jax.experimental.pallas import tpu as pltpu

f32 = jnp.float32
bf16 = jnp.bfloat16
i32 = jnp.int32
HI = lax.Precision.HIGHEST

D_MODEL = 1024
DEPTH = 2
PAGE_SIZE = 128
MOBA_HEADS = 8
MOBA_HEAD_DIM = 64
MOBA_WIDTH = MOBA_HEADS * MOBA_HEAD_DIM
MOBA_BLOCK = 256
MOBA_TOPK = 3
MOBA_SCALE = MOBA_HEAD_DIM ** -0.5
CONV_CH = D_MODEL // 2
CONV_WIDTH = 31
HGRN_HEADS = 8
HGRN_DK = D_MODEL // HGRN_HEADS
HGRN_CHUNK = 128
HGRN_SUB = 16
PEER_HEADS = 8
PEER_NKEYS = 128
PEER_DHALF = 128
PEER_TOPK = 16
LN_EPS = 1e-5
RMS_EPS = 1e-6
DN_ALPHA = (2 * DEPTH) ** 0.25

LANES = 128
TOKEN_TILE = 512
SAMPLE_TILE = 128
WIDE_OUT_TILE = 256
MASK_NEG = -1e30
FLOOR = -3e38
EXP_CLAMP = 60.0
VMEM_LIMIT = 56 << 20

NT_DIMS = (((1,), (1,)), ((), ()))
TN_DIMS = (((0,), (0,)), ((), ()))


def _params(semantics, vmem=VMEM_LIMIT):
    return pltpu.CompilerParams(dimension_semantics=semantics, vmem_limit_bytes=vmem)


def _layer_norm(y, g, b):
    mu = jnp.mean(y, axis=-1, keepdims=True)
    d = y - mu
    var = jnp.mean(d * d, axis=-1, keepdims=True)
    return d * lax.rsqrt(var + LN_EPS) * g + b


def _sigmoid(x):
    return 1.0 / (1.0 + jnp.exp(-x))


MM_CHUNK = 512


def _mm_kernel(x_ref, *refs, split, paged_chunk):
    refs = list(refs)
    t_ref = refs.pop() if paged_chunk is not None else None
    o_ref = refs.pop()
    wh_ref = refs[0]
    x = x_ref[...]
    xh = x.astype(bf16)
    if split:
        xl = (x - xh.astype(f32)).astype(bf16)
    for c, j in enumerate(range(0, o_ref.shape[1], MM_CHUNK)):
        wh = wh_ref[:, j:j + MM_CHUNK]
        acc = jnp.dot(xh, wh, preferred_element_type=f32)
        if split:
            acc = acc + jnp.dot(xl, wh, preferred_element_type=f32)
            acc = acc + jnp.dot(xh, refs[1][:, j:j + MM_CHUNK], preferred_element_type=f32)
        o_ref[:, j:j + MM_CHUNK] = acc
        if c == paged_chunk:
            for pg in range(o_ref.shape[0] // PAGE_SIZE):
                t_ref[pg] = acc[pg * PAGE_SIZE:(pg + 1) * PAGE_SIZE, :].T


def _mm(x, w, *, split=False, tm=TOKEN_TILE, paged_chunk=None):
    m, k = x.shape
    n = w.shape[1]
    wh = w.astype(bf16)
    ws = [wh] + ([(w - wh.astype(f32)).astype(bf16)] if split else [])
    out_specs = [pl.BlockSpec((tm, n), lambda i: (i, 0))]
    out_shape = [jax.ShapeDtypeStruct((m, n), f32)]
    if paged_chunk is not None:
        out_specs.append(pl.BlockSpec((tm // PAGE_SIZE, MM_CHUNK, PAGE_SIZE), lambda i: (i, 0, 0)))
        out_shape.append(jax.ShapeDtypeStruct((m // PAGE_SIZE, MM_CHUNK, PAGE_SIZE), f32))
    outs = pl.pallas_call(
        functools.partial(_mm_kernel, split=split, paged_chunk=paged_chunk),
        grid=(m // tm,),
        in_specs=[pl.BlockSpec((tm, k), lambda i: (i, 0))] + [pl.BlockSpec((k, n), lambda i: (0, 0))] * len(ws),
        out_specs=out_specs,
        out_shape=out_shape,
        compiler_params=_params(("parallel",)),
        name="proj_split" if split else "proj",
    )(x, *ws)
    return outs[0] if paged_chunk is None else outs


def _mm_res_ln_kernel(*refs):
    *a_refs, w_ref, x_ref, g_ref, b_ref, o_ref = refs
    acc = DN_ALPHA * x_ref[...]
    row = 0
    for a_ref in a_refs:
        k = a_ref.shape[1]
        acc = acc + jnp.dot(a_ref[...].astype(bf16), w_ref[row:row + k, :], preferred_element_type=f32)
        row += k
    o_ref[...] = _layer_norm(acc, g_ref[...], b_ref[...])


def _mm_res_ln(parts, w, x, g, b, *, tm=TOKEN_TILE):
    m, n = x.shape
    k = w.shape[0]
    return pl.pallas_call(
        _mm_res_ln_kernel,
        grid=(m // tm,),
        in_specs=[pl.BlockSpec((tm, a.shape[1]), lambda i: (i, 0)) for a in parts]
                 + [pl.BlockSpec((k, n), lambda i: (0, 0)),
                    pl.BlockSpec((tm, n), lambda i: (i, 0)), pl.BlockSpec((1, n), lambda i: (0, 0)),
                    pl.BlockSpec((1, n), lambda i: (0, 0))],
        out_specs=pl.BlockSpec((tm, n), lambda i: (i, 0)),
        out_shape=jax.ShapeDtypeStruct((m, n), f32),
        compiler_params=_params(("parallel",)),
        name="out_proj_ln",
    )(*parts, w.astype(bf16), x, g.reshape(1, n), b.reshape(1, n))


CONV_TILE = 256
CONV_ROWS = 64
CONV_HIST = 32
SUBLANES = 8
CONV_SHIFT_ROWS = CONV_TILE + CONV_HIST - SUBLANES


def _conv_prompt_kernel(a_ref, gt_ref, w_ref, cb_ref, g_ref, b_ref, y_ref, st_ref, buf_ref, shift_ref):
    t = pl.program_id(1)

    @pl.when(t == 0)
    def _():
        buf_ref[0:CONV_HIST, :] = jnp.zeros((CONV_HIST, CONV_CH), f32)

    buf_ref[CONV_HIST:CONV_HIST + CONV_TILE, :] = a_ref[...] * _sigmoid(gt_ref[...])
    for q in range(1, SUBLANES):
        shift_ref[q - 1] = buf_ref[q:q + CONV_SHIFT_ROWS, :]
    first = CONV_HIST - (CONV_WIDTH - 1)
    for r in range(0, CONV_TILE, CONV_ROWS):
        acc = jnp.zeros((CONV_ROWS, CONV_CH), f32) + cb_ref[...]
        for j in range(CONV_WIDTH):
            q = (first + j) % SUBLANES
            base = r + first + j - q
            rows = buf_ref[base:base + CONV_ROWS, :] if q == 0 else shift_ref[q - 1, base:base + CONV_ROWS, :]
            acc = acc + w_ref[j:j + 1, :] * rows
        y = _layer_norm(acc, g_ref[...], b_ref[...])
        y_ref[r:r + CONV_ROWS, :] = y * _sigmoid(y)
    tail = buf_ref[CONV_TILE:CONV_TILE + CONV_HIST, :]
    buf_ref[0:CONV_HIST, :] = tail

    @pl.when(t == pl.num_programs(1) - 1)
    def _():
        st_ref[0] = tail


def _conv_prompt(vag, batch, seq, w, cb, g, b):
    nt = seq // CONV_TILE
    wpad = jnp.zeros((CONV_HIST, CONV_CH), f32).at[:CONV_WIDTH].set(w)
    row = lambda v: v.reshape(1, CONV_CH)
    y, st = pl.pallas_call(
        _conv_prompt_kernel,
        grid=(batch, nt),
        in_specs=[pl.BlockSpec((CONV_TILE, CONV_CH), lambda bi, t: (bi * nt + t, 1)),
                  pl.BlockSpec((CONV_TILE, CONV_CH), lambda bi, t: (bi * nt + t, 2)),
                  pl.BlockSpec((CONV_HIST, CONV_CH), lambda bi, t: (0, 0))]
                 + [pl.BlockSpec((1, CONV_CH), lambda bi, t: (0, 0))] * 3,
        out_specs=[pl.BlockSpec((CONV_TILE, CONV_CH), lambda bi, t: (bi * nt + t, 0)),
                   pl.BlockSpec((1, CONV_HIST, CONV_CH), lambda bi, t: (bi, 0, 0))],
        out_shape=[jax.ShapeDtypeStruct((batch * seq, CONV_CH), f32),
                   jax.ShapeDtypeStruct((batch, CONV_HIST, CONV_CH), f32)],
        scratch_shapes=[pltpu.VMEM((CONV_TILE + CONV_HIST, CONV_CH), f32),
                        pltpu.VMEM((SUBLANES - 1, CONV_SHIFT_ROWS, CONV_CH), f32)],
        compiler_params=_params(("arbitrary", "arbitrary")),
        name="conv_prompt",
    )(vag, vag, wpad, row(cb), row(g), row(b))
    return y, st[:, CONV_HIST - (CONV_WIDTH - 1):]


def _conv_sample_kernel(a_ref, gt_ref, st_ref, w_ref, cb_ref, g_ref, b_ref, y_ref, nst_ref):
    u = a_ref[...] * _sigmoid(gt_ref[...])
    acc = cb_ref[...] + w_ref[CONV_WIDTH - 1:CONV_WIDTH, :] * u
    for j in range(CONV_WIDTH - 1):
        acc = acc + w_ref[j:j + 1, :] * st_ref[j]
    y = _layer_norm(acc, g_ref[...], b_ref[...])
    y_ref[...] = y * _sigmoid(y)
    for j in range(CONV_WIDTH - 2):
        nst_ref[j] = st_ref[j + 1]
    nst_ref[CONV_WIDTH - 2] = u


def _conv_sample(a, gt, state, w, cb, g, b):
    db = a.shape[0]
    wpad = jnp.zeros((CONV_HIST, CONV_CH), f32).at[:CONV_WIDTH].set(w)
    row = lambda v: v.reshape(1, CONV_CH)
    y, nst = pl.pallas_call(
        _conv_sample_kernel,
        out_shape=[jax.ShapeDtypeStruct((db, CONV_CH), f32),
                   jax.ShapeDtypeStruct((CONV_WIDTH - 1, db, CONV_CH), f32)],
        name="conv_sample",
    )(a, gt, state.transpose(1, 0, 2), wpad, row(cb), row(g), row(b))
    return y, nst.transpose(1, 0, 2)


GATE_ROWS = 1024
ATTN_KEYS = 2 * MOBA_BLOCK


def _block_sum_kernel(k_ref, o_ref):
    o_ref[0] = jnp.sum(k_ref[...], axis=0, keepdims=True)


def _block_sums(qk, batch, seq):
    nb = seq // MOBA_BLOCK
    return pl.pallas_call(
        _block_sum_kernel,
        grid=(batch * nb,),
        in_specs=[pl.BlockSpec((MOBA_BLOCK, MOBA_WIDTH), lambda i: (i, 1))],
        out_specs=pl.BlockSpec((1, 1, MOBA_WIDTH), lambda i: (i, 0, 0)),
        out_shape=jax.ShapeDtypeStruct((batch * nb, 1, MOBA_WIDTH), f32),
        compiler_params=_params(("parallel",)),
        name="moba_block_sums",
    )(qk)


def _moba_gate_kernel(q_ref, k_ref, v_ref, ks_ref, qa_ref, ka_ref, va_ref, *, nb):
    t = pl.program_id(2)
    lane = lax.broadcasted_iota(i32, (MOBA_BLOCK, LANES), 1)
    blk = lane - MOBA_HEAD_DIM
    km_lane = lax.broadcasted_iota(i32, (nb, LANES), 1)
    ksum = ks_ref[0] * (1.0 / MOBA_BLOCK)
    pad_top = jnp.zeros((MOBA_HEAD_DIM, LANES), f32)
    pad_bot = jnp.zeros((LANES - MOBA_HEAD_DIM - nb, LANES), f32)
    for c in range(GATE_ROWS // MOBA_BLOCK):
        own = t * (GATE_ROWS // MOBA_BLOCK) + c
        rows = slice(c * MOBA_BLOCK, (c + 1) * MOBA_BLOCK)
        q2 = q_ref[rows, :]
        k2 = k_ref[rows, :]
        v2 = v_ref[rows, :]
        cand = (blk >= 0) & (blk < own)
        for j in range(2):
            head = (km_lane >= j * MOBA_HEAD_DIM) & (km_lane < (j + 1) * MOBA_HEAD_DIM)
            km = jnp.concatenate([pad_top, jnp.where(head, ksum, 0.0), pad_bot], axis=0)
            gate = lax.dot_general(q2, km, NT_DIMS, precision=HI, preferred_element_type=f32)
            masked = jnp.where(cand, gate, FLOOR)
            sel = jnp.zeros((MOBA_BLOCK, LANES), f32)
            for _ in range(MOBA_TOPK):
                mx = jnp.max(masked, axis=1, keepdims=True)
                hit = (masked == mx) & cand
                sel = jnp.where(hit, 1.0, sel)
                masked = jnp.where(hit, FLOOR, masked)
            bias = jnp.where((sel > 0.0) | (blk == own), 0.0, MASK_NEG)
            qj = q2 if j == 0 else pltpu.roll(q2, MOBA_HEAD_DIM, 1)
            kj = k2 if j == 0 else pltpu.roll(k2, MOBA_HEAD_DIM, 1)
            vj = v2 if j == 0 else pltpu.roll(v2, MOBA_HEAD_DIM, 1)
            in_head = lane < MOBA_HEAD_DIM
            in_bias = lane < MOBA_HEAD_DIM + nb
            qa = jnp.where(in_head, qj * MOBA_SCALE, jnp.where(in_bias, bias, 0.0))
            ka = jnp.where(in_head, kj, jnp.where(blk == own, 1.0, 0.0))
            va = jnp.where(in_head, vj, jnp.where(blk == 0, 1.0, 0.0))
            qa_ref[0, j, rows, :] = qa.astype(bf16)
            ka_ref[0, j, rows, :] = ka.astype(bf16)
            va_ref[0, j, rows, :] = va.astype(bf16)


def _moba_attn_kernel(qa_ref, ka_ref, va_ref, o_ref):
    i = pl.program_id(1)
    last = i // 2
    row = lax.broadcasted_iota(i32, (MOBA_BLOCK, ATTN_KEYS), 0)
    col = lax.broadcasted_iota(i32, (MOBA_BLOCK, ATTN_KEYS), 1)
    lane = lax.broadcasted_iota(i32, (MOBA_BLOCK, LANES), 1)

    def tile(h, k0, mask):
        s = lax.dot_general(qa_ref[0, h], ka_ref[0, h, pl.ds(k0, ATTN_KEYS), :], NT_DIMS, preferred_element_type=f32)
        return s if mask is None else jnp.where(mask, s, MASK_NEG)

    def weighted(p, h, k0):
        return jnp.dot(p.astype(bf16), va_ref[0, h, pl.ds(k0, ATTN_KEYS), :], preferred_element_type=f32)

    k_last = pl.multiple_of(last * ATTN_KEYS, ATTN_KEYS)
    visible = (k_last + col) <= (i * MOBA_BLOCK + row)
    state = []
    for h in range(MOBA_HEADS):
        s = tile(h, k_last, visible)
        m = jnp.max(s, axis=1, keepdims=True)
        state += [m, weighted(jnp.exp(s - m), h, k_last)]

    def body(n, carry):
        k0 = pl.multiple_of(n * ATTN_KEYS, ATTN_KEYS)
        out = []
        for h in range(MOBA_HEADS):
            m, acc = carry[2 * h:2 * h + 2]
            s = tile(h, k0, None)
            mn = jnp.maximum(m, jnp.max(s, axis=1, keepdims=True))
            out += [mn, jnp.exp(m - mn) * acc + weighted(jnp.exp(s - mn), h, k0)]
        return tuple(out)

    final = lax.fori_loop(0, last, body, tuple(state))
    for pair in range(MOBA_HEADS // 2):
        acc0, acc1 = final[4 * pair + 1], final[4 * pair + 3]
        out0 = acc0 / acc0[:, MOBA_HEAD_DIM:MOBA_HEAD_DIM + 1]
        out1 = acc1 / acc1[:, MOBA_HEAD_DIM:MOBA_HEAD_DIM + 1]
        o_ref[:, pair * LANES:(pair + 1) * LANES] = jnp.where(lane < MOBA_HEAD_DIM, out0,
                                                              pltpu.roll(out1, MOBA_HEAD_DIM, 1))


def _moba_prompt(qk, vag, batch, seq):
    nb = seq // MOBA_BLOCK
    npair = MOBA_WIDTH // LANES
    nt = seq // GATE_ROWS
    ksums = _block_sums(qk, batch, seq).reshape(batch, nb, MOBA_WIDTH)
    per_head = pl.BlockSpec((1, 2, GATE_ROWS, LANES), lambda b, p, t: (b, p, t, 0))
    qa, ka, va = pl.pallas_call(
        functools.partial(_moba_gate_kernel, nb=nb),
        grid=(batch, npair, nt),
        in_specs=[pl.BlockSpec((GATE_ROWS, LANES), lambda b, p, t: (b * nt + t, p)),
                  pl.BlockSpec((GATE_ROWS, LANES), lambda b, p, t: (b * nt + t, npair + p)),
                  pl.BlockSpec((GATE_ROWS, LANES), lambda b, p, t: (b * nt + t, p)),
                  pl.BlockSpec((1, nb, LANES), lambda b, p, t: (b, 0, p))],
        out_specs=[per_head] * 3,
        out_shape=[jax.ShapeDtypeStruct((batch, MOBA_HEADS, seq, LANES), bf16)] * 3,
        compiler_params=_params(("parallel", "parallel", "parallel")),
        name="moba_gate",
    )(qk, qk, vag, ksums)
    return pl.pallas_call(
        _moba_attn_kernel,
        grid=(batch, nb),
        in_specs=[pl.BlockSpec((1, MOBA_HEADS, MOBA_BLOCK, LANES), lambda b, i: (b, 0, i, 0)),
                  pl.BlockSpec((1, MOBA_HEADS, seq, LANES), lambda b, i: (b, 0, 0, 0),
                               pipeline_mode=pl.Buffered(1)),
                  pl.BlockSpec((1, MOBA_HEADS, seq, LANES), lambda b, i: (b, 0, 0, 0),
                               pipeline_mode=pl.Buffered(1))],
        out_specs=pl.BlockSpec((MOBA_BLOCK, MOBA_WIDTH), lambda b, i: (b * nb + i, 0)),
        out_shape=jax.ShapeDtypeStruct((batch * seq, MOBA_WIDTH), f32),
        compiler_params=_params(("parallel", "arbitrary")),
        name="moba_attn",
    )(qa, ka, va)


PAGES_PER_STEP = 8
PAGES_PER_BLOCK = MOBA_BLOCK // PAGE_SIZE
BLOCKS_PER_STEP = PAGES_PER_STEP // PAGES_PER_BLOCK


def _moba_sample_stream_kernel(pt_ref, qb_ref, *refs):
    del pt_ref
    k_refs = refs[:PAGES_PER_STEP]
    v_refs = refs[PAGES_PER_STEP:2 * PAGES_PER_STEP]
    gate_ref, m_ref, l_ref, acc_ref = refs[2 * PAGES_PER_STEP:]
    qb = qb_ref[0]
    head = lax.broadcasted_iota(i32, (MOBA_HEADS, LANES), 0)
    tile = (MOBA_HEADS, LANES)
    for jj in range(BLOCKS_PER_STEP):
        pages = range(PAGES_PER_BLOCK * jj, PAGES_PER_BLOCK * (jj + 1))
        raw = [jnp.sum(k_refs[x][0] * qb, axis=1) for x in pages]
        gate = functools.reduce(jnp.add, [jnp.sum(r, axis=1, keepdims=True) for r in raw]) * (1.0 / MOBA_BLOCK)
        m = functools.reduce(jnp.maximum, [jnp.max(r, axis=1, keepdims=True) for r in raw]) * MOBA_SCALE
        l = jnp.zeros((MOBA_HEADS, 1), f32)
        pv = jnp.zeros((MOBA_HEADS, MOBA_WIDTH), f32)
        for r, x in zip(raw, pages):
            p = jnp.exp(r * MOBA_SCALE - m)
            l = l + jnp.sum(p, axis=1, keepdims=True)
            vt = v_refs[x][0].reshape(MOBA_WIDTH, PAGE_SIZE).astype(bf16)
            pv = pv + lax.dot_general(p.astype(bf16), vt, NT_DIMS, preferred_element_type=f32)
        acc = jnp.zeros(tile, f32)
        for pair in range(MOBA_WIDTH // LANES):
            both = pv[:, pair * LANES:(pair + 1) * LANES]
            acc = acc + jnp.where(head == 2 * pair, both, 0.0)
            acc = acc + jnp.where(head == 2 * pair + 1, pltpu.roll(both, MOBA_HEAD_DIM, 1), 0.0)
        gate_ref[0, 0, jj] = jnp.broadcast_to(gate, tile)
        m_ref[0, 0, jj] = jnp.broadcast_to(m, tile)
        l_ref[0, 0, jj] = jnp.broadcast_to(l, tile)
        acc_ref[0, 0, jj] = acc


def _moba_sample_combine_kernel(gate_ref, m_ref, l_ref, acc_ref, q_ref, kn_ref, vn_ref, o_ref):
    db, nb, nh, width = gate_ref.shape
    masked = gate_ref[...]
    sel = jnp.zeros((db, nb, nh, width), f32)
    for _ in range(MOBA_TOPK):
        mx = jnp.max(masked, axis=1, keepdims=True)
        hit = masked == mx
        sel = jnp.where(hit, 1.0, sel)
        masked = jnp.where(hit, FLOOR, masked)
    chosen = sel > 0.0
    s_self = jnp.sum(q_ref[...] * kn_ref[...], axis=-1, keepdims=True) * MOBA_SCALE
    m = m_ref[...]
    top = jnp.maximum(jnp.max(jnp.where(chosen, m, FLOOR), axis=1), s_self)
    w = jnp.where(chosen, jnp.exp(m - top[:, None]), 0.0)
    w_self = jnp.exp(s_self - top)
    denom = jnp.sum(w * l_ref[...], axis=1) + w_self
    num = jnp.sum(w * acc_ref[...], axis=1) + w_self * vn_ref[...]
    o_ref[...] = num / denom


def _moba_sample(q, kn, vn, k_pool, v_pool, page_table):
    db, n_pages = page_table.shape
    nb = n_pages // PAGES_PER_BLOCK
    ng = n_pages // PAGES_PER_STEP
    page_block = (1, MOBA_HEADS, MOBA_HEAD_DIM, PAGE_SIZE)
    page_spec = lambda x: pl.BlockSpec(page_block, lambda b, g, pt, x=x: (pt[b, g * PAGES_PER_STEP + x], 0, 0, 0))
    stat = pl.BlockSpec((1, 1, BLOCKS_PER_STEP, MOBA_HEADS, LANES), lambda b, g, pt: (b, g, 0, 0, 0))
    stat_shape = jax.ShapeDtypeStruct((db, ng, BLOCKS_PER_STEP, MOBA_HEADS, LANES), f32)
    qb = jnp.broadcast_to(q[..., None], (db,) + page_block[1:])
    stats = pl.pallas_call(
        _moba_sample_stream_kernel,
        grid_spec=pltpu.PrefetchScalarGridSpec(
            num_scalar_prefetch=1,
            grid=(db, ng),
            in_specs=[pl.BlockSpec(page_block, lambda b, g, pt: (b, 0, 0, 0))]
                     + [page_spec(x) for x in range(PAGES_PER_STEP)] * 2,
            out_specs=[stat] * 4),
        out_shape=[stat_shape] * 4,
        compiler_params=_params(("parallel", "parallel")),
        name="moba_sample_stream",
    )(page_table, qb, *([k_pool] * PAGES_PER_STEP), *([v_pool] * PAGES_PER_STEP))
    gate, m, l, acc = (a.reshape(db, nb, MOBA_HEADS, LANES) for a in stats)
    vn_wide = jnp.pad(vn, ((0, 0), (0, 0), (0, LANES - MOBA_HEAD_DIM)))
    return pl.pallas_call(
        _moba_sample_combine_kernel,
        out_shape=jax.ShapeDtypeStruct((db, MOBA_HEADS, LANES), f32),
        compiler_params=pltpu.CompilerParams(vmem_limit_bytes=VMEM_LIMIT),
        name="moba_sample_combine",
    )(gate, m, l, acc, q, kn, vn_wide)


HGRN_TILE = 512


def _hgrn_prompt_kernel(q_ref, fz_ref, i_ref, g_ref, lb_ref, ng_ref, o_ref, st_ref, state_ref):
    t = pl.program_id(1)
    c_rows = HGRN_CHUNK

    @pl.when(t == 0)
    def _():
        state_ref[...] = jnp.zeros(state_ref.shape, f32)

    row = lax.broadcasted_iota(i32, (c_rows, c_rows), 0)
    col = lax.broadcasted_iota(i32, (c_rows, c_rows), 1)
    causal = col <= row
    cum = jnp.where(causal, 1.0, 0.0)
    cum_sub = jnp.where(col < (row // HGRN_SUB) * HGRN_SUB, 1.0, 0.0)
    cum_both16 = jnp.concatenate([cum, cum_sub], axis=0).astype(bf16)

    def chunk(c, carry):
        rows = pl.ds(pl.multiple_of(c * c_rows, c_rows), c_rows)
        lb_all = lb_ref[...]
        f_all = lb_all + (1.0 - lb_all) * _sigmoid(fz_ref[rows, :])
        log_f = jnp.log(f_all)
        both_all = jnp.zeros((2 * c_rows, log_f.shape[1]), f32)
        rest = log_f
        for _ in range(3):
            piece = rest.astype(bf16)
            both_all = both_all + jnp.dot(cum_both16, piece, preferred_element_type=f32)
            rest = rest - piece.astype(f32)
        for h in range(HGRN_HEADS):
            lanes = slice(h * HGRN_DK, (h + 1) * HGRN_DK)
            ng = ng_ref[:, lanes]
            q = q_ref[rows, lanes]
            v = i_ref[rows, lanes]
            gg = g_ref[rows, lanes]
            kk = 1.0 - f_all[:, lanes]
            b = both_all[:c_rows, lanes]
            ref_row = both_all[c_rows:, lanes]
            st = state_ref[h]
            o = lax.dot_general((q * jnp.exp(b)).astype(bf16), st.astype(bf16), NT_DIMS, preferred_element_type=f32)
            qh = (q * jnp.exp(b - ref_row)).astype(bf16)
            parts = []
            for s in range(c_rows // HGRN_SUB):
                ref_s = ref_row[s * HGRN_SUB:s * HGRN_SUB + 1, :]
                kh = (kk * jnp.exp(jnp.minimum(ref_s - b, EXP_CLAMP))).astype(bf16)
                parts.append(lax.dot_general(qh[s * HGRN_SUB:(s + 1) * HGRN_SUB], kh, NT_DIMS,
                                             preferred_element_type=f32))
            att = jnp.where(causal, jnp.concatenate(parts, axis=0), 0.0)
            o = o + jnp.dot(att.astype(bf16), v.astype(bf16), preferred_element_type=f32)
            b_last = b[c_rows - 1:c_rows, :]
            kd = (kk * jnp.exp(b_last - b)).astype(bf16)
            state_ref[h] = st * jnp.exp(b_last) + lax.dot_general(v.astype(bf16), kd, TN_DIMS,
                                                                   preferred_element_type=f32)
            ms = jnp.mean(o * o, axis=1, keepdims=True)
            o_ref[rows, lanes] = o * lax.rsqrt(ms + RMS_EPS) * ng * (gg * _sigmoid(gg))
        return carry

    lax.fori_loop(0, HGRN_TILE // c_rows, chunk, 0)

    @pl.when(t == pl.num_programs(1) - 1)
    def _():
        for h in range(HGRN_HEADS):
            st_ref[0, h] = state_ref[h].T


def _hgrn_prompt(hproj, batch, seq, lb, norm_g):
    nt = seq // HGRN_TILE
    width = HGRN_HEADS * HGRN_DK
    col = lambda j: pl.BlockSpec((HGRN_TILE, width), lambda b, t, j=j: (b * nt + t, j))
    vec = pl.BlockSpec((1, width), lambda b, t: (0, 0))
    return pl.pallas_call(
        _hgrn_prompt_kernel,
        grid=(batch, nt),
        in_specs=[col(0), col(1), col(2), col(3), vec, vec],
        out_specs=[pl.BlockSpec((HGRN_TILE, width), lambda b, t: (b * nt + t, 0)),
                   pl.BlockSpec((1, HGRN_HEADS, HGRN_DK, HGRN_DK), lambda b, t: (b, 0, 0, 0))],
        out_shape=[jax.ShapeDtypeStruct((batch * seq, width), f32),
                   jax.ShapeDtypeStruct((batch, HGRN_HEADS, HGRN_DK, HGRN_DK), f32)],
        scratch_shapes=[pltpu.VMEM((HGRN_HEADS, HGRN_DK, HGRN_DK), f32)],
        compiler_params=_params(("parallel", "arbitrary")),
        name="hgrn_prompt",
    )(hproj, hproj, hproj, hproj, lb.reshape(1, width), norm_g.reshape(1, width))


def _hgrn_sample_kernel(qc_ref, fzc_ref, lbc_ref, i_ref, g_ref, ng_ref, st_ref, o_ref, nst_ref):
    for h in range(HGRN_HEADS):
        lb = lbc_ref[h]
        f = lb + (1.0 - lb) * _sigmoid(fzc_ref[0, h])
        st = f * st_ref[0, h] + (1.0 - f) * i_ref[0, h]
        nst_ref[0, h] = st
        o = jnp.sum(qc_ref[0, h] * st, axis=0, keepdims=True)
        ms = jnp.mean(o * o, axis=1, keepdims=True)
        gg = g_ref[0, h]
        o_ref[0, h] = o * lax.rsqrt(ms + RMS_EPS) * ng_ref[h] * (gg * _sigmoid(gg))


def _hgrn_sample(hs, state, lb, norm_g):
    db = hs.shape[0]
    width = HGRN_HEADS * HGRN_DK
    colv = lambda x: x.reshape(db, HGRN_HEADS, HGRN_DK, 1)
    rowv = lambda x: x.reshape(db, HGRN_HEADS, 1, HGRN_DK)
    cspec = pl.BlockSpec((1, HGRN_HEADS, HGRN_DK, 1), lambda b: (b, 0, 0, 0))
    rspec = pl.BlockSpec((1, HGRN_HEADS, 1, HGRN_DK), lambda b: (b, 0, 0, 0))
    sspec = pl.BlockSpec((1, HGRN_HEADS, HGRN_DK, HGRN_DK), lambda b: (b, 0, 0, 0))
    o, nst = pl.pallas_call(
        _hgrn_sample_kernel,
        grid=(db,),
        in_specs=[cspec, cspec, pl.BlockSpec((HGRN_HEADS, HGRN_DK, 1), lambda b: (0, 0, 0)), rspec, rspec,
                  pl.BlockSpec((HGRN_HEADS, 1, HGRN_DK), lambda b: (0, 0, 0)), sspec],
        out_specs=[rspec, sspec],
        out_shape=[jax.ShapeDtypeStruct((db, HGRN_HEADS, 1, HGRN_DK), f32),
                   jax.ShapeDtypeStruct((db, HGRN_HEADS, HGRN_DK, HGRN_DK), f32)],
        compiler_params=_params(("parallel",)),
        name="hgrn_sample",
    )(colv(hs[:, :width]), colv(hs[:, width:2 * width]), lb.reshape(HGRN_HEADS, HGRN_DK, 1),
      rowv(hs[:, 2 * width:3 * width]), rowv(hs[:, 3 * width:]), norm_g.reshape(HGRN_HEADS, 1, HGRN_DK), state)
    return o.reshape(db, width), nst


PEER_EBLK = 1024
PEER_RANKS = PEER_TOPK + 1
PEER_VROWS = 24
PEER_SUB = SUBLANES
PEER_ROWS = 16


def _extract_max(tiles):
    mx = functools.reduce(jnp.maximum, tiles)
    mx = jnp.max(mx, axis=0, keepdims=True)
    return mx, [jnp.where(x == mx, FLOOR, x) for x in tiles]


def _peer_kernel(x_ref, q_ref, k1_ref, k2_ref, u_ref, v_ref, g_ref, b_ref, o_ref,
                 xt_ref, acc_ref, act_ref, p_ref, a_ref, bw_ref, c_ref, s2_ref, v1_ref, v2_ref, ab_ref, cb_ref):
    e = pl.program_id(1)
    tt = x_ref.shape[0]
    n_lane_chunks = tt // LANES
    sub = PEER_SUB
    n_sub = PEER_NKEYS // sub
    keys_per_step = PEER_EBLK // PEER_NKEYS

    @pl.when(e == 0)
    def _():
        xt_ref[...] = x_ref[...].T.astype(bf16)
        acc_ref[...] = jnp.zeros(acc_ref.shape, f32)
        floor_rows = jnp.full((PEER_VROWS, LANES), FLOOR, f32)
        for h in range(PEER_HEADS):
            q1 = q_ref[:, (2 * h) * PEER_DHALF:(2 * h + 1) * PEER_DHALF]
            q2 = q_ref[:, (2 * h + 1) * PEER_DHALF:(2 * h + 2) * PEER_DHALF]
            s1 = lax.dot_general(k1_ref[h], q1, NT_DIMS, precision=HI, preferred_element_type=f32)
            c_ref[h] = s1.reshape(n_sub, sub, tt)
            s2_ref[h] = lax.dot_general(k2_ref[h], q2, NT_DIMS, precision=HI, preferred_element_type=f32)

        def per_head(h, carry):
            for lc in range(n_lane_chunks):
                lanes = slice(lc * LANES, (lc + 1) * LANES)
                for side, vals_ref in enumerate((v1_ref, v2_ref)):
                    vals_ref[...] = floor_rows
                    if side == 0:
                        tiles = [c_ref[h, k, :, lanes] for k in range(n_sub)]
                    else:
                        tiles = [s2_ref[h, k * sub:(k + 1) * sub, lanes] for k in range(n_sub)]
                    for r in range(PEER_RANKS):
                        mx, tiles = _extract_max(tiles)
                        vals_ref[r:r + 1, :] = mx
                cands = [v1_ref[0:1, :] + v2_ref[r:r + sub, :] for r in range(0, PEER_VROWS, sub)]
                cands += [v1_ref[r:r + 1, :] + v2_ref[0:sub, :] for r in range(1, sub)]
                cands += [v1_ref[r:r + sub, :] + v2_ref[0:1, :] for r in range(sub, PEER_VROWS, sub)]
                best = v1_ref[0:1, :] + v2_ref[0:1, :]
                zsum = jnp.zeros((1, LANES), f32)
                kth = best
                for r in range(PEER_TOPK):
                    kth, cands = _extract_max(cands)
                    zsum = zsum + jnp.exp(kth - best)
                nxt, _ = _extract_max(cands)
                thresh = 0.5 * (kth + nxt)
                s1 = c_ref[h, :, :, lanes]
                a_ref[h, :, :, lanes] = jnp.exp(s1 - v1_ref[0:1, :]) / zsum
                bw_ref[h, :, lanes] = jnp.exp(s2_ref[h, :, lanes] - v2_ref[0:1, :])
                c_ref[h, :, :, lanes] = thresh - s1
            return carry

        lax.fori_loop(0, PEER_HEADS, per_head, 0)

    act_ref[...] = jnp.dot(u_ref[0], xt_ref[...], preferred_element_type=f32)
    for h in range(PEER_HEADS):
        for ii in range(keys_per_step):
            ab_ref[ii, h] = jnp.broadcast_to(a_ref[h, e, ii:ii + 1, :], (sub, tt))
            cb_ref[ii, h] = jnp.broadcast_to(c_ref[h, e, ii:ii + 1, :], (sub, tt))

    def per_second_keys(k, carry):
        r0 = pl.multiple_of(k * PEER_ROWS, PEER_ROWS)
        for lc in range(n_lane_chunks):
            lanes = slice(lc * LANES, (lc + 1) * LANES)
            w = [None] * keys_per_step
            for h in range(PEER_HEADS):
                s2 = s2_ref[h, pl.ds(r0, PEER_ROWS), lanes].reshape(PEER_ROWS // sub, sub, LANES)
                bw = bw_ref[h, pl.ds(r0, PEER_ROWS), lanes].reshape(PEER_ROWS // sub, sub, LANES)
                for ii in range(keys_per_step):
                    hit = s2 >= cb_ref[ii, h, :, lanes][None]
                    term = ab_ref[ii, h, :, lanes][None] * jnp.where(hit, bw, 0.0)
                    w[ii] = term if h == 0 else w[ii] + term
            for ii in range(keys_per_step):
                rows = pl.ds(pl.multiple_of(ii * PEER_NKEYS + r0, PEER_ROWS), PEER_ROWS)
                act = act_ref[rows, lanes]
                gelu = 0.5 * act * (1.0 + lax.erf(act * (2.0 ** -0.5)))
                p_ref[rows, lanes] = (w[ii].reshape(PEER_ROWS, LANES) * gelu).astype(bf16)
        return carry

    lax.fori_loop(0, PEER_NKEYS // PEER_ROWS, per_second_keys, 0)
    acc_ref[...] += lax.dot_general(p_ref[...], v_ref[0], TN_DIMS, preferred_element_type=f32)

    @pl.when(e == pl.num_programs(1) - 1)
    def _():
        o_ref[...] = _layer_norm(DN_ALPHA * x_ref[...] + acc_ref[...], g_ref[...], b_ref[...])


def _peer_ln(x, wq, keys, u, v, layer, g, b, *, split_q, tt=TOKEN_TILE):
    m, d = x.shape
    n_exp = u.shape[1]
    q = _mm(x, wq, split=split_q, tm=tt)
    nq = q.shape[1]
    return pl.pallas_call(
        _peer_kernel,
        grid=(m // tt, n_exp // PEER_EBLK),
        in_specs=[pl.BlockSpec((tt, d), lambda t, e: (t, 0)),
                  pl.BlockSpec((tt, nq), lambda t, e: (t, 0)),
                  pl.BlockSpec((PEER_HEADS, PEER_NKEYS, PEER_DHALF), lambda t, e: (0, 0, 0)),
                  pl.BlockSpec((PEER_HEADS, PEER_NKEYS, PEER_DHALF), lambda t, e: (0, 0, 0)),
                  pl.BlockSpec((1, PEER_EBLK, d), lambda t, e: (layer, e, 0)),
                  pl.BlockSpec((1, PEER_EBLK, d), lambda t, e: (layer, e, 0)),
                  pl.BlockSpec((1, d), lambda t, e: (0, 0)),
                  pl.BlockSpec((1, d), lambda t, e: (0, 0))],
        out_specs=pl.BlockSpec((tt, d), lambda t, e: (t, 0)),
        out_shape=jax.ShapeDtypeStruct((m, d), f32),
        scratch_shapes=[pltpu.VMEM((d, tt), bf16), pltpu.VMEM((tt, d), f32),
                        pltpu.VMEM((PEER_EBLK, tt), f32), pltpu.VMEM((PEER_EBLK, tt), bf16)]
                       + [pltpu.VMEM((PEER_HEADS, PEER_NKEYS // PEER_SUB, PEER_SUB, tt), f32),
                          pltpu.VMEM((PEER_HEADS, PEER_NKEYS, tt), f32),
                          pltpu.VMEM((PEER_HEADS, PEER_NKEYS // PEER_SUB, PEER_SUB, tt), f32),
                          pltpu.VMEM((PEER_HEADS, PEER_NKEYS, tt), f32)]
                       + [pltpu.VMEM((PEER_VROWS, LANES), f32)] * 2
                       + [pltpu.VMEM((PEER_EBLK // PEER_NKEYS, PEER_HEADS, PEER_SUB, tt), f32)] * 2,
        compiler_params=_params(("parallel", "arbitrary")),
        name="peer",
    )(x, q, keys[0], keys[1], u, v, g.reshape(1, d), b.reshape(1, d))


def kernel(x_prompt, x_sample, cache_k, cache_v, state_conv, state_hgrn, page_table, w_in_even, conv_w, conv_b, conv_ln_g, conv_ln_b, w_out_even, w_in_odd, hgrn_lb_logits, hgrn_norm_g, w_out_odd, ln_g, ln_b, peer_wq, peer_keys, peer_u, peer_v):
    batch, seq, d = x_prompt.shape
    db = x_sample.shape[0]
    n_prompt = batch * seq
    pad_sample = lambda a: jnp.pad(a, ((0, SAMPLE_TILE - a.shape[0]), (0, 0)))
    pages = lambda a: a.reshape(batch, seq // PAGE_SIZE, MOBA_HEADS, MOBA_HEAD_DIM, PAGE_SIZE).transpose(0, 1, 4, 2, 3)
    heads = lambda a: a.reshape(db, MOBA_HEADS, MOBA_HEAD_DIM)

    lb_p = jax.nn.softmax(hgrn_lb_logits.astype(f32), axis=0)
    lb_all = jnp.cumsum(lb_p, axis=0) - lb_p[0]

    peer_u16, peer_v16 = peer_u.astype(bf16), peer_v.astype(bf16)
    xp = x_prompt.reshape(n_prompt, d)
    xs = pad_sample(x_sample.reshape(db, d))
    kp_l, vp_l, ks_l, vs_l, cp_l, cs_l, hp_l, hs_l = [], [], [], [], [], [], [], []
    for l in range(DEPTH):
        if l % 2 == 0:
            e = l // 2
            w_qk, w_vag = w_in_even[e][:, :2 * MOBA_WIDTH], w_in_even[e][:, 2 * MOBA_WIDTH:]
            qk_p, k_pages = _mm(xp, w_qk, split=True, paged_chunk=1)
            vag_p, v_pages = _mm(xp, w_vag, paged_chunk=0)
            qk_s, vag_s = _mm(xs, w_qk, split=True, tm=SAMPLE_TILE)[:db], _mm(xs, w_vag, tm=SAMPLE_TILE)[:db]
            kp_l.append(pages(k_pages))
            vp_l.append(pages(v_pages))
            q_s, k_s, v_s = qk_s[:, :MOBA_WIDTH], qk_s[:, MOBA_WIDTH:], vag_s[:, :MOBA_WIDTH]
            ks_l.append(k_s.reshape(db, 1, MOBA_HEADS, MOBA_HEAD_DIM))
            vs_l.append(v_s.reshape(db, 1, MOBA_HEADS, MOBA_HEAD_DIM))
            att_p = _moba_prompt(qk_p, vag_p, batch, seq)
            att_s = _moba_sample(heads(q_s), heads(k_s), heads(v_s),
                                 cache_k[e].transpose(0, 2, 3, 1), cache_v[e].transpose(0, 2, 3, 1), page_table)
            att_s = att_s[:, :, :MOBA_HEAD_DIM].reshape(db, MOBA_WIDTH)
            cprm = (conv_w[e], conv_b[e], conv_ln_g[e], conv_ln_b[e])
            cy_p, buf_p = _conv_prompt(vag_p, batch, seq, *cprm)
            cy_s, buf_s = _conv_sample(vag_s[:, MOBA_WIDTH:MOBA_WIDTH + CONV_CH], vag_s[:, MOBA_WIDTH + CONV_CH:],
                                       state_conv[e], *cprm)
            cp_l.append(buf_p)
            cs_l.append(buf_s)
            mix_p, mix_s = [att_p, cy_p], [pad_sample(att_s), pad_sample(cy_s)]
            w_out = w_out_even[e]
        else:
            oi = l // 2
            hproj_p = _mm(xp, w_in_odd[oi], tm=WIDE_OUT_TILE)
            hproj_s = _mm(xs, w_in_odd[oi], tm=SAMPLE_TILE)[:db]
            o_p, st_p = _hgrn_prompt(hproj_p, batch, seq, lb_all[l], hgrn_norm_g[oi])
            o_s, st_s = _hgrn_sample(hproj_s, state_hgrn[oi].astype(f32), lb_all[l], hgrn_norm_g[oi])
            hp_l.append(st_p)
            hs_l.append(st_s)
            mix_p, mix_s = [o_p], [pad_sample(o_s)]
            w_out = w_out_odd[oi]
        xp = _mm_res_ln(mix_p, w_out, xp, ln_g[l, 0], ln_b[l, 0])
        xs = _mm_res_ln(mix_s, w_out, xs, ln_g[l, 0], ln_b[l, 0], tm=SAMPLE_TILE)
        prm = (peer_wq[l], peer_keys[l], peer_u16, peer_v16, l, ln_g[l, 1], ln_b[l, 1])
        split_q = l < DEPTH - 1
        xp = _peer_ln(xp, *prm, split_q=split_q)
        xs = _peer_ln(xs, *prm, split_q=split_q, tt=SAMPLE_TILE)
    y_prompt = xp.reshape(batch, seq, d)
    y_sample = xs[:db].reshape(db, 1, d)
    return (y_prompt, y_sample, jnp.stack(kp_l), jnp.stack(vp_l), jnp.stack(ks_l), jnp.stack(vs_l),
            jnp.stack(cp_l), jnp.stack(cs_l), jnp.stack(hp_l), jnp.stack(hs_l))
```
